```python
import math
import jax
import jax.numpy as jnp
from jax import lax
import numpy as np

D_MODEL = 1024
BATCH = 2
SEQ = 8192
DEPTH = 2
DEC_BATCH = 128
DEC_SEQ = 8
PAST_LEN = 2048
PAGE_SIZE = 128

MI_HEADS = 4
MI_DQK = 64
MI_DV = 128
MI_CHUNK = 64
GD_HEADS = 4
GD_DK = 128
GD_DV = 128
GD_CONV = 4
GD_CHUNK = 64
NSA_HEADS = 8
NSA_KV_HEADS = 2
HEAD_DIM = 64
CMP_BLOCK = 32
CMP_STRIDE = CMP_BLOCK // 2
CMP_HIDDEN = 256
SEL_BLOCK = 64
SEL_TOPN = 16
WINDOW = 512
FORCE_SCORE = 1e4
FOX_HEADS = 8
Q_BLOCK = 128
N_BUCKETS = 32
MAX_DISTANCE = 128
D_FF = 4 * D_MODEL
EPS = 1e-6
NEG_BIG = -1e30
N_EVEN = (DEPTH + 1) // 2
N_ODD = DEPTH // 2

MI_QKW = MI_HEADS * MI_DQK
MI_VW = MI_HEADS * MI_DV
GD_KW = GD_HEADS * GD_DK
GD_VW = GD_HEADS * GD_DV
GD_CONV_CH = 2 * GD_KW + GD_VW
EVEN_SPLITS = (MI_QKW, MI_QKW, MI_VW, MI_VW, MI_HEADS, MI_HEADS,
               GD_KW, GD_KW, GD_VW, GD_VW, GD_HEADS, GD_HEADS)
EVEN_MIX = MI_VW + GD_VW
NSA_KVW = NSA_KV_HEADS * HEAD_DIM
ODD_SPLITS = (NSA_HEADS * HEAD_DIM, 6 * NSA_KVW, 3 * NSA_HEADS,
              FOX_HEADS * HEAD_DIM, FOX_HEADS * HEAD_DIM, FOX_HEADS * HEAD_DIM, FOX_HEADS)
ODD_MIX = (NSA_HEADS + FOX_HEADS) * HEAD_DIM

kernel_name = 'hybrid_mlstm_gdn_nsa_fox_step'


def rmsnorm(x, g):
    xf = x.astype(jnp.float32)
    return (xf * lax.rsqrt(jnp.mean(xf * xf, axis=-1, keepdims=True) + EPS) * g).astype(x.dtype)


def _split(a, sizes):
    return jnp.split(a, np.cumsum(sizes)[:-1].tolist(), axis=-1)


def l2norm(x):
    return x * lax.rsqrt(jnp.sum(x * x, axis=-1, keepdims=True) + EPS)


def masked_softmax(logits, mask):
    lg = jnp.where(mask, logits.astype(jnp.float32), NEG_BIG)
    p = jnp.exp(lg - jnp.max(lg, axis=-1, keepdims=True)) * mask
    return p / jnp.maximum(jnp.sum(p, axis=-1, keepdims=True), 1e-30)


def t5_bucket(dist):
    n = jnp.maximum(dist, 0)
    exact = N_BUCKETS // 2
    nf = jnp.maximum(n, 1).astype(jnp.float32)
    large = exact + (jnp.log(nf / exact) / math.log(MAX_DISTANCE / exact)
                     * (N_BUCKETS - exact)).astype(jnp.int32)
    return jnp.where(n < exact, n, jnp.minimum(large, N_BUCKETS - 1))


def causal_conv(x, buf, w):
    T = x.shape[1]
    xp = jnp.concatenate([buf, x], axis=1)
    y = sum(xp[:, j:j + T] * w[j] for j in range(GD_CONV))
    return jax.nn.silu(y), xp[:, -(GD_CONV - 1):]


def mlstm_chunked(q, k, v, log_i, log_f, c0, n0, m0):
    B, T, H, _ = q.shape
    L = math.gcd(T, MI_CHUNK)
    nc = T // L
    tri = jnp.tril(jnp.ones((L, L), dtype=bool))

    def chunks(a):
        return jnp.moveaxis(a.reshape(B, nc, L, *a.shape[2:]), 1, 0)

    def step(carry, inp):
        c, n, m = carry
        qc, kc, vc, lic, lfc = inp
        b = jnp.cumsum(lfc, axis=1)
        d = jnp.where(tri[None, :, :, None],
                      b[:, :, None, :] - b[:, None, :, :] + lic[:, None, :, :], -jnp.inf)
        inter = b + m[:, None, :]
        m_t = jnp.maximum(inter, jnp.max(d, axis=2))
        s = jnp.einsum('bthd,bshd->btsh', qc, kc) * jnp.exp(d - m_t[:, :, None, :])
        a_inter = jnp.exp(inter - m_t)
        num = jnp.einsum('btsh,bshv->bthv', s, vc) + a_inter[..., None] * jnp.einsum('bthd,bhdv->bthv', qc, c)
        den = jnp.sum(s, axis=2) + a_inter * jnp.einsum('bthd,bhd->bth', qc, n)
        h = num / jnp.maximum(jnp.abs(den), jnp.exp(-m_t))[..., None]
        b_last = b[:, -1]
        g = b_last[:, None, :] - b + lic
        m_new = jnp.maximum(b_last + m, jnp.max(g, axis=1))
        a_prev = jnp.exp(b_last + m - m_new)
        wgt = jnp.exp(g - m_new[:, None, :])
        c = a_prev[:, :, None, None] * c + jnp.einsum('bsh,bshd,bshv->bhdv', wgt, kc, vc)
        n = a_prev[:, :, None] * n + jnp.einsum('bsh,bshd->bhd', wgt, kc)
        return (c, n, m_new), h

    (c, n, m), h = lax.scan(step, (c0, n0, m0), tuple(chunks(a) for a in (q, k, v, log_i, log_f)))
    return jnp.moveaxis(h, 0, 1).reshape(B, T, H, v.shape[-1]), c, n, m


def gated_delta_chunked(q, k, v, g, beta, s0):
    B, T, H, DK = q.shape
    DV = v.shape[-1]
    L = math.gcd(T, GD_CHUNK)
    nc = T // L

    def ch(a):
        return jnp.swapaxes(a.reshape(B, nc, L, *a.shape[2:]), 2, 3)

    q, k, v, g, beta = (ch(a) for a in (q, k, v, g, beta))
    gc = jnp.cumsum(g, axis=-1)
    idx = jnp.arange(L)
    incl = idx[:, None] >= idx[None, :]
    strict = idx[:, None] > idx[None, :]
    dec_incl = jnp.exp(jnp.where(incl, gc[..., :, None] - gc[..., None, :], -jnp.inf))
    dec_strict = jnp.where(strict, dec_incl, 0.0)
    kb = k * beta[..., None]
    a = jnp.einsum('bchtd,bchsd->bchts', kb, k) * dec_strict + jnp.eye(L, dtype=jnp.float32)
    rhs = jnp.concatenate([v * beta[..., None], kb * jnp.exp(gc)[..., None]], axis=-1)
    sol = lax.linalg.triangular_solve(a, rhs, left_side=True, lower=True)
    u, w = sol[..., :DV], sol[..., DV:]
    attn = jnp.einsum('bchtd,bchsd->bchts', q, k) * dec_incl
    q_dec = q * jnp.exp(gc)[..., None]
    g_last = gc[..., -1]
    k_dec = k * jnp.exp(g_last[..., None] - gc)[..., None]

    def step(s, inp):
        u_c, w_c, at_c, qd_c, kd_c, gl_c = inp
        v_new = u_c - jnp.einsum('bhld,bhdv->bhlv', w_c, s)
        o = jnp.einsum('bhld,bhdv->bhlv', qd_c, s) + jnp.einsum('bhts,bhsv->bhtv', at_c, v_new)
        s = jnp.exp(gl_c)[..., None, None] * s + jnp.einsum('bhld,bhlv->bhdv', kd_c, v_new)
        return s, o

    xs = tuple(jnp.moveaxis(t, 1, 0) for t in (u, w, attn, q_dec, k_dec, g_last))
    s, o = lax.scan(step, s0, xs)
    return jnp.transpose(o, (1, 0, 3, 2, 4)).reshape(B, T, H, DV), s


def even_mixer(h, w_in, w_out, b_i, b_f, mi_norm, conv_w, a_log, dt_bias, gd_norm, past):
    B, T, _ = h.shape
    f32 = jnp.float32
    mq, mk, mv, mo, mi, mf, gq, gk, gv, gz, gb, ga = _split((h @ w_in).astype(f32), EVEN_SPLITS)
    if past is None:
        c0 = jnp.zeros((B, MI_HEADS, MI_DQK, MI_DV), f32)
        n0 = jnp.zeros((B, MI_HEADS, MI_DQK), f32)
        m0 = jnp.zeros((B, MI_HEADS), f32)
        s0 = jnp.zeros((B, GD_HEADS, GD_DK, GD_DV), f32)
        conv0 = jnp.zeros((B, GD_CONV - 1, GD_CONV_CH), f32)
    else:
        c0, n0, m0, s0, conv0 = (a.astype(f32) for a in past)
    hm, c1, n1, m1 = mlstm_chunked(
        mq.reshape(B, T, MI_HEADS, MI_DQK),
        mk.reshape(B, T, MI_HEADS, MI_DQK) * MI_DQK ** -0.5,
        mv.reshape(B, T, MI_HEADS, MI_DV),
        mi + b_i, jax.nn.log_sigmoid(mf + b_f), c0, n0, m0)
    hm = rmsnorm(hm, mi_norm.reshape(MI_HEADS, MI_DV)).reshape(B, T, MI_VW) * jax.nn.sigmoid(mo)
    conv_out, conv1 = causal_conv(jnp.concatenate([gq, gk, gv], axis=-1), conv0, conv_w)
    cq, ck, cv = _split(conv_out, (GD_KW, GD_KW, GD_VW))
    cq = l2norm(cq.reshape(B, T, GD_HEADS, GD_DK)) * GD_DK ** -0.5
    ck = l2norm(ck.reshape(B, T, GD_HEADS, GD_DK))
    g = -jnp.exp(a_log) * jax.nn.softplus(ga + dt_bias)
    og, s1 = gated_delta_chunked(cq, ck, cv.reshape(B, T, GD_HEADS, GD_DV), g, jax.nn.sigmoid(gb), s0)
    og = rmsnorm(og, gd_norm) * jax.nn.silu(gz.reshape(B, T, GD_HEADS, GD_DV))
    y = jnp.concatenate([hm, og.reshape(B, T, GD_VW)], axis=-1).astype(h.dtype) @ w_out
    dt = h.dtype
    return y, (c1.astype(dt), n1.astype(dt), m1.astype(dt), s1.astype(dt), conv1.astype(dt))


def compress(kf, pos_emb, w1, w2):
    B, Tp, G, D = kf.shape
    c = kf.reshape(B, Tp // CMP_STRIDE, CMP_STRIDE, G, D)
    blocks = jnp.concatenate([c[:, :-1], c[:, 1:]], axis=2) + pos_emb[:, None, :]
    nc = blocks.shape[1]
    flat = jnp.moveaxis(blocks, 3, 2).reshape(B, nc, G, CMP_BLOCK * D)
    return jax.nn.silu(flat @ w1) @ w2


def selection_overlap(nc, ns):
    c_start = np.arange(nc)[:, None] * CMP_STRIDE
    s_start = np.arange(ns)[None, :] * SEL_BLOCK
    return ((c_start < s_start + SEL_BLOCK) & (c_start + CMP_BLOCK > s_start)).astype(np.float32)


def nsa_attend(q, q_pos, gates, kc, vc, c_end, ksel, vsel, kw, vw, w_pos, rel_bias, sel_map):
    B, Lq, G, R, D = q.shape
    scale = D ** -0.5
    tbl = rel_bias.reshape(N_BUCKETS, G, R)
    dist_c = q_pos[:, None] - c_end[None, :]
    bias_c = jnp.moveaxis(tbl[t5_bucket(dist_c)], (2, 3), (0, 1))
    pc = masked_softmax(jnp.einsum('bqgrd,bngd->bgrqn', q, kc) * scale + bias_c, dist_c >= 0)
    o_c = jnp.einsum('bgrqn,bngd->bqgrd', pc, vc)
    ns = ksel.shape[2]
    ps = jnp.einsum('bgrqn,ns->bgqs', pc, sel_map)
    blk = jnp.arange(ns)
    cur = q_pos // SEL_BLOCK
    visible = blk[None, :] * SEL_BLOCK <= q_pos[:, None]
    forced = (blk[None, :] == 0) | (blk[None, :] == cur[:, None]) | (blk[None, :] == cur[:, None] - 1)
    score = jnp.where(forced, FORCE_SCORE, jnp.where(visible, ps, -1.0))
    n_top = min(SEL_TOPN, ns)
    _, idx = lax.top_k(score, n_top)
    pick = jax.vmap(jax.vmap(lambda kb, ib: kb[ib]))
    ks = pick(ksel, idx).reshape(B, G, Lq, n_top * SEL_BLOCK, D)
    vs = pick(vsel, idx).reshape(B, G, Lq, n_top * SEL_BLOCK, D)
    pos_s = (idx[..., None] * SEL_BLOCK + jnp.arange(SEL_BLOCK)).reshape(B, G, Lq, n_top * SEL_BLOCK)
    dist_s = q_pos[None, None, :, None] - pos_s
    tbl_g = jnp.transpose(tbl, (1, 0, 2))
    bias_s = jnp.moveaxis(tbl_g[jnp.arange(G)[None, :, None, None], t5_bucket(dist_s)], -1, 2)
    p_s = masked_softmax(jnp.einsum('bqgrd,bgqkd->bgrqk', q, ks) * scale + bias_s, (dist_s >= 0)[:, :, None])
    o_s = jnp.einsum('bgrqk,bgqkd->bqgrd', p_s, vs)
    dist_w = q_pos[:, None] - w_pos[None, :]
    mask_w = (dist_w >= 0) & (dist_w < WINDOW) & (w_pos[None, :] >= 0)
    bias_w = jnp.moveaxis(tbl[t5_bucket(dist_w)], (2, 3), (0, 1))
    p_w = masked_softmax(jnp.einsum('bqgrd,bkgd->bgrqk', q, kw) * scale + bias_w, mask_w)
    o_w = jnp.einsum('bgrqk,bkgd->bqgrd', p_w, vw)
    return gates[..., 0:1] * o_c + gates[..., 1:2] * o_s + gates[..., 2:3] * o_w


def fox_attend(q, F_q, q_pos, k, v, F_k, k_pos):
    logits = (jnp.einsum('bqhd,bkhd->bhqk', q, k).astype(jnp.float32) * HEAD_DIM ** -0.5
              + jnp.moveaxis(F_q, 2, 1)[..., None] - jnp.moveaxis(F_k, 2, 1)[:, :, None, :])
    p = masked_softmax(logits, k_pos[None, :] <= q_pos[:, None])
    return jnp.einsum('bhqk,bkhd->bqhd', p, v)


def odd_mixer(h, w_in, w_out, cmp_pos, cmp_w1, cmp_w2, fox_b_f, rel_bias, w_buf, past):
    B, T, _ = h.shape
    G, R, D = NSA_KV_HEADS, NSA_HEADS // NSA_KV_HEADS, HEAD_DIM
    nq, nkv, ngate, fq, fk, fv, ff = _split(h @ w_in, ODD_SPLITS)
    q = nq.reshape(B, T, G, R, D)
    kv = nkv.reshape(B, T, 6, G, D)
    gates = jax.nn.sigmoid(ngate.astype(jnp.float32)).reshape(B, T, G, R, 3)
    fq = fq.reshape(B, T, FOX_HEADS, D)
    new_fox = jnp.stack([fk.reshape(B, T, FOX_HEADS, D), fv.reshape(B, T, FOX_HEADS, D)], axis=2)
    logf = jax.nn.log_sigmoid((ff + fox_b_f).astype(jnp.float32))
    new_nsa = kv[:, :, :4]
    new_win = kv[:, :, 4:]
    if past is None:
        start = 0
        nsa_full, fox_full, logf_full = new_nsa, new_fox, logf
        win_prev = jnp.zeros((B, WINDOW, 2, G, D), h.dtype)
    else:
        nsa_past, win_prev, fox_past, logf_past = past
        start = nsa_past.shape[1]
        nsa_full = jnp.concatenate([nsa_past.astype(h.dtype), new_nsa], axis=1)
        fox_full = jnp.concatenate([fox_past.astype(h.dtype), new_fox], axis=1)
        logf_full = jnp.concatenate([logf_past.astype(jnp.float32), logf], axis=1)
    win_all = jnp.concatenate([win_prev.astype(h.dtype), new_win], axis=1)
    wp = win_all.shape[1] - T
    new_win_state = win_all[:, -w_buf:]
    Tk = nsa_full.shape[1]
    Tp = -(-Tk // SEL_BLOCK) * SEL_BLOCK
    nsa_pad = jnp.pad(nsa_full, ((0, 0), (0, Tp - Tk), (0, 0), (0, 0), (0, 0)))
    kc = compress(nsa_pad[:, :, 0], cmp_pos[0], cmp_w1[0], cmp_w2[0])
    vc = compress(nsa_pad[:, :, 1], cmp_pos[1], cmp_w1[1], cmp_w2[1])
    nc = kc.shape[1]
    ns = Tp // SEL_BLOCK
    c_end = jnp.arange(nc) * CMP_STRIDE + (CMP_BLOCK - 1)
    ksel = jnp.transpose(nsa_pad[:, :, 2].reshape(B, ns, SEL_BLOCK, G, D), (0, 3, 1, 2, 4))
    vsel = jnp.transpose(nsa_pad[:, :, 3].reshape(B, ns, SEL_BLOCK, G, D), (0, 3, 1, 2, 4))
    sel_map = jnp.asarray(selection_overlap(nc, ns))
    fk_full, fv_full = fox_full[:, :, 0], fox_full[:, :, 1]
    F_full = jnp.cumsum(logf_full, axis=1)
    F_q = F_full[:, start:]
    k_pos = jnp.arange(Tk)
    if past is None:
        qb = math.gcd(T, Q_BLOCK)

        def block(i):
            qs = i * qb
            sl = lambda a: lax.dynamic_slice_in_dim(a, qs, qb, axis=1)
            qp = qs + jnp.arange(qb)
            wk = lax.dynamic_slice_in_dim(win_all, qs, qb + WINDOW, axis=1)
            wpos = qs - WINDOW + jnp.arange(qb + WINDOW)
            o_n = nsa_attend(sl(q), qp, sl(gates), kc, vc, c_end, ksel, vsel,
                             wk[:, :, 0], wk[:, :, 1], wpos, rel_bias, sel_map)
            o_f = fox_attend(sl(fq), sl(F_q), qp, fk_full, fv_full, F_full, k_pos)
            return o_n, o_f

        o_n, o_f = lax.map(block, jnp.arange(T // qb))
        o_n = jnp.moveaxis(o_n, 0, 1).reshape(B, T, NSA_HEADS * D)
        o_f = jnp.moveaxis(o_f, 0, 1).reshape(B, T, FOX_HEADS * D)
    else:
        q_pos = start + jnp.arange(T)
        w_pos = start - wp + jnp.arange(wp + T)
        o_n = nsa_attend(q, q_pos, gates, kc, vc, c_end, ksel, vsel, win_all[:, :, 0], win_all[:, :, 1],
                         w_pos, rel_bias, sel_map).reshape(B, T, NSA_HEADS * D)
        o_f = fox_attend(fq, F_q, q_pos, fk_full, fv_full, F_full, k_pos).reshape(B, T, FOX_HEADS * D)
    y = jnp.concatenate([o_n, o_f], axis=-1).astype(h.dtype) @ w_out
    return y, (new_nsa, new_win_state, new_fox, logf.astype(h.dtype))


def gather_pages(pool, page_table):
    g = pool[page_table]
    return g.reshape(page_table.shape[0], -1, *pool.shape[2:])


def trunk(x, past, page_table, p, w_buf):
    new = [[] for _ in range(9)]
    ie = 0
    io = 0
    for layer in range(DEPTH):
        hn = rmsnorm(x, p['norm_mix'][layer])
        if layer % 2 == 0:
            st = None if past is None else (past['mc'][ie], past['mn'][ie], past['mm'][ie],
                                            past['gs'][ie], past['gc'][ie])
            y, ns = even_mixer(hn, p['w_in_even'][ie], p['w_out_even'][ie], p['mi_b_i'][ie], p['mi_b_f'][ie],
                               p['mi_norm'][ie], p['gd_conv_w'][ie], p['gd_a_log'][ie], p['gd_dt_bias'][ie],
                               p['gd_norm'][ie], st)
            for lst, a in zip(new[:5], ns):
                lst.append(a)
            ie += 1
        else:
            st = None if past is None else (gather_pages(past['nsa_kv'][io], page_table), past['nsa_win'][io],
                                            gather_pages(past['fox_kv'][io], page_table),
                                            gather_pages(past['fox_logf'][io], page_table))
            y, ns = odd_mixer(hn, p['w_in_odd'][io], p['w_out_odd'][io], p['nsa_cmp_pos'][io],
                              p['nsa_cmp_w1'][io], p['nsa_cmp_w2'][io], p['fox_b_f'][io], p['rel_bias'],
                              w_buf, st)
            for lst, a in zip(new[5:], ns):
                lst.append(a)
            io += 1
        x = x + y
        hn = rmsnorm(x, p['norm_mlp'][layer])
        u = jnp.maximum(hn @ p['w_up'][layer], 0)
        x = x + (u * u) @ p['w_down'][layer]
    return rmsnorm(x, p['norm_final']), tuple(jnp.stack(lst) for lst in new)


def setup_inputs(seed: int = 0) -> dict:
    key = jax.random.key(seed)
    keys = iter(jax.random.split(key, 48))

    def nrm(shape, scale):
        return jax.random.normal(next(keys), shape, jnp.float32) * scale

    n_pages = PAST_LEN // PAGE_SIZE
    n_pool = (DEC_BATCH * n_pages * 5 + 3) // 4
    w_buf = min(WINDOW, PAST_LEN)
    G, HD = NSA_KV_HEADS, HEAD_DIM
    inp = {}
    inp['x_prompt'] = nrm((BATCH, SEQ, D_MODEL), 1.0)
    inp['x_sample'] = nrm((DEC_BATCH, DEC_SEQ, D_MODEL), 1.0)
    inp['state_mlstm_c'] = nrm((N_EVEN, DEC_BATCH, MI_HEADS, MI_DQK, MI_DV), 0.1)
    inp['state_mlstm_n'] = nrm((N_EVEN, DEC_BATCH, MI_HEADS, MI_DQK), 0.1)
    inp['state_mlstm_m'] = nrm((N_EVEN, DEC_BATCH, MI_HEADS), 1.0)
    inp['state_gdn_s'] = nrm((N_EVEN, DEC_BATCH, GD_HEADS, GD_DK, GD_DV), 0.1)
    inp['state_gdn_conv'] = nrm((N_EVEN, DEC_BATCH, GD_CONV - 1, GD_CONV_CH), 1.0)
    inp['cache_nsa_kv'] = nrm((N_ODD, n_pool, PAGE_SIZE, 4, G, HD), 1.0)
    inp['state_nsa_win'] = nrm((N_ODD, DEC_BATCH, w_buf, 2, G, HD), 1.0)
    inp['cache_fox_kv'] = nrm((N_ODD, n_pool, PAGE_SIZE, 2, FOX_HEADS, HD), 1.0)
    inp['cache_fox_logf'] = jax.nn.log_sigmoid(3.0 + nrm((N_ODD, n_pool, PAGE_SIZE, FOX_HEADS), 0.5))
    perm = jax.random.permutation(next(keys), n_pool)
    inp['page_table'] = perm[:DEC_BATCH * n_pages].reshape(DEC_BATCH, n_pages).astype(jnp.int32)
    inp['norm_mix'] = 1.0 + nrm((DEPTH, D_MODEL), 0.05)
    inp['norm_mlp'] = 1.0 + nrm((DEPTH, D_MODEL), 0.05)
    inp['norm_final'] = 1.0 + nrm((D_MODEL,), 0.05)
    inp['w_up'] = nrm((DEPTH, D_MODEL, D_FF), D_MODEL ** -0.5)
    inp['w_down'] = nrm((DEPTH, D_FF, D_MODEL), D_FF ** -0.5)
    inp['w_in_even'] = nrm((N_EVEN, D_MODEL, sum(EVEN_SPLITS)), D_MODEL ** -0.5)
    inp['w_out_even'] = nrm((N_EVEN, EVEN_MIX, D_MODEL), EVEN_MIX ** -0.5)
    inp['mi_b_i'] = nrm((N_EVEN, MI_HEADS), 0.1)
    inp['mi_b_f'] = 3.0 + nrm((N_EVEN, MI_HEADS), 0.5)
    inp['mi_norm'] = 1.0 + nrm((N_EVEN, MI_VW), 0.05)
    inp['gd_conv_w'] = nrm((N_EVEN, GD_CONV, GD_CONV_CH), GD_CONV ** -0.5)
    inp['gd_a_log'] = jnp.log(jax.random.uniform(next(keys), (N_EVEN, GD_HEADS), jnp.float32, 1.0, 16.0))
    dt = jnp.exp(jax.random.uniform(next(keys), (N_EVEN, GD_HEADS), jnp.float32,
                                    math.log(1e-3), math.log(1e-1)))
    inp['gd_dt_bias'] = dt + jnp.log(-jnp.expm1(-dt))
    inp['gd_norm'] = 1.0 + nrm((N_EVEN, GD_DV), 0.05)
    inp['w_in_odd'] = nrm((N_ODD, D_MODEL, sum(ODD_SPLITS)), D_MODEL ** -0.5)
    inp['w_out_odd'] = nrm((N_ODD, ODD_MIX, D_MODEL), ODD_MIX ** -0.5)
    inp['nsa_cmp_pos'] = nrm((N_ODD, 2, CMP_BLOCK, HD), 0.1)
    inp['nsa_cmp_w1'] = nrm((N_ODD, 2, CMP_BLOCK * HD, CMP_HIDDEN), (CMP_BLOCK * HD) ** -0.5)
    inp['nsa_cmp_w2'] = nrm((N_ODD, 2, CMP_HIDDEN, HD), CMP_HIDDEN ** -0.5)
    inp['fox_b_f'] = 3.0 + nrm((N_ODD, FOX_HEADS), 0.5)
    inp['rel_bias'] = nrm((N_BUCKETS, NSA_HEADS), 0.5)
    return inp


def reference(x_prompt, x_sample, state_mlstm_c, state_mlstm_n, state_mlstm_m, state_gdn_s, state_gdn_conv,
              cache_nsa_kv, state_nsa_win, cache_fox_kv, cache_fox_logf, page_table,
              norm_mix, norm_mlp, norm_final, w_up, w_down,
              w_in_even, w_out_even, mi_b_i, mi_b_f, mi_norm, gd_conv_w, gd_a_log, gd_dt_bias, gd_norm,
              w_in_odd, w_out_odd, nsa_cmp_pos, nsa_cmp_w1, nsa_cmp_w2, fox_b_f, rel_bias):
    params = dict(norm_mix=norm_mix, norm_mlp=norm_mlp, norm_final=norm_final, w_up=w_up, w_down=w_down,
                  w_in_even=w_in_even, w_out_even=w_out_even, mi_b_i=mi_b_i, mi_b_f=mi_b_f, mi_norm=mi_norm,
                  gd_conv_w=gd_conv_w, gd_a_log=gd_a_log, gd_dt_bias=gd_dt_bias, gd_norm=gd_norm,
                  w_in_odd=w_in_odd, w_out_odd=w_out_odd, nsa_cmp_pos=nsa_cmp_pos, nsa_cmp_w1=nsa_cmp_w1,
                  nsa_cmp_w2=nsa_cmp_w2, fox_b_f=fox_b_f, rel_bias=rel_bias)
    w_buf = state_nsa_win.shape[2]
    y_prompt, (p_mc, p_mn, p_mm, p_gs, p_gc, p_nkv, p_nwin, p_fkv, p_flogf) = trunk(
        x_prompt, None, None, params, w_buf)
    past = dict(mc=state_mlstm_c, mn=state_mlstm_n, mm=state_mlstm_m, gs=state_gdn_s, gc=state_gdn_conv,
                nsa_kv=cache_nsa_kv, nsa_win=state_nsa_win, fox_kv=cache_fox_kv, fox_logf=cache_fox_logf)
    y_sample, (s_mc, s_mn, s_mm, s_gs, s_gc, s_nkv, s_nwin, s_fkv, s_flogf) = trunk(
        x_sample, past, page_table, params, w_buf)
    return (y_prompt, y_sample,
            p_mc, p_mn, p_mm, p_gs, p_gc, p_nkv, p_nwin, p_fkv, p_flogf,
            s_mc, s_mn, s_mm, s_gs, s_gc, s_nkv, s_nwin, s_fkv, s_flogf)
```

```python
import functools
import math

import jax
import jax.numpy as jnp
import numpy as np
from jax import lax
from jax.experimental import pallas as pl
from jax.experimental.pallas import tpu as pltpu

F32 = jnp.float32
BF16 = jnp.bfloat16
HI = lax.Precision.HIGHEST

D_MODEL = 1024
D_FF = 4 * D_MODEL
EPS = 1e-6
NEG_BIG = -1e30
PAGE = 128

MI_H, MI_DQK, MI_DV = 4, 64, 128
GD_H, GD_DK, GD_DV, GD_CONV = 4, 128, 128, 4
GD_CH = 3 * GD_H * GD_DK
NSA_G, NSA_R, HD = 2, 4, 64
FOX_H = 8
CMP_BLOCK, CMP_STRIDE, CMP_HIDDEN = 32, 16, 256
SEL_BLOCK, SEL_TOPN, WINDOW = 64, 16, 512
FORCE_SCORE = 1e4
N_BUCKETS, MAX_DISTANCE = 32, 128
BUCKET_EXACT = N_BUCKETS // 2
BUCKET_SAT_DIST = 113
LANE = 128
VMEM_LIMIT = 56 * 1024 * 1024


def _cparams(sem):
    return pltpu.CompilerParams(dimension_semantics=sem, vmem_limit_bytes=VMEM_LIMIT)


def _dot(a, b, precision=None):
    return jnp.dot(a, b, preferred_element_type=F32, precision=precision)


def _dot_nt(a, b, precision=None):
    return lax.dot_general(a, b, (((1,), (1,)), ((), ())), preferred_element_type=F32, precision=precision)


def _dot_tn(a, b, precision=None):
    return lax.dot_general(a, b, (((0,), (0,)), ((), ())), preferred_element_type=F32, precision=precision)


def _softplus(x):
    return jnp.maximum(x, 0.0) + jnp.log1p(jnp.exp(-jnp.abs(x)))


def _sigmoid(x):
    return 1.0 / (1.0 + jnp.exp(-x))


def _silu(x):
    return x * _sigmoid(x)


def _iota(shape, dim):
    return lax.broadcasted_iota(jnp.int32, shape, dim)


ACT_ID, ACT_LOGSIG, ACT_SIG, ACT_DECAY = 0.0, 1.0, 2.0, 3.0


def _proj_body(x_ref, g_ref, w_ref, sp_ref, *out_refs, widths):
    x = x_ref[...]
    hn = (x * lax.rsqrt(jnp.mean(x * x, axis=-1, keepdims=True) + EPS) * g_ref[...]).astype(BF16)
    off = 0
    for i, (o_ref, n) in enumerate(zip(out_refs, widths)):
        r = _dot(hn, w_ref[:, off:off + n])
        if i == len(widths) - 1:
            z = r + sp_ref[0:1, :]
            mode = sp_ref[1:2, :]
            decay = -jnp.exp(sp_ref[2:3, :]) * _softplus(z)
            r = jnp.where(mode == ACT_LOGSIG, -_softplus(-z),
                          jnp.where(mode == ACT_SIG, _sigmoid(z),
                                    jnp.where(mode == ACT_DECAY, decay, z)))
        o_ref[...] = r.astype(o_ref.dtype)
        off += n


def norm_proj(x, g, w_bf16, small_params, widths, tm):
    m, d = x.shape
    n_total = sum(widths)
    assert w_bf16.shape == (d, n_total) and m % tm == 0
    out_shape = [jax.ShapeDtypeStruct((m, n), F32) for n in widths]
    return pl.pallas_call(
        functools.partial(_proj_body, widths=tuple(widths)),
        grid=(m // tm,),
        in_specs=[pl.BlockSpec((tm, d), lambda i: (i, 0)),
                  pl.BlockSpec((1, d), lambda i: (0, 0)),
                  pl.BlockSpec((d, n_total), lambda i: (0, 0)),
                  pl.BlockSpec((8, LANE), lambda i: (0, 0))],
        out_specs=[pl.BlockSpec((tm, n), lambda i: (i, 0)) for n in widths],
        out_shape=out_shape,
        compiler_params=_cparams(("parallel",)),
        name="norm_proj",
    )(x, g.reshape(1, d), w_bf16, small_params)


def _outproj_body(x_ref, a1_ref, a2_ref, w_ref, o_ref):
    k1 = a1_ref.shape[1]
    y = _dot(a1_ref[...].astype(BF16), w_ref[0:k1, :]) + _dot(a2_ref[...].astype(BF16), w_ref[k1:, :])
    o_ref[...] = x_ref[...] + y


def out_proj_residual(x, a1, a2, w_bf16, tm):
    m, d = x.shape
    k1, k2 = a1.shape[1], a2.shape[1]
    return pl.pallas_call(
        _outproj_body,
        grid=(m // tm,),
        in_specs=[pl.BlockSpec((tm, d), lambda i: (i, 0)),
                  pl.BlockSpec((tm, k1), lambda i: (i, 0)),
                  pl.BlockSpec((tm, k2), lambda i: (i, 0)),
                  pl.BlockSpec((k1 + k2, d), lambda i: (0, 0))],
        out_specs=pl.BlockSpec((tm, d), lambda i: (i, 0)),
        out_shape=jax.ShapeDtypeStruct((m, d), F32),
        compiler_params=_cparams(("parallel",)),
        name="out_proj",
    )(x, a1, a2, w_bf16)


def _mlp_body(x_ref, g_ref, wu_ref, wd_ref, gf_ref, o_ref, hn_scr, acc_scr, *, final_norm):
    j = pl.program_id(1)

    @pl.when(j == 0)
    def _():
        x = x_ref[...]
        hn_scr[...] = (x * lax.rsqrt(jnp.mean(x * x, axis=-1, keepdims=True) + EPS) * g_ref[...]).astype(BF16)
        acc_scr[...] = jnp.zeros_like(acc_scr)

    u = jnp.maximum(_dot(hn_scr[...], wu_ref[...]), 0.0)
    acc_scr[...] += _dot((u * u).astype(BF16), wd_ref[...])

    @pl.when(j == pl.num_programs(1) - 1)
    def _():
        y = x_ref[...] + acc_scr[...]
        if final_norm:
            y = y * lax.rsqrt(jnp.mean(y * y, axis=-1, keepdims=True) + EPS) * gf_ref[...]
        o_ref[...] = y


def mlp_residual(x, g, w_up_bf16, w_down_bf16, g_final, final_norm, tm, tf):
    m, d = x.shape
    f = w_up_bf16.shape[1]
    return pl.pallas_call(
        functools.partial(_mlp_body, final_norm=final_norm),
        grid=(m // tm, f // tf),
        in_specs=[pl.BlockSpec((tm, d), lambda i, j: (i, 0)),
                  pl.BlockSpec((1, d), lambda i, j: (0, 0)),
                  pl.BlockSpec((d, tf), lambda i, j: (0, j)),
                  pl.BlockSpec((tf, d), lambda i, j: (j, 0)),
                  pl.BlockSpec((1, d), lambda i, j: (0, 0))],
        out_specs=pl.BlockSpec((tm, d), lambda i, j: (i, 0)),
        out_shape=jax.ShapeDtypeStruct((m, d), F32),
        scratch_shapes=[pltpu.VMEM((tm, d), BF16), pltpu.VMEM((tm, d), F32)],
        compiler_params=_cparams(("parallel", "arbitrary")),
        name="mlp",
    )(x, g.reshape(1, d), w_up_bf16, w_down_bf16, g_final.reshape(1, d))


def _tri(n):
    r = _iota((n, n), 0)
    c = _iota((n, n), 1)
    return r, c


def _mlstm_chunk(bb, q_ref, k_ref, v_ref, o_ref, sc_ref, sr_ref, nw_ref, h_ref, c_ref, n_ref, m_ref, L):
    rows, cols = _tri(L)
    lower = rows >= cols
    tril = lower.astype(F32)
    triu = (rows <= cols).astype(F32)
    sc = sc_ref[bb]
    sr = sr_ref[bb, 0]
    b_col = _dot(tril, sc[:, 4:8], HI)
    b_row = _dot(sr[4:8, :], triu, HI)
    m_vec = m_ref[bb]
    lane4 = _iota((1, MI_H), 1)
    for h in range(MI_H):
        q = q_ref[bb, h]
        k = k_ref[bb, h] * (MI_DQK ** -0.5)
        v = v_ref[bb, :, h * MI_DV:(h + 1) * MI_DV]
        bc = b_col[:, h:h + 1]
        br = b_row[h:h + 1, :]
        li_r = sr[h:h + 1, :]
        li_c = sc[:, h:h + 1]
        m_prev = m_vec[:, h:h + 1]
        c_prev = c_ref[bb, h]
        n_prev = n_ref[bb, h]
        d = jnp.where(lower, bc - br + li_r, NEG_BIG)
        inter = bc + m_prev
        m_t = jnp.maximum(inter, jnp.max(d, axis=1, keepdims=True))
        s = _dot_nt(q, k, HI) * jnp.exp(d - m_t)
        a_inter = jnp.exp(inter - m_t)
        num = _dot(s, v, HI) + a_inter * _dot(q, c_prev, HI)
        den = jnp.sum(s, axis=1, keepdims=True) + a_inter * jnp.sum(q * n_prev, axis=1, keepdims=True)
        hh = num / jnp.maximum(jnp.abs(den), jnp.exp(-m_t))
        b_last = bc[L - 1:L, :]
        g_col = b_last - bc + li_c
        m_new = jnp.maximum(b_last + m_prev, jnp.max(g_col, axis=0, keepdims=True))
        a_prev = jnp.exp(b_last + m_prev - m_new)
        kw = k * jnp.exp(g_col - m_new)
        c_ref[bb, h] = a_prev * c_prev + _dot_tn(kw, v, HI)
        n_ref[bb, h] = a_prev * n_prev + jnp.sum(kw, axis=0, keepdims=True)
        m_vec = jnp.where(lane4 == h, m_new, m_vec)
        hn = hh * lax.rsqrt(jnp.mean(hh * hh, axis=-1, keepdims=True) + EPS) * nw_ref[h:h + 1, :]
        gate = _sigmoid(o_ref[bb, :, h * MI_DV:(h + 1) * MI_DV])
        h_ref[bb, :, h * MI_DV:(h + 1) * MI_DV] = hn * gate
    m_ref[bb] = m_vec


def _mlstm_body(q_ref, k_ref, v_ref, o_ref, sc_ref, sr_ref, c0_ref, n0_ref, m0_ref, nw_ref,
                h_ref, c_ref, n_ref, m_ref, *, L, Bb):
    @pl.when(pl.program_id(1) == 0)
    def _():
        c_ref[...] = c0_ref[...]
        n_ref[...] = n0_ref[...]
        m_ref[...] = m0_ref[...]

    def one(bb, carry):
        _mlstm_chunk(bb, q_ref, k_ref, v_ref, o_ref, sc_ref, sr_ref, nw_ref, h_ref, c_ref, n_ref, m_ref, L)
        return carry

    if Bb == 1:
        one(0, 0)
    else:
        lax.fori_loop(0, Bb, one, 0)


def mlstm(q, k, v, o, sc, sr, c0, n0, m0, norm_w, L, Bb):
    B, H, T, _ = q.shape
    nc = T // L
    hv = H * MI_DV
    bmap = lambda b, c: (b, 0, 0, 0)
    return pl.pallas_call(
        functools.partial(_mlstm_body, L=L, Bb=Bb),
        grid=(B // Bb, nc),
        in_specs=[pl.BlockSpec((Bb, H, L, MI_DQK), lambda b, c: (b, 0, c, 0)),
                  pl.BlockSpec((Bb, H, L, MI_DQK), lambda b, c: (b, 0, c, 0)),
                  pl.BlockSpec((Bb, L, hv), lambda b, c: (b, c, 0)),
                  pl.BlockSpec((Bb, L, hv), lambda b, c: (b, c, 0)),
                  pl.BlockSpec((Bb, L, LANE), lambda b, c: (b, c, 0)),
                  pl.BlockSpec((Bb, 1, 8, L), lambda b, c: (b, c, 0, 0)),
                  pl.BlockSpec((Bb, H, MI_DQK, MI_DV), bmap),
                  pl.BlockSpec((Bb, H, 1, MI_DQK), bmap),
                  pl.BlockSpec((Bb, 1, H), lambda b, c: (b, 0, 0)),
                  pl.BlockSpec((H, MI_DV), lambda b, c: (0, 0))],
        out_specs=[pl.BlockSpec((Bb, L, hv), lambda b, c: (b, c, 0)),
                   pl.BlockSpec((Bb, H, MI_DQK, MI_DV), bmap),
                   pl.BlockSpec((Bb, H, 1, MI_DQK), bmap),
                   pl.BlockSpec((Bb, 1, H), lambda b, c: (b, 0, 0))],
        out_shape=[jax.ShapeDtypeStruct((B, T, hv), F32),
                   jax.ShapeDtypeStruct((B, H, MI_DQK, MI_DV), F32),
                   jax.ShapeDtypeStruct((B, H, 1, MI_DQK), F32),
                   jax.ShapeDtypeStruct((B, 1, H), F32)],
        compiler_params=_cparams(("parallel", "arbitrary")),
        name="mlstm",
    )(q, k, v, o, sc, sr, c0, n0, m0, norm_w)


def _gdn_chunk(bb, x_ref, z_ref, sc_ref, sr_ref, conv0_ref, cw_ref, nw_ref,
               o_ref, s_ref, conv_ref, xp_scr, L, first_chunk):
    base = 8 - (GD_CONV - 1)
    @pl.when(first_chunk)
    def _():
        xp_scr[base:8, :] = conv0_ref[bb]
    xp_scr[8:8 + L, :] = x_ref[bb]
    y = xp_scr[base:base + L, :] * cw_ref[0:1, :]
    for j in range(1, GD_CONV):
        y = y + xp_scr[base + j:base + j + L, :] * cw_ref[j:j + 1, :]
    tail = xp_scr[L + base:L + 8, :]
    xp_scr[base:8, :] = tail
    conv_ref[bb] = tail
    y = _silu(y)

    rows, cols = _tri(L)
    incl = rows >= cols
    strict = rows > cols
    eye = (rows == cols).astype(F32)
    tril = incl.astype(F32)
    triu = (rows <= cols).astype(F32)
    sc = sc_ref[bb]
    sr = sr_ref[bb, 0]
    gc_col = _dot(tril, sc[:, 12:16], HI)
    gc_row = _dot(sr[0:4, :], triu, HI)
    kw = GD_H * GD_DK
    for h in range(GD_H):
        qh = y[:, h * GD_DK:(h + 1) * GD_DK]
        kh = y[:, kw + h * GD_DK:kw + (h + 1) * GD_DK]
        vh = y[:, 2 * kw + h * GD_DV:2 * kw + (h + 1) * GD_DV]
        qh = qh * lax.rsqrt(jnp.sum(qh * qh, axis=-1, keepdims=True) + EPS) * (GD_DK ** -0.5)
        kh = kh * lax.rsqrt(jnp.sum(kh * kh, axis=-1, keepdims=True) + EPS)
        beta = sc[:, 8 + h:9 + h]
        gcc = gc_col[:, h:h + 1]
        gcr = gc_row[h:h + 1, :]
        dec = jnp.exp(jnp.where(incl, gcc - gcr, NEG_BIG))
        kb = kh * beta
        neg_a = -(_dot_nt(kb, kh, HI) * jnp.where(strict, dec, 0.0))
        tinv = eye + neg_a
        pw = neg_a
        for _ in range(int(math.log2(L)) - 1):
            pw = _dot(pw, pw, HI)
            tinv = tinv + _dot(tinv, pw, HI)
        egc = jnp.exp(gcc)
        u = _dot(tinv, vh * beta, HI)
        w = _dot(tinv, kb * egc, HI)
        attn = _dot_nt(qh, kh, HI) * dec
        g_last = gcc[L - 1:L, :]
        s_prev = s_ref[bb, h]
        v_new = u - _dot(w, s_prev, HI)
        o = _dot(qh * egc, s_prev, HI) + _dot(attn, v_new, HI)
        s_ref[bb, h] = jnp.exp(g_last) * s_prev + _dot_tn(kh * jnp.exp(g_last - gcc), v_new, HI)
        on = o * lax.rsqrt(jnp.mean(o * o, axis=-1, keepdims=True) + EPS) * nw_ref[...]
        o_ref[bb, :, h * GD_DV:(h + 1) * GD_DV] = on * _silu(z_ref[bb, :, h * GD_DV:(h + 1) * GD_DV])


def _gdn_body(x_ref, z_ref, sc_ref, sr_ref, s0_ref, conv0_ref, cw_ref, nw_ref,
              o_ref, s_ref, conv_ref, xp_scr, *, L, Bb):
    first = pl.program_id(1) == 0

    @pl.when(first)
    def _():
        s_ref[...] = s0_ref[...]

    def one(bb, carry):
        _gdn_chunk(bb, x_ref, z_ref, sc_ref, sr_ref, conv0_ref, cw_ref, nw_ref,
                   o_ref, s_ref, conv_ref, xp_scr, L, first)
        return carry

    if Bb == 1:
        one(0, 0)
    else:
        lax.fori_loop(0, Bb, one, 0)


def gdn(x, z, sc, sr, s0, conv0, conv_w, norm_w, L, Bb):
    B, T, ch = x.shape
    H = GD_H
    nc = T // L
    hv = H * GD_DV
    assert Bb == 1 or nc == 1
    bmap = lambda b, c: (b, 0, 0, 0)
    return pl.pallas_call(
        functools.partial(_gdn_body, L=L, Bb=Bb),
        grid=(B // Bb, nc),
        in_specs=[pl.BlockSpec((Bb, L, ch), lambda b, c: (b, c, 0)),
                  pl.BlockSpec((Bb, L, hv), lambda b, c: (b, c, 0)),
                  pl.BlockSpec((Bb, L, LANE), lambda b, c: (b, c, 0)),
                  pl.BlockSpec((Bb, 1, 8, L), lambda b, c: (b, c, 0, 0)),
                  pl.BlockSpec((Bb, H, GD_DK, GD_DV), bmap),
                  pl.BlockSpec((Bb, GD_CONV - 1, ch), lambda b, c: (b, 0, 0)),
                  pl.BlockSpec((GD_CONV, ch), lambda b, c: (0, 0)),
                  pl.BlockSpec((1, GD_DV), lambda b, c: (0, 0))],
        out_specs=[pl.BlockSpec((Bb, L, hv), lambda b, c: (b, c, 0)),
                   pl.BlockSpec((Bb, H, GD_DK, GD_DV), bmap),
                   pl.BlockSpec((Bb, GD_CONV - 1, ch), lambda b, c: (b, 0, 0))],
        out_shape=[jax.ShapeDtypeStruct((B, T, hv), F32),
                   jax.ShapeDtypeStruct((B, H, GD_DK, GD_DV), F32),
                   jax.ShapeDtypeStruct((B, GD_CONV - 1, ch), F32)],
        scratch_shapes=[pltpu.VMEM((L + 8, ch), F32)],
        compiler_params=_cparams(("parallel", "arbitrary")),
        name="gdn",
    )(x, z, sc, sr, s0, conv0, conv_w, norm_w.reshape(1, GD_DV))


def _compress_body(x_ref, pos_ref, w1_ref, w2_ref, o_ref):
    x = x_ref[0, 0].astype(F32)
    R = x.shape[0]
    ua = _dot((x + pos_ref[0, 0:1, :]).astype(BF16), w1_ref[0, 0])
    ub = _dot((x + pos_ref[0, 1:2, :]).astype(BF16), w1_ref[0, 1])
    h = _silu(ua + pltpu.roll(ub, R - 1, 0))
    o_ref[0, 0, 0:R, :] = _dot(h.astype(BF16), w2_ref[0])
    rp = o_ref.shape[2]
    if rp > R:
        o_ref[0, 0, R:rp, :] = jnp.zeros((rp - R, HD), F32)


def nsa_compress(xr, R, pos, w1, w2):
    B = xr.shape[0]
    half = CMP_STRIDE * HD
    rp = -(-R // LANE) * LANE
    return pl.pallas_call(
        _compress_body,
        grid=(B, 4),
        in_specs=[pl.BlockSpec((1, 1, R, half), lambda b, c: (b, c, 0, 0)),
                  pl.BlockSpec((1, 2, half), lambda b, c: (c // 2, 0, 0)),
                  pl.BlockSpec((1, 2, half, CMP_HIDDEN), lambda b, c: (c // 2, 0, 0, 0)),
                  pl.BlockSpec((1, CMP_HIDDEN, HD), lambda b, c: (c // 2, 0, 0))],
        out_specs=pl.BlockSpec((1, 1, rp, HD), lambda b, c: (b, c, 0, 0)),
        out_shape=jax.ShapeDtypeStruct((B, 4, rp, HD), F32),
        compiler_params=_cparams(("parallel", "parallel")),
        name="nsa_compress",
    )(xr, pos.reshape(2, 2, half), w1.reshape(2, 2, half, CMP_HIDDEN).astype(BF16), w2.astype(BF16))


def _cumsum_body(x_ref, o_ref):
    Bb, H, T = x_ref.shape
    rows, cols = _tri(LANE)
    triu = (rows <= cols).astype(F32)
    carry = jnp.zeros((Bb * H, 1), F32)
    for c in range(T // LANE):
        seg = x_ref[:, :, c * LANE:(c + 1) * LANE].reshape(Bb * H, LANE)
        loc = _dot(seg, triu, HI) + carry
        o_ref[:, :, c * LANE:(c + 1) * LANE] = loc.reshape(Bb, H, LANE)
        carry = loc[:, LANE - 1:LANE]


def cumsum_lanes(x, Bb):
    B, H, T = x.shape
    return pl.pallas_call(
        _cumsum_body,
        grid=(B // Bb,),
        in_specs=[pl.BlockSpec((Bb, H, T), lambda b: (b, 0, 0))],
        out_specs=pl.BlockSpec((Bb, H, T), lambda b: (b, 0, 0)),
        out_shape=jax.ShapeDtypeStruct((B, H, T), F32),
        compiler_params=_cparams(("parallel",)),
        name="cumsum",
    )(x)


def _fox_body(q_ref, k_ref, v_ref, fq_ref, fk_ref, o_ref, m_scr, l_scr, acc_scr, *, tq, tk):
    qi = pl.program_id(1)
    ki = pl.program_id(2)

    @pl.when(ki == 0)
    def _():
        m_scr[...] = jnp.full_like(m_scr, NEG_BIG)
        l_scr[...] = jnp.zeros_like(l_scr)
        acc_scr[...] = jnp.zeros_like(acc_scr)

    q0 = qi * tq
    k0 = ki * tk

    @pl.when(k0 <= q0 + tq - 1)
    def _():
        mask = (k0 + _iota((tq, tk), 1)) <= (q0 + _iota((tq, tk), 0))
        fq = fq_ref[0]
        fk = fk_ref[0]
        for h in range(FOX_H):
            s = _dot_nt(q_ref[0, h], k_ref[0, h]) * (HD ** -0.5) + fq[:, h:h + 1] - fk[h:h + 1, :]
            s = jnp.where(mask, s, NEG_BIG)
            m_prev = m_scr[h]
            m_new = jnp.maximum(m_prev, jnp.max(s, axis=1, keepdims=True))
            p = jnp.where(mask, jnp.exp(s - m_new), 0.0)
            alpha = jnp.exp(m_prev - m_new)
            l_scr[h] = alpha * l_scr[h] + jnp.sum(p, axis=1, keepdims=True)
            acc_scr[h] = alpha * acc_scr[h] + _dot(p.astype(BF16), v_ref[0, h])
            m_scr[h] = m_new

    @pl.when(ki == pl.num_programs(2) - 1)
    def _():
        for h in range(FOX_H):
            o_ref[0, h] = acc_scr[h] / jnp.maximum(l_scr[h], 1e-30)


def fox_prompt(q, kv, fq, fk, tq, tk):
    B, H, T, _ = q.shape
    last_k = lambda qi: (qi * tq + tq - 1) // tk
    return pl.pallas_call(
        functools.partial(_fox_body, tq=tq, tk=tk),
        grid=(B, T // tq, T // tk),
        in_specs=[pl.BlockSpec((1, H, tq, HD), lambda b, i, j: (b, 0, i, 0)),
                  pl.BlockSpec((1, H, tk, HD), lambda b, i, j: (b, 0, jnp.minimum(j, last_k(i)), 0)),
                  pl.BlockSpec((1, H, tk, HD), lambda b, i, j: (b, 1, jnp.minimum(j, last_k(i)), 0)),
                  pl.BlockSpec((1, tq, H), lambda b, i, j: (b, i, 0)),
                  pl.BlockSpec((1, H, tk), lambda b, i, j: (b, 0, jnp.minimum(j, last_k(i))))],
        out_specs=pl.BlockSpec((1, H, tq, HD), lambda b, i, j: (b, 0, i, 0)),
        out_shape=jax.ShapeDtypeStruct((B, H, T, HD), F32),
        scratch_shapes=[pltpu.VMEM((H, tq, 1), F32), pltpu.VMEM((H, tq, 1), F32), pltpu.VMEM((H, tq, HD), F32)],
        compiler_params=_cparams(("parallel", "parallel", "arbitrary")),
        name="fox_prompt",
    )(q, kv, kv, fq, fk)


def _fox_decode_body(pt_ref, q_ref, newkv_ref, newlf_ref, *refs, n_pages, tn):
    kv_refs = refs[:n_pages]
    lf_refs = refs[n_pages:2 * n_pages]
    o_ref = refs[2 * n_pages]
    hw = FOX_H * HD
    R = FOX_H * tn
    q = q_ref[0]
    qrep = jnp.concatenate([q] * FOX_H, axis=0)
    blockmask = (_iota((R, hw), 0) // tn) == (_iota((R, hw), 1) // HD)
    qbd = jnp.where(blockmask, qrep, 0.0).astype(BF16)
    rows, cols = _tri(PAGE)
    triu = (rows <= cols).astype(F32)
    expand = lambda a: jnp.concatenate([jnp.broadcast_to(a[h:h + 1, :], (tn, a.shape[1])) for h in range(FOX_H)], 0)
    carry_c = jnp.zeros((FOX_H, 1), F32)
    carry_r = jnp.zeros((1, FOX_H), F32)
    s_tiles = []
    for pg in range(n_pages):
        lf = lf_refs[pg][0]
        f_t = _dot_tn(lf, triu, HI) + carry_c
        carry_c = f_t[:, PAGE - 1:PAGE]
        carry_r = carry_r + jnp.sum(lf, axis=0, keepdims=True)
        k = kv_refs[pg][0, :, 0:hw].astype(BF16)
        s_tiles.append(_dot_nt(qbd, k) * (HD ** -0.5) - expand(f_t))
    lfn = newlf_ref[0]
    r8, c8 = _tri(tn)
    fq_c = _dot((r8 >= c8).astype(F32), lfn, HI) + carry_r
    fq_t = _dot_tn(lfn, (r8 <= c8).astype(F32), HI) + carry_c
    fq_rows = jnp.concatenate([fq_c[:, h:h + 1] for h in range(FOX_H)], axis=0)
    kn = newkv_ref[0, :, 0:hw]
    s_new = _dot_nt(qbd.astype(F32), kn, HI) * (HD ** -0.5) - expand(fq_t)
    causal = _iota((R, tn), 1) <= (_iota((R, tn), 0) % tn)
    s_new = jnp.where(causal, s_new + fq_rows, NEG_BIG)
    s_tiles = [s + fq_rows for s in s_tiles]
    m = jnp.max(s_new, axis=1, keepdims=True)
    for s in s_tiles:
        m = jnp.maximum(m, jnp.max(s, axis=1, keepdims=True))
    p_new = jnp.where(causal, jnp.exp(s_new - m), 0.0)
    l = jnp.sum(p_new, axis=1, keepdims=True)
    acc = _dot(p_new, newkv_ref[0, :, hw:2 * hw], HI)
    for pg, s in enumerate(s_tiles):
        p = jnp.exp(s - m)
        l = l + jnp.sum(p, axis=1, keepdims=True)
        acc = acc + _dot(p.astype(BF16), kv_refs[pg][0, :, hw:2 * hw].astype(BF16))
    acc = acc / jnp.maximum(l, 1e-30)
    o_ref[0] = jnp.concatenate([acc[h * tn:(h + 1) * tn, h * HD:(h + 1) * HD] for h in range(FOX_H)], axis=1)


def fox_decode(page_table, q, newkv, newlf, kv_pool, lf_pool):
    B, tn, hw = q.shape
    n_pages = page_table.shape[1]
    page_spec = lambda width, pg: pl.BlockSpec((1, PAGE, width), lambda b, pt: (pt[b, pg], 0, 0))
    grid_spec = pltpu.PrefetchScalarGridSpec(
        num_scalar_prefetch=1,
        grid=(B,),
        in_specs=[pl.BlockSpec((1, tn, hw), lambda b, pt: (b, 0, 0)),
                  pl.BlockSpec((1, tn, 2 * hw), lambda b, pt: (b, 0, 0)),
                  pl.BlockSpec((1, tn, FOX_H), lambda b, pt: (b, 0, 0))]
                 + [page_spec(2 * hw, pg) for pg in range(n_pages)]
                 + [page_spec(FOX_H, pg) for pg in range(n_pages)],
        out_specs=pl.BlockSpec((1, tn, hw), lambda b, pt: (b, 0, 0)),
    )
    return pl.pallas_call(
        functools.partial(_fox_decode_body, n_pages=n_pages, tn=tn),
        grid_spec=grid_spec,
        out_shape=jax.ShapeDtypeStruct((B, tn, hw), F32),
        compiler_params=_cparams(("arbitrary",)),
        name="fox_decode",
    )(page_table, q, newkv, newlf, *([kv_pool] * n_pages), *([lf_pool] * n_pages))


def _t5_bucket(dist):
    n = jnp.maximum(dist, 0)
    nf = jnp.maximum(n, 1).astype(F32)
    large = BUCKET_EXACT + (jnp.log(nf / BUCKET_EXACT) / math.log(MAX_DISTANCE / BUCKET_EXACT)
                            * (N_BUCKETS - BUCKET_EXACT)).astype(jnp.int32)
    return jnp.where(n < BUCKET_EXACT, n, jnp.minimum(large, N_BUCKETS - 1))


def _bias_from_bucket(bucket, tbl_ref, head):
    out = jnp.zeros(bucket.shape, F32)
    for kk in range(N_BUCKETS):
        out = jnp.where(bucket == kk, tbl_ref[kk, head], out)
    return out


def _flash_update(s, mask, v, m_scr, l_scr, acc_scr):
    m_prev = m_scr[...]
    m_new = jnp.maximum(m_prev, jnp.max(s, axis=1, keepdims=True))
    p = jnp.where(mask, jnp.exp(s - m_new), 0.0)
    alpha = jnp.exp(m_prev - m_new)
    l_scr[...] = alpha * l_scr[...] + jnp.sum(p, axis=1, keepdims=True)
    acc_scr[...] = alpha * acc_scr[...] + _dot(p.astype(BF16), v)
    m_scr[...] = m_new


def _nsa_body(tbl_ref, q_ref, gate_ref, kc_ref, vc_ref, ks_ref, vs_ref, kw_ref, vw_ref, smap_ref, o_ref,
              bias_scr, score_scr, m_scr, l_scr, acc_scr, *, tq, q_pos0, win_pos0, n_sel, ncp, tw):
    g = pl.program_id(1)
    qi = pl.program_id(2)
    q0 = q_pos0 + qi * tq
    R = NSA_R * tq
    scale = HD ** -0.5
    last_bias = tuple(tbl_ref[N_BUCKETS - 1, g * NSA_R + r] for r in range(NSA_R))

    @pl.when(qi == 0)
    def _():
        ii = _iota((tq, LANE), 0)
        jj = _iota((tq, LANE), 1)
        for dd in range(2):
            bucket = _t5_bucket(ii - jj + dd * LANE)
            for r in range(NSA_R):
                bias_scr[dd, r * tq:(r + 1) * tq, :] = _bias_from_bucket(bucket, tbl_ref, g * NSA_R + r)
        for r in range(NSA_R):
            bias_scr[2, r * tq:(r + 1) * tq, :] = jnp.full((tq, LANE), last_bias[r], F32)

    t_col = q0 + _iota((tq, 1), 0)

    bias_tiles, mask_tiles = [], []
    for nt in range(ncp // LANE):
        c_end = (nt * LANE + _iota((tq, LANE), 1)) * CMP_STRIDE + (CMP_BLOCK - 1)
        dist = t_col - c_end
        max_dist = q0 + tq - 1 - (nt * LANE * CMP_STRIDE + CMP_BLOCK - 1)
        min_dist = q0 - ((nt * LANE + LANE - 1) * CMP_STRIDE + CMP_BLOCK - 1)
        special = jnp.logical_and(max_dist >= 0, min_dist < BUCKET_SAT_DIST)

        def general(dist=dist):
            bucket = _t5_bucket(dist)
            return jnp.stack([_bias_from_bucket(bucket, tbl_ref, g * NSA_R + r) for r in range(NSA_R)])

        def saturated():
            return jnp.stack([jnp.full((tq, LANE), last_bias[r], F32) for r in range(NSA_R)])

        bias_tiles.append(lax.cond(special, general, saturated))
        mask_tiles.append(dist >= 0)
    mask_c = jnp.concatenate(mask_tiles, axis=1) if len(mask_tiles) > 1 else mask_tiles[0]
    kc = kc_ref[0, 0].astype(BF16)
    vc = vc_ref[0, 0].astype(BF16)
    pcsum = jnp.zeros((tq, ncp), F32)
    o_c = []
    for r in range(NSA_R):
        bias_r = jnp.concatenate([b[r] for b in bias_tiles], axis=1) if len(bias_tiles) > 1 else bias_tiles[0][r]
        s = _dot_nt(q_ref[0, r].astype(BF16), kc) * scale + bias_r
        s = jnp.where(mask_c, s, NEG_BIG)
        m = jnp.max(s, axis=1, keepdims=True)
        p = jnp.where(mask_c, jnp.exp(s - m), 0.0)
        pc = p / jnp.maximum(jnp.sum(p, axis=1, keepdims=True), 1e-30)
        o_c.append(_dot(pc.astype(BF16), vc))
        pcsum = pcsum + pc

    ps_t = _dot_nt(smap_ref[...], pcsum, HI)
    j_col = _iota((LANE, 1), 0)
    t_row = q0 + _iota((1, tq), 1)
    cur = lax.shift_right_logical(t_row, int(math.log2(SEL_BLOCK)))
    score = jnp.where(j_col * SEL_BLOCK <= t_row, ps_t, -1.0)
    score = jnp.where(j_col == cur - 1, FORCE_SCORE, score)
    score = jnp.where(j_col == cur, FORCE_SCORE, score)
    score = jnp.where(j_col == 0, FORCE_SCORE, score)
    score = jnp.where(j_col < n_sel, score, -3e38)
    score_scr[...] = score

    def rank_body(jp, rank):
        row = score_scr[pl.ds(jp, 1), :]
        tie = jnp.where(j_col > jp, 1.0, 0.0)
        return rank + jnp.where(row > score, 1.0, jnp.where(row == score, tie, 0.0))

    rank = lax.fori_loop(0, LANE, rank_body, jnp.zeros((LANE, tq), F32))
    sel_t = jnp.where(rank < SEL_TOPN, 1.0, 0.0).astype(BF16)
    eye = (_iota((tq, tq), 0) == _iota((tq, tq), 1)).astype(BF16)
    sel = _dot_nt(eye, sel_t).astype(BF16)

    qs = q_ref[0].reshape(R, HD).astype(BF16)
    t_tile = q0 + _iota((tq, LANE), 0)
    c_tile = _iota((tq, LANE), 1)

    def reset():
        m_scr[...] = jnp.full_like(m_scr, NEG_BIG)
        l_scr[...] = jnp.zeros_like(l_scr)
        acc_scr[...] = jnp.zeros_like(acc_scr)

    def stack(a):
        return jnp.concatenate([a] * NSA_R, axis=0)

    reset()

    def sel_body(kt, carry):
        k0 = pl.multiple_of(kt * LANE, LANE)
        k = ks_ref[0, 0, pl.ds(k0, LANE), :]
        v = vs_ref[0, 0, pl.ds(k0, LANE), :]
        blk = kt * (LANE // SEL_BLOCK) + lax.shift_right_logical(_iota((LANE, LANE), 1), int(math.log2(SEL_BLOCK)))
        expand = jnp.where(_iota((LANE, LANE), 0) == blk, 1.0, 0.0).astype(BF16)
        chosen = _dot(sel, expand)
        dist = t_tile - (k0 + c_tile)
        mask = stack(jnp.where(dist >= 0, chosen, 0.0)) > 0.5
        bidx = jnp.minimum(lax.shift_right_logical(q0 - k0, 7), 2)
        s = _dot_nt(qs, k) * scale + bias_scr[bidx]
        _flash_update(jnp.where(mask, s, NEG_BIG), mask, v, m_scr, l_scr, acc_scr)
        return carry

    lax.fori_loop(0, (q0 + tq + LANE - 1) // LANE, sel_body, 0)
    o_s = acc_scr[...] / jnp.maximum(l_scr[...], 1e-30)

    reset()

    def win_body(kt, carry):
        k0 = pl.multiple_of(kt * LANE, LANE)
        k = kw_ref[0, 0, pl.ds(k0, LANE), :]
        v = vw_ref[0, 0, pl.ds(k0, LANE), :]
        dist = t_tile - (win_pos0 + k0 + c_tile)
        mask = stack(jnp.where(dist >= 0, jnp.where(dist < WINDOW, 1.0, 0.0), 0.0)) > 0.5
        bidx = jnp.minimum(lax.shift_right_logical(q0 - win_pos0 - k0, 7), 2)
        s = _dot_nt(qs, k) * scale + bias_scr[bidx]
        _flash_update(jnp.where(mask, s, NEG_BIG), mask, v, m_scr, l_scr, acc_scr)
        return carry

    kt_lo = jnp.maximum(q0 - (WINDOW - 1) - win_pos0, 0) // LANE
    kt_hi = (jnp.minimum(q0 + tq - win_pos0, tw) + LANE - 1) // LANE
    lax.fori_loop(kt_lo, kt_hi, win_body, 0)
    o_w = acc_scr[...] / jnp.maximum(l_scr[...], 1e-30)

    gates = gate_ref[0, 0]
    for r in range(NSA_R):
        o_ref[0, r] = (gates[:, 3 * r:3 * r + 1] * o_c[r]
                       + gates[:, 3 * r + 1:3 * r + 2] * o_s[r * tq:(r + 1) * tq]
                       + gates[:, 3 * r + 2:3 * r + 3] * o_w[r * tq:(r + 1) * tq])


def _selection_overlap_t(ncp):
    c_start = np.arange(ncp)[None, :] * CMP_STRIDE
    s_start = np.arange(LANE)[:, None] * SEL_BLOCK
    return ((c_start < s_start + SEL_BLOCK) & (c_start + CMP_BLOCK > s_start)).astype(np.float32)


def nsa_attend(tbl, q, gates, kcvc, sel_arr, sel_off, win_arr, win_off, *, tq, q_pos0, win_pos0, n_sel):
    B, _, Tq, _ = q.shape
    ncp = kcvc.shape[2]
    tks = sel_arr.shape[2]
    tw = win_arr.shape[2]
    R = NSA_R * tq
    smap = jnp.asarray(_selection_overlap_t(ncp))
    kv_spec = lambda rows, off: pl.BlockSpec((1, 1, rows, HD), lambda b, g, i: (b, off + g, 0, 0))
    return pl.pallas_call(
        functools.partial(_nsa_body, tq=tq, q_pos0=q_pos0, win_pos0=win_pos0, n_sel=n_sel, ncp=ncp, tw=tw),
        grid=(B, NSA_G, Tq // tq),
        in_specs=[pl.BlockSpec(memory_space=pltpu.SMEM),
                  pl.BlockSpec((1, NSA_R, tq, HD), lambda b, g, i: (b, g, i, 0)),
                  pl.BlockSpec((1, 1, tq, 3 * NSA_R), lambda b, g, i: (b, g, i, 0)),
                  kv_spec(ncp, 0), kv_spec(ncp, 2),
                  kv_spec(tks, sel_off), kv_spec(tks, sel_off + 2),
                  kv_spec(tw, win_off), kv_spec(tw, win_off + 2),
                  pl.BlockSpec((LANE, ncp), lambda b, g, i: (0, 0))],
        out_specs=pl.BlockSpec((1, NSA_R, tq, HD), lambda b, g, i: (b, g, i, 0)),
        out_shape=jax.ShapeDtypeStruct((B, NSA_G * NSA_R, Tq, HD), F32),
        scratch_shapes=[pltpu.VMEM((3, R, LANE), F32), pltpu.VMEM((LANE, tq), F32),
                        pltpu.VMEM((R, 1), F32), pltpu.VMEM((R, 1), F32), pltpu.VMEM((R, HD), F32)],
        compiler_params=_cparams(("parallel", "parallel", "arbitrary")),
        name="nsa_attend",
    )(tbl, q, gates, kcvc, kcvc, sel_arr, sel_arr, win_arr, win_arr, smap)


def _nsa_gather_body(pt_ref, new_ref, *refs, n_pages):
    pages = refs[:n_pages]
    cmp_ref, sel_ref = refs[n_pages:]
    ngrp = 2 * NSA_G
    for pg in range(n_pages):
        for j in range(ngrp):
            cmp_ref[0, j, pg * PAGE:(pg + 1) * PAGE, :] = pages[pg][0, :, j * HD:(j + 1) * HD]
            sel_ref[0, j, pg * PAGE:(pg + 1) * PAGE, :] = (
                pages[pg][0, :, (ngrp + j) * HD:(ngrp + j + 1) * HD].astype(BF16))
    tn = new_ref.shape[1]
    for j in range(ngrp):
        new = new_ref[0, :, (ngrp + j) * HD:(ngrp + j + 1) * HD]
        tile = jnp.concatenate([new, jnp.zeros((PAGE - tn, HD), F32)], axis=0)
        sel_ref[0, j, n_pages * PAGE:(n_pages + 1) * PAGE, :] = tile.astype(BF16)


def nsa_gather(page_table, new_rows, pool):
    B, tn, width = new_rows.shape
    n_pages = page_table.shape[1]
    ngrp = 2 * NSA_G
    grid_spec = pltpu.PrefetchScalarGridSpec(
        num_scalar_prefetch=1,
        grid=(B,),
        in_specs=[pl.BlockSpec((1, tn, width), lambda b, pt: (b, 0, 0))]
                 + [pl.BlockSpec((1, PAGE, width), functools.partial(lambda b, pt, pg: (pt[b, pg], 0, 0), pg=pg))
                    for pg in range(n_pages)],
        out_specs=[pl.BlockSpec((1, ngrp, n_pages * PAGE, HD), lambda b, pt: (b, 0, 0, 0)),
                   pl.BlockSpec((1, ngrp, (n_pages + 1) * PAGE, HD), lambda b, pt: (b, 0, 0, 0))],
    )
    return pl.pallas_call(
        functools.partial(_nsa_gather_body, n_pages=n_pages),
        grid_spec=grid_spec,
        out_shape=[jax.ShapeDtypeStruct((B, ngrp, n_pages * PAGE, HD), F32),
                   jax.ShapeDtypeStruct((B, ngrp, (n_pages + 1) * PAGE, HD), BF16)],
        compiler_params=_cparams(("arbitrary",)),
        name="nsa_gather",
    )(page_table, new_rows, *([pool] * n_pages))


def _row_tile(m):
    return 512 if m % 512 == 0 else m


def _small_params(entries):
    sp = jnp.zeros((8, LANE), F32)
    for off, bias, act, log_scale in entries:
        n = bias.shape[0]
        sp = sp.at[0, off:off + n].set(bias.astype(F32))
        sp = sp.at[1, off:off + n].set(act)
        if log_scale is not None:
            sp = sp.at[2, off:off + n].set(log_scale.astype(F32))
    return sp


EVEN_WIDTHS = (256, 256, 512, 512, GD_CH, 512, LANE)


def _even_weights(w_in):
    s = np.cumsum((0, 256, 256, 512, 512, 4, 4, 512, 512, 512, 512, 4, 4))
    col = lambda i: w_in[:, s[i]:s[i + 1]]
    small = jnp.concatenate([col(4), col(5), col(10), col(11)], axis=1)
    small = jnp.pad(small, ((0, 0), (0, LANE - small.shape[1])))
    return jnp.concatenate([col(0), col(1), col(2), col(3), col(6), col(7), col(8), col(9), small], axis=1).astype(BF16)


def _chunk_rows(small, B, T, L, lanes):
    r = small.reshape(B, T // L, L, LANE)[..., lanes[0]:lanes[1]]
    r = jnp.swapaxes(r, 2, 3)
    return jnp.pad(r, ((0, 0), (0, 0), (0, 8 - r.shape[2]), (0, 0)))


def even_layer(x, p, past, L, Bb):
    B, T, D = x.shape
    M = B * T
    tm = _row_tile(M)
    sp = _small_params([(0, p['mi_b_i'], ACT_ID, None), (4, p['mi_b_f'], ACT_LOGSIG, None),
                        (8, jnp.zeros((4,), F32), ACT_SIG, None), (12, p['gd_dt_bias'], ACT_DECAY, p['gd_a_log'])])
    mq, mk, mv, mo, gx, gz, small = norm_proj(x.reshape(M, D), p['norm_mix'], _even_weights(p['w_in']), sp,
                                              EVEN_WIDTHS, tm)
    heads = lambda a: jnp.transpose(a.reshape(B, T, MI_H, MI_DQK), (0, 2, 1, 3))
    sc = small.reshape(B, T, LANE)
    if past is None:
        c0 = jnp.zeros((B, MI_H, MI_DQK, MI_DV), F32)
        n0 = jnp.zeros((B, MI_H, 1, MI_DQK), F32)
        m0 = jnp.zeros((B, 1, MI_H), F32)
        s0 = jnp.zeros((B, GD_H, GD_DK, GD_DV), F32)
        conv0 = jnp.zeros((B, GD_CONV - 1, GD_CH), F32)
    else:
        c0, n0, m0, s0, conv0 = past
        n0 = n0.reshape(B, MI_H, 1, MI_DQK)
        m0 = m0.reshape(B, 1, MI_H)
    hm, c1, n1, m1 = mlstm(heads(mq), heads(mk), mv.reshape(B, T, -1), mo.reshape(B, T, -1), sc,
                           _chunk_rows(small, B, T, L, (0, 8)), c0, n0, m0,
                           p['mi_norm'].reshape(MI_H, MI_DV), L, Bb)
    og, s1, conv1 = gdn(gx.reshape(B, T, GD_CH), gz.reshape(B, T, -1), sc,
                        _chunk_rows(small, B, T, L, (12, 16)), s0, conv0, p['gd_conv_w'], p['gd_norm'], L, Bb)
    y = out_proj_residual(x.reshape(M, D), hm.reshape(M, -1), og.reshape(M, -1), p['w_out'].astype(BF16), tm)
    return y.reshape(B, T, D), (c1, n1.reshape(B, MI_H, MI_DQK), m1.reshape(B, MI_H), s1, conv1)


NSA_QW = NSA_G * NSA_R * HD
NSA_KVW = 6 * NSA_G * HD
NSA_CACHE_W = 4 * NSA_G * HD
FOX_W = FOX_H * HD
N_GATE = 3 * NSA_G * NSA_R
ODD_WIDTHS = (NSA_QW, NSA_KVW, FOX_W, 2 * FOX_W, LANE)


def _odd_weights(w_in):
    s = np.cumsum((0, NSA_QW, NSA_KVW, N_GATE, FOX_W, FOX_W, FOX_W, FOX_H))
    col = lambda i: w_in[:, s[i]:s[i + 1]]
    small = jnp.concatenate([col(2), col(6)], axis=1)
    small = jnp.pad(small, ((0, 0), (0, LANE - small.shape[1])))
    return jnp.concatenate([col(0), col(1), col(3), col(4), col(5), small], axis=1).astype(BF16)


def _heads(a, B, T, n):
    return jnp.transpose(a.reshape(B, T, n, HD), (0, 2, 1, 3))


def _unheads(a):
    B, n, T, _ = a.shape
    return jnp.transpose(a, (0, 2, 1, 3)).reshape(B * T, n * HD)


def odd_layer(x, p, rel_bias, w_buf, past, page_table):
    B, T, D = x.shape
    M = B * T
    tm = _row_tile(M)
    sp = _small_params([(0, jnp.zeros((N_GATE,), F32), ACT_SIG, None), (N_GATE, p['fox_b_f'], ACT_LOGSIG, None)])
    nq, nkv, fq, fkv, small = norm_proj(x.reshape(M, D), p['norm_mix'], _odd_weights(p['w_in']), sp, ODD_WIDTHS, tm)
    new_nsa = nkv[:, :NSA_CACHE_W].reshape(B, T, 4, NSA_G, HD)
    new_win = nkv[:, NSA_CACHE_W:].reshape(B, T, 2, NSA_G, HD)
    new_fox = fkv.reshape(B, T, 2, FOX_H, HD)
    logf = small[:, N_GATE:N_GATE + FOX_H].reshape(B, T, FOX_H)
    q_heads = _heads(nq, B, T, NSA_G * NSA_R)
    gates = jnp.transpose(small[:, :N_GATE].reshape(B, T, NSA_G, 3 * NSA_R), (0, 2, 1, 3))
    cmp_args = (p['nsa_cmp_pos'], p['nsa_cmp_w1'], p['nsa_cmp_w2'])
    rows16 = CMP_STRIDE * HD
    if past is None:
        groups = _heads(nkv, B, T, 6 * NSA_G)
        kcvc = nsa_compress(groups.reshape(B, 6 * NSA_G, T // CMP_STRIDE, rows16), T // CMP_STRIDE, *cmp_args)
        arr = groups.astype(BF16)
        o_n = nsa_attend(rel_bias, q_heads, gates, kcvc, arr, 2 * NSA_G, arr, 4 * NSA_G,
                         tq=min(T, LANE), q_pos0=0, win_pos0=0, n_sel=-(-T // SEL_BLOCK))
        f_t = cumsum_lanes(jnp.transpose(logf, (0, 2, 1)), B)
        tile = min(T, 256)
        o_f = fox_prompt(_heads(fq, B, T, FOX_H).astype(BF16), _heads(fkv, B, T, 2 * FOX_H).astype(BF16),
                         jnp.transpose(f_t, (0, 2, 1)), f_t, tile, tile)
        o_f = _unheads(o_f)
        win_prev = jnp.zeros((B, WINDOW, 2, NSA_G, HD), F32)
    else:
        nsa_pool, win_prev, fox_pool, logf_pool = past
        n_pool = nsa_pool.shape[0]
        n_pages = page_table.shape[1]
        start = n_pages * PAGE
        cmp_rows, sel_arr = nsa_gather(page_table, nkv[:, :NSA_CACHE_W].reshape(B, T, NSA_CACHE_W),
                                       nsa_pool.reshape(n_pool, PAGE, NSA_CACHE_W))
        kcvc = nsa_compress(cmp_rows.reshape(B, 2 * NSA_G, start // CMP_STRIDE, rows16), start // CMP_STRIDE, *cmp_args)
        wp = win_prev.shape[1]
        win_all = jnp.concatenate([win_prev.reshape(B, wp, 2 * NSA_G * HD), nkv[:, NSA_CACHE_W:].reshape(B, T, -1)], 1)
        tw = -(-(wp + T) // LANE) * LANE
        win_arr = _heads(jnp.pad(win_all, ((0, 0), (0, tw - wp - T), (0, 0))), B, tw, 2 * NSA_G).astype(BF16)
        o_n = nsa_attend(rel_bias, q_heads, gates, kcvc, sel_arr, 0, win_arr, 0,
                         tq=T, q_pos0=start, win_pos0=start - wp, n_sel=-(-(start + T) // SEL_BLOCK))
        o_f = fox_decode(page_table, fq.reshape(B, T, FOX_W), fkv.reshape(B, T, 2 * FOX_W), logf,
                         fox_pool.reshape(n_pool, PAGE, 2 * FOX_W), logf_pool)
        o_f = o_f.reshape(M, FOX_W)
    win_state = jnp.concatenate([win_prev, new_win], axis=1)[:, -w_buf:]
    y = out_proj_residual(x.reshape(M, D), _unheads(o_n), o_f, p['w_out'].astype(BF16), tm)
    return y.reshape(B, T, D), (new_nsa, win_state, new_fox, logf)


def _trunk(x, past, page_table, P, w_buf, L, Bb):
    B, T, D = x.shape
    pe = dict(norm_mix=P['norm_mix'][0], w_in=P['w_in_even'][0], w_out=P['w_out_even'][0], mi_b_i=P['mi_b_i'][0],
              mi_b_f=P['mi_b_f'][0], mi_norm=P['mi_norm'][0], gd_conv_w=P['gd_conv_w'][0], gd_a_log=P['gd_a_log'][0],
              gd_dt_bias=P['gd_dt_bias'][0], gd_norm=P['gd_norm'][0])
    po = dict(norm_mix=P['norm_mix'][1], w_in=P['w_in_odd'][0], w_out=P['w_out_odd'][0],
              nsa_cmp_pos=P['nsa_cmp_pos'][0], nsa_cmp_w1=P['nsa_cmp_w1'][0], nsa_cmp_w2=P['nsa_cmp_w2'][0],
              fox_b_f=P['fox_b_f'][0])
    tm = _row_tile(B * T)
    mlp = lambda x, layer, final: mlp_residual(
        x.reshape(B * T, D), P['norm_mlp'][layer], P['w_up'][layer].astype(BF16), P['w_down'][layer].astype(BF16),
        P['norm_final'], final, tm, 1024).reshape(B, T, D)
    even_past = None if past is None else tuple(past[k][0] for k in ('mc', 'mn', 'mm', 'gs', 'gc'))
    odd_past = None if past is None else tuple(past[k][0] for k in ('nsa_kv', 'nsa_win', 'fox_kv', 'fox_logf'))
    x, st_e = even_layer(x, pe, even_past, L, Bb)
    x = mlp(x, 0, False)
    x, st_o = odd_layer(x, po, P['rel_bias'], w_buf, odd_past, page_table)
    y = mlp(x, 1, True)
    return y, tuple(a[None] for a in st_e + st_o)


def kernel(x_prompt, x_sample, state_mlstm_c, state_mlstm_n, state_mlstm_m, state_gdn_s, state_gdn_conv,
           cache_nsa_kv, state_nsa_win, cache_fox_kv, cache_fox_logf, page_table,
           norm_mix, norm_mlp, norm_final, w_up, w_down,
           w_in_even, w_out_even, mi_b_i, mi_b_f, mi_norm, gd_conv_w, gd_a_log, gd_dt_bias, gd_norm,
           w_in_odd, w_out_odd, nsa_cmp_pos, nsa_cmp_w1, nsa_cmp_w2, fox_b_f, rel_bias):
    P = dict(norm_mix=norm_mix, norm_mlp=norm_mlp, norm_final=norm_final, w_up=w_up, w_down=w_down,
             w_in_even=w_in_even, w_out_even=w_out_even, mi_b_i=mi_b_i, mi_b_f=mi_b_f, mi_norm=mi_norm,
             gd_conv_w=gd_conv_w, gd_a_log=gd_a_log, gd_dt_bias=gd_dt_bias, gd_norm=gd_norm,
             w_in_odd=w_in_odd, w_out_odd=w_out_odd, nsa_cmp_pos=nsa_cmp_pos, nsa_cmp_w1=nsa_cmp_w1,
             nsa_cmp_w2=nsa_cmp_w2, fox_b_f=fox_b_f, rel_bias=rel_bias)
    w_buf = state_nsa_win.shape[2]
    t_p = x_prompt.shape[1]
    y_p, st_p = _trunk(x_prompt, None, None, P, w_buf, math.gcd(t_p, 64), 1)
    past = dict(mc=state_mlstm_c, mn=state_mlstm_n, mm=state_mlstm_m, gs=state_gdn_s, gc=state_gdn_conv,
                nsa_kv=cache_nsa_kv, nsa_win=state_nsa_win, fox_kv=cache_fox_kv, fox_logf=cache_fox_logf)
    b_s, t_s = x_sample.shape[:2]
    y_s, st_s = _trunk(x_sample, past, page_table, P, w_buf, math.gcd(t_s, 64), math.gcd(b_s, 8))
    return (y_p, y_s) + st_p + st_s
```

```python
import functools
import math

import jax
import jax.numpy as jnp
import numpy as np
from jax import lax
from jax.experimental import pallas as pl
from jax.experimental.pallas import tpu as pltpu

F32 = jnp.float32
BF16 = jnp.bfloat16
HI = lax.Precision.HIGHEST

D_MODEL = 1024
D_FF = 4 * D_MODEL
EPS = 1e-6
NEG_BIG = -1e30
PAGE = 128

MI_H, MI_DQK, MI_DV = 4, 64, 128
GD_H, GD_DK, GD_DV, GD_CONV = 4, 128, 128, 4
GD_CH = 3 * GD_H * GD_DK
NSA_G, NSA_R, HD = 2, 4, 64
FOX_H = 8
CMP_BLOCK, CMP_STRIDE, CMP_HIDDEN = 32, 16, 256
SEL_BLOCK, SEL_TOPN, WINDOW = 64, 16, 512
FORCE_SCORE = 1e4
N_BUCKETS, MAX_DISTANCE = 32, 128
BUCKET_EXACT = N_BUCKETS // 2
BUCKET_SAT_DIST = 113
LANE = 128
VMEM_LIMIT = 56 * 1024 * 1024


def _cparams(sem):
    return pltpu.CompilerParams(dimension_semantics=sem, vmem_limit_bytes=VMEM_LIMIT)


def _dot(a, b, precision=None):
    return jnp.dot(a, b, preferred_element_type=F32, precision=precision)


def _dot_nt(a, b, precision=None):
    return lax.dot_general(a, b, (((1,), (1,)), ((), ())), preferred_element_type=F32, precision=precision)


def _dot_tn(a, b, precision=None):
    return lax.dot_general(a, b, (((0,), (0,)), ((), ())), preferred_element_type=F32, precision=precision)


def _softplus(x):
    return jnp.maximum(x, 0.0) + jnp.log1p(jnp.exp(-jnp.abs(x)))


def _sigmoid(x):
    return 1.0 / (1.0 + jnp.exp(-x))


def _silu(x):
    return x * _sigmoid(x)


def _iota(shape, dim):
    return lax.broadcasted_iota(jnp.int32, shape, dim)


ACT_ID, ACT_LOGSIG, ACT_SIG, ACT_DECAY = 0.0, 1.0, 2.0, 3.0


def _proj_body(x_ref, g_ref, w_ref, sp_ref, *out_refs, widths):
    x = x_ref[...]
    hn = (x * lax.rsqrt(jnp.mean(x * x, axis=-1, keepdims=True) + EPS) * g_ref[...]).astype(BF16)
    off = 0
    for i, (o_ref, n) in enumerate(zip(out_refs, widths)):
        r = _dot(hn, w_ref[:, off:off + n])
        if i == len(widths) - 1:
            z = r + sp_ref[0:1, :]
            mode = sp_ref[1:2, :]
            decay = -jnp.exp(sp_ref[2:3, :]) * _softplus(z)
            r = jnp.where(mode == ACT_LOGSIG, -_softplus(-z),
                          jnp.where(mode == ACT_SIG, _sigmoid(z),
                                    jnp.where(mode == ACT_DECAY, decay, z)))
        o_ref[...] = r.astype(o_ref.dtype)
        off += n


def norm_proj(x, g, w_bf16, small_params, widths, tm):
    m, d = x.shape
    n_total = sum(widths)
    assert w_bf16.shape == (d, n_total) and m % tm == 0
    out_shape = [jax.ShapeDtypeStruct((m, n), F32) for n in widths]
    return pl.pallas_call(
        functools.partial(_proj_body, widths=tuple(widths)),
        grid=(m // tm,),
        in_specs=[pl.BlockSpec((tm, d), lambda i: (i, 0)),
                  pl.BlockSpec((1, d), lambda i: (0, 0)),
                  pl.BlockSpec((d, n_total), lambda i: (0, 0)),
                  pl.BlockSpec((8, LANE), lambda i: (0, 0))],
        out_specs=[pl.BlockSpec((tm, n), lambda i: (i, 0)) for n in widths],
        out_shape=out_shape,
        compiler_params=_cparams(("parallel",)),
        name="norm_proj",
    )(x, g.reshape(1, d), w_bf16, small_params)


def _outproj_body(x_ref, a1_ref, a2_ref, w_ref, o_ref):
    k1 = a1_ref.shape[1]
    y = _dot(a1_ref[...].astype(BF16), w_ref[0:k1, :]) + _dot(a2_ref[...].astype(BF16), w_ref[k1:, :])
    o_ref[...] = x_ref[...] + y


def out_proj_residual(x, a1, a2, w_bf16, tm):
    m, d = x.shape
    k1, k2 = a1.shape[1], a2.shape[1]
    return pl.pallas_call(
        _outproj_body,
        grid=(m // tm,),
        in_specs=[pl.BlockSpec((tm, d), lambda i: (i, 0)),
                  pl.BlockSpec((tm, k1), lambda i: (i, 0)),
                  pl.BlockSpec((tm, k2), lambda i: (i, 0)),
                  pl.BlockSpec((k1 + k2, d), lambda i: (0, 0))],
        out_specs=pl.BlockSpec((tm, d), lambda i: (i, 0)),
        out_shape=jax.ShapeDtypeStruct((m, d), F32),
        compiler_params=_cparams(("parallel",)),
        name="out_proj",
    )(x, a1, a2, w_bf16)


def _mlp_body(x_ref, g_ref, wu_ref, wd_ref, gf_ref, o_ref, hn_scr, acc_scr, *, final_norm):
    j = pl.program_id(1)

    @pl.when(j == 0)
    def _():
        x = x_ref[...]
        hn_scr[...] = (x * lax.rsqrt(jnp.mean(x * x, axis=-1, keepdims=True) + EPS) * g_ref[...]).astype(BF16)
        acc_scr[...] = jnp.zeros_like(acc_scr)

    u = jnp.maximum(_dot(hn_scr[...], wu_ref[...]), 0.0)
    acc_scr[...] += _dot((u * u).astype(BF16), wd_ref[...])

    @pl.when(j == pl.num_programs(1) - 1)
    def _():
        y = x_ref[...] + acc_scr[...]
        if final_norm:
            y = y * lax.rsqrt(jnp.mean(y * y, axis=-1, keepdims=True) + EPS) * gf_ref[...]
        o_ref[...] = y


def mlp_residual(x, g, w_up_bf16, w_down_bf16, g_final, final_norm, tm, tf):
    m, d = x.shape
    f = w_up_bf16.shape[1]
    return pl.pallas_call(
        functools.partial(_mlp_body, final_norm=final_norm),
        grid=(m // tm, f // tf),
        in_specs=[pl.BlockSpec((tm, d), lambda i, j: (i, 0)),
                  pl.BlockSpec((1, d), lambda i, j: (0, 0)),
                  pl.BlockSpec((d, tf), lambda i, j: (0, j)),
                  pl.BlockSpec((tf, d), lambda i, j: (j, 0)),
                  pl.BlockSpec((1, d), lambda i, j: (0, 0))],
        out_specs=pl.BlockSpec((tm, d), lambda i, j: (i, 0)),
        out_shape=jax.ShapeDtypeStruct((m, d), F32),
        scratch_shapes=[pltpu.VMEM((tm, d), BF16), pltpu.VMEM((tm, d), F32)],
        compiler_params=_cparams(("parallel", "arbitrary")),
        name="mlp",
    )(x, g.reshape(1, d), w_up_bf16, w_down_bf16, g_final.reshape(1, d))


def _tri(n):
    r = _iota((n, n), 0)
    c = _iota((n, n), 1)
    return r, c


def _mlstm_chunk(bb, q_ref, k_ref, v_ref, o_ref, sc_ref, sr_ref, nw_ref, h_ref, c_ref, n_ref, m_ref, L):
    rows, cols = _tri(L)
    lower = rows >= cols
    tril = lower.astype(F32)
    triu = (rows <= cols).astype(F32)
    sc = sc_ref[bb]
    sr = sr_ref[bb, 0]
    b_col = _dot(tril, sc[:, 4:8], HI)
    b_row = _dot(sr[4:8, :], triu, HI)
    m_vec = m_ref[bb]
    lane4 = _iota((1, MI_H), 1)
    for h in range(MI_H):
        q = q_ref[bb, h]
        k = k_ref[bb, h] * (MI_DQK ** -0.5)
        v = v_ref[bb, :, h * MI_DV:(h + 1) * MI_DV]
        bc = b_col[:, h:h + 1]
        br = b_row[h:h + 1, :]
        li_r = sr[h:h + 1, :]
        li_c = sc[:, h:h + 1]
        m_prev = m_vec[:, h:h + 1]
        c_prev = c_ref[bb, h]
        n_prev = n_ref[bb, h]
        d = jnp.where(lower, bc - br + li_r, NEG_BIG)
        inter = bc + m_prev
        m_t = jnp.maximum(inter, jnp.max(d, axis=1, keepdims=True))
        s = _dot_nt(q, k, HI) * jnp.exp(d - m_t)
        a_inter = jnp.exp(inter - m_t)
        num = _dot(s, v, HI) + a_inter * _dot(q, c_prev, HI)
        den = jnp.sum(s, axis=1, keepdims=True) + a_inter * jnp.sum(q * n_prev, axis=1, keepdims=True)
        hh = num / jnp.maximum(jnp.abs(den), jnp.exp(-m_t))
        b_last = bc[L - 1:L, :]
        g_col = b_last - bc + li_c
        m_new = jnp.maximum(b_last + m_prev, jnp.max(g_col, axis=0, keepdims=True))
        a_prev = jnp.exp(b_last + m_prev - m_new)
        kw = k * jnp.exp(g_col - m_new)
        c_ref[bb, h] = a_prev * c_prev + _dot_tn(kw, v, HI)
        n_ref[bb, h] = a_prev * n_prev + jnp.sum(kw, axis=0, keepdims=True)
        m_vec = jnp.where(lane4 == h, m_new, m_vec)
        hn = hh * lax.rsqrt(jnp.mean(hh * hh, axis=-1, keepdims=True) + EPS) * nw_ref[h:h + 1, :]
        gate = _sigmoid(o_ref[bb, :, h * MI_DV:(h + 1) * MI_DV])
        h_ref[bb, :, h * MI_DV:(h + 1) * MI_DV] = hn * gate
    m_ref[bb] = m_vec


def _mlstm_body(q_ref, k_ref, v_ref, o_ref, sc_ref, sr_ref, c0_ref, n0_ref, m0_ref, nw_ref,
                h_ref, c_ref, n_ref, m_ref, *, L, Bb):
    @pl.when(pl.program_id(1) == 0)
    def _():
        c_ref[...] = c0_ref[...]
        n_ref[...] = n0_ref[...]
        m_ref[...] = m0_ref[...]

    def one(bb, carry):
        _mlstm_chunk(bb, q_ref, k_ref, v_ref, o_ref, sc_ref, sr_ref, nw_ref, h_ref, c_ref, n_ref, m_ref, L)
        return carry

    if Bb == 1:
        one(0, 0)
    else:
        lax.fori_loop(0, Bb, one, 0)


def mlstm(q, k, v, o, sc, sr, c0, n0, m0, norm_w, L, Bb):
    B, H, T, _ = q.shape
    nc = T // L
    hv = H * MI_DV
    bmap = lambda b, c: (b, 0, 0, 0)
    return pl.pallas_call(
        functools.partial(_mlstm_body, L=L, Bb=Bb),
        grid=(B // Bb, nc),
        in_specs=[pl.BlockSpec((Bb, H, L, MI_DQK), lambda b, c: (b, 0, c, 0)),
                  pl.BlockSpec((Bb, H, L, MI_DQK), lambda b, c: (b, 0, c, 0)),
                  pl.BlockSpec((Bb, L, hv), lambda b, c: (b, c, 0)),
                  pl.BlockSpec((Bb, L, hv), lambda b, c: (b, c, 0)),
                  pl.BlockSpec((Bb, L, LANE), lambda b, c: (b, c, 0)),
                  pl.BlockSpec((Bb, 1, 8, L), lambda b, c: (b, c, 0, 0)),
                  pl.BlockSpec((Bb, H, MI_DQK, MI_DV), bmap),
                  pl.BlockSpec((Bb, H, 1, MI_DQK), bmap),
                  pl.BlockSpec((Bb, 1, H), lambda b, c: (b, 0, 0)),
                  pl.BlockSpec((H, MI_DV), lambda b, c: (0, 0))],
        out_specs=[pl.BlockSpec((Bb, L, hv), lambda b, c: (b, c, 0)),
                   pl.BlockSpec((Bb, H, MI_DQK, MI_DV), bmap),
                   pl.BlockSpec((Bb, H, 1, MI_DQK), bmap),
                   pl.BlockSpec((Bb, 1, H), lambda b, c: (b, 0, 0))],
        out_shape=[jax.ShapeDtypeStruct((B, T, hv), F32),
                   jax.ShapeDtypeStruct((B, H, MI_DQK, MI_DV), F32),
                   jax.ShapeDtypeStruct((B, H, 1, MI_DQK), F32),
                   jax.ShapeDtypeStruct((B, 1, H), F32)],
        compiler_params=_cparams(("parallel", "arbitrary")),
        name="mlstm",
    )(q, k, v, o, sc, sr, c0, n0, m0, norm_w)


def _gdn_chunk(bb, x_ref, z_ref, sc_ref, sr_ref, conv0_ref, cw_ref, nw_ref,
               o_ref, s_ref, conv_ref, xp_scr, L, first_chunk):
    base = 8 - (GD_CONV - 1)
    @pl.when(first_chunk)
    def _():
        xp_scr[base:8, :] = conv0_ref[bb]
    xp_scr[8:8 + L, :] = x_ref[bb]
    y = xp_scr[base:base + L, :] * cw_ref[0:1, :]
    for j in range(1, GD_CONV):
        y = y + xp_scr[base + j:base + j + L, :] * cw_ref[j:j + 1, :]
    tail = xp_scr[L + base:L + 8, :]
    xp_scr[base:8, :] = tail
    conv_ref[bb] = tail
    y = _silu(y)

    rows, cols = _tri(L)
    incl = rows >= cols
    strict = rows > cols
    eye = (rows == cols).astype(F32)
    tril = incl.astype(F32)
    triu = (rows <= cols).astype(F32)
    sc = sc_ref[bb]
    sr = sr_ref[bb, 0]
    gc_col = _dot(tril, sc[:, 12:16], HI)
    gc_row = _dot(sr[0:4, :], triu, HI)
    kw = GD_H * GD_DK
    for h in range(GD_H):
        qh = y[:, h * GD_DK:(h + 1) * GD_DK]
        kh = y[:, kw + h * GD_DK:kw + (h + 1) * GD_DK]
        vh = y[:, 2 * kw + h * GD_DV:2 * kw + (h + 1) * GD_DV]
        qh = qh * lax.rsqrt(jnp.sum(qh * qh, axis=-1, keepdims=True) + EPS) * (GD_DK ** -0.5)
        kh = kh * lax.rsqrt(jnp.sum(kh * kh, axis=-1, keepdims=True) + EPS)
        beta = sc[:, 8 + h:9 + h]
        gcc = gc_col[:, h:h + 1]
        gcr = gc_row[h:h + 1, :]
        dec = jnp.exp(jnp.where(incl, gcc - gcr, NEG_BIG))
        kb = kh * beta
        neg_a = -(_dot_nt(kb, kh, HI) * jnp.where(strict, dec, 0.0))
        tinv = eye + neg_a
        pw = neg_a
        for _ in range(int(math.log2(L)) - 1):
            pw = _dot(pw, pw, HI)
            tinv = tinv + _dot(tinv, pw, HI)
        egc = jnp.exp(gcc)
        u = _dot(tinv, vh * beta, HI)
        w = _dot(tinv, kb * egc, HI)
        attn = _dot_nt(qh, kh, HI) * dec
        g_last = gcc[L - 1:L, :]
        s_prev = s_ref[bb, h]
        v_new = u - _dot(w, s_prev, HI)
        o = _dot(qh * egc, s_prev, HI) + _dot(attn, v_new, HI)
        s_ref[bb, h] = jnp.exp(g_last) * s_prev + _dot_tn(kh * jnp.exp(g_last - gcc), v_new, HI)
        on = o * lax.rsqrt(jnp.mean(o * o, axis=-1, keepdims=True) + EPS) * nw_ref[...]
        o_ref[bb, :, h * GD_DV:(h + 1) * GD_DV] = on * _silu(z_ref[bb, :, h * GD_DV:(h + 1) * GD_DV])


def _gdn_body(x_ref, z_ref, sc_ref, sr_ref, s0_ref, conv0_ref, cw_ref, nw_ref,
              o_ref, s_ref, conv_ref, xp_scr, *, L, Bb):
    first = pl.program_id(1) == 0

    @pl.when(first)
    def _():
        s_ref[...] = s0_ref[...]

    def one(bb, carry):
        _gdn_chunk(bb, x_ref, z_ref, sc_ref, sr_ref, conv0_ref, cw_ref, nw_ref,
                   o_ref, s_ref, conv_ref, xp_scr, L, first)
        return carry

    if Bb == 1:
        one(0, 0)
    else:
        lax.fori_loop(0, Bb, one, 0)


def gdn(x, z, sc, sr, s0, conv0, conv_w, norm_w, L, Bb):
    B, T, ch = x.shape
    H = GD_H
    nc = T // L
    hv = H * GD_DV
    assert Bb == 1 or nc == 1
    bmap = lambda b, c: (b, 0, 0, 0)
    return pl.pallas_call(
        functools.partial(_gdn_body, L=L, Bb=Bb),
        grid=(B // Bb, nc),
        in_specs=[pl.BlockSpec((Bb, L, ch), lambda b, c: (b, c, 0)),
                  pl.BlockSpec((Bb, L, hv), lambda b, c: (b, c, 0)),
                  pl.BlockSpec((Bb, L, LANE), lambda b, c: (b, c, 0)),
                  pl.BlockSpec((Bb, 1, 8, L), lambda b, c: (b, c, 0, 0)),
                  pl.BlockSpec((Bb, H, GD_DK, GD_DV), bmap),
                  pl.BlockSpec((Bb, GD_CONV - 1, ch), lambda b, c: (b, 0, 0)),
                  pl.BlockSpec((GD_CONV, ch), lambda b, c: (0, 0)),
                  pl.BlockSpec((1, GD_DV), lambda b, c: (0, 0))],
        out_specs=[pl.BlockSpec((Bb, L, hv), lambda b, c: (b, c, 0)),
                   pl.BlockSpec((Bb, H, GD_DK, GD_DV), bmap),
                   pl.BlockSpec((Bb, GD_CONV - 1, ch), lambda b, c: (b, 0, 0))],
        out_shape=[jax.ShapeDtypeStruct((B, T, hv), F32),
                   jax.ShapeDtypeStruct((B, H, GD_DK, GD_DV), F32),
                   jax.ShapeDtypeStruct((B, GD_CONV - 1, ch), F32)],
        scratch_shapes=[pltpu.VMEM((L + 8, ch), F32)],
        compiler_params=_cparams(("parallel", "arbitrary")),
        name="gdn",
    )(x, z, sc, sr, s0, conv0, conv_w, norm_w.reshape(1, GD_DV))


def _compress_body(x_ref, pos_ref, w1_ref, w2_ref, o_ref):
    Bb, _, R, half = x_ref.shape
    x = x_ref[:, 0].reshape(Bb * R, half).astype(F32)
    ua = _dot((x + pos_ref[0, 0:1, :]).astype(BF16), w1_ref[0, 0])
    ub = _dot((x + pos_ref[0, 1:2, :]).astype(BF16), w1_ref[0, 1])
    h = _silu(ua + pltpu.roll(ub, Bb * R - 1, 0))
    o_ref[:, 0, 0:R, :] = _dot(h.astype(BF16), w2_ref[0]).reshape(Bb, R, HD)
    rp = o_ref.shape[2]
    if rp > R:
        o_ref[:, 0, R:rp, :] = jnp.zeros((Bb, rp - R, HD), F32)


def nsa_compress(xr, R, pos, w1, w2, Bb):
    B = xr.shape[0]
    half = CMP_STRIDE * HD
    rp = -(-R // LANE) * LANE
    return pl.pallas_call(
        _compress_body,
        grid=(4, B // Bb),
        in_specs=[pl.BlockSpec((Bb, 1, R, half), lambda c, b: (b, c, 0, 0)),
                  pl.BlockSpec((1, 2, half), lambda c, b: (c // 2, 0, 0)),
                  pl.BlockSpec((1, 2, half, CMP_HIDDEN), lambda c, b: (c // 2, 0, 0, 0)),
                  pl.BlockSpec((1, CMP_HIDDEN, HD), lambda c, b: (c // 2, 0, 0))],
        out_specs=pl.BlockSpec((Bb, 1, rp, HD), lambda c, b: (b, c, 0, 0)),
        out_shape=jax.ShapeDtypeStruct((B, 4, rp, HD), F32),
        compiler_params=_cparams(("parallel", "parallel")),
        name="nsa_compress",
    )(xr, pos.reshape(2, 2, half), w1.reshape(2, 2, half, CMP_HIDDEN).astype(BF16), w2.astype(BF16))


def _cumsum_body(x_ref, o_ref):
    Bb, H, T = x_ref.shape
    rows, cols = _tri(LANE)
    triu = (rows <= cols).astype(F32)
    carry = jnp.zeros((Bb * H, 1), F32)
    for c in range(T // LANE):
        seg = x_ref[:, :, c * LANE:(c + 1) * LANE].reshape(Bb * H, LANE)
        loc = _dot(seg, triu, HI) + carry
        o_ref[:, :, c * LANE:(c + 1) * LANE] = loc.reshape(Bb, H, LANE)
        carry = loc[:, LANE - 1:LANE]


def cumsum_lanes(x, Bb):
    B, H, T = x.shape
    return pl.pallas_call(
        _cumsum_body,
        grid=(B // Bb,),
        in_specs=[pl.BlockSpec((Bb, H, T), lambda b: (b, 0, 0))],
        out_specs=pl.BlockSpec((Bb, H, T), lambda b: (b, 0, 0)),
        out_shape=jax.ShapeDtypeStruct((B, H, T), F32),
        compiler_params=_cparams(("parallel",)),
        name="cumsum",
    )(x)


def _flash_tile(s_blocks, v, m_scr, l_scr, acc_scr):
    dv = acc_scr.shape[-1]
    m_prev = m_scr[...]
    mx = s_blocks[0]
    for sb in s_blocks[1:]:
        mx = jnp.maximum(mx, sb)
    m_new = jnp.maximum(m_prev, jnp.max(mx, axis=1, keepdims=True))
    p_blocks = [jnp.exp(sb - m_new) for sb in s_blocks]
    sm = p_blocks[0]
    for pb in p_blocks[1:]:
        sm = sm + pb
    alpha = jnp.exp(m_prev - m_new)
    l_scr[...] = alpha * l_scr[...] + jnp.sum(sm, axis=1, keepdims=True)
    p = (jnp.concatenate(p_blocks, axis=1) if len(p_blocks) > 1 else p_blocks[0]).astype(BF16)
    acc_scr[...] = alpha[:, :dv] * acc_scr[...] + _dot(p, v)
    m_scr[...] = m_new


def _flash_reset(m_scr, l_scr, acc_scr):
    m_scr[...] = jnp.full_like(m_scr, NEG_BIG)
    l_scr[...] = jnp.zeros_like(l_scr)
    acc_scr[...] = jnp.zeros_like(acc_scr)


def _lane_blocks(s):
    return [s[:, i * LANE:(i + 1) * LANE] for i in range(s.shape[1] // LANE)]


def _fox_body(q_ref, k_ref, v_ref, fq_ref, fk_ref, o_ref, m_scr, l_scr, acc_scr, *, tq, tk):
    qi = pl.program_id(1)
    j = pl.program_id(2)
    top = (qi * tq + tq - 1) // tk

    @pl.when(j == 0)
    def _():
        _flash_reset(m_scr, l_scr, acc_scr)

    def tile(diag):
        fq = fq_ref[0]
        fk = fk_ref[0]
        if diag:
            mask = (top * tk + _iota((tq, tk), 1)) <= (qi * tq + _iota((tq, tk), 0))
        for h in range(FOX_H):
            s = _dot_nt(q_ref[0, h], k_ref[0, h]) + fq[:, h:h + 1] - fk[h:h + 1, :]
            if diag:
                s = jnp.where(mask, s, NEG_BIG)
            _flash_tile(_lane_blocks(s), v_ref[0, h], m_scr.at[h], l_scr.at[h], acc_scr.at[h])

    @pl.when(j == 0)
    def _():
        tile(True)

    @pl.when(jnp.logical_and(j > 0, j <= top))
    def _():
        tile(False)

    @pl.when(j == pl.num_programs(2) - 1)
    def _():
        for h in range(FOX_H):
            o_ref[0, h] = acc_scr[h] / jnp.maximum(l_scr[h][:, :HD], 1e-30)


def fox_prompt(q, kv, fq, fk, tq, tk):
    B, H, T, _ = q.shape
    kmap = lambda i, j: jnp.maximum((i * tq + tq - 1) // tk - j, 0)
    return pl.pallas_call(
        functools.partial(_fox_body, tq=tq, tk=tk),
        grid=(B, T // tq, T // tk),
        in_specs=[pl.BlockSpec((1, H, tq, HD), lambda b, i, j: (b, 0, i, 0)),
                  pl.BlockSpec((1, H, tk, HD), lambda b, i, j: (b, 0, kmap(i, j), 0)),
                  pl.BlockSpec((1, H, tk, HD), lambda b, i, j: (b, 1, kmap(i, j), 0)),
                  pl.BlockSpec((1, tq, H), lambda b, i, j: (b, i, 0)),
                  pl.BlockSpec((1, H, tk), lambda b, i, j: (b, 0, kmap(i, j)))],
        out_specs=pl.BlockSpec((1, H, tq, HD), lambda b, i, j: (b, 0, i, 0)),
        out_shape=jax.ShapeDtypeStruct((B, H, T, HD), F32),
        scratch_shapes=[pltpu.VMEM((H, tq, LANE), F32), pltpu.VMEM((H, tq, LANE), F32), pltpu.VMEM((H, tq, HD), F32)],
        compiler_params=_cparams(("parallel", "parallel", "arbitrary")),
        name="fox_prompt",
    )(q, kv, kv, fq, fk)


def _fox_decode_body(pt_ref, q_ref, newkv_ref, newlf_ref, *refs, n_pages, tn):
    kv_refs = refs[:n_pages]
    lf_refs = refs[n_pages:2 * n_pages]
    o_ref = refs[2 * n_pages]
    hw = FOX_H * HD
    R = FOX_H * tn
    q = q_ref[0]
    qrep = jnp.concatenate([q] * FOX_H, axis=0)
    blockmask = (_iota((R, hw), 0) // tn) == (_iota((R, hw), 1) // HD)
    qbd = jnp.where(blockmask, qrep, 0.0).astype(BF16)
    rows, cols = _tri(PAGE)
    triu = (rows <= cols).astype(F32)
    expand = lambda a: jnp.concatenate([jnp.broadcast_to(a[h:h + 1, :], (tn, a.shape[1])) for h in range(FOX_H)], 0)
    carry_c = jnp.zeros((FOX_H, 1), F32)
    carry_r = jnp.zeros((1, FOX_H), F32)
    s_tiles = []
    for pg in range(n_pages):
        lf = lf_refs[pg][0]
        f_t = _dot_tn(lf, triu, HI) + carry_c
        carry_c = f_t[:, PAGE - 1:PAGE]
        carry_r = carry_r + jnp.sum(lf, axis=0, keepdims=True)
        k = kv_refs[pg][0, :, 0:hw].astype(BF16)
        s_tiles.append(_dot_nt(qbd, k) * (HD ** -0.5) - expand(f_t))
    lfn = newlf_ref[0]
    r8, c8 = _tri(tn)
    fq_c = _dot((r8 >= c8).astype(F32), lfn, HI) + carry_r
    fq_t = _dot_tn(lfn, (r8 <= c8).astype(F32), HI) + carry_c
    fq_rows = jnp.concatenate([fq_c[:, h:h + 1] for h in range(FOX_H)], axis=0)
    kn = newkv_ref[0, :, 0:hw]
    s_new = _dot_nt(qbd.astype(F32), kn, HI) * (HD ** -0.5) - expand(fq_t)
    causal = _iota((R, tn), 1) <= (_iota((R, tn), 0) % tn)
    s_new = jnp.where(causal, s_new + fq_rows, NEG_BIG)
    s_tiles = [s + fq_rows for s in s_tiles]
    m = jnp.max(s_new, axis=1, keepdims=True)
    for s in s_tiles:
        m = jnp.maximum(m, jnp.max(s, axis=1, keepdims=True))
    p_new = jnp.where(causal, jnp.exp(s_new - m), 0.0)
    l = jnp.sum(p_new, axis=1, keepdims=True)
    acc = _dot(p_new, newkv_ref[0, :, hw:2 * hw], HI)
    for pg, s in enumerate(s_tiles):
        p = jnp.exp(s - m)
        l = l + jnp.sum(p, axis=1, keepdims=True)
        acc = acc + _dot(p.astype(BF16), kv_refs[pg][0, :, hw:2 * hw].astype(BF16))
    acc = acc / jnp.maximum(l, 1e-30)
    o_ref[0] = jnp.concatenate([acc[h * tn:(h + 1) * tn, h * HD:(h + 1) * HD] for h in range(FOX_H)], axis=1)


def fox_decode(page_table, q, newkv, newlf, kv_pool, lf_pool):
    B, tn, hw = q.shape
    n_pages = page_table.shape[1]
    page_spec = lambda width, pg: pl.BlockSpec((1, PAGE, width), lambda b, pt: (pt[b, pg], 0, 0))
    grid_spec = pltpu.PrefetchScalarGridSpec(
        num_scalar_prefetch=1,
        grid=(B,),
        in_specs=[pl.BlockSpec((1, tn, hw), lambda b, pt: (b, 0, 0)),
                  pl.BlockSpec((1, tn, 2 * hw), lambda b, pt: (b, 0, 0)),
                  pl.BlockSpec((1, tn, FOX_H), lambda b, pt: (b, 0, 0))]
                 + [page_spec(2 * hw, pg) for pg in range(n_pages)]
                 + [page_spec(FOX_H, pg) for pg in range(n_pages)],
        out_specs=pl.BlockSpec((1, tn, hw), lambda b, pt: (b, 0, 0)),
    )
    return pl.pallas_call(
        functools.partial(_fox_decode_body, n_pages=n_pages, tn=tn),
        grid_spec=grid_spec,
        out_shape=jax.ShapeDtypeStruct((B, tn, hw), F32),
        compiler_params=_cparams(("arbitrary",)),
        name="fox_decode",
    )(page_table, q, newkv, newlf, *([kv_pool] * n_pages), *([lf_pool] * n_pages))


def _t5_bucket(dist):
    n = jnp.maximum(dist, 0)
    nf = jnp.maximum(n, 1).astype(F32)
    large = BUCKET_EXACT + (jnp.log(nf / BUCKET_EXACT) / math.log(MAX_DISTANCE / BUCKET_EXACT)
                            * (N_BUCKETS - BUCKET_EXACT)).astype(jnp.int32)
    return jnp.where(n < BUCKET_EXACT, n, jnp.minimum(large, N_BUCKETS - 1))


def _bias_from_bucket(bucket, tbl_ref, head):
    out = jnp.zeros(bucket.shape, F32)
    for kk in range(N_BUCKETS):
        out = jnp.where(bucket == kk, tbl_ref[kk, head], out)
    return out


FOX_TQ, FOX_TK = 256, 512
NSA_TK = 2 * LANE
NSA_FAR_GROUP = 4
NSA_BIAS_TILES = (BUCKET_SAT_DIST + NSA_TK + LANE - 1) // LANE


def _nsa_body(tbl_ref, q_ref, gate_ref, kc_ref, vc_ref, ks_ref, vs_ref, kw_ref, vw_ref, smap_ref, o_ref,
              bias_scr, score_scr, m_scr, l_scr, acc_scr, *, tq, q_pos0, win_pos0, n_sel, ncp, tw):
    g = pl.program_id(1)
    qi = pl.program_id(2)
    q0 = q_pos0 + qi * tq
    R = NSA_R * tq
    scale = HD ** -0.5
    last_bias = tuple(tbl_ref[N_BUCKETS - 1, g * NSA_R + r] for r in range(NSA_R))

    @pl.when(qi == 0)
    def _():
        ii = _iota((tq, NSA_TK), 0)
        jj = _iota((tq, NSA_TK), 1)
        for dd in range(NSA_BIAS_TILES):
            bucket = _t5_bucket(ii - jj + dd * LANE)
            for r in range(NSA_R):
                bias_scr[dd, r * tq:(r + 1) * tq, :] = (
                    _bias_from_bucket(bucket, tbl_ref, g * NSA_R + r) - last_bias[r])
        bias_scr[NSA_BIAS_TILES] = jnp.zeros((R, NSA_TK), F32)

    t_col = q0 + _iota((tq, 1), 0)

    bias_tiles, mask_tiles = [], []
    for nt in range(ncp // LANE):
        c_end = (nt * LANE + _iota((tq, LANE), 1)) * CMP_STRIDE + (CMP_BLOCK - 1)
        dist = t_col - c_end
        max_dist = q0 + tq - 1 - (nt * LANE * CMP_STRIDE + CMP_BLOCK - 1)
        min_dist = q0 - ((nt * LANE + LANE - 1) * CMP_STRIDE + CMP_BLOCK - 1)
        special = jnp.logical_and(max_dist >= 0, min_dist < BUCKET_SAT_DIST)

        def general(dist=dist):
            bucket = _t5_bucket(dist)
            return jnp.stack([_bias_from_bucket(bucket, tbl_ref, g * NSA_R + r) - last_bias[r]
                              for r in range(NSA_R)])

        def saturated():
            return jnp.zeros((NSA_R, tq, LANE), F32)

        bias_tiles.append(lax.cond(special, general, saturated))
        mask_tiles.append(dist >= 0)
    mask_c = jnp.concatenate(mask_tiles, axis=1) if len(mask_tiles) > 1 else mask_tiles[0]
    kc = kc_ref[0, 0].astype(BF16)
    vc = vc_ref[0, 0].astype(BF16)
    pcsum = jnp.zeros((tq, ncp), F32)
    o_c = []
    for r in range(NSA_R):
        bias_r = jnp.concatenate([b[r] for b in bias_tiles], axis=1) if len(bias_tiles) > 1 else bias_tiles[0][r]
        s = _dot_nt(q_ref[0, r].astype(BF16), kc) * scale + bias_r
        s = jnp.where(mask_c, s, NEG_BIG)
        m = jnp.max(s, axis=1, keepdims=True)
        p = jnp.where(mask_c, jnp.exp(s - m), 0.0)
        pc = p / jnp.maximum(jnp.sum(p, axis=1, keepdims=True), 1e-30)
        o_c.append(_dot(pc.astype(BF16), vc))
        pcsum = pcsum + pc

    ps_t = _dot_nt(smap_ref[...], pcsum, HI)
    j_col = _iota((LANE, 1), 0)
    t_row = q0 + _iota((1, tq), 1)
    cur = lax.shift_right_logical(t_row, int(math.log2(SEL_BLOCK)))
    score = jnp.where(j_col * SEL_BLOCK <= t_row, ps_t, -1.0)
    score = jnp.where(j_col == cur - 1, FORCE_SCORE, score)
    score = jnp.where(j_col == cur, FORCE_SCORE, score)
    score = jnp.where(j_col == 0, FORCE_SCORE, score)
    score = jnp.where(j_col < n_sel, score, -3e38)
    score_scr[...] = score

    def rank_body(jp, rank):
        row = score_scr[pl.ds(jp, 1), :]
        tie = jnp.where(j_col > jp, 1.0, 0.0)
        return rank + jnp.where(row > score, 1.0, jnp.where(row == score, tie, 0.0))

    n_rank = jnp.minimum((q0 + tq - 1) // SEL_BLOCK + 1, LANE)
    rank = lax.fori_loop(0, n_rank, rank_body, jnp.zeros((LANE, tq), F32))
    sel_t = jnp.where(rank < SEL_TOPN, 1.0, 0.0).astype(BF16)
    eye = (_iota((tq, tq), 0) == _iota((tq, tq), 1)).astype(BF16)
    sel = _dot_nt(eye, sel_t).astype(BF16)

    qs = (q_ref[0].reshape(R, HD) * scale).astype(BF16)
    t_tile = q0 + _iota((tq, NSA_TK), 0)
    c_tile = _iota((tq, NSA_TK), 1)
    log2_blk = int(math.log2(SEL_BLOCK))

    def stack(a):
        return jnp.concatenate([a] * NSA_R, axis=0)

    def bias_tile(offset):
        return bias_scr[jnp.minimum(lax.shift_right_logical(offset, 7), NSA_BIAS_TILES)]

    def sel_tile(k0, width, near):
        k = ks_ref[0, 0, pl.ds(k0, width), :]
        v = vs_ref[0, 0, pl.ds(k0, width), :]
        blk = lax.shift_right_logical(k0 + _iota((LANE, width), 1), log2_blk)
        expand = jnp.where(_iota((LANE, width), 0) == blk, 1.0, 0.0).astype(BF16)
        chosen = _dot(sel, expand)
        s = _dot_nt(qs, k)
        if near:
            chosen = jnp.where(t_tile - (k0 + c_tile) >= 0, chosen, 0.0)
            s = s + bias_tile(q0 - k0)
        s = jnp.where(stack(chosen) > 0.5, s, NEG_BIG)
        _flash_tile(_lane_blocks(s), v, m_scr, l_scr, acc_scr)

    _flash_reset(m_scr, l_scr, acc_scr)
    kt_top = (q0 + tq - 1) // NSA_TK
    sel_tile(pl.multiple_of(kt_top * NSA_TK, NSA_TK), NSA_TK, True)

    @pl.when(kt_top >= 1)
    def _():
        sel_tile(pl.multiple_of((kt_top - 1) * NSA_TK, NSA_TK), NSA_TK, True)

    n_far = jnp.maximum(kt_top - 1, 0)
    n_groups = n_far // NSA_FAR_GROUP

    def sel_far_group(gi, carry):
        sel_tile(pl.multiple_of(gi * (NSA_FAR_GROUP * NSA_TK), NSA_FAR_GROUP * NSA_TK), NSA_FAR_GROUP * NSA_TK, False)
        return carry

    def sel_far(kt, carry):
        sel_tile(pl.multiple_of(kt * NSA_TK, NSA_TK), NSA_TK, False)
        return carry

    lax.fori_loop(0, n_groups, sel_far_group, 0)
    lax.fori_loop(n_groups * NSA_FAR_GROUP, n_far, sel_far, 0)
    o_s = acc_scr[...] / jnp.maximum(l_scr[...][:, :HD], 1e-30)

    def win_tile(i, carry):
        k0 = pl.multiple_of((wt_top - i) * NSA_TK, NSA_TK)
        k = kw_ref[0, 0, pl.ds(k0, NSA_TK), :]
        v = vw_ref[0, 0, pl.ds(k0, NSA_TK), :]
        dist = t_tile - (win_pos0 + k0 + c_tile)
        ok = jnp.where(dist >= 0, jnp.where(dist < WINDOW, 1.0, 0.0), 0.0)
        s = _dot_nt(qs, k) + bias_tile(q0 - win_pos0 - k0)
        s = jnp.where(stack(ok) > 0.5, s, NEG_BIG)
        _flash_tile(_lane_blocks(s), v, m_scr, l_scr, acc_scr)
        return carry

    _flash_reset(m_scr, l_scr, acc_scr)
    wt_top = (jnp.minimum(q0 + tq - win_pos0, tw) - 1) // NSA_TK
    wt_lo = jnp.maximum(q0 - (WINDOW - 1) - win_pos0, 0) // NSA_TK
    lax.fori_loop(0, wt_top - wt_lo + 1, win_tile, 0)
    o_w = acc_scr[...] / jnp.maximum(l_scr[...][:, :HD], 1e-30)

    gates = gate_ref[0, 0]
    for r in range(NSA_R):
        o_ref[0, r] = (gates[:, 3 * r:3 * r + 1] * o_c[r]
                       + gates[:, 3 * r + 1:3 * r + 2] * o_s[r * tq:(r + 1) * tq]
                       + gates[:, 3 * r + 2:3 * r + 3] * o_w[r * tq:(r + 1) * tq])


def _selection_overlap_t(ncp):
    c_start = np.arange(ncp)[None, :] * CMP_STRIDE
    s_start = np.arange(LANE)[:, None] * SEL_BLOCK
    return ((c_start < s_start + SEL_BLOCK) & (c_start + CMP_BLOCK > s_start)).astype(np.float32)


def nsa_attend(tbl, q, gates, kcvc, sel_arr, sel_off, win_arr, win_off, *, tq, q_pos0, win_pos0, n_sel):
    B, _, Tq, _ = q.shape
    ncp = kcvc.shape[2]
    tks = sel_arr.shape[2]
    tw = win_arr.shape[2]
    R = NSA_R * tq
    smap = jnp.asarray(_selection_overlap_t(ncp))
    kv_spec = lambda rows, off: pl.BlockSpec((1, 1, rows, HD), lambda b, g, i: (b, off + g, 0, 0))
    return pl.pallas_call(
        functools.partial(_nsa_body, tq=tq, q_pos0=q_pos0, win_pos0=win_pos0, n_sel=n_sel, ncp=ncp, tw=tw),
        grid=(B, NSA_G, Tq // tq),
        in_specs=[pl.BlockSpec(memory_space=pltpu.SMEM),
                  pl.BlockSpec((1, NSA_R, tq, HD), lambda b, g, i: (b, g, i, 0)),
                  pl.BlockSpec((1, 1, tq, 3 * NSA_R), lambda b, g, i: (b, g, i, 0)),
                  kv_spec(ncp, 0), kv_spec(ncp, 2),
                  kv_spec(tks, sel_off), kv_spec(tks, sel_off + 2),
                  kv_spec(tw, win_off), kv_spec(tw, win_off + 2),
                  pl.BlockSpec((LANE, ncp), lambda b, g, i: (0, 0))],
        out_specs=pl.BlockSpec((1, NSA_R, tq, HD), lambda b, g, i: (b, g, i, 0)),
        out_shape=jax.ShapeDtypeStruct((B, NSA_G * NSA_R, Tq, HD), F32),
        scratch_shapes=[pltpu.VMEM((NSA_BIAS_TILES + 1, R, NSA_TK), F32), pltpu.VMEM((LANE, tq), F32),
                        pltpu.VMEM((R, LANE), F32), pltpu.VMEM((R, LANE), F32), pltpu.VMEM((R, HD), F32)],
        compiler_params=_cparams(("parallel", "parallel", "arbitrary")),
        name="nsa_attend",
    )(tbl, q, gates, kcvc, kcvc, sel_arr, sel_arr, win_arr, win_arr, smap)


def _nsa_gather_body(pt_ref, new_ref, *refs, n_pages):
    pages = refs[:n_pages]
    cmp_ref, sel_ref = refs[n_pages:]
    ngrp = 2 * NSA_G
    for pg in range(n_pages):
        for j in range(ngrp):
            cmp_ref[0, j, pg * PAGE:(pg + 1) * PAGE, :] = pages[pg][0, :, j * HD:(j + 1) * HD]
            sel_ref[0, j, pg * PAGE:(pg + 1) * PAGE, :] = (
                pages[pg][0, :, (ngrp + j) * HD:(ngrp + j + 1) * HD].astype(BF16))
    tn = new_ref.shape[1]
    tail = sel_ref.shape[2] - n_pages * PAGE
    for j in range(ngrp):
        new = new_ref[0, :, (ngrp + j) * HD:(ngrp + j + 1) * HD]
        tile = jnp.concatenate([new, jnp.zeros((tail - tn, HD), F32)], axis=0)
        sel_ref[0, j, n_pages * PAGE:, :] = tile.astype(BF16)


def nsa_gather(page_table, new_rows, pool):
    B, tn, width = new_rows.shape
    n_pages = page_table.shape[1]
    ngrp = 2 * NSA_G
    sel_rows = -(-(n_pages * PAGE + tn) // NSA_TK) * NSA_TK
    grid_spec = pltpu.PrefetchScalarGridSpec(
        num_scalar_prefetch=1,
        grid=(B,),
        in_specs=[pl.BlockSpec((1, tn, width), lambda b, pt: (b, 0, 0))]
                 + [pl.BlockSpec((1, PAGE, width), functools.partial(lambda b, pt, pg: (pt[b, pg], 0, 0), pg=pg))
                    for pg in range(n_pages)],
        out_specs=[pl.BlockSpec((1, ngrp, n_pages * PAGE, HD), lambda b, pt: (b, 0, 0, 0)),
                   pl.BlockSpec((1, ngrp, sel_rows, HD), lambda b, pt: (b, 0, 0, 0))],
    )
    return pl.pallas_call(
        functools.partial(_nsa_gather_body, n_pages=n_pages),
        grid_spec=grid_spec,
        out_shape=[jax.ShapeDtypeStruct((B, ngrp, n_pages * PAGE, HD), F32),
                   jax.ShapeDtypeStruct((B, ngrp, sel_rows, HD), BF16)],
        compiler_params=_cparams(("arbitrary",)),
        name="nsa_gather",
    )(page_table, new_rows, *([pool] * n_pages))


def _row_tile(m):
    return 512 if m % 512 == 0 else m


def _small_params(entries):
    sp = jnp.zeros((8, LANE), F32)
    for off, bias, act, log_scale in entries:
        n = bias.shape[0]
        sp = sp.at[0, off:off + n].set(bias.astype(F32))
        sp = sp.at[1, off:off + n].set(act)
        if log_scale is not None:
            sp = sp.at[2, off:off + n].set(log_scale.astype(F32))
    return sp


EVEN_WIDTHS = (256, 256, 512, 512, GD_CH, 512, LANE)


def _even_weights(w_in):
    s = np.cumsum((0, 256, 256, 512, 512, 4, 4, 512, 512, 512, 512, 4, 4))
    col = lambda i: w_in[:, s[i]:s[i + 1]]
    small = jnp.concatenate([col(4), col(5), col(10), col(11)], axis=1)
    small = jnp.pad(small, ((0, 0), (0, LANE - small.shape[1])))
    return jnp.concatenate([col(0), col(1), col(2), col(3), col(6), col(7), col(8), col(9), small], axis=1).astype(BF16)


def _chunk_rows(small, B, T, L, lanes):
    r = small.reshape(B, T // L, L, LANE)[..., lanes[0]:lanes[1]]
    r = jnp.swapaxes(r, 2, 3)
    return jnp.pad(r, ((0, 0), (0, 0), (0, 8 - r.shape[2]), (0, 0)))


def even_layer(x, p, past, L, Bb):
    B, T, D = x.shape
    M = B * T
    tm = _row_tile(M)
    sp = _small_params([(0, p['mi_b_i'], ACT_ID, None), (4, p['mi_b_f'], ACT_LOGSIG, None),
                        (8, jnp.zeros((4,), F32), ACT_SIG, None), (12, p['gd_dt_bias'], ACT_DECAY, p['gd_a_log'])])
    mq, mk, mv, mo, gx, gz, small = norm_proj(x.reshape(M, D), p['norm_mix'], _even_weights(p['w_in']), sp,
                                              EVEN_WIDTHS, tm)
    heads = lambda a: jnp.transpose(a.reshape(B, T, MI_H, MI_DQK), (0, 2, 1, 3))
    sc = small.reshape(B, T, LANE)
    if past is None:
        c0 = jnp.zeros((B, MI_H, MI_DQK, MI_DV), F32)
        n0 = jnp.zeros((B, MI_H, 1, MI_DQK), F32)
        m0 = jnp.zeros((B, 1, MI_H), F32)
        s0 = jnp.zeros((B, GD_H, GD_DK, GD_DV), F32)
        conv0 = jnp.zeros((B, GD_CONV - 1, GD_CH), F32)
    else:
        c0, n0, m0, s0, conv0 = past
        n0 = n0.reshape(B, MI_H, 1, MI_DQK)
        m0 = m0.reshape(B, 1, MI_H)
    hm, c1, n1, m1 = mlstm(heads(mq), heads(mk), mv.reshape(B, T, -1), mo.reshape(B, T, -1), sc,
                           _chunk_rows(small, B, T, L, (0, 8)), c0, n0, m0,
                           p['mi_norm'].reshape(MI_H, MI_DV), L, Bb)
    og, s1, conv1 = gdn(gx.reshape(B, T, GD_CH), gz.reshape(B, T, -1), sc,
                        _chunk_rows(small, B, T, L, (12, 16)), s0, conv0, p['gd_conv_w'], p['gd_norm'], L, Bb)
    y = out_proj_residual(x.reshape(M, D), hm.reshape(M, -1), og.reshape(M, -1), p['w_out'].astype(BF16), tm)
    return y.reshape(B, T, D), (c1, n1.reshape(B, MI_H, MI_DQK), m1.reshape(B, MI_H), s1, conv1)


NSA_QW = NSA_G * NSA_R * HD
NSA_KVW = 6 * NSA_G * HD
NSA_CACHE_W = 4 * NSA_G * HD
FOX_W = FOX_H * HD
N_GATE = 3 * NSA_G * NSA_R
ODD_WIDTHS = (NSA_QW, NSA_KVW, FOX_W, 2 * FOX_W, LANE)


def _odd_weights(w_in):
    s = np.cumsum((0, NSA_QW, NSA_KVW, N_GATE, FOX_W, FOX_W, FOX_W, FOX_H))
    col = lambda i: w_in[:, s[i]:s[i + 1]]
    small = jnp.concatenate([col(2), col(6)], axis=1)
    small = jnp.pad(small, ((0, 0), (0, LANE - small.shape[1])))
    return jnp.concatenate([col(0), col(1), col(3), col(4), col(5), small], axis=1).astype(BF16)


def _heads(a, B, T, n):
    return jnp.transpose(a.reshape(B, T, n, HD), (0, 2, 1, 3))


def _unheads(a):
    B, n, T, _ = a.shape
    return jnp.transpose(a, (0, 2, 1, 3)).reshape(B * T, n * HD)


def odd_layer(x, p, rel_bias, w_buf, past, page_table):
    B, T, D = x.shape
    M = B * T
    tm = _row_tile(M)
    sp = _small_params([(0, jnp.zeros((N_GATE,), F32), ACT_SIG, None), (N_GATE, p['fox_b_f'], ACT_LOGSIG, None)])
    nq, nkv, fq, fkv, small = norm_proj(x.reshape(M, D), p['norm_mix'], _odd_weights(p['w_in']), sp, ODD_WIDTHS, tm)
    new_nsa = nkv[:, :NSA_CACHE_W].reshape(B, T, 4, NSA_G, HD)
    new_win = nkv[:, NSA_CACHE_W:].reshape(B, T, 2, NSA_G, HD)
    new_fox = fkv.reshape(B, T, 2, FOX_H, HD)
    logf = small[:, N_GATE:N_GATE + FOX_H].reshape(B, T, FOX_H)
    q_heads = _heads(nq, B, T, NSA_G * NSA_R)
    gates = jnp.transpose(small[:, :N_GATE].reshape(B, T, NSA_G, 3 * NSA_R), (0, 2, 1, 3))
    cmp_args = (p['nsa_cmp_pos'], p['nsa_cmp_w1'], p['nsa_cmp_w2'])
    rows16 = CMP_STRIDE * HD
    if past is None:
        groups = _heads(nkv, B, T, 6 * NSA_G)
        kcvc = nsa_compress(groups.reshape(B, 6 * NSA_G, T // CMP_STRIDE, rows16), T // CMP_STRIDE, *cmp_args, 1)
        arr = groups.astype(BF16)
        o_n = nsa_attend(rel_bias, q_heads, gates, kcvc, arr, 2 * NSA_G, arr, 4 * NSA_G,
                         tq=min(T, LANE), q_pos0=0, win_pos0=0, n_sel=-(-T // SEL_BLOCK))
        f_t = cumsum_lanes(jnp.transpose(logf, (0, 2, 1)), B)
        o_f = fox_prompt(_heads(fq * HD ** -0.5, B, T, FOX_H).astype(BF16), _heads(fkv, B, T, 2 * FOX_H).astype(BF16),
                         jnp.transpose(f_t, (0, 2, 1)), f_t, min(T, FOX_TQ), min(T, FOX_TK))
        o_f = _unheads(o_f)
        win_prev = jnp.zeros((B, WINDOW, 2, NSA_G, HD), F32)
    else:
        nsa_pool, win_prev, fox_pool, logf_pool = past
        n_pool = nsa_pool.shape[0]
        n_pages = page_table.shape[1]
        start = n_pages * PAGE
        cmp_rows, sel_arr = nsa_gather(page_table, nkv[:, :NSA_CACHE_W].reshape(B, T, NSA_CACHE_W),
                                       nsa_pool.reshape(n_pool, PAGE, NSA_CACHE_W))
        kcvc = nsa_compress(cmp_rows.reshape(B, 2 * NSA_G, start // CMP_STRIDE, rows16), start // CMP_STRIDE,
                            *cmp_args, math.gcd(B, 8))
        wp = win_prev.shape[1]
        win_all = jnp.concatenate([win_prev.reshape(B, wp, 2 * NSA_G * HD), nkv[:, NSA_CACHE_W:].reshape(B, T, -1)], 1)
        tw = -(-(wp + T) // NSA_TK) * NSA_TK
        win_arr = _heads(jnp.pad(win_all, ((0, 0), (0, tw - wp - T), (0, 0))), B, tw, 2 * NSA_G).astype(BF16)
        o_n = nsa_attend(rel_bias, q_heads, gates, kcvc, sel_arr, 0, win_arr, 0,
                         tq=T, q_pos0=start, win_pos0=start - wp, n_sel=-(-(start + T) // SEL_BLOCK))
        o_f = fox_decode(page_table, fq.reshape(B, T, FOX_W), fkv.reshape(B, T, 2 * FOX_W), logf,
                         fox_pool.reshape(n_pool, PAGE, 2 * FOX_W), logf_pool)
        o_f = o_f.reshape(M, FOX_W)
    win_state = jnp.concatenate([win_prev, new_win], axis=1)[:, -w_buf:]
    y = out_proj_residual(x.reshape(M, D), _unheads(o_n), o_f, p['w_out'].astype(BF16), tm)
    return y.reshape(B, T, D), (new_nsa, win_state, new_fox, logf)


def _trunk(x, past, page_table, P, w_buf, L, Bb):
    B, T, D = x.shape
    pe = dict(norm_mix=P['norm_mix'][0], w_in=P['w_in_even'][0], w_out=P['w_out_even'][0], mi_b_i=P['mi_b_i'][0],
              mi_b_f=P['mi_b_f'][0], mi_norm=P['mi_norm'][0], gd_conv_w=P['gd_conv_w'][0], gd_a_log=P['gd_a_log'][0],
              gd_dt_bias=P['gd_dt_bias'][0], gd_norm=P['gd_norm'][0])
    po = dict(norm_mix=P['norm_mix'][1], w_in=P['w_in_odd'][0], w_out=P['w_out_odd'][0],
              nsa_cmp_pos=P['nsa_cmp_pos'][0], nsa_cmp_w1=P['nsa_cmp_w1'][0], nsa_cmp_w2=P['nsa_cmp_w2'][0],
              fox_b_f=P['fox_b_f'][0])
    tm = _row_tile(B * T)
    mlp = lambda x, layer, final: mlp_residual(
        x.reshape(B * T, D), P['norm_mlp'][layer], P['w_up'][layer].astype(BF16), P['w_down'][layer].astype(BF16),
        P['norm_final'], final, tm, 1024).reshape(B, T, D)
    even_past = None if past is None else tuple(past[k][0] for k in ('mc', 'mn', 'mm', 'gs', 'gc'))
    odd_past = None if past is None else tuple(past[k][0] for k in ('nsa_kv', 'nsa_win', 'fox_kv', 'fox_logf'))
    x, st_e = even_layer(x, pe, even_past, L, Bb)
    x = mlp(x, 0, False)
    x, st_o = odd_layer(x, po, P['rel_bias'], w_buf, odd_past, page_table)
    y = mlp(x, 1, True)
    return y, tuple(a[None] for a in st_e + st_o)


def kernel(x_prompt, x_sample, state_mlstm_c, state_mlstm_n, state_mlstm_m, state_gdn_s, state_gdn_conv,
           cache_nsa_kv, state_nsa_win, cache_fox_kv, cache_fox_logf, page_table,
           norm_mix, norm_mlp, norm_final, w_up, w_down,
           w_in_even, w_out_even, mi_b_i, mi_b_f, mi_norm, gd_conv_w, gd_a_log, gd_dt_bias, gd_norm,
           w_in_odd, w_out_odd, nsa_cmp_pos, nsa_cmp_w1, nsa_cmp_w2, fox_b_f, rel_bias):
    P = dict(norm_mix=norm_mix, norm_mlp=norm_mlp, norm_final=norm_final, w_up=w_up, w_down=w_down,
             w_in_even=w_in_even, w_out_even=w_out_even, mi_b_i=mi_b_i, mi_b_f=mi_b_f, mi_norm=mi_norm,
             gd_conv_w=gd_conv_w, gd_a_log=gd_a_log, gd_dt_bias=gd_dt_bias, gd_norm=gd_norm,
             w_in_odd=w_in_odd, w_out_odd=w_out_odd, nsa_cmp_pos=nsa_cmp_pos, nsa_cmp_w1=nsa_cmp_w1,
             nsa_cmp_w2=nsa_cmp_w2, fox_b_f=fox_b_f, rel_bias=rel_bias)
    w_buf = state_nsa_win.shape[2]
    t_p = x_prompt.shape[1]
    y_p, st_p = _trunk(x_prompt, None, None, P, w_buf, math.gcd(t_p, 64), 1)
    past = dict(mc=state_mlstm_c, mn=state_mlstm_n, mm=state_mlstm_m, gs=state_gdn_s, gc=state_gdn_conv,
                nsa_kv=cache_nsa_kv, nsa_win=state_nsa_win, fox_kv=cache_fox_kv, fox_logf=cache_fox_logf)
    b_s, t_s = x_sample.shape[:2]
    y_s, st_s = _trunk(x_sample, past, page_table, P, w_buf, math.gcd(t_s, 64), math.gcd(b_s, 8))
    return (y_p, y_s) + st_p + st_s
```

```python
import functools
import math

import jax
import jax.numpy as jnp
import numpy as np
from jax import lax
from jax.experimental import pallas as pl
from jax.experimental.pallas import tpu as pltpu

F32 = jnp.float32
BF16 = jnp.bfloat16
HI = lax.Precision.HIGHEST

D_MODEL = 1024
D_FF = 4 * D_MODEL
EPS = 1e-6
NEG_BIG = -1e30
PAGE = 128

MI_H, MI_DQK, MI_DV = 4, 64, 128
GD_H, GD_DK, GD_DV, GD_CONV = 4, 128, 128, 4
GD_CH = 3 * GD_H * GD_DK
NSA_G, NSA_R, HD = 2, 4, 64
FOX_H = 8
CMP_BLOCK, CMP_STRIDE, CMP_HIDDEN = 32, 16, 256
SEL_BLOCK, SEL_TOPN, WINDOW = 64, 16, 512
FORCE_SCORE = 1e4
N_BUCKETS, MAX_DISTANCE = 32, 128
BUCKET_EXACT = N_BUCKETS // 2
BUCKET_SAT_DIST = 113
LANE = 128
VMEM_LIMIT = 56 * 1024 * 1024


def _cparams(sem):
    return pltpu.CompilerParams(dimension_semantics=sem, vmem_limit_bytes=VMEM_LIMIT)


def _dot(a, b, precision=None):
    return jnp.dot(a, b, preferred_element_type=F32, precision=precision)


def _dot_nt(a, b, precision=None):
    return lax.dot_general(a, b, (((1,), (1,)), ((), ())), preferred_element_type=F32, precision=precision)


def _dot_tn(a, b, precision=None):
    return lax.dot_general(a, b, (((0,), (0,)), ((), ())), preferred_element_type=F32, precision=precision)


def _softplus(x):
    return jnp.maximum(x, 0.0) + jnp.log1p(jnp.exp(-jnp.abs(x)))


def _sigmoid(x):
    return 1.0 / (1.0 + jnp.exp(-x))


def _silu(x):
    return x * _sigmoid(x)


def _iota(shape, dim):
    return lax.broadcasted_iota(jnp.int32, shape, dim)


ACT_ID, ACT_LOGSIG, ACT_SIG, ACT_DECAY = 0.0, 1.0, 2.0, 3.0


def _proj_body(x_ref, g_ref, w_ref, sp_ref, *out_refs, widths):
    x = x_ref[...]
    hn = (x * lax.rsqrt(jnp.mean(x * x, axis=-1, keepdims=True) + EPS) * g_ref[...]).astype(BF16)
    off = 0
    for i, (o_ref, n) in enumerate(zip(out_refs, widths)):
        r = _dot(hn, w_ref[:, off:off + n])
        if i == len(widths) - 1:
            z = r + sp_ref[0:1, :]
            mode = sp_ref[1:2, :]
            decay = -jnp.exp(sp_ref[2:3, :]) * _softplus(z)
            r = jnp.where(mode == ACT_LOGSIG, -_softplus(-z),
                          jnp.where(mode == ACT_SIG, _sigmoid(z),
                                    jnp.where(mode == ACT_DECAY, decay, z)))
        o_ref[...] = r.astype(o_ref.dtype)
        off += n


def norm_proj(x, g, w_bf16, small_params, widths, tm):
    m, d = x.shape
    n_total = sum(widths)
    assert w_bf16.shape == (d, n_total) and m % tm == 0
    out_shape = [jax.ShapeDtypeStruct((m, n), F32) for n in widths]
    return pl.pallas_call(
        functools.partial(_proj_body, widths=tuple(widths)),
        grid=(m // tm,),
        in_specs=[pl.BlockSpec((tm, d), lambda i: (i, 0)),
                  pl.BlockSpec((1, d), lambda i: (0, 0)),
                  pl.BlockSpec((d, n_total), lambda i: (0, 0)),
                  pl.BlockSpec((8, LANE), lambda i: (0, 0))],
        out_specs=[pl.BlockSpec((tm, n), lambda i: (i, 0)) for n in widths],
        out_shape=out_shape,
        compiler_params=_cparams(("parallel",)),
        name="norm_proj",
    )(x, g.reshape(1, d), w_bf16, small_params)


def _outproj_body(x_ref, a1_ref, a2_ref, w_ref, o_ref):
    k1 = a1_ref.shape[1]
    y = _dot(a1_ref[...].astype(BF16), w_ref[0:k1, :]) + _dot(a2_ref[...].astype(BF16), w_ref[k1:, :])
    o_ref[...] = x_ref[...] + y


def out_proj_residual(x, a1, a2, w_bf16, tm):
    m, d = x.shape
    k1, k2 = a1.shape[1], a2.shape[1]
    return pl.pallas_call(
        _outproj_body,
        grid=(m // tm,),
        in_specs=[pl.BlockSpec((tm, d), lambda i: (i, 0)),
                  pl.BlockSpec((tm, k1), lambda i: (i, 0)),
                  pl.BlockSpec((tm, k2), lambda i: (i, 0)),
                  pl.BlockSpec((k1 + k2, d), lambda i: (0, 0))],
        out_specs=pl.BlockSpec((tm, d), lambda i: (i, 0)),
        out_shape=jax.ShapeDtypeStruct((m, d), F32),
        compiler_params=_cparams(("parallel",)),
        name="out_proj",
    )(x, a1, a2, w_bf16)


def _mlp_body(x_ref, g_ref, wu_ref, wd_ref, gf_ref, o_ref, hn_scr, acc_scr, *, final_norm):
    j = pl.program_id(1)

    @pl.when(j == 0)
    def _():
        x = x_ref[...]
        hn_scr[...] = (x * lax.rsqrt(jnp.mean(x * x, axis=-1, keepdims=True) + EPS) * g_ref[...]).astype(BF16)
        acc_scr[...] = jnp.zeros_like(acc_scr)

    u = jnp.maximum(_dot(hn_scr[...], wu_ref[...]), 0.0)
    acc_scr[...] += _dot((u * u).astype(BF16), wd_ref[...])

    @pl.when(j == pl.num_programs(1) - 1)
    def _():
        y = x_ref[...] + acc_scr[...]
        if final_norm:
            y = y * lax.rsqrt(jnp.mean(y * y, axis=-1, keepdims=True) + EPS) * gf_ref[...]
        o_ref[...] = y


def mlp_residual(x, g, w_up_bf16, w_down_bf16, g_final, final_norm, tm, tf):
    m, d = x.shape
    f = w_up_bf16.shape[1]
    return pl.pallas_call(
        functools.partial(_mlp_body, final_norm=final_norm),
        grid=(m // tm, f // tf),
        in_specs=[pl.BlockSpec((tm, d), lambda i, j: (i, 0)),
                  pl.BlockSpec((1, d), lambda i, j: (0, 0)),
                  pl.BlockSpec((d, tf), lambda i, j: (0, j)),
                  pl.BlockSpec((tf, d), lambda i, j: (j, 0)),
                  pl.BlockSpec((1, d), lambda i, j: (0, 0))],
        out_specs=pl.BlockSpec((tm, d), lambda i, j: (i, 0)),
        out_shape=jax.ShapeDtypeStruct((m, d), F32),
        scratch_shapes=[pltpu.VMEM((tm, d), BF16), pltpu.VMEM((tm, d), F32)],
        compiler_params=_cparams(("parallel", "arbitrary")),
        name="mlp",
    )(x, g.reshape(1, d), w_up_bf16, w_down_bf16, g_final.reshape(1, d))


def _tri(n):
    r = _iota((n, n), 0)
    c = _iota((n, n), 1)
    return r, c


def _chunk_dots(L):
    if L % 16 == 0:
        cast = lambda f: (lambda a, b: f(a.astype(BF16), b.astype(BF16)))
        return cast(_dot), cast(_dot_nt), cast(_dot_tn)
    full = lambda f: (lambda a, b: f(a, b, HI))
    return full(_dot), full(_dot_nt), full(_dot_tn)


def _batch_groups(Bb, unroll, one):
    def group(gi, carry):
        for u in range(unroll):
            one(gi * unroll + u)
        return carry

    if Bb == unroll:
        group(0, 0)
    else:
        lax.fori_loop(0, Bb // unroll, group, 0)


def _mlstm_chunk(bb, q_ref, k_ref, v_ref, o_ref, sc_ref, sr_ref, nw_ref, h_ref, c_ref, n_ref, m_ref, L):
    nn, nt, tn = _chunk_dots(L)
    rows, cols = _tri(L)
    lower = rows >= cols
    tril = lower.astype(F32)
    triu = (rows <= cols).astype(F32)
    sc = sc_ref[bb]
    sr = sr_ref[bb, 0]
    b_col = _dot(tril, sc[:, 4:8], HI)
    b_row = _dot(sr[4:8, :], triu, HI)
    m_vec = m_ref[bb]
    lane4 = _iota((1, MI_H), 1)
    for h in range(MI_H):
        q = q_ref[bb, h]
        k = k_ref[bb, h] * (MI_DQK ** -0.5)
        v = v_ref[bb, :, h * MI_DV:(h + 1) * MI_DV]
        bc = b_col[:, h:h + 1]
        br = b_row[h:h + 1, :]
        li_r = sr[h:h + 1, :]
        li_c = sc[:, h:h + 1]
        m_prev = m_vec[:, h:h + 1]
        c_prev = c_ref[bb, h]
        n_prev = n_ref[bb, h]
        d = jnp.where(lower, bc - br + li_r, NEG_BIG)
        inter = bc + m_prev
        m_t = jnp.maximum(inter, jnp.max(d, axis=1, keepdims=True))
        s = nt(q, k) * jnp.exp(d - m_t)
        a_inter = jnp.exp(inter - m_t)
        num = nn(s, v) + a_inter * nn(q, c_prev)
        den = jnp.sum(s, axis=1, keepdims=True) + a_inter * jnp.sum(q * n_prev, axis=1, keepdims=True)
        hh = num / jnp.maximum(jnp.abs(den), jnp.exp(-m_t))
        b_last = bc[L - 1:L, :]
        g_col = b_last - bc + li_c
        m_new = jnp.maximum(b_last + m_prev, jnp.max(g_col, axis=0, keepdims=True))
        a_prev = jnp.exp(b_last + m_prev - m_new)
        kw = k * jnp.exp(g_col - m_new)
        c_ref[bb, h] = a_prev * c_prev + tn(kw, v)
        n_ref[bb, h] = a_prev * n_prev + jnp.sum(kw, axis=0, keepdims=True)
        m_vec = jnp.where(lane4 == h, m_new, m_vec)
        hn = hh * lax.rsqrt(jnp.mean(hh * hh, axis=-1, keepdims=True) + EPS) * nw_ref[h:h + 1, :]
        gate = _sigmoid(o_ref[bb, :, h * MI_DV:(h + 1) * MI_DV])
        h_ref[bb, :, h * MI_DV:(h + 1) * MI_DV] = hn * gate
    m_ref[bb] = m_vec


def _mlstm_body(q_ref, k_ref, v_ref, o_ref, sc_ref, sr_ref, c0_ref, n0_ref, m0_ref, nw_ref,
                h_ref, c_ref, n_ref, m_ref, *, L, Bb, unroll):
    @pl.when(pl.program_id(1) == 0)
    def _():
        c_ref[...] = c0_ref[...]
        n_ref[...] = n0_ref[...]
        m_ref[...] = m0_ref[...]

    _batch_groups(Bb, unroll, lambda bb: _mlstm_chunk(
        bb, q_ref, k_ref, v_ref, o_ref, sc_ref, sr_ref, nw_ref, h_ref, c_ref, n_ref, m_ref, L))


def mlstm(q, k, v, o, sc, sr, c0, n0, m0, norm_w, L, Bb, unroll):
    B, H, T, _ = q.shape
    nc = T // L
    hv = H * MI_DV
    bmap = lambda b, c: (b, 0, 0, 0)
    return pl.pallas_call(
        functools.partial(_mlstm_body, L=L, Bb=Bb, unroll=unroll),
        grid=(B // Bb, nc),
        in_specs=[pl.BlockSpec((Bb, H, L, MI_DQK), lambda b, c: (b, 0, c, 0)),
                  pl.BlockSpec((Bb, H, L, MI_DQK), lambda b, c: (b, 0, c, 0)),
                  pl.BlockSpec((Bb, L, hv), lambda b, c: (b, c, 0)),
                  pl.BlockSpec((Bb, L, hv), lambda b, c: (b, c, 0)),
                  pl.BlockSpec((Bb, L, LANE), lambda b, c: (b, c, 0)),
                  pl.BlockSpec((Bb, 1, 8, L), lambda b, c: (b, c, 0, 0)),
                  pl.BlockSpec((Bb, H, MI_DQK, MI_DV), bmap),
                  pl.BlockSpec((Bb, H, 1, MI_DQK), bmap),
                  pl.BlockSpec((Bb, 1, H), lambda b, c: (b, 0, 0)),
                  pl.BlockSpec((H, MI_DV), lambda b, c: (0, 0))],
        out_specs=[pl.BlockSpec((Bb, L, hv), lambda b, c: (b, c, 0)),
                   pl.BlockSpec((Bb, H, MI_DQK, MI_DV), bmap),
                   pl.BlockSpec((Bb, H, 1, MI_DQK), bmap),
                   pl.BlockSpec((Bb, 1, H), lambda b, c: (b, 0, 0))],
        out_shape=[jax.ShapeDtypeStruct((B, T, hv), F32),
                   jax.ShapeDtypeStruct((B, H, MI_DQK, MI_DV), F32),
                   jax.ShapeDtypeStruct((B, H, 1, MI_DQK), F32),
                   jax.ShapeDtypeStruct((B, 1, H), F32)],
        compiler_params=_cparams(("parallel", "arbitrary")),
        name="mlstm",
    )(q, k, v, o, sc, sr, c0, n0, m0, norm_w)


def _gdn_chunk(bb, x_ref, z_ref, sc_ref, sr_ref, conv0_ref, cw_ref, nw_ref,
               o_ref, s_ref, conv_ref, xp_scr, L):
    base = 8 - (GD_CONV - 1)
    nn, nt, tn = _chunk_dots(L)
    xp_scr[bb, 8:8 + L, :] = x_ref[bb]
    y = xp_scr[bb, base:base + L, :] * cw_ref[0:1, :]
    for j in range(1, GD_CONV):
        y = y + xp_scr[bb, base + j:base + j + L, :] * cw_ref[j:j + 1, :]
    tail = xp_scr[bb, L + base:L + 8, :]
    xp_scr[bb, base:8, :] = tail
    conv_ref[bb] = tail
    y = _silu(y)

    rows, cols = _tri(L)
    incl = rows >= cols
    strict = rows > cols
    eye = (rows == cols).astype(F32)
    tril = incl.astype(F32)
    triu = (rows <= cols).astype(F32)
    sc = sc_ref[bb]
    sr = sr_ref[bb, 0]
    gc_col = _dot(tril, sc[:, 12:16], HI)
    gc_row = _dot(sr[0:4, :], triu, HI)
    kw = GD_H * GD_DK
    for h in range(GD_H):
        qh = y[:, h * GD_DK:(h + 1) * GD_DK]
        kh = y[:, kw + h * GD_DK:kw + (h + 1) * GD_DK]
        vh = y[:, 2 * kw + h * GD_DV:2 * kw + (h + 1) * GD_DV]
        qh = qh * lax.rsqrt(jnp.sum(qh * qh, axis=-1, keepdims=True) + EPS) * (GD_DK ** -0.5)
        kh = kh * lax.rsqrt(jnp.sum(kh * kh, axis=-1, keepdims=True) + EPS)
        beta = sc[:, 8 + h:9 + h]
        gcc = gc_col[:, h:h + 1]
        gcr = gc_row[h:h + 1, :]
        dec = jnp.exp(jnp.where(incl, gcc - gcr, NEG_BIG))
        kb = kh * beta
        neg_a = -(nt(kb, kh) * jnp.where(strict, dec, 0.0))
        tinv = eye + neg_a
        pw = neg_a
        for _ in range(int(math.log2(L)) - 1):
            pw = _dot(pw, pw, HI)
            tinv = tinv + _dot(tinv, pw, HI)
        egc = jnp.exp(gcc)
        u = _dot(tinv, vh * beta, HI)
        w = _dot(tinv, kb * egc, HI)
        attn = nt(qh, kh) * dec
        g_last = gcc[L - 1:L, :]
        s_prev = s_ref[bb, h]
        v_new = u - nn(w, s_prev)
        o = nn(qh * egc, s_prev) + nn(attn, v_new)
        s_ref[bb, h] = jnp.exp(g_last) * s_prev + tn(kh * jnp.exp(g_last - gcc), v_new)
        on = o * lax.rsqrt(jnp.mean(o * o, axis=-1, keepdims=True) + EPS) * nw_ref[...]
        o_ref[bb, :, h * GD_DV:(h + 1) * GD_DV] = on * _silu(z_ref[bb, :, h * GD_DV:(h + 1) * GD_DV])


def _gdn_body(x_ref, z_ref, sc_ref, sr_ref, s0_ref, conv0_ref, cw_ref, nw_ref,
              o_ref, s_ref, conv_ref, xp_scr, *, L, Bb, unroll):
    @pl.when(pl.program_id(1) == 0)
    def _():
        s_ref[...] = s0_ref[...]
        xp_scr[:, 8 - (GD_CONV - 1):8, :] = conv0_ref[...]

    _batch_groups(Bb, unroll, lambda bb: _gdn_chunk(
        bb, x_ref, z_ref, sc_ref, sr_ref, conv0_ref, cw_ref, nw_ref, o_ref, s_ref, conv_ref, xp_scr, L))


def gdn(x, z, sc, sr, s0, conv0, conv_w, norm_w, L, Bb, unroll):
    B, T, ch = x.shape
    H = GD_H
    nc = T // L
    hv = H * GD_DV
    bmap = lambda b, c: (b, 0, 0, 0)
    return pl.pallas_call(
        functools.partial(_gdn_body, L=L, Bb=Bb, unroll=unroll),
        grid=(B // Bb, nc),
        in_specs=[pl.BlockSpec((Bb, L, ch), lambda b, c: (b, c, 0)),
                  pl.BlockSpec((Bb, L, hv), lambda b, c: (b, c, 0)),
                  pl.BlockSpec((Bb, L, LANE), lambda b, c: (b, c, 0)),
                  pl.BlockSpec((Bb, 1, 8, L), lambda b, c: (b, c, 0, 0)),
                  pl.BlockSpec((Bb, H, GD_DK, GD_DV), bmap),
                  pl.BlockSpec((Bb, GD_CONV - 1, ch), lambda b, c: (b, 0, 0)),
                  pl.BlockSpec((GD_CONV, ch), lambda b, c: (0, 0)),
                  pl.BlockSpec((1, GD_DV), lambda b, c: (0, 0))],
        out_specs=[pl.BlockSpec((Bb, L, hv), lambda b, c: (b, c, 0)),
                   pl.BlockSpec((Bb, H, GD_DK, GD_DV), bmap),
                   pl.BlockSpec((Bb, GD_CONV - 1, ch), lambda b, c: (b, 0, 0))],
        out_shape=[jax.ShapeDtypeStruct((B, T, hv), F32),
                   jax.ShapeDtypeStruct((B, H, GD_DK, GD_DV), F32),
                   jax.ShapeDtypeStruct((B, GD_CONV - 1, ch), F32)],
        scratch_shapes=[pltpu.VMEM((Bb, L + 8, ch), F32)],
        compiler_params=_cparams(("parallel", "arbitrary")),
        name="gdn",
    )(x, z, sc, sr, s0, conv0, conv_w, norm_w.reshape(1, GD_DV))


def _compress_body(x_ref, pos_ref, w1_ref, w2_ref, o_ref):
    Bb, _, R, half = x_ref.shape
    x = x_ref[:, 0].reshape(Bb * R, half).astype(F32)
    ua = _dot((x + pos_ref[0, 0:1, :]).astype(BF16), w1_ref[0, 0])
    ub = _dot((x + pos_ref[0, 1:2, :]).astype(BF16), w1_ref[0, 1])
    h = _silu(ua + pltpu.roll(ub, Bb * R - 1, 0))
    o_ref[:, 0, 0:R, :] = _dot(h.astype(BF16), w2_ref[0]).reshape(Bb, R, HD)
    rp = o_ref.shape[2]
    if rp > R:
        o_ref[:, 0, R:rp, :] = jnp.zeros((Bb, rp - R, HD), F32)


def nsa_compress(xr, R, pos, w1, w2, Bb):
    B = xr.shape[0]
    half = CMP_STRIDE * HD
    rp = -(-R // LANE) * LANE
    return pl.pallas_call(
        _compress_body,
        grid=(4, B // Bb),
        in_specs=[pl.BlockSpec((Bb, 1, R, half), lambda c, b: (b, c, 0, 0)),
                  pl.BlockSpec((1, 2, half), lambda c, b: (c // 2, 0, 0)),
                  pl.BlockSpec((1, 2, half, CMP_HIDDEN), lambda c, b: (c // 2, 0, 0, 0)),
                  pl.BlockSpec((1, CMP_HIDDEN, HD), lambda c, b: (c // 2, 0, 0))],
        out_specs=pl.BlockSpec((Bb, 1, rp, HD), lambda c, b: (b, c, 0, 0)),
        out_shape=jax.ShapeDtypeStruct((B, 4, rp, HD), F32),
        compiler_params=_cparams(("parallel", "parallel")),
        name="nsa_compress",
    )(xr, pos.reshape(2, 2, half), w1.reshape(2, 2, half, CMP_HIDDEN).astype(BF16), w2.astype(BF16))


def _cumsum_body(x_ref, o_ref):
    Bb, H, T = x_ref.shape
    rows, cols = _tri(LANE)
    triu = (rows <= cols).astype(F32)
    carry = jnp.zeros((Bb * H, 1), F32)
    for c in range(T // LANE):
        seg = x_ref[:, :, c * LANE:(c + 1) * LANE].reshape(Bb * H, LANE)
        loc = _dot(seg, triu, HI) + carry
        o_ref[:, :, c * LANE:(c + 1) * LANE] = loc.reshape(Bb, H, LANE)
        carry = loc[:, LANE - 1:LANE]


def cumsum_lanes(x, Bb):
    B, H, T = x.shape
    return pl.pallas_call(
        _cumsum_body,
        grid=(B // Bb,),
        in_specs=[pl.BlockSpec((Bb, H, T), lambda b: (b, 0, 0))],
        out_specs=pl.BlockSpec((Bb, H, T), lambda b: (b, 0, 0)),
        out_shape=jax.ShapeDtypeStruct((B, H, T), F32),
        compiler_params=_cparams(("parallel",)),
        name="cumsum",
    )(x)


def _flash_tile(s_blocks, v, m_scr, l_scr, acc_scr):
    dv = acc_scr.shape[-1]
    m_prev = m_scr[...]
    mx = s_blocks[0]
    for sb in s_blocks[1:]:
        mx = jnp.maximum(mx, sb)
    m_new = jnp.maximum(m_prev, jnp.max(mx, axis=1, keepdims=True))
    p_blocks = [jnp.exp(sb - m_new) for sb in s_blocks]
    sm = p_blocks[0]
    for pb in p_blocks[1:]:
        sm = sm + pb
    alpha = jnp.exp(m_prev - m_new)
    l_scr[...] = alpha * l_scr[...] + jnp.sum(sm, axis=1, keepdims=True)
    p = (jnp.concatenate(p_blocks, axis=1) if len(p_blocks) > 1 else p_blocks[0]).astype(BF16)
    acc_scr[...] = alpha[:, :dv] * acc_scr[...] + _dot(p, v)
    m_scr[...] = m_new


def _flash_reset(m_scr, l_scr, acc_scr):
    m_scr[...] = jnp.full_like(m_scr, NEG_BIG)
    l_scr[...] = jnp.zeros_like(l_scr)
    acc_scr[...] = jnp.zeros_like(acc_scr)


def _lane_blocks(s):
    return [s[:, i * LANE:(i + 1) * LANE] for i in range(s.shape[1] // LANE)]


def _fox_body(q_ref, k_ref, v_ref, fq_ref, fk_ref, o_ref, m_scr, l_scr, acc_scr, *, tq, tk):
    qi = pl.program_id(1)
    j = pl.program_id(2)
    top = (qi * tq + tq - 1) // tk

    @pl.when(j == 0)
    def _():
        _flash_reset(m_scr, l_scr, acc_scr)

    def tile(diag):
        fq = fq_ref[0]
        fk = fk_ref[0]
        if diag:
            mask = (top * tk + _iota((tq, tk), 1)) <= (qi * tq + _iota((tq, tk), 0))
        for h in range(FOX_H):
            s = _dot_nt(q_ref[0, h], k_ref[0, h]) + fq[:, h:h + 1] - fk[h:h + 1, :]
            if diag:
                s = jnp.where(mask, s, NEG_BIG)
            _flash_tile(_lane_blocks(s), v_ref[0, h], m_scr.at[h], l_scr.at[h], acc_scr.at[h])

    @pl.when(j == 0)
    def _():
        tile(True)

    @pl.when(jnp.logical_and(j > 0, j <= top))
    def _():
        tile(False)

    @pl.when(j == pl.num_programs(2) - 1)
    def _():
        for h in range(FOX_H):
            o_ref[0, h] = acc_scr[h] / jnp.maximum(l_scr[h][:, :HD], 1e-30)


def fox_prompt(q, kv, fq, fk, tq, tk):
    B, H, T, _ = q.shape
    kmap = lambda i, j: jnp.maximum((i * tq + tq - 1) // tk - j, 0)
    return pl.pallas_call(
        functools.partial(_fox_body, tq=tq, tk=tk),
        grid=(B, T // tq, T // tk),
        in_specs=[pl.BlockSpec((1, H, tq, HD), lambda b, i, j: (b, 0, i, 0)),
                  pl.BlockSpec((1, H, tk, HD), lambda b, i, j: (b, 0, kmap(i, j), 0)),
                  pl.BlockSpec((1, H, tk, HD), lambda b, i, j: (b, 1, kmap(i, j), 0)),
                  pl.BlockSpec((1, tq, H), lambda b, i, j: (b, i, 0)),
                  pl.BlockSpec((1, H, tk), lambda b, i, j: (b, 0, kmap(i, j)))],
        out_specs=pl.BlockSpec((1, H, tq, HD), lambda b, i, j: (b, 0, i, 0)),
        out_shape=jax.ShapeDtypeStruct((B, H, T, HD), F32),
        scratch_shapes=[pltpu.VMEM((H, tq, LANE), F32), pltpu.VMEM((H, tq, LANE), F32), pltpu.VMEM((H, tq, HD), F32)],
        compiler_params=_cparams(("parallel", "parallel", "arbitrary")),
        name="fox_prompt",
    )(q, kv, kv, fq, fk)


def _fox_decode_body(pt_ref, q_ref, newkv_ref, newlf_ref, *refs, n_pages, tn):
    kv_refs = refs[:n_pages]
    lf_refs = refs[n_pages:2 * n_pages]
    o_ref = refs[2 * n_pages]
    hw = FOX_H * HD
    R = FOX_H * tn
    q = q_ref[0]
    qrep = jnp.concatenate([q] * FOX_H, axis=0)
    blockmask = (_iota((R, hw), 0) // tn) == (_iota((R, hw), 1) // HD)
    qbd = jnp.where(blockmask, qrep, 0.0).astype(BF16)
    rows, cols = _tri(PAGE)
    triu = (rows <= cols).astype(F32)
    expand = lambda a: jnp.concatenate([jnp.broadcast_to(a[h:h + 1, :], (tn, a.shape[1])) for h in range(FOX_H)], 0)
    carry_c = jnp.zeros((FOX_H, 1), F32)
    carry_r = jnp.zeros((1, FOX_H), F32)
    s_tiles = []
    for pg in range(n_pages):
        lf = lf_refs[pg][0]
        f_t = _dot_tn(lf, triu, HI) + carry_c
        carry_c = f_t[:, PAGE - 1:PAGE]
        carry_r = carry_r + jnp.sum(lf, axis=0, keepdims=True)
        k = kv_refs[pg][0, :, 0:hw].astype(BF16)
        s_tiles.append(_dot_nt(qbd, k) * (HD ** -0.5) - expand(f_t))
    lfn = newlf_ref[0]
    r8, c8 = _tri(tn)
    fq_c = _dot((r8 >= c8).astype(F32), lfn, HI) + carry_r
    fq_t = _dot_tn(lfn, (r8 <= c8).astype(F32), HI) + carry_c
    fq_rows = jnp.concatenate([fq_c[:, h:h + 1] for h in range(FOX_H)], axis=0)
    kn = newkv_ref[0, :, 0:hw]
    s_new = _dot_nt(qbd.astype(F32), kn, HI) * (HD ** -0.5) - expand(fq_t)
    causal = _iota((R, tn), 1) <= (_iota((R, tn), 0) % tn)
    s_new = jnp.where(causal, s_new + fq_rows, NEG_BIG)
    s_tiles = [s + fq_rows for s in s_tiles]
    m = jnp.max(s_new, axis=1, keepdims=True)
    for s in s_tiles:
        m = jnp.maximum(m, jnp.max(s, axis=1, keepdims=True))
    p_new = jnp.where(causal, jnp.exp(s_new - m), 0.0)
    l = jnp.sum(p_new, axis=1, keepdims=True)
    acc = _dot(p_new, newkv_ref[0, :, hw:2 * hw], HI)
    for pg, s in enumerate(s_tiles):
        p = jnp.exp(s - m)
        l = l + jnp.sum(p, axis=1, keepdims=True)
        acc = acc + _dot(p.astype(BF16), kv_refs[pg][0, :, hw:2 * hw].astype(BF16))
    acc = acc / jnp.maximum(l, 1e-30)
    o_ref[0] = jnp.concatenate([acc[h * tn:(h + 1) * tn, h * HD:(h + 1) * HD] for h in range(FOX_H)], axis=1)


def fox_decode(page_table, q, newkv, newlf, kv_pool, lf_pool):
    B, tn, hw = q.shape
    n_pages = page_table.shape[1]
    page_spec = lambda width, pg: pl.BlockSpec((1, PAGE, width), lambda b, pt: (pt[b, pg], 0, 0))
    grid_spec = pltpu.PrefetchScalarGridSpec(
        num_scalar_prefetch=1,
        grid=(B,),
        in_specs=[pl.BlockSpec((1, tn, hw), lambda b, pt: (b, 0, 0)),
                  pl.BlockSpec((1, tn, 2 * hw), lambda b, pt: (b, 0, 0)),
                  pl.BlockSpec((1, tn, FOX_H), lambda b, pt: (b, 0, 0))]
                 + [page_spec(2 * hw, pg) for pg in range(n_pages)]
                 + [page_spec(FOX_H, pg) for pg in range(n_pages)],
        out_specs=pl.BlockSpec((1, tn, hw), lambda b, pt: (b, 0, 0)),
    )
    return pl.pallas_call(
        functools.partial(_fox_decode_body, n_pages=n_pages, tn=tn),
        grid_spec=grid_spec,
        out_shape=jax.ShapeDtypeStruct((B, tn, hw), F32),
        compiler_params=_cparams(("arbitrary",)),
        name="fox_decode",
    )(page_table, q, newkv, newlf, *([kv_pool] * n_pages), *([lf_pool] * n_pages))


def _t5_bucket(dist):
    n = jnp.maximum(dist, 0)
    nf = jnp.maximum(n, 1).astype(F32)
    large = BUCKET_EXACT + (jnp.log(nf / BUCKET_EXACT) / math.log(MAX_DISTANCE / BUCKET_EXACT)
                            * (N_BUCKETS - BUCKET_EXACT)).astype(jnp.int32)
    return jnp.where(n < BUCKET_EXACT, n, jnp.minimum(large, N_BUCKETS - 1))


def _bias_from_bucket(bucket, tbl_ref, head):
    out = jnp.zeros(bucket.shape, F32)
    for kk in range(N_BUCKETS):
        out = jnp.where(bucket == kk, tbl_ref[kk, head], out)
    return out


FOX_TQ, FOX_TK = 256, 512
NSA_TK = 2 * LANE
NSA_FAR_GROUP = 4
NSA_BIAS_TILES = (BUCKET_SAT_DIST + NSA_TK + LANE - 1) // LANE


def _nsa_body(tbl_ref, q_ref, gate_ref, kc_ref, vc_ref, ks_ref, vs_ref, kw_ref, vw_ref, smap_ref, o_ref,
              bias_scr, score_scr, m_scr, l_scr, acc_scr, *, tq, q_pos0, win_pos0, n_sel, ncp, tw):
    g = pl.program_id(1)
    qi = pl.program_id(2)
    q0 = q_pos0 + qi * tq
    R = NSA_R * tq
    scale = HD ** -0.5
    last_bias = tuple(tbl_ref[N_BUCKETS - 1, g * NSA_R + r] for r in range(NSA_R))

    @pl.when(qi == 0)
    def _():
        ii = _iota((tq, NSA_TK), 0)
        jj = _iota((tq, NSA_TK), 1)
        for dd in range(NSA_BIAS_TILES):
            bucket = _t5_bucket(ii - jj + dd * LANE)
            for r in range(NSA_R):
                bias_scr[dd, r * tq:(r + 1) * tq, :] = (
                    _bias_from_bucket(bucket, tbl_ref, g * NSA_R + r) - last_bias[r])
        bias_scr[NSA_BIAS_TILES] = jnp.zeros((R, NSA_TK), F32)

    t_col = q0 + _iota((tq, 1), 0)

    bias_tiles, mask_tiles = [], []
    for nt in range(ncp // LANE):
        c_end = (nt * LANE + _iota((tq, LANE), 1)) * CMP_STRIDE + (CMP_BLOCK - 1)
        dist = t_col - c_end
        max_dist = q0 + tq - 1 - (nt * LANE * CMP_STRIDE + CMP_BLOCK - 1)
        min_dist = q0 - ((nt * LANE + LANE - 1) * CMP_STRIDE + CMP_BLOCK - 1)
        special = jnp.logical_and(max_dist >= 0, min_dist < BUCKET_SAT_DIST)

        def general(dist=dist):
            bucket = _t5_bucket(dist)
            return jnp.stack([_bias_from_bucket(bucket, tbl_ref, g * NSA_R + r) - last_bias[r]
                              for r in range(NSA_R)])

        def saturated():
            return jnp.zeros((NSA_R, tq, LANE), F32)

        bias_tiles.append(lax.cond(special, general, saturated))
        mask_tiles.append(dist >= 0)
    mask_c = jnp.concatenate(mask_tiles, axis=1) if len(mask_tiles) > 1 else mask_tiles[0]
    kc = kc_ref[0, 0].astype(BF16)
    vc = vc_ref[0, 0].astype(BF16)
    pcsum = jnp.zeros((tq, ncp), F32)
    o_c = []
    for r in range(NSA_R):
        bias_r = jnp.concatenate([b[r] for b in bias_tiles], axis=1) if len(bias_tiles) > 1 else bias_tiles[0][r]
        s = _dot_nt(q_ref[0, r].astype(BF16), kc) * scale + bias_r
        s = jnp.where(mask_c, s, NEG_BIG)
        m = jnp.max(s, axis=1, keepdims=True)
        p = jnp.where(mask_c, jnp.exp(s - m), 0.0)
        pc = p / jnp.maximum(jnp.sum(p, axis=1, keepdims=True), 1e-30)
        o_c.append(_dot(pc.astype(BF16), vc))
        pcsum = pcsum + pc

    ps_t = _dot_nt(smap_ref[...], pcsum, HI)
    j_col = _iota((LANE, 1), 0)
    t_row = q0 + _iota((1, tq), 1)
    cur = lax.shift_right_logical(t_row, int(math.log2(SEL_BLOCK)))
    score = jnp.where(j_col * SEL_BLOCK <= t_row, ps_t, -1.0)
    score = jnp.where(j_col == cur - 1, FORCE_SCORE, score)
    score = jnp.where(j_col == cur, FORCE_SCORE, score)
    score = jnp.where(j_col == 0, FORCE_SCORE, score)
    score = jnp.where(j_col < n_sel, score, -3e38)
    score_scr[...] = score

    def rank_body(jp, rank):
        row = score_scr[pl.ds(jp, 1), :]
        tie = jnp.where(j_col > jp, 1.0, 0.0)
        return rank + jnp.where(row > score, 1.0, jnp.where(row == score, tie, 0.0))

    n_rank = jnp.minimum((q0 + tq - 1) // SEL_BLOCK + 1, LANE)
    rank = lax.fori_loop(0, n_rank, rank_body, jnp.zeros((LANE, tq), F32))
    sel_t = jnp.where(rank < SEL_TOPN, 1.0, 0.0).astype(BF16)
    eye = (_iota((tq, tq), 0) == _iota((tq, tq), 1)).astype(BF16)
    sel = _dot_nt(eye, sel_t).astype(BF16)

    qs = (q_ref[0].reshape(R, HD) * scale).astype(BF16)
    t_tile = q0 + _iota((tq, NSA_TK), 0)
    c_tile = _iota((tq, NSA_TK), 1)
    log2_blk = int(math.log2(SEL_BLOCK))

    def stack(a):
        return jnp.concatenate([a] * NSA_R, axis=0)

    def bias_tile(offset):
        return bias_scr[jnp.minimum(lax.shift_right_logical(offset, 7), NSA_BIAS_TILES)]

    def sel_tile(k0, width, near):
        k = ks_ref[0, 0, pl.ds(k0, width), :]
        v = vs_ref[0, 0, pl.ds(k0, width), :]
        blk = lax.shift_right_logical(k0 + _iota((LANE, width), 1), log2_blk)
        expand = jnp.where(_iota((LANE, width), 0) == blk, 1.0, 0.0).astype(BF16)
        chosen = _dot(sel, expand)
        s = _dot_nt(qs, k)
        if near:
            chosen = jnp.where(t_tile - (k0 + c_tile) >= 0, chosen, 0.0)
            s = s + bias_tile(q0 - k0)
        s = jnp.where(stack(chosen) > 0.5, s, NEG_BIG)
        _flash_tile(_lane_blocks(s), v, m_scr, l_scr, acc_scr)

    _flash_reset(m_scr, l_scr, acc_scr)
    kt_top = (q0 + tq - 1) // NSA_TK
    sel_tile(pl.multiple_of(kt_top * NSA_TK, NSA_TK), NSA_TK, True)

    @pl.when(kt_top >= 1)
    def _():
        sel_tile(pl.multiple_of((kt_top - 1) * NSA_TK, NSA_TK), NSA_TK, True)

    n_far = jnp.maximum(kt_top - 1, 0)
    n_groups = n_far // NSA_FAR_GROUP

    def sel_far_group(gi, carry):
        sel_tile(pl.multiple_of(gi * (NSA_FAR_GROUP * NSA_TK), NSA_FAR_GROUP * NSA_TK), NSA_FAR_GROUP * NSA_TK, False)
        return carry

    def sel_far(kt, carry):
        sel_tile(pl.multiple_of(kt * NSA_TK, NSA_TK), NSA_TK, False)
        return carry

    lax.fori_loop(0, n_groups, sel_far_group, 0)
    lax.fori_loop(n_groups * NSA_FAR_GROUP, n_far, sel_far, 0)
    o_s = acc_scr[...] / jnp.maximum(l_scr[...][:, :HD], 1e-30)

    def win_tile(i, carry):
        k0 = pl.multiple_of((wt_top - i) * NSA_TK, NSA_TK)
        k = kw_ref[0, 0, pl.ds(k0, NSA_TK), :]
        v = vw_ref[0, 0, pl.ds(k0, NSA_TK), :]
        dist = t_tile - (win_pos0 + k0 + c_tile)
        ok = jnp.where(dist >= 0, jnp.where(dist < WINDOW, 1.0, 0.0), 0.0)
        s = _dot_nt(qs, k) + bias_tile(q0 - win_pos0 - k0)
        s = jnp.where(stack(ok) > 0.5, s, NEG_BIG)
        _flash_tile(_lane_blocks(s), v, m_scr, l_scr, acc_scr)
        return carry

    _flash_reset(m_scr, l_scr, acc_scr)
    wt_top = (jnp.minimum(q0 + tq - win_pos0, tw) - 1) // NSA_TK
    wt_lo = jnp.maximum(q0 - (WINDOW - 1) - win_pos0, 0) // NSA_TK
    lax.fori_loop(0, wt_top - wt_lo + 1, win_tile, 0)
    o_w = acc_scr[...] / jnp.maximum(l_scr[...][:, :HD], 1e-30)

    gates = gate_ref[0, 0]
    for r in range(NSA_R):
        o_ref[0, r] = (gates[:, 3 * r:3 * r + 1] * o_c[r]
                       + gates[:, 3 * r + 1:3 * r + 2] * o_s[r * tq:(r + 1) * tq]
                       + gates[:, 3 * r + 2:3 * r + 3] * o_w[r * tq:(r + 1) * tq])


def _selection_overlap_t(ncp):
    c_start = np.arange(ncp)[None, :] * CMP_STRIDE
    s_start = np.arange(LANE)[:, None] * SEL_BLOCK
    return ((c_start < s_start + SEL_BLOCK) & (c_start + CMP_BLOCK > s_start)).astype(np.float32)


def nsa_attend(tbl, q, gates, kcvc, sel_arr, sel_off, win_arr, win_off, *, tq, q_pos0, win_pos0, n_sel):
    B, _, Tq, _ = q.shape
    ncp = kcvc.shape[2]
    tks = sel_arr.shape[2]
    tw = win_arr.shape[2]
    R = NSA_R * tq
    smap = jnp.asarray(_selection_overlap_t(ncp))
    kv_spec = lambda rows, off: pl.BlockSpec((1, 1, rows, HD), lambda b, g, i: (b, off + g, 0, 0))
    return pl.pallas_call(
        functools.partial(_nsa_body, tq=tq, q_pos0=q_pos0, win_pos0=win_pos0, n_sel=n_sel, ncp=ncp, tw=tw),
        grid=(B, NSA_G, Tq // tq),
        in_specs=[pl.BlockSpec(memory_space=pltpu.SMEM),
                  pl.BlockSpec((1, NSA_R, tq, HD), lambda b, g, i: (b, g, i, 0)),
                  pl.BlockSpec((1, 1, tq, 3 * NSA_R), lambda b, g, i: (b, g, i, 0)),
                  kv_spec(ncp, 0), kv_spec(ncp, 2),
                  kv_spec(tks, sel_off), kv_spec(tks, sel_off + 2),
                  kv_spec(tw, win_off), kv_spec(tw, win_off + 2),
                  pl.BlockSpec((LANE, ncp), lambda b, g, i: (0, 0))],
        out_specs=pl.BlockSpec((1, NSA_R, tq, HD), lambda b, g, i: (b, g, i, 0)),
        out_shape=jax.ShapeDtypeStruct((B, NSA_G * NSA_R, Tq, HD), F32),
        scratch_shapes=[pltpu.VMEM((NSA_BIAS_TILES + 1, R, NSA_TK), F32), pltpu.VMEM((LANE, tq), F32),
                        pltpu.VMEM((R, LANE), F32), pltpu.VMEM((R, LANE), F32), pltpu.VMEM((R, HD), F32)],
        compiler_params=_cparams(("parallel", "parallel", "arbitrary")),
        name="nsa_attend",
    )(tbl, q, gates, kcvc, kcvc, sel_arr, sel_arr, win_arr, win_arr, smap)


def _nsa_gather_body(pt_ref, new_ref, *refs, n_pages):
    pages = refs[:n_pages]
    cmp_ref, sel_ref = refs[n_pages:]
    ngrp = 2 * NSA_G
    for pg in range(n_pages):
        for j in range(ngrp):
            cmp_ref[0, j, pg * PAGE:(pg + 1) * PAGE, :] = pages[pg][0, :, j * HD:(j + 1) * HD]
            sel_ref[0, j, pg * PAGE:(pg + 1) * PAGE, :] = (
                pages[pg][0, :, (ngrp + j) * HD:(ngrp + j + 1) * HD].astype(BF16))
    tn = new_ref.shape[1]
    tail = sel_ref.shape[2] - n_pages * PAGE
    for j in range(ngrp):
        new = new_ref[0, :, (ngrp + j) * HD:(ngrp + j + 1) * HD]
        tile = jnp.concatenate([new, jnp.zeros((tail - tn, HD), F32)], axis=0)
        sel_ref[0, j, n_pages * PAGE:, :] = tile.astype(BF16)


def nsa_gather(page_table, new_rows, pool):
    B, tn, width = new_rows.shape
    n_pages = page_table.shape[1]
    ngrp = 2 * NSA_G
    sel_rows = -(-(n_pages * PAGE + tn) // NSA_TK) * NSA_TK
    grid_spec = pltpu.PrefetchScalarGridSpec(
        num_scalar_prefetch=1,
        grid=(B,),
        in_specs=[pl.BlockSpec((1, tn, width), lambda b, pt: (b, 0, 0))]
                 + [pl.BlockSpec((1, PAGE, width), functools.partial(lambda b, pt, pg: (pt[b, pg], 0, 0), pg=pg))
                    for pg in range(n_pages)],
        out_specs=[pl.BlockSpec((1, ngrp, n_pages * PAGE, HD), lambda b, pt: (b, 0, 0, 0)),
                   pl.BlockSpec((1, ngrp, sel_rows, HD), lambda b, pt: (b, 0, 0, 0))],
    )
    return pl.pallas_call(
        functools.partial(_nsa_gather_body, n_pages=n_pages),
        grid_spec=grid_spec,
        out_shape=[jax.ShapeDtypeStruct((B, ngrp, n_pages * PAGE, HD), F32),
                   jax.ShapeDtypeStruct((B, ngrp, sel_rows, HD), BF16)],
        compiler_params=_cparams(("arbitrary",)),
        name="nsa_gather",
    )(page_table, new_rows, *([pool] * n_pages))


def _row_tile(m):
    return 512 if m % 512 == 0 else m


def _small_params(entries):
    sp = jnp.zeros((8, LANE), F32)
    for off, bias, act, log_scale in entries:
        n = bias.shape[0]
        sp = sp.at[0, off:off + n].set(bias.astype(F32))
        sp = sp.at[1, off:off + n].set(act)
        if log_scale is not None:
            sp = sp.at[2, off:off + n].set(log_scale.astype(F32))
    return sp


EVEN_WIDTHS = (256, 256, 512, 512, GD_CH, 512, LANE)


def _even_weights(w_in):
    s = np.cumsum((0, 256, 256, 512, 512, 4, 4, 512, 512, 512, 512, 4, 4))
    col = lambda i: w_in[:, s[i]:s[i + 1]]
    small = jnp.concatenate([col(4), col(5), col(10), col(11)], axis=1)
    small = jnp.pad(small, ((0, 0), (0, LANE - small.shape[1])))
    return jnp.concatenate([col(0), col(1), col(2), col(3), col(6), col(7), col(8), col(9), small], axis=1).astype(BF16)


def _chunk_rows(small, B, T, L, lanes):
    r = small.reshape(B, T // L, L, LANE)[..., lanes[0]:lanes[1]]
    r = jnp.swapaxes(r, 2, 3)
    return jnp.pad(r, ((0, 0), (0, 0), (0, 8 - r.shape[2]), (0, 0)))


def even_layer(x, p, past, L, Bb, unroll):
    B, T, D = x.shape
    M = B * T
    tm = _row_tile(M)
    sp = _small_params([(0, p['mi_b_i'], ACT_ID, None), (4, p['mi_b_f'], ACT_LOGSIG, None),
                        (8, jnp.zeros((4,), F32), ACT_SIG, None), (12, p['gd_dt_bias'], ACT_DECAY, p['gd_a_log'])])
    mq, mk, mv, mo, gx, gz, small = norm_proj(x.reshape(M, D), p['norm_mix'], _even_weights(p['w_in']), sp,
                                              EVEN_WIDTHS, tm)
    heads = lambda a: jnp.transpose(a.reshape(B, T, MI_H, MI_DQK), (0, 2, 1, 3))
    sc = small.reshape(B, T, LANE)
    if past is None:
        c0 = jnp.zeros((B, MI_H, MI_DQK, MI_DV), F32)
        n0 = jnp.zeros((B, MI_H, 1, MI_DQK), F32)
        m0 = jnp.zeros((B, 1, MI_H), F32)
        s0 = jnp.zeros((B, GD_H, GD_DK, GD_DV), F32)
        conv0 = jnp.zeros((B, GD_CONV - 1, GD_CH), F32)
    else:
        c0, n0, m0, s0, conv0 = past
        n0 = n0.reshape(B, MI_H, 1, MI_DQK)
        m0 = m0.reshape(B, 1, MI_H)
    hm, c1, n1, m1 = mlstm(heads(mq), heads(mk), mv.reshape(B, T, -1), mo.reshape(B, T, -1), sc,
                           _chunk_rows(small, B, T, L, (0, 8)), c0, n0, m0,
                           p['mi_norm'].reshape(MI_H, MI_DV), L, Bb, unroll)
    og, s1, conv1 = gdn(gx.reshape(B, T, GD_CH), gz.reshape(B, T, -1), sc,
                        _chunk_rows(small, B, T, L, (12, 16)), s0, conv0, p['gd_conv_w'], p['gd_norm'], L, Bb, unroll)
    y = out_proj_residual(x.reshape(M, D), hm.reshape(M, -1), og.reshape(M, -1), p['w_out'].astype(BF16), tm)
    return y.reshape(B, T, D), (c1, n1.reshape(B, MI_H, MI_DQK), m1.reshape(B, MI_H), s1, conv1)


NSA_QW = NSA_G * NSA_R * HD
NSA_KVW = 6 * NSA_G * HD
NSA_CACHE_W = 4 * NSA_G * HD
FOX_W = FOX_H * HD
N_GATE = 3 * NSA_G * NSA_R
ODD_WIDTHS = (NSA_QW, NSA_KVW, FOX_W, 2 * FOX_W, LANE)


def _odd_weights(w_in):
    s = np.cumsum((0, NSA_QW, NSA_KVW, N_GATE, FOX_W, FOX_W, FOX_W, FOX_H))
    col = lambda i: w_in[:, s[i]:s[i + 1]]
    small = jnp.concatenate([col(2), col(6)], axis=1)
    small = jnp.pad(small, ((0, 0), (0, LANE - small.shape[1])))
    return jnp.concatenate([col(0), col(1), col(3), col(4), col(5), small], axis=1).astype(BF16)


def _heads(a, B, T, n):
    return jnp.transpose(a.reshape(B, T, n, HD), (0, 2, 1, 3))


def _unheads(a):
    B, n, T, _ = a.shape
    return jnp.transpose(a, (0, 2, 1, 3)).reshape(B * T, n * HD)


def odd_layer(x, p, rel_bias, w_buf, past, page_table):
    B, T, D = x.shape
    M = B * T
    tm = _row_tile(M)
    sp = _small_params([(0, jnp.zeros((N_GATE,), F32), ACT_SIG, None), (N_GATE, p['fox_b_f'], ACT_LOGSIG, None)])
    nq, nkv, fq, fkv, small = norm_proj(x.reshape(M, D), p['norm_mix'], _odd_weights(p['w_in']), sp, ODD_WIDTHS, tm)
    new_nsa = nkv[:, :NSA_CACHE_W].reshape(B, T, 4, NSA_G, HD)
    new_win = nkv[:, NSA_CACHE_W:].reshape(B, T, 2, NSA_G, HD)
    new_fox = fkv.reshape(B, T, 2, FOX_H, HD)
    logf = small[:, N_GATE:N_GATE + FOX_H].reshape(B, T, FOX_H)
    q_heads = _heads(nq, B, T, NSA_G * NSA_R)
    gates = jnp.transpose(small[:, :N_GATE].reshape(B, T, NSA_G, 3 * NSA_R), (0, 2, 1, 3))
    cmp_args = (p['nsa_cmp_pos'], p['nsa_cmp_w1'], p['nsa_cmp_w2'])
    rows16 = CMP_STRIDE * HD
    if past is None:
        groups = _heads(nkv, B, T, 6 * NSA_G)
        kcvc = nsa_compress(groups.reshape(B, 6 * NSA_G, T // CMP_STRIDE, rows16), T // CMP_STRIDE, *cmp_args, 1)
        arr = groups.astype(BF16)
        o_n = nsa_attend(rel_bias, q_heads, gates, kcvc, arr, 2 * NSA_G, arr, 4 * NSA_G,
                         tq=min(T, LANE), q_pos0=0, win_pos0=0, n_sel=-(-T // SEL_BLOCK))
        f_t = cumsum_lanes(jnp.transpose(logf, (0, 2, 1)), B)
        o_f = fox_prompt(_heads(fq * HD ** -0.5, B, T, FOX_H).astype(BF16), _heads(fkv, B, T, 2 * FOX_H).astype(BF16),
                         jnp.transpose(f_t, (0, 2, 1)), f_t, min(T, FOX_TQ), min(T, FOX_TK))
        o_f = _unheads(o_f)
        win_prev = jnp.zeros((B, WINDOW, 2, NSA_G, HD), F32)
    else:
        nsa_pool, win_prev, fox_pool, logf_pool = past
        n_pool = nsa_pool.shape[0]
        n_pages = page_table.shape[1]
        start = n_pages * PAGE
        cmp_rows, sel_arr = nsa_gather(page_table, nkv[:, :NSA_CACHE_W].reshape(B, T, NSA_CACHE_W),
                                       nsa_pool.reshape(n_pool, PAGE, NSA_CACHE_W))
        kcvc = nsa_compress(cmp_rows.reshape(B, 2 * NSA_G, start // CMP_STRIDE, rows16), start // CMP_STRIDE,
                            *cmp_args, math.gcd(B, 8))
        wp = win_prev.shape[1]
        win_all = jnp.concatenate([win_prev.reshape(B, wp, 2 * NSA_G * HD), nkv[:, NSA_CACHE_W:].reshape(B, T, -1)], 1)
        tw = -(-(wp + T) // NSA_TK) * NSA_TK
        win_arr = _heads(jnp.pad(win_all, ((0, 0), (0, tw - wp - T), (0, 0))), B, tw, 2 * NSA_G).astype(BF16)
        o_n = nsa_attend(rel_bias, q_heads, gates, kcvc, sel_arr, 0, win_arr, 0,
                         tq=T, q_pos0=start, win_pos0=start - wp, n_sel=-(-(start + T) // SEL_BLOCK))
        o_f = fox_decode(page_table, fq.reshape(B, T, FOX_W), fkv.reshape(B, T, 2 * FOX_W), logf,
                         fox_pool.reshape(n_pool, PAGE, 2 * FOX_W), logf_pool)
        o_f = o_f.reshape(M, FOX_W)
    win_state = jnp.concatenate([win_prev, new_win], axis=1)[:, -w_buf:]
    y = out_proj_residual(x.reshape(M, D), _unheads(o_n), o_f, p['w_out'].astype(BF16), tm)
    return y.reshape(B, T, D), (new_nsa, win_state, new_fox, logf)


def _trunk(x, past, page_table, P, w_buf, L, Bb, unroll):
    B, T, D = x.shape
    pe = dict(norm_mix=P['norm_mix'][0], w_in=P['w_in_even'][0], w_out=P['w_out_even'][0], mi_b_i=P['mi_b_i'][0],
              mi_b_f=P['mi_b_f'][0], mi_norm=P['mi_norm'][0], gd_conv_w=P['gd_conv_w'][0], gd_a_log=P['gd_a_log'][0],
              gd_dt_bias=P['gd_dt_bias'][0], gd_norm=P['gd_norm'][0])
    po = dict(norm_mix=P['norm_mix'][1], w_in=P['w_in_odd'][0], w_out=P['w_out_odd'][0],
              nsa_cmp_pos=P['nsa_cmp_pos'][0], nsa_cmp_w1=P['nsa_cmp_w1'][0], nsa_cmp_w2=P['nsa_cmp_w2'][0],
              fox_b_f=P['fox_b_f'][0])
    tm = _row_tile(B * T)
    mlp = lambda x, layer, final: mlp_residual(
        x.reshape(B * T, D), P['norm_mlp'][layer], P['w_up'][layer].astype(BF16), P['w_down'][layer].astype(BF16),
        P['norm_final'], final, tm, 1024).reshape(B, T, D)
    even_past = None if past is None else tuple(past[k][0] for k in ('mc', 'mn', 'mm', 'gs', 'gc'))
    odd_past = None if past is None else tuple(past[k][0] for k in ('nsa_kv', 'nsa_win', 'fox_kv', 'fox_logf'))
    x, st_e = even_layer(x, pe, even_past, L, Bb, unroll)
    x = mlp(x, 0, False)
    x, st_o = odd_layer(x, po, P['rel_bias'], w_buf, odd_past, page_table)
    y = mlp(x, 1, True)
    return y, tuple(a[None] for a in st_e + st_o)


def kernel(x_prompt, x_sample, state_mlstm_c, state_mlstm_n, state_mlstm_m, state_gdn_s, state_gdn_conv,
           cache_nsa_kv, state_nsa_win, cache_fox_kv, cache_fox_logf, page_table,
           norm_mix, norm_mlp, norm_final, w_up, w_down,
           w_in_even, w_out_even, mi_b_i, mi_b_f, mi_norm, gd_conv_w, gd_a_log, gd_dt_bias, gd_norm,
           w_in_odd, w_out_odd, nsa_cmp_pos, nsa_cmp_w1, nsa_cmp_w2, fox_b_f, rel_bias):
    P = dict(norm_mix=norm_mix, norm_mlp=norm_mlp, norm_final=norm_final, w_up=w_up, w_down=w_down,
             w_in_even=w_in_even, w_out_even=w_out_even, mi_b_i=mi_b_i, mi_b_f=mi_b_f, mi_norm=mi_norm,
             gd_conv_w=gd_conv_w, gd_a_log=gd_a_log, gd_dt_bias=gd_dt_bias, gd_norm=gd_norm,
             w_in_odd=w_in_odd, w_out_odd=w_out_odd, nsa_cmp_pos=nsa_cmp_pos, nsa_cmp_w1=nsa_cmp_w1,
             nsa_cmp_w2=nsa_cmp_w2, fox_b_f=fox_b_f, rel_bias=rel_bias)
    w_buf = state_nsa_win.shape[2]
    b_p, t_p = x_prompt.shape[:2]
    y_p, st_p = _trunk(x_prompt, None, None, P, w_buf, math.gcd(t_p, 64), b_p, b_p)
    past = dict(mc=state_mlstm_c, mn=state_mlstm_n, mm=state_mlstm_m, gs=state_gdn_s, gc=state_gdn_conv,
                nsa_kv=cache_nsa_kv, nsa_win=state_nsa_win, fox_kv=cache_fox_kv, fox_logf=cache_fox_logf)
    b_s, t_s = x_sample.shape[:2]
    y_s, st_s = _trunk(x_sample, past, page_table, P, w_buf, math.gcd(t_s, 64), math.gcd(b_s, 8), math.gcd(b_s, 4))
    return (y_p, y_s) + st_p + st_s
```

```python
import functools
import math

import jax
import jax.numpy as jnp
import numpy as np
from jax import lax
from jax.experimental import pallas as pl
from jax.experimental.pallas import tpu as pltpu

F32 = jnp.float32
BF16 = jnp.bfloat16
HI = lax.Precision.HIGHEST

D_MODEL = 1024
D_FF = 4 * D_MODEL
EPS = 1e-6
NEG_BIG = -1e30
PAGE = 128

MI_H, MI_DQK, MI_DV = 4, 64, 128
GD_H, GD_DK, GD_DV, GD_CONV = 4, 128, 128, 4
GD_CH = 3 * GD_H * GD_DK
NSA_G, NSA_R, HD = 2, 4, 64
FOX_H = 8
CMP_BLOCK, CMP_STRIDE, CMP_HIDDEN = 32, 16, 256
SEL_BLOCK, SEL_TOPN, WINDOW = 64, 16, 512
FORCE_SCORE = 1e4
N_BUCKETS, MAX_DISTANCE = 32, 128
BUCKET_EXACT = N_BUCKETS // 2
BUCKET_SAT_DIST = 113
LANE = 128
VMEM_LIMIT = 56 * 1024 * 1024


def _cparams(sem):
    return pltpu.CompilerParams(dimension_semantics=sem, vmem_limit_bytes=VMEM_LIMIT)


def _dot(a, b, precision=None):
    return jnp.dot(a, b, preferred_element_type=F32, precision=precision)


def _dot_nt(a, b, precision=None):
    return lax.dot_general(a, b, (((1,), (1,)), ((), ())), preferred_element_type=F32, precision=precision)


def _dot_tn(a, b, precision=None):
    return lax.dot_general(a, b, (((0,), (0,)), ((), ())), preferred_element_type=F32, precision=precision)


def _softplus(x):
    return jnp.maximum(x, 0.0) + jnp.log1p(jnp.exp(-jnp.abs(x)))


def _sigmoid(x):
    return 1.0 / (1.0 + jnp.exp(-x))


def _silu(x):
    return x * _sigmoid(x)


def _iota(shape, dim):
    return lax.broadcasted_iota(jnp.int32, shape, dim)


ACT_ID, ACT_LOGSIG, ACT_SIG, ACT_DECAY = 0.0, 1.0, 2.0, 3.0


def _proj_body(x_ref, g_ref, w_ref, sp_ref, *out_refs, widths):
    x = x_ref[...]
    hn = (x * lax.rsqrt(jnp.mean(x * x, axis=-1, keepdims=True) + EPS) * g_ref[...]).astype(BF16)
    off = 0
    for i, (o_ref, n) in enumerate(zip(out_refs, widths)):
        r = _dot(hn, w_ref[:, off:off + n])
        if i == len(widths) - 1:
            z = r + sp_ref[0:1, :]
            mode = sp_ref[1:2, :]
            decay = -jnp.exp(sp_ref[2:3, :]) * _softplus(z)
            r = jnp.where(mode == ACT_LOGSIG, -_softplus(-z),
                          jnp.where(mode == ACT_SIG, _sigmoid(z),
                                    jnp.where(mode == ACT_DECAY, decay, z)))
        o_ref[...] = r.astype(o_ref.dtype)
        off += n


def norm_proj(x, g, w_bf16, small_params, widths, tm):
    m, d = x.shape
    n_total = sum(widths)
    assert w_bf16.shape == (d, n_total) and m % tm == 0
    out_shape = [jax.ShapeDtypeStruct((m, n), F32) for n in widths]
    return pl.pallas_call(
        functools.partial(_proj_body, widths=tuple(widths)),
        grid=(m // tm,),
        in_specs=[pl.BlockSpec((tm, d), lambda i: (i, 0)),
                  pl.BlockSpec((1, d), lambda i: (0, 0)),
                  pl.BlockSpec((d, n_total), lambda i: (0, 0)),
                  pl.BlockSpec((8, LANE), lambda i: (0, 0))],
        out_specs=[pl.BlockSpec((tm, n), lambda i: (i, 0)) for n in widths],
        out_shape=out_shape,
        compiler_params=_cparams(("parallel",)),
        name="norm_proj",
    )(x, g.reshape(1, d), w_bf16, small_params)


def _outproj_body(x_ref, a1_ref, a2_ref, w_ref, o_ref):
    k1 = a1_ref.shape[1]
    y = _dot(a1_ref[...].astype(BF16), w_ref[0:k1, :]) + _dot(a2_ref[...].astype(BF16), w_ref[k1:, :])
    o_ref[...] = x_ref[...] + y


def out_proj_residual(x, a1, a2, w_bf16, tm):
    m, d = x.shape
    k1, k2 = a1.shape[1], a2.shape[1]
    return pl.pallas_call(
        _outproj_body,
        grid=(m // tm,),
        in_specs=[pl.BlockSpec((tm, d), lambda i: (i, 0)),
                  pl.BlockSpec((tm, k1), lambda i: (i, 0)),
                  pl.BlockSpec((tm, k2), lambda i: (i, 0)),
                  pl.BlockSpec((k1 + k2, d), lambda i: (0, 0))],
        out_specs=pl.BlockSpec((tm, d), lambda i: (i, 0)),
        out_shape=jax.ShapeDtypeStruct((m, d), F32),
        compiler_params=_cparams(("parallel",)),
        name="out_proj",
    )(x, a1, a2, w_bf16)


def _mlp_body(x_ref, g_ref, wu_ref, wd_ref, gf_ref, o_ref, hn_scr, acc_scr, *, final_norm):
    j = pl.program_id(1)

    @pl.when(j == 0)
    def _():
        x = x_ref[...]
        hn_scr[...] = (x * lax.rsqrt(jnp.mean(x * x, axis=-1, keepdims=True) + EPS) * g_ref[...]).astype(BF16)
        acc_scr[...] = jnp.zeros_like(acc_scr)

    u = jnp.maximum(_dot(hn_scr[...], wu_ref[...]), 0.0)
    acc_scr[...] += _dot((u * u).astype(BF16), wd_ref[...])

    @pl.when(j == pl.num_programs(1) - 1)
    def _():
        y = x_ref[...] + acc_scr[...]
        if final_norm:
            y = y * lax.rsqrt(jnp.mean(y * y, axis=-1, keepdims=True) + EPS) * gf_ref[...]
        o_ref[...] = y


def mlp_residual(x, g, w_up_bf16, w_down_bf16, g_final, final_norm, tm, tf):
    m, d = x.shape
    f = w_up_bf16.shape[1]
    return pl.pallas_call(
        functools.partial(_mlp_body, final_norm=final_norm),
        grid=(m // tm, f // tf),
        in_specs=[pl.BlockSpec((tm, d), lambda i, j: (i, 0)),
                  pl.BlockSpec((1, d), lambda i, j: (0, 0)),
                  pl.BlockSpec((d, tf), lambda i, j: (0, j)),
                  pl.BlockSpec((tf, d), lambda i, j: (j, 0)),
                  pl.BlockSpec((1, d), lambda i, j: (0, 0))],
        out_specs=pl.BlockSpec((tm, d), lambda i, j: (i, 0)),
        out_shape=jax.ShapeDtypeStruct((m, d), F32),
        scratch_shapes=[pltpu.VMEM((tm, d), BF16), pltpu.VMEM((tm, d), F32)],
        compiler_params=_cparams(("parallel", "arbitrary")),
        name="mlp",
    )(x, g.reshape(1, d), w_up_bf16, w_down_bf16, g_final.reshape(1, d))


def _tri(n):
    r = _iota((n, n), 0)
    c = _iota((n, n), 1)
    return r, c


def _bf16_terms(a, n):
    terms, rest = [], a
    for _ in range(n):
        t = rest.astype(BF16)
        terms.append(t)
        rest = rest - t.astype(F32)
    return terms


def _dot_split(a, b, f=None):
    f = f or _dot
    a_hi, a_lo = _bf16_terms(a, 2)
    b_hi, b_lo = _bf16_terms(b, 2)
    return f(a_hi, b_hi) + (f(a_hi, b_lo) + f(a_lo, b_hi))


def _dot_pick(a, b, f=None, exact_lhs=False):
    f = f or _dot
    if exact_lhs:
        return sum(f(a.astype(BF16), t) for t in _bf16_terms(b, 3))
    return sum(f(t, b.astype(BF16)) for t in _bf16_terms(a, 3))


def _chunk_dots(L):
    if L % 16 == 0:
        cast = lambda f: (lambda a, b: f(a.astype(BF16), b.astype(BF16)))
        return cast(_dot), cast(_dot_nt), cast(_dot_tn)
    full = lambda f: (lambda a, b: _dot_split(a, b, f))
    return full(_dot), full(_dot_nt), full(_dot_tn)


def _mlstm_group(bbs, q_ref, k_ref, v_ref, o_ref, sc_ref, sr_ref, nw_ref, h_ref, c_ref, n_ref, m_ref, L):
    nn, nt, tn = _chunk_dots(L)
    rows, cols = _tri(L)
    lower = rows >= cols
    tril = lower.astype(F32)
    triu = (rows <= cols).astype(F32)
    lane4 = _iota((1, MI_H), 1)
    scs = {bb: sc_ref[bb] for bb in bbs}
    srs = {bb: sr_ref[bb, 0] for bb in bbs}
    b_col = {bb: _dot_pick(tril, scs[bb][:, 4:8], exact_lhs=True) for bb in bbs}
    b_row = {bb: _dot_pick(srs[bb][4:8, :], triu) for bb in bbs}
    m_vec = {bb: m_ref[bb] for bb in bbs}
    chains = [(bb, h) for bb in bbs for h in range(MI_H)]
    q = {c: q_ref[c[0], c[1]] for c in chains}
    k = {c: k_ref[c[0], c[1]] * (MI_DQK ** -0.5) for c in chains}
    v = {c: v_ref[c[0], :, c[1] * MI_DV:(c[1] + 1) * MI_DV] for c in chains}
    qk = {c: nt(q[c], k[c]) for c in chains}
    c_prev = {c: c_ref[c[0], c[1]] for c in chains}
    qc = {c: nn(q[c], c_prev[c]) for c in chains}
    s, a_inter, m_t, m_new, a_prev, kw = {}, {}, {}, {}, {}, {}
    for c in chains:
        bb, h = c
        bc = b_col[bb][:, h:h + 1]
        m_prev = m_vec[bb][:, h:h + 1]
        d = jnp.where(lower, bc - b_row[bb][h:h + 1, :] + srs[bb][h:h + 1, :], NEG_BIG)
        inter = bc + m_prev
        m_t[c] = jnp.maximum(inter, jnp.max(d, axis=1, keepdims=True))
        s[c] = qk[c] * jnp.exp(d - m_t[c])
        a_inter[c] = jnp.exp(inter - m_t[c])
        b_last = bc[L - 1:L, :]
        g_col = b_last - bc + scs[bb][:, h:h + 1]
        m_new[c] = jnp.maximum(b_last + m_prev, jnp.max(g_col, axis=0, keepdims=True))
        a_prev[c] = jnp.exp(b_last + m_prev - m_new[c])
        kw[c] = k[c] * jnp.exp(g_col - m_new[c])
    sv = {c: nn(s[c], v[c]) for c in chains}
    kv = {c: tn(kw[c], v[c]) for c in chains}
    for c in chains:
        bb, h = c
        n_prev = n_ref[bb, h]
        num = sv[c] + a_inter[c] * qc[c]
        den = (jnp.sum(s[c], axis=1, keepdims=True)
               + a_inter[c] * jnp.sum(q[c] * n_prev, axis=1, keepdims=True))
        hh = num / jnp.maximum(jnp.abs(den), jnp.exp(-m_t[c]))
        c_ref[bb, h] = a_prev[c] * c_prev[c] + kv[c]
        n_ref[bb, h] = a_prev[c] * n_prev + jnp.sum(kw[c], axis=0, keepdims=True)
        m_vec[bb] = jnp.where(lane4 == h, m_new[c], m_vec[bb])
        hn = hh * lax.rsqrt(jnp.mean(hh * hh, axis=-1, keepdims=True) + EPS) * nw_ref[h:h + 1, :]
        gate = _sigmoid(o_ref[bb, :, h * MI_DV:(h + 1) * MI_DV])
        h_ref[bb, :, h * MI_DV:(h + 1) * MI_DV] = hn * gate
    for bb in bbs:
        m_ref[bb] = m_vec[bb]


def _mlstm_body(q_ref, k_ref, v_ref, o_ref, sc_ref, sr_ref, c0_ref, n0_ref, m0_ref, nw_ref,
                h_ref, c_ref, n_ref, m_ref, *, L, Bb):
    @pl.when(pl.program_id(1) == 0)
    def _():
        c_ref[...] = c0_ref[...]
        n_ref[...] = n0_ref[...]
        m_ref[...] = m0_ref[...]

    _mlstm_group(list(range(Bb)), q_ref, k_ref, v_ref, o_ref, sc_ref, sr_ref, nw_ref, h_ref, c_ref, n_ref, m_ref, L)


def mlstm(q, k, v, o, sc, sr, c0, n0, m0, norm_w, L, Bb):
    B, H, T, _ = q.shape
    nc = T // L
    hv = H * MI_DV
    bmap = lambda b, c: (b, 0, 0, 0)
    return pl.pallas_call(
        functools.partial(_mlstm_body, L=L, Bb=Bb),
        grid=(B // Bb, nc),
        in_specs=[pl.BlockSpec((Bb, H, L, MI_DQK), lambda b, c: (b, 0, c, 0)),
                  pl.BlockSpec((Bb, H, L, MI_DQK), lambda b, c: (b, 0, c, 0)),
                  pl.BlockSpec((Bb, L, hv), lambda b, c: (b, c, 0)),
                  pl.BlockSpec((Bb, L, hv), lambda b, c: (b, c, 0)),
                  pl.BlockSpec((Bb, L, LANE), lambda b, c: (b, c, 0)),
                  pl.BlockSpec((Bb, 1, 8, L), lambda b, c: (b, c, 0, 0)),
                  pl.BlockSpec((Bb, H, MI_DQK, MI_DV), bmap),
                  pl.BlockSpec((Bb, H, 1, MI_DQK), bmap),
                  pl.BlockSpec((Bb, 1, H), lambda b, c: (b, 0, 0)),
                  pl.BlockSpec((H, MI_DV), lambda b, c: (0, 0))],
        out_specs=[pl.BlockSpec((Bb, L, hv), lambda b, c: (b, c, 0)),
                   pl.BlockSpec((Bb, H, MI_DQK, MI_DV), bmap),
                   pl.BlockSpec((Bb, H, 1, MI_DQK), bmap),
                   pl.BlockSpec((Bb, 1, H), lambda b, c: (b, 0, 0))],
        out_shape=[jax.ShapeDtypeStruct((B, T, hv), F32),
                   jax.ShapeDtypeStruct((B, H, MI_DQK, MI_DV), F32),
                   jax.ShapeDtypeStruct((B, H, 1, MI_DQK), F32),
                   jax.ShapeDtypeStruct((B, 1, H), F32)],
        compiler_params=_cparams(("parallel", "arbitrary")),
        name="mlstm",
    )(q, k, v, o, sc, sr, c0, n0, m0, norm_w)


def _gdn_group(bbs, x_ref, z_ref, sc_ref, sr_ref, cw_ref, nw_ref, o_ref, s_ref, conv_ref, xp_scr, L):
    base = 8 - (GD_CONV - 1)
    nn, nt, tn = _chunk_dots(L)
    rows, cols = _tri(L)
    incl = rows >= cols
    strict = rows > cols
    eye = (rows == cols).astype(F32)
    tril = incl.astype(F32)
    triu = (rows <= cols).astype(F32)
    kw = GD_H * GD_DK
    ys, scs, gcols, grows = {}, {}, {}, {}
    for bb in bbs:
        xp_scr[bb, 8:8 + L, :] = x_ref[bb]
        y = xp_scr[bb, base:base + L, :] * cw_ref[0:1, :]
        for j in range(1, GD_CONV):
            y = y + xp_scr[bb, base + j:base + j + L, :] * cw_ref[j:j + 1, :]
        tail = xp_scr[bb, L + base:L + 8, :]
        xp_scr[bb, base:8, :] = tail
        conv_ref[bb] = tail
        ys[bb] = _silu(y)
        scs[bb] = sc_ref[bb]
        gcols[bb] = _dot_pick(tril, scs[bb][:, 12:16], exact_lhs=True)
        grows[bb] = _dot_pick(sr_ref[bb, 0][0:4, :], triu)
    chains = [(bb, h) for bb in bbs for h in range(GD_H)]
    q, k, vb, kb, dec, gcc, egc = {}, {}, {}, {}, {}, {}, {}
    for c in chains:
        bb, h = c
        y = ys[bb]
        qh = y[:, h * GD_DK:(h + 1) * GD_DK]
        kh = y[:, kw + h * GD_DK:kw + (h + 1) * GD_DK]
        vh = y[:, 2 * kw + h * GD_DV:2 * kw + (h + 1) * GD_DV]
        q[c] = qh * lax.rsqrt(jnp.sum(qh * qh, axis=-1, keepdims=True) + EPS) * (GD_DK ** -0.5)
        k[c] = kh * lax.rsqrt(jnp.sum(kh * kh, axis=-1, keepdims=True) + EPS)
        beta = scs[bb][:, 8 + h:9 + h]
        gcc[c] = gcols[bb][:, h:h + 1]
        dec[c] = jnp.exp(jnp.where(incl, gcc[c] - grows[bb][h:h + 1, :], NEG_BIG))
        egc[c] = jnp.exp(gcc[c])
        kb[c] = k[c] * beta
        vb[c] = vh * beta
    pw = {c: -(nt(kb[c], k[c]) * jnp.where(strict, dec[c], 0.0)) for c in chains}
    attn = {c: nt(q[c], k[c]) * dec[c] for c in chains}
    tinv = {c: eye + pw[c] for c in chains}
    for _ in range(int(math.log2(L)) - 1):
        pw = {c: _dot_split(pw[c], pw[c]) for c in chains}
        tinv = {c: tinv[c] + _dot_split(tinv[c], pw[c]) for c in chains}
    u = {c: _dot_split(tinv[c], vb[c]) for c in chains}
    w = {c: _dot_split(tinv[c], kb[c] * egc[c]) for c in chains}
    s_prev = {c: s_ref[c[0], c[1]] for c in chains}
    v_new = {c: u[c] - nn(w[c], s_prev[c]) for c in chains}
    o = {c: nn(q[c] * egc[c], s_prev[c]) + nn(attn[c], v_new[c]) for c in chains}
    for c in chains:
        bb, h = c
        g_last = gcc[c][L - 1:L, :]
        s_ref[bb, h] = jnp.exp(g_last) * s_prev[c] + tn(k[c] * jnp.exp(g_last - gcc[c]), v_new[c])
        on = o[c] * lax.rsqrt(jnp.mean(o[c] * o[c], axis=-1, keepdims=True) + EPS) * nw_ref[...]
        o_ref[bb, :, h * GD_DV:(h + 1) * GD_DV] = on * _silu(z_ref[bb, :, h * GD_DV:(h + 1) * GD_DV])


def _gdn_body(x_ref, z_ref, sc_ref, sr_ref, s0_ref, conv0_ref, cw_ref, nw_ref,
              o_ref, s_ref, conv_ref, xp_scr, *, L, Bb):
    @pl.when(pl.program_id(1) == 0)
    def _():
        s_ref[...] = s0_ref[...]
        xp_scr[:, 8 - (GD_CONV - 1):8, :] = conv0_ref[...]

    _gdn_group(list(range(Bb)), x_ref, z_ref, sc_ref, sr_ref, cw_ref, nw_ref, o_ref, s_ref, conv_ref, xp_scr, L)


def gdn(x, z, sc, sr, s0, conv0, conv_w, norm_w, L, Bb):
    B, T, ch = x.shape
    H = GD_H
    nc = T // L
    hv = H * GD_DV
    bmap = lambda b, c: (b, 0, 0, 0)
    return pl.pallas_call(
        functools.partial(_gdn_body, L=L, Bb=Bb),
        grid=(B // Bb, nc),
        in_specs=[pl.BlockSpec((Bb, L, ch), lambda b, c: (b, c, 0)),
                  pl.BlockSpec((Bb, L, hv), lambda b, c: (b, c, 0)),
                  pl.BlockSpec((Bb, L, LANE), lambda b, c: (b, c, 0)),
                  pl.BlockSpec((Bb, 1, 8, L), lambda b, c: (b, c, 0, 0)),
                  pl.BlockSpec((Bb, H, GD_DK, GD_DV), bmap),
                  pl.BlockSpec((Bb, GD_CONV - 1, ch), lambda b, c: (b, 0, 0)),
                  pl.BlockSpec((GD_CONV, ch), lambda b, c: (0, 0)),
                  pl.BlockSpec((1, GD_DV), lambda b, c: (0, 0))],
        out_specs=[pl.BlockSpec((Bb, L, hv), lambda b, c: (b, c, 0)),
                   pl.BlockSpec((Bb, H, GD_DK, GD_DV), bmap),
                   pl.BlockSpec((Bb, GD_CONV - 1, ch), lambda b, c: (b, 0, 0))],
        out_shape=[jax.ShapeDtypeStruct((B, T, hv), F32),
                   jax.ShapeDtypeStruct((B, H, GD_DK, GD_DV), F32),
                   jax.ShapeDtypeStruct((B, GD_CONV - 1, ch), F32)],
        scratch_shapes=[pltpu.VMEM((Bb, L + 8, ch), F32)],
        compiler_params=_cparams(("parallel", "arbitrary")),
        name="gdn",
    )(x, z, sc, sr, s0, conv0, conv_w, norm_w.reshape(1, GD_DV))


def _compress_body(x_ref, pos_ref, w1_ref, w2_ref, o_ref):
    Bb, _, R, half = x_ref.shape
    x = x_ref[:, 0].reshape(Bb * R, half).astype(F32)
    ua = _dot((x + pos_ref[0, 0:1, :]).astype(BF16), w1_ref[0, 0])
    ub = _dot((x + pos_ref[0, 1:2, :]).astype(BF16), w1_ref[0, 1])
    h = _silu(ua + pltpu.roll(ub, Bb * R - 1, 0))
    o_ref[:, 0, 0:R, :] = _dot(h.astype(BF16), w2_ref[0]).reshape(Bb, R, HD)
    rp = o_ref.shape[2]
    if rp > R:
        o_ref[:, 0, R:rp, :] = jnp.zeros((Bb, rp - R, HD), F32)


def nsa_compress(xr, R, pos, w1, w2, Bb):
    B = xr.shape[0]
    half = CMP_STRIDE * HD
    rp = -(-R // LANE) * LANE
    return pl.pallas_call(
        _compress_body,
        grid=(4, B // Bb),
        in_specs=[pl.BlockSpec((Bb, 1, R, half), lambda c, b: (b, c, 0, 0)),
                  pl.BlockSpec((1, 2, half), lambda c, b: (c // 2, 0, 0)),
                  pl.BlockSpec((1, 2, half, CMP_HIDDEN), lambda c, b: (c // 2, 0, 0, 0)),
                  pl.BlockSpec((1, CMP_HIDDEN, HD), lambda c, b: (c // 2, 0, 0))],
        out_specs=pl.BlockSpec((Bb, 1, rp, HD), lambda c, b: (b, c, 0, 0)),
        out_shape=jax.ShapeDtypeStruct((B, 4, rp, HD), F32),
        compiler_params=_cparams(("parallel", "parallel")),
        name="nsa_compress",
    )(xr, pos.reshape(2, 2, half), w1.reshape(2, 2, half, CMP_HIDDEN).astype(BF16), w2.astype(BF16))


def _cumsum_body(x_ref, o_ref):
    Bb, H, T = x_ref.shape
    rows, cols = _tri(LANE)
    triu = (rows <= cols).astype(F32)
    carry = jnp.zeros((Bb * H, 1), F32)
    for c in range(T // LANE):
        seg = x_ref[:, :, c * LANE:(c + 1) * LANE].reshape(Bb * H, LANE)
        loc = _dot_pick(seg, triu) + carry
        o_ref[:, :, c * LANE:(c + 1) * LANE] = loc.reshape(Bb, H, LANE)
        carry = loc[:, LANE - 1:LANE]


def cumsum_lanes(x, Bb):
    B, H, T = x.shape
    return pl.pallas_call(
        _cumsum_body,
        grid=(B // Bb,),
        in_specs=[pl.BlockSpec((Bb, H, T), lambda b: (b, 0, 0))],
        out_specs=pl.BlockSpec((Bb, H, T), lambda b: (b, 0, 0)),
        out_shape=jax.ShapeDtypeStruct((B, H, T), F32),
        compiler_params=_cparams(("parallel",)),
        name="cumsum",
    )(x)


def _flash_tile(s_blocks, v, m_scr, l_scr, acc_scr):
    dv = acc_scr.shape[-1]
    m_prev = m_scr[...]
    mx = s_blocks[0]
    for sb in s_blocks[1:]:
        mx = jnp.maximum(mx, sb)
    m_new = jnp.maximum(m_prev, jnp.max(mx, axis=1, keepdims=True))
    p_blocks = [jnp.exp(sb - m_new) for sb in s_blocks]
    sm = p_blocks[0]
    for pb in p_blocks[1:]:
        sm = sm + pb
    alpha = jnp.exp(m_prev - m_new)
    l_scr[...] = alpha * l_scr[...] + jnp.sum(sm, axis=1, keepdims=True)
    p = (jnp.concatenate(p_blocks, axis=1) if len(p_blocks) > 1 else p_blocks[0]).astype(BF16)
    acc_scr[...] = alpha[:, :dv] * acc_scr[...] + _dot(p, v)
    m_scr[...] = m_new


def _flash_reset(m_scr, l_scr, acc_scr):
    m_scr[...] = jnp.full_like(m_scr, NEG_BIG)
    l_scr[...] = jnp.zeros_like(l_scr)
    acc_scr[...] = jnp.zeros_like(acc_scr)


def _lane_blocks(s):
    return [s[:, i * LANE:(i + 1) * LANE] for i in range(s.shape[1] // LANE)]


def _fox_body(q_ref, k_ref, v_ref, fq_ref, fk_ref, o_ref, m_scr, l_scr, acc_scr, *, tq, tk):
    qi = pl.program_id(1)
    j = pl.program_id(2)
    top = (qi * tq + tq - 1) // tk

    @pl.when(j == 0)
    def _():
        _flash_reset(m_scr, l_scr, acc_scr)

    def tile(diag):
        fq = fq_ref[0]
        fk = fk_ref[0]
        if diag:
            mask = (top * tk + _iota((tq, tk), 1)) <= (qi * tq + _iota((tq, tk), 0))
        for h in range(FOX_H):
            s = _dot_nt(q_ref[0, h], k_ref[0, h]) + fq[:, h:h + 1] - fk[h:h + 1, :]
            if diag:
                s = jnp.where(mask, s, NEG_BIG)
            _flash_tile(_lane_blocks(s), v_ref[0, h], m_scr.at[h], l_scr.at[h], acc_scr.at[h])

    @pl.when(j == 0)
    def _():
        tile(True)

    @pl.when(jnp.logical_and(j > 0, j <= top))
    def _():
        tile(False)

    @pl.when(j == pl.num_programs(2) - 1)
    def _():
        for h in range(FOX_H):
            o_ref[0, h] = acc_scr[h] / jnp.maximum(l_scr[h][:, :HD], 1e-30)


def fox_prompt(q, kv, fq, fk, tq, tk):
    B, H, T, _ = q.shape
    kmap = lambda i, j: jnp.maximum((i * tq + tq - 1) // tk - j, 0)
    return pl.pallas_call(
        functools.partial(_fox_body, tq=tq, tk=tk),
        grid=(B, T // tq, T // tk),
        in_specs=[pl.BlockSpec((1, H, tq, HD), lambda b, i, j: (b, 0, i, 0)),
                  pl.BlockSpec((1, H, tk, HD), lambda b, i, j: (b, 0, kmap(i, j), 0)),
                  pl.BlockSpec((1, H, tk, HD), lambda b, i, j: (b, 1, kmap(i, j), 0)),
                  pl.BlockSpec((1, tq, H), lambda b, i, j: (b, i, 0)),
                  pl.BlockSpec((1, H, tk), lambda b, i, j: (b, 0, kmap(i, j)))],
        out_specs=pl.BlockSpec((1, H, tq, HD), lambda b, i, j: (b, 0, i, 0)),
        out_shape=jax.ShapeDtypeStruct((B, H, T, HD), F32),
        scratch_shapes=[pltpu.VMEM((H, tq, LANE), F32), pltpu.VMEM((H, tq, LANE), F32), pltpu.VMEM((H, tq, HD), F32)],
        compiler_params=_cparams(("parallel", "parallel", "arbitrary")),
        name="fox_prompt",
    )(q, kv, kv, fq, fk)


def _fox_decode_body(pt_ref, q_ref, newkv_ref, newlf_ref, *refs, n_pages, tn):
    kv_refs = refs[:n_pages]
    lf_refs = refs[n_pages:2 * n_pages]
    o_ref = refs[2 * n_pages]
    hw = FOX_H * HD
    R = FOX_H * tn
    q = q_ref[0]
    qrep = jnp.concatenate([q] * FOX_H, axis=0)
    blockmask = (_iota((R, hw), 0) // tn) == (_iota((R, hw), 1) // HD)
    qbd = jnp.where(blockmask, qrep, 0.0).astype(BF16)
    rows, cols = _tri(PAGE)
    triu = (rows <= cols).astype(F32)
    expand = lambda a: jnp.concatenate([jnp.broadcast_to(a[h:h + 1, :], (tn, a.shape[1])) for h in range(FOX_H)], 0)
    carry_c = jnp.zeros((FOX_H, 1), F32)
    carry_r = jnp.zeros((1, FOX_H), F32)
    s_tiles = []
    for pg in range(n_pages):
        lf = lf_refs[pg][0]
        f_t = _dot_pick(lf, triu, _dot_tn) + carry_c
        carry_c = f_t[:, PAGE - 1:PAGE]
        carry_r = carry_r + jnp.sum(lf, axis=0, keepdims=True)
        k = kv_refs[pg][0, :, 0:hw].astype(BF16)
        s_tiles.append(_dot_nt(qbd, k) * (HD ** -0.5) - expand(f_t))
    lfn = newlf_ref[0]
    r8, c8 = _tri(tn)
    fq_c = _dot_pick((r8 >= c8).astype(F32), lfn, exact_lhs=True) + carry_r
    fq_t = _dot_pick(lfn, (r8 <= c8).astype(F32), _dot_tn) + carry_c
    fq_rows = jnp.concatenate([fq_c[:, h:h + 1] for h in range(FOX_H)], axis=0)
    kn = newkv_ref[0, :, 0:hw]
    s_new = _dot_split(qbd.astype(F32), kn, _dot_nt) * (HD ** -0.5) - expand(fq_t)
    causal = _iota((R, tn), 1) <= (_iota((R, tn), 0) % tn)
    s_new = jnp.where(causal, s_new + fq_rows, NEG_BIG)
    s_tiles = [s + fq_rows for s in s_tiles]
    m = jnp.max(s_new, axis=1, keepdims=True)
    for s in s_tiles:
        m = jnp.maximum(m, jnp.max(s, axis=1, keepdims=True))
    p_new = jnp.where(causal, jnp.exp(s_new - m), 0.0)
    l = jnp.sum(p_new, axis=1, keepdims=True)
    acc = _dot_split(p_new, newkv_ref[0, :, hw:2 * hw])
    for pg, s in enumerate(s_tiles):
        p = jnp.exp(s - m)
        l = l + jnp.sum(p, axis=1, keepdims=True)
        acc = acc + _dot(p.astype(BF16), kv_refs[pg][0, :, hw:2 * hw].astype(BF16))
    acc = acc / jnp.maximum(l, 1e-30)
    o_ref[0] = jnp.concatenate([acc[h * tn:(h + 1) * tn, h * HD:(h + 1) * HD] for h in range(FOX_H)], axis=1)


def fox_decode(page_table, q, newkv, newlf, kv_pool, lf_pool):
    B, tn, hw = q.shape
    n_pages = page_table.shape[1]
    page_spec = lambda width, pg: pl.BlockSpec((1, PAGE, width), lambda b, pt: (pt[b, pg], 0, 0))
    grid_spec = pltpu.PrefetchScalarGridSpec(
        num_scalar_prefetch=1,
        grid=(B,),
        in_specs=[pl.BlockSpec((1, tn, hw), lambda b, pt: (b, 0, 0)),
                  pl.BlockSpec((1, tn, 2 * hw), lambda b, pt: (b, 0, 0)),
                  pl.BlockSpec((1, tn, FOX_H), lambda b, pt: (b, 0, 0))]
                 + [page_spec(2 * hw, pg) for pg in range(n_pages)]
                 + [page_spec(FOX_H, pg) for pg in range(n_pages)],
        out_specs=pl.BlockSpec((1, tn, hw), lambda b, pt: (b, 0, 0)),
    )
    return pl.pallas_call(
        functools.partial(_fox_decode_body, n_pages=n_pages, tn=tn),
        grid_spec=grid_spec,
        out_shape=jax.ShapeDtypeStruct((B, tn, hw), F32),
        compiler_params=_cparams(("arbitrary",)),
        name="fox_decode",
    )(page_table, q, newkv, newlf, *([kv_pool] * n_pages), *([lf_pool] * n_pages))


def _t5_bucket(dist):
    n = jnp.maximum(dist, 0)
    nf = jnp.maximum(n, 1).astype(F32)
    large = BUCKET_EXACT + (jnp.log(nf / BUCKET_EXACT) / math.log(MAX_DISTANCE / BUCKET_EXACT)
                            * (N_BUCKETS - BUCKET_EXACT)).astype(jnp.int32)
    return jnp.where(n < BUCKET_EXACT, n, jnp.minimum(large, N_BUCKETS - 1))


def _bias_from_bucket(bucket, tbl_ref, head):
    out = jnp.zeros(bucket.shape, F32)
    for kk in range(N_BUCKETS):
        out = jnp.where(bucket == kk, tbl_ref[kk, head], out)
    return out


FOX_TQ, FOX_TK = 256, 512
RECURRENT_BATCH = 4
NSA_TK = 2 * LANE
NSA_FAR_GROUP = 4
NSA_BIAS_TILES = (BUCKET_SAT_DIST + NSA_TK + LANE - 1) // LANE


def _nsa_body(tbl_ref, q_ref, gate_ref, kc_ref, vc_ref, ks_ref, vs_ref, kw_ref, vw_ref, smap_ref, o_ref,
              bias_scr, score_scr, m_scr, l_scr, acc_scr, *, tq, q_pos0, win_pos0, n_sel, ncp, tw):
    g = pl.program_id(1)
    qi = pl.program_id(2)
    q0 = q_pos0 + qi * tq
    R = NSA_R * tq
    scale = HD ** -0.5
    last_bias = tuple(tbl_ref[N_BUCKETS - 1, g * NSA_R + r] for r in range(NSA_R))

    @pl.when(qi == 0)
    def _():
        ii = _iota((tq, NSA_TK), 0)
        jj = _iota((tq, NSA_TK), 1)
        for dd in range(NSA_BIAS_TILES):
            bucket = _t5_bucket(ii - jj + dd * LANE)
            for r in range(NSA_R):
                bias_scr[dd, r * tq:(r + 1) * tq, :] = (
                    _bias_from_bucket(bucket, tbl_ref, g * NSA_R + r) - last_bias[r])
        bias_scr[NSA_BIAS_TILES] = jnp.zeros((R, NSA_TK), F32)

    t_col = q0 + _iota((tq, 1), 0)

    bias_tiles, mask_tiles = [], []
    for nt in range(ncp // LANE):
        c_end = (nt * LANE + _iota((tq, LANE), 1)) * CMP_STRIDE + (CMP_BLOCK - 1)
        dist = t_col - c_end
        max_dist = q0 + tq - 1 - (nt * LANE * CMP_STRIDE + CMP_BLOCK - 1)
        min_dist = q0 - ((nt * LANE + LANE - 1) * CMP_STRIDE + CMP_BLOCK - 1)
        special = jnp.logical_and(max_dist >= 0, min_dist < BUCKET_SAT_DIST)

        def general(dist=dist):
            bucket = _t5_bucket(dist)
            return jnp.stack([_bias_from_bucket(bucket, tbl_ref, g * NSA_R + r) - last_bias[r]
                              for r in range(NSA_R)])

        def saturated():
            return jnp.zeros((NSA_R, tq, LANE), F32)

        bias_tiles.append(lax.cond(special, general, saturated))
        mask_tiles.append(dist >= 0)
    mask_c = jnp.concatenate(mask_tiles, axis=1) if len(mask_tiles) > 1 else mask_tiles[0]
    kc = kc_ref[0, 0].astype(BF16)
    vc = vc_ref[0, 0].astype(BF16)
    pcsum = jnp.zeros((tq, ncp), F32)
    o_c = []
    for r in range(NSA_R):
        bias_r = jnp.concatenate([b[r] for b in bias_tiles], axis=1) if len(bias_tiles) > 1 else bias_tiles[0][r]
        s = _dot_nt(q_ref[0, r].astype(BF16), kc) * scale + bias_r
        s = jnp.where(mask_c, s, NEG_BIG)
        m = jnp.max(s, axis=1, keepdims=True)
        p = jnp.where(mask_c, jnp.exp(s - m), 0.0)
        pc = p / jnp.maximum(jnp.sum(p, axis=1, keepdims=True), 1e-30)
        o_c.append(_dot(pc.astype(BF16), vc))
        pcsum = pcsum + pc

    ps_t = _dot_pick(smap_ref[...], pcsum, _dot_nt, exact_lhs=True)
    j_col = _iota((LANE, 1), 0)
    t_row = q0 + _iota((1, tq), 1)
    cur = lax.shift_right_logical(t_row, int(math.log2(SEL_BLOCK)))
    score = jnp.where(j_col * SEL_BLOCK <= t_row, ps_t, -1.0)
    score = jnp.where(j_col == cur - 1, FORCE_SCORE, score)
    score = jnp.where(j_col == cur, FORCE_SCORE, score)
    score = jnp.where(j_col == 0, FORCE_SCORE, score)
    score = jnp.where(j_col < n_sel, score, -3e38)
    score_scr[...] = score

    def rank_body(jp, rank):
        row = score_scr[pl.ds(jp, 1), :]
        tie = jnp.where(j_col > jp, 1.0, 0.0)
        return rank + jnp.where(row > score, 1.0, jnp.where(row == score, tie, 0.0))

    n_rank = jnp.minimum((q0 + tq - 1) // SEL_BLOCK + 1, LANE)
    rank = lax.fori_loop(0, n_rank, rank_body, jnp.zeros((LANE, tq), F32))
    sel_t = jnp.where(rank < SEL_TOPN, 1.0, 0.0).astype(BF16)
    eye = (_iota((tq, tq), 0) == _iota((tq, tq), 1)).astype(BF16)
    sel = _dot_nt(eye, sel_t).astype(BF16)

    qs = (q_ref[0].reshape(R, HD) * scale).astype(BF16)
    t_tile = q0 + _iota((tq, NSA_TK), 0)
    c_tile = _iota((tq, NSA_TK), 1)
    log2_blk = int(math.log2(SEL_BLOCK))

    def stack(a):
        return jnp.concatenate([a] * NSA_R, axis=0)

    def bias_tile(offset):
        return bias_scr[jnp.minimum(lax.shift_right_logical(offset, 7), NSA_BIAS_TILES)]

    def sel_tile(k0, width, near):
        k = ks_ref[0, 0, pl.ds(k0, width), :]
        v = vs_ref[0, 0, pl.ds(k0, width), :]
        blk = lax.shift_right_logical(k0 + _iota((LANE, width), 1), log2_blk)
        expand = jnp.where(_iota((LANE, width), 0) == blk, 1.0, 0.0).astype(BF16)
        chosen = _dot(sel, expand)
        s = _dot_nt(qs, k)
        if near:
            chosen = jnp.where(t_tile - (k0 + c_tile) >= 0, chosen, 0.0)
            s = s + bias_tile(q0 - k0)
        s = jnp.where(stack(chosen) > 0.5, s, NEG_BIG)
        _flash_tile(_lane_blocks(s), v, m_scr, l_scr, acc_scr)

    _flash_reset(m_scr, l_scr, acc_scr)
    kt_top = (q0 + tq - 1) // NSA_TK
    sel_tile(pl.multiple_of(kt_top * NSA_TK, NSA_TK), NSA_TK, True)

    @pl.when(kt_top >= 1)
    def _():
        sel_tile(pl.multiple_of((kt_top - 1) * NSA_TK, NSA_TK), NSA_TK, True)

    n_far = jnp.maximum(kt_top - 1, 0)
    n_groups = n_far // NSA_FAR_GROUP

    def sel_far_group(gi, carry):
        sel_tile(pl.multiple_of(gi * (NSA_FAR_GROUP * NSA_TK), NSA_FAR_GROUP * NSA_TK), NSA_FAR_GROUP * NSA_TK, False)
        return carry

    def sel_far(kt, carry):
        sel_tile(pl.multiple_of(kt * NSA_TK, NSA_TK), NSA_TK, False)
        return carry

    lax.fori_loop(0, n_groups, sel_far_group, 0)
    lax.fori_loop(n_groups * NSA_FAR_GROUP, n_far, sel_far, 0)
    o_s = acc_scr[...] / jnp.maximum(l_scr[...][:, :HD], 1e-30)

    def win_tile(i, carry):
        k0 = pl.multiple_of((wt_top - i) * NSA_TK, NSA_TK)
        k = kw_ref[0, 0, pl.ds(k0, NSA_TK), :]
        v = vw_ref[0, 0, pl.ds(k0, NSA_TK), :]
        dist = t_tile - (win_pos0 + k0 + c_tile)
        ok = jnp.where(dist >= 0, jnp.where(dist < WINDOW, 1.0, 0.0), 0.0)
        s = _dot_nt(qs, k) + bias_tile(q0 - win_pos0 - k0)
        s = jnp.where(stack(ok) > 0.5, s, NEG_BIG)
        _flash_tile(_lane_blocks(s), v, m_scr, l_scr, acc_scr)
        return carry

    _flash_reset(m_scr, l_scr, acc_scr)
    wt_top = (jnp.minimum(q0 + tq - win_pos0, tw) - 1) // NSA_TK
    wt_lo = jnp.maximum(q0 - (WINDOW - 1) - win_pos0, 0) // NSA_TK
    lax.fori_loop(0, wt_top - wt_lo + 1, win_tile, 0)
    o_w = acc_scr[...] / jnp.maximum(l_scr[...][:, :HD], 1e-30)

    gates = gate_ref[0, 0]
    for r in range(NSA_R):
        o_ref[0, r] = (gates[:, 3 * r:3 * r + 1] * o_c[r]
                       + gates[:, 3 * r + 1:3 * r + 2] * o_s[r * tq:(r + 1) * tq]
                       + gates[:, 3 * r + 2:3 * r + 3] * o_w[r * tq:(r + 1) * tq])


def _selection_overlap_t(ncp):
    c_start = np.arange(ncp)[None, :] * CMP_STRIDE
    s_start = np.arange(LANE)[:, None] * SEL_BLOCK
    return ((c_start < s_start + SEL_BLOCK) & (c_start + CMP_BLOCK > s_start)).astype(np.float32)


def nsa_attend(tbl, q, gates, kcvc, sel_arr, sel_off, win_arr, win_off, *, tq, q_pos0, win_pos0, n_sel):
    B, _, Tq, _ = q.shape
    ncp = kcvc.shape[2]
    tks = sel_arr.shape[2]
    tw = win_arr.shape[2]
    R = NSA_R * tq
    smap = jnp.asarray(_selection_overlap_t(ncp))
    kv_spec = lambda rows, off: pl.BlockSpec((1, 1, rows, HD), lambda b, g, i: (b, off + g, 0, 0))
    return pl.pallas_call(
        functools.partial(_nsa_body, tq=tq, q_pos0=q_pos0, win_pos0=win_pos0, n_sel=n_sel, ncp=ncp, tw=tw),
        grid=(B, NSA_G, Tq // tq),
        in_specs=[pl.BlockSpec(memory_space=pltpu.SMEM),
                  pl.BlockSpec((1, NSA_R, tq, HD), lambda b, g, i: (b, g, i, 0)),
                  pl.BlockSpec((1, 1, tq, 3 * NSA_R), lambda b, g, i: (b, g, i, 0)),
                  kv_spec(ncp, 0), kv_spec(ncp, 2),
                  kv_spec(tks, sel_off), kv_spec(tks, sel_off + 2),
                  kv_spec(tw, win_off), kv_spec(tw, win_off + 2),
                  pl.BlockSpec((LANE, ncp), lambda b, g, i: (0, 0))],
        out_specs=pl.BlockSpec((1, NSA_R, tq, HD), lambda b, g, i: (b, g, i, 0)),
        out_shape=jax.ShapeDtypeStruct((B, NSA_G * NSA_R, Tq, HD), F32),
        scratch_shapes=[pltpu.VMEM((NSA_BIAS_TILES + 1, R, NSA_TK), F32), pltpu.VMEM((LANE, tq), F32),
                        pltpu.VMEM((R, LANE), F32), pltpu.VMEM((R, LANE), F32), pltpu.VMEM((R, HD), F32)],
        compiler_params=_cparams(("parallel", "parallel", "arbitrary")),
        name="nsa_attend",
    )(tbl, q, gates, kcvc, kcvc, sel_arr, sel_arr, win_arr, win_arr, smap)


def _nsa_gather_body(pt_ref, new_ref, *refs, n_pages):
    pages = refs[:n_pages]
    cmp_ref, sel_ref = refs[n_pages:]
    ngrp = 2 * NSA_G
    for pg in range(n_pages):
        for j in range(ngrp):
            cmp_ref[0, j, pg * PAGE:(pg + 1) * PAGE, :] = pages[pg][0, :, j * HD:(j + 1) * HD]
            sel_ref[0, j, pg * PAGE:(pg + 1) * PAGE, :] = (
                pages[pg][0, :, (ngrp + j) * HD:(ngrp + j + 1) * HD].astype(BF16))
    tn = new_ref.shape[1]
    tail = sel_ref.shape[2] - n_pages * PAGE
    for j in range(ngrp):
        new = new_ref[0, :, (ngrp + j) * HD:(ngrp + j + 1) * HD]
        tile = jnp.concatenate([new, jnp.zeros((tail - tn, HD), F32)], axis=0)
        sel_ref[0, j, n_pages * PAGE:, :] = tile.astype(BF16)


def nsa_gather(page_table, new_rows, pool):
    B, tn, width = new_rows.shape
    n_pages = page_table.shape[1]
    ngrp = 2 * NSA_G
    sel_rows = -(-(n_pages * PAGE + tn) // NSA_TK) * NSA_TK
    grid_spec = pltpu.PrefetchScalarGridSpec(
        num_scalar_prefetch=1,
        grid=(B,),
        in_specs=[pl.BlockSpec((1, tn, width), lambda b, pt: (b, 0, 0))]
                 + [pl.BlockSpec((1, PAGE, width), functools.partial(lambda b, pt, pg: (pt[b, pg], 0, 0), pg=pg))
                    for pg in range(n_pages)],
        out_specs=[pl.BlockSpec((1, ngrp, n_pages * PAGE, HD), lambda b, pt: (b, 0, 0, 0)),
                   pl.BlockSpec((1, ngrp, sel_rows, HD), lambda b, pt: (b, 0, 0, 0))],
    )
    return pl.pallas_call(
        functools.partial(_nsa_gather_body, n_pages=n_pages),
        grid_spec=grid_spec,
        out_shape=[jax.ShapeDtypeStruct((B, ngrp, n_pages * PAGE, HD), F32),
                   jax.ShapeDtypeStruct((B, ngrp, sel_rows, HD), BF16)],
        compiler_params=_cparams(("arbitrary",)),
        name="nsa_gather",
    )(page_table, new_rows, *([pool] * n_pages))


def _row_tile(m):
    return 512 if m % 512 == 0 else m


def _small_params(entries):
    sp = jnp.zeros((8, LANE), F32)
    for off, bias, act, log_scale in entries:
        n = bias.shape[0]
        sp = sp.at[0, off:off + n].set(bias.astype(F32))
        sp = sp.at[1, off:off + n].set(act)
        if log_scale is not None:
            sp = sp.at[2, off:off + n].set(log_scale.astype(F32))
    return sp


EVEN_WIDTHS = (256, 256, 512, 512, GD_CH, 512, LANE)


def _even_weights(w_in):
    s = np.cumsum((0, 256, 256, 512, 512, 4, 4, 512, 512, 512, 512, 4, 4))
    col = lambda i: w_in[:, s[i]:s[i + 1]]
    small = jnp.concatenate([col(4), col(5), col(10), col(11)], axis=1)
    small = jnp.pad(small, ((0, 0), (0, LANE - small.shape[1])))
    return jnp.concatenate([col(0), col(1), col(2), col(3), col(6), col(7), col(8), col(9), small], axis=1).astype(BF16)


def _chunk_rows(small, B, T, L, lanes):
    r = small.reshape(B, T // L, L, LANE)[..., lanes[0]:lanes[1]]
    r = jnp.swapaxes(r, 2, 3)
    return jnp.pad(r, ((0, 0), (0, 0), (0, 8 - r.shape[2]), (0, 0)))


def even_layer(x, p, past, L, Bb):
    B, T, D = x.shape
    M = B * T
    tm = _row_tile(M)
    sp = _small_params([(0, p['mi_b_i'], ACT_ID, None), (4, p['mi_b_f'], ACT_LOGSIG, None),
                        (8, jnp.zeros((4,), F32), ACT_SIG, None), (12, p['gd_dt_bias'], ACT_DECAY, p['gd_a_log'])])
    mq, mk, mv, mo, gx, gz, small = norm_proj(x.reshape(M, D), p['norm_mix'], _even_weights(p['w_in']), sp,
                                              EVEN_WIDTHS, tm)
    heads = lambda a: jnp.transpose(a.reshape(B, T, MI_H, MI_DQK), (0, 2, 1, 3))
    sc = small.reshape(B, T, LANE)
    if past is None:
        c0 = jnp.zeros((B, MI_H, MI_DQK, MI_DV), F32)
        n0 = jnp.zeros((B, MI_H, 1, MI_DQK), F32)
        m0 = jnp.zeros((B, 1, MI_H), F32)
        s0 = jnp.zeros((B, GD_H, GD_DK, GD_DV), F32)
        conv0 = jnp.zeros((B, GD_CONV - 1, GD_CH), F32)
    else:
        c0, n0, m0, s0, conv0 = past
        n0 = n0.reshape(B, MI_H, 1, MI_DQK)
        m0 = m0.reshape(B, 1, MI_H)
    hm, c1, n1, m1 = mlstm(heads(mq), heads(mk), mv.reshape(B, T, -1), mo.reshape(B, T, -1), sc,
                           _chunk_rows(small, B, T, L, (0, 8)), c0, n0, m0,
                           p['mi_norm'].reshape(MI_H, MI_DV), L, Bb)
    og, s1, conv1 = gdn(gx.reshape(B, T, GD_CH), gz.reshape(B, T, -1), sc,
                        _chunk_rows(small, B, T, L, (12, 16)), s0, conv0, p['gd_conv_w'], p['gd_norm'], L, Bb)
    y = out_proj_residual(x.reshape(M, D), hm.reshape(M, -1), og.reshape(M, -1), p['w_out'].astype(BF16), tm)
    return y.reshape(B, T, D), (c1, n1.reshape(B, MI_H, MI_DQK), m1.reshape(B, MI_H), s1, conv1)


NSA_QW = NSA_G * NSA_R * HD
NSA_KVW = 6 * NSA_G * HD
NSA_CACHE_W = 4 * NSA_G * HD
FOX_W = FOX_H * HD
N_GATE = 3 * NSA_G * NSA_R
ODD_WIDTHS = (NSA_QW, NSA_KVW, FOX_W, 2 * FOX_W, LANE)


def _odd_weights(w_in):
    s = np.cumsum((0, NSA_QW, NSA_KVW, N_GATE, FOX_W, FOX_W, FOX_W, FOX_H))
    col = lambda i: w_in[:, s[i]:s[i + 1]]
    small = jnp.concatenate([col(2), col(6)], axis=1)
    small = jnp.pad(small, ((0, 0), (0, LANE - small.shape[1])))
    return jnp.concatenate([col(0), col(1), col(3), col(4), col(5), small], axis=1).astype(BF16)


def _heads(a, B, T, n):
    return jnp.transpose(a.reshape(B, T, n, HD), (0, 2, 1, 3))


def _unheads(a):
    B, n, T, _ = a.shape
    return jnp.transpose(a, (0, 2, 1, 3)).reshape(B * T, n * HD)


def odd_layer(x, p, rel_bias, w_buf, past, page_table):
    B, T, D = x.shape
    M = B * T
    tm = _row_tile(M)
    sp = _small_params([(0, jnp.zeros((N_GATE,), F32), ACT_SIG, None), (N_GATE, p['fox_b_f'], ACT_LOGSIG, None)])
    nq, nkv, fq, fkv, small = norm_proj(x.reshape(M, D), p['norm_mix'], _odd_weights(p['w_in']), sp, ODD_WIDTHS, tm)
    new_nsa = nkv[:, :NSA_CACHE_W].reshape(B, T, 4, NSA_G, HD)
    new_win = nkv[:, NSA_CACHE_W:].reshape(B, T, 2, NSA_G, HD)
    new_fox = fkv.reshape(B, T, 2, FOX_H, HD)
    logf = small[:, N_GATE:N_GATE + FOX_H].reshape(B, T, FOX_H)
    q_heads = _heads(nq, B, T, NSA_G * NSA_R)
    gates = jnp.transpose(small[:, :N_GATE].reshape(B, T, NSA_G, 3 * NSA_R), (0, 2, 1, 3))
    cmp_args = (p['nsa_cmp_pos'], p['nsa_cmp_w1'], p['nsa_cmp_w2'])
    rows16 = CMP_STRIDE * HD
    if past is None:
        groups = _heads(nkv, B, T, 6 * NSA_G)
        kcvc = nsa_compress(groups.reshape(B, 6 * NSA_G, T // CMP_STRIDE, rows16), T // CMP_STRIDE, *cmp_args, 1)
        arr = groups.astype(BF16)
        o_n = nsa_attend(rel_bias, q_heads, gates, kcvc, arr, 2 * NSA_G, arr, 4 * NSA_G,
                         tq=min(T, LANE), q_pos0=0, win_pos0=0, n_sel=-(-T // SEL_BLOCK))
        f_t = cumsum_lanes(jnp.transpose(logf, (0, 2, 1)), B)
        o_f = fox_prompt(_heads(fq * HD ** -0.5, B, T, FOX_H).astype(BF16), _heads(fkv, B, T, 2 * FOX_H).astype(BF16),
                         jnp.transpose(f_t, (0, 2, 1)), f_t, min(T, FOX_TQ), min(T, FOX_TK))
        o_f = _unheads(o_f)
        win_prev = jnp.zeros((B, WINDOW, 2, NSA_G, HD), F32)
    else:
        nsa_pool, win_prev, fox_pool, logf_pool = past
        n_pool = nsa_pool.shape[0]
        n_pages = page_table.shape[1]
        start = n_pages * PAGE
        cmp_rows, sel_arr = nsa_gather(page_table, nkv[:, :NSA_CACHE_W].reshape(B, T, NSA_CACHE_W),
                                       nsa_pool.reshape(n_pool, PAGE, NSA_CACHE_W))
        kcvc = nsa_compress(cmp_rows.reshape(B, 2 * NSA_G, start // CMP_STRIDE, rows16), start // CMP_STRIDE,
                            *cmp_args, math.gcd(B, 8))
        wp = win_prev.shape[1]
        win_all = jnp.concatenate([win_prev.reshape(B, wp, 2 * NSA_G * HD), nkv[:, NSA_CACHE_W:].reshape(B, T, -1)], 1)
        tw = -(-(wp + T) // NSA_TK) * NSA_TK
        win_arr = _heads(jnp.pad(win_all, ((0, 0), (0, tw - wp - T), (0, 0))), B, tw, 2 * NSA_G).astype(BF16)
        o_n = nsa_attend(rel_bias, q_heads, gates, kcvc, sel_arr, 0, win_arr, 0,
                         tq=T, q_pos0=start, win_pos0=start - wp, n_sel=-(-(start + T) // SEL_BLOCK))
        o_f = fox_decode(page_table, fq.reshape(B, T, FOX_W), fkv.reshape(B, T, 2 * FOX_W), logf,
                         fox_pool.reshape(n_pool, PAGE, 2 * FOX_W), logf_pool)
        o_f = o_f.reshape(M, FOX_W)
    win_state = jnp.concatenate([win_prev, new_win], axis=1)[:, -w_buf:]
    y = out_proj_residual(x.reshape(M, D), _unheads(o_n), o_f, p['w_out'].astype(BF16), tm)
    return y.reshape(B, T, D), (new_nsa, win_state, new_fox, logf)


def _trunk(x, past, page_table, P, w_buf, L, Bb):
    B, T, D = x.shape
    pe = dict(norm_mix=P['norm_mix'][0], w_in=P['w_in_even'][0], w_out=P['w_out_even'][0], mi_b_i=P['mi_b_i'][0],
              mi_b_f=P['mi_b_f'][0], mi_norm=P['mi_norm'][0], gd_conv_w=P['gd_conv_w'][0], gd_a_log=P['gd_a_log'][0],
              gd_dt_bias=P['gd_dt_bias'][0], gd_norm=P['gd_norm'][0])
    po = dict(norm_mix=P['norm_mix'][1], w_in=P['w_in_odd'][0], w_out=P['w_out_odd'][0],
              nsa_cmp_pos=P['nsa_cmp_pos'][0], nsa_cmp_w1=P['nsa_cmp_w1'][0], nsa_cmp_w2=P['nsa_cmp_w2'][0],
              fox_b_f=P['fox_b_f'][0])
    tm = _row_tile(B * T)
    mlp = lambda x, layer, final: mlp_residual(
        x.reshape(B * T, D), P['norm_mlp'][layer], P['w_up'][layer].astype(BF16), P['w_down'][layer].astype(BF16),
        P['norm_final'], final, tm, 1024).reshape(B, T, D)
    even_past = None if past is None else tuple(past[k][0] for k in ('mc', 'mn', 'mm', 'gs', 'gc'))
    odd_past = None if past is None else tuple(past[k][0] for k in ('nsa_kv', 'nsa_win', 'fox_kv', 'fox_logf'))
    x, st_e = even_layer(x, pe, even_past, L, Bb)
    x = mlp(x, 0, False)
    x, st_o = odd_layer(x, po, P['rel_bias'], w_buf, odd_past, page_table)
    y = mlp(x, 1, True)
    return y, tuple(a[None] for a in st_e + st_o)


def kernel(x_prompt, x_sample, state_mlstm_c, state_mlstm_n, state_mlstm_m, state_gdn_s, state_gdn_conv,
           cache_nsa_kv, state_nsa_win, cache_fox_kv, cache_fox_logf, page_table,
           norm_mix, norm_mlp, norm_final, w_up, w_down,
           w_in_even, w_out_even, mi_b_i, mi_b_f, mi_norm, gd_conv_w, gd_a_log, gd_dt_bias, gd_norm,
           w_in_odd, w_out_odd, nsa_cmp_pos, nsa_cmp_w1, nsa_cmp_w2, fox_b_f, rel_bias):
    P = dict(norm_mix=norm_mix, norm_mlp=norm_mlp, norm_final=norm_final, w_up=w_up, w_down=w_down,
             w_in_even=w_in_even, w_out_even=w_out_even, mi_b_i=mi_b_i, mi_b_f=mi_b_f, mi_norm=mi_norm,
             gd_conv_w=gd_conv_w, gd_a_log=gd_a_log, gd_dt_bias=gd_dt_bias, gd_norm=gd_norm,
             w_in_odd=w_in_odd, w_out_odd=w_out_odd, nsa_cmp_pos=nsa_cmp_pos, nsa_cmp_w1=nsa_cmp_w1,
             nsa_cmp_w2=nsa_cmp_w2, fox_b_f=fox_b_f, rel_bias=rel_bias)
    w_buf = state_nsa_win.shape[2]
    b_p, t_p = x_prompt.shape[:2]
    y_p, st_p = _trunk(x_prompt, None, None, P, w_buf, math.gcd(t_p, 64), math.gcd(b_p, RECURRENT_BATCH))
    past = dict(mc=state_mlstm_c, mn=state_mlstm_n, mm=state_mlstm_m, gs=state_gdn_s, gc=state_gdn_conv,
                nsa_kv=cache_nsa_kv, nsa_win=state_nsa_win, fox_kv=cache_fox_kv, fox_logf=cache_fox_logf)
    b_s, t_s = x_sample.shape[:2]
    y_s, st_s = _trunk(x_sample, past, page_table, P, w_buf, math.gcd(t_s, 64), math.gcd(b_s, RECURRENT_BATCH))
    return (y_p, y_s) + st_p + st_s
```

```python
import functools
import math

import jax
import jax.numpy as jnp
import numpy as np
from jax import lax
from jax.experimental import pallas as pl
from jax.experimental.pallas import tpu as pltpu

F32 = jnp.float32
BF16 = jnp.bfloat16
HI = lax.Precision.HIGHEST

D_MODEL = 1024
D_FF = 4 * D_MODEL
EPS = 1e-6
NEG_BIG = -1e30
PAGE = 128

MI_H, MI_DQK, MI_DV = 4, 64, 128
GD_H, GD_DK, GD_DV, GD_CONV = 4, 128, 128, 4
GD_CH = 3 * GD_H * GD_DK
NSA_G, NSA_R, HD = 2, 4, 64
FOX_H = 8
CMP_BLOCK, CMP_STRIDE, CMP_HIDDEN = 32, 16, 256
SEL_BLOCK, SEL_TOPN, WINDOW = 64, 16, 512
FORCE_SCORE = 1e4
N_BUCKETS, MAX_DISTANCE = 32, 128
BUCKET_EXACT = N_BUCKETS // 2
BUCKET_SAT_DIST = 113
LANE = 128
VMEM_LIMIT = 56 * 1024 * 1024


def _cparams(sem):
    return pltpu.CompilerParams(dimension_semantics=sem, vmem_limit_bytes=VMEM_LIMIT)


def _dot(a, b, precision=None):
    return jnp.dot(a, b, preferred_element_type=F32, precision=precision)


def _dot_nt(a, b, precision=None):
    return lax.dot_general(a, b, (((1,), (1,)), ((), ())), preferred_element_type=F32, precision=precision)


def _dot_tn(a, b, precision=None):
    return lax.dot_general(a, b, (((0,), (0,)), ((), ())), preferred_element_type=F32, precision=precision)


def _softplus(x):
    return jnp.maximum(x, 0.0) + jnp.log1p(jnp.exp(-jnp.abs(x)))


def _sigmoid(x):
    return 1.0 / (1.0 + jnp.exp(-x))


def _silu(x):
    return x * _sigmoid(x)


def _iota(shape, dim):
    return lax.broadcasted_iota(jnp.int32, shape, dim)


ACT_ID, ACT_LOGSIG, ACT_SIG, ACT_DECAY = 0.0, 1.0, 2.0, 3.0


def _proj_body(x_ref, g_ref, w_ref, sp_ref, *out_refs, widths):
    x = x_ref[...]
    hn = (x * lax.rsqrt(jnp.mean(x * x, axis=-1, keepdims=True) + EPS) * g_ref[...]).astype(BF16)
    off = 0
    for i, (o_ref, n) in enumerate(zip(out_refs, widths)):
        r = _dot(hn, w_ref[:, off:off + n])
        if i == len(widths) - 1:
            z = r + sp_ref[0:1, :]
            mode = sp_ref[1:2, :]
            decay = -jnp.exp(sp_ref[2:3, :]) * _softplus(z)
            r = jnp.where(mode == ACT_LOGSIG, -_softplus(-z),
                          jnp.where(mode == ACT_SIG, _sigmoid(z),
                                    jnp.where(mode == ACT_DECAY, decay, z)))
        o_ref[...] = r.astype(o_ref.dtype)
        off += n


def norm_proj(x, g, w_bf16, small_params, widths, tm):
    m, d = x.shape
    n_total = sum(widths)
    assert w_bf16.shape == (d, n_total) and m % tm == 0
    out_shape = [jax.ShapeDtypeStruct((m, n), F32) for n in widths]
    return pl.pallas_call(
        functools.partial(_proj_body, widths=tuple(widths)),
        grid=(m // tm,),
        in_specs=[pl.BlockSpec((tm, d), lambda i: (i, 0)),
                  pl.BlockSpec((1, d), lambda i: (0, 0)),
                  pl.BlockSpec((d, n_total), lambda i: (0, 0)),
                  pl.BlockSpec((8, LANE), lambda i: (0, 0))],
        out_specs=[pl.BlockSpec((tm, n), lambda i: (i, 0)) for n in widths],
        out_shape=out_shape,
        compiler_params=_cparams(("parallel",)),
        name="norm_proj",
    )(x, g.reshape(1, d), w_bf16, small_params)


def _outproj_body(x_ref, a1_ref, a2_ref, w_ref, o_ref):
    k1 = a1_ref.shape[1]
    y = _dot(a1_ref[...].astype(BF16), w_ref[0:k1, :]) + _dot(a2_ref[...].astype(BF16), w_ref[k1:, :])
    o_ref[...] = x_ref[...] + y


def out_proj_residual(x, a1, a2, w_bf16, tm):
    m, d = x.shape
    k1, k2 = a1.shape[1], a2.shape[1]
    return pl.pallas_call(
        _outproj_body,
        grid=(m // tm,),
        in_specs=[pl.BlockSpec((tm, d), lambda i: (i, 0)),
                  pl.BlockSpec((tm, k1), lambda i: (i, 0)),
                  pl.BlockSpec((tm, k2), lambda i: (i, 0)),
                  pl.BlockSpec((k1 + k2, d), lambda i: (0, 0))],
        out_specs=pl.BlockSpec((tm, d), lambda i: (i, 0)),
        out_shape=jax.ShapeDtypeStruct((m, d), F32),
        compiler_params=_cparams(("parallel",)),
        name="out_proj",
    )(x, a1, a2, w_bf16)


def _mlp_body(x_ref, g_ref, wu_ref, wd_ref, gf_ref, o_ref, hn_scr, acc_scr, *, final_norm):
    j = pl.program_id(1)

    @pl.when(j == 0)
    def _():
        x = x_ref[...]
        hn_scr[...] = (x * lax.rsqrt(jnp.mean(x * x, axis=-1, keepdims=True) + EPS) * g_ref[...]).astype(BF16)
        acc_scr[...] = jnp.zeros_like(acc_scr)

    u = jnp.maximum(_dot(hn_scr[...], wu_ref[...]), 0.0)
    acc_scr[...] += _dot((u * u).astype(BF16), wd_ref[...])

    @pl.when(j == pl.num_programs(1) - 1)
    def _():
        y = x_ref[...] + acc_scr[...]
        if final_norm:
            y = y * lax.rsqrt(jnp.mean(y * y, axis=-1, keepdims=True) + EPS) * gf_ref[...]
        o_ref[...] = y


def mlp_residual(x, g, w_up_bf16, w_down_bf16, g_final, final_norm, tm, tf):
    m, d = x.shape
    f = w_up_bf16.shape[1]
    return pl.pallas_call(
        functools.partial(_mlp_body, final_norm=final_norm),
        grid=(m // tm, f // tf),
        in_specs=[pl.BlockSpec((tm, d), lambda i, j: (i, 0)),
                  pl.BlockSpec((1, d), lambda i, j: (0, 0)),
                  pl.BlockSpec((d, tf), lambda i, j: (0, j)),
                  pl.BlockSpec((tf, d), lambda i, j: (j, 0)),
                  pl.BlockSpec((1, d), lambda i, j: (0, 0))],
        out_specs=pl.BlockSpec((tm, d), lambda i, j: (i, 0)),
        out_shape=jax.ShapeDtypeStruct((m, d), F32),
        scratch_shapes=[pltpu.VMEM((tm, d), BF16), pltpu.VMEM((tm, d), F32)],
        compiler_params=_cparams(("parallel", "arbitrary")),
        name="mlp",
    )(x, g.reshape(1, d), w_up_bf16, w_down_bf16, g_final.reshape(1, d))


def _tri(n):
    r = _iota((n, n), 0)
    c = _iota((n, n), 1)
    return r, c


def _bf16_terms(a, n):
    terms, rest = [], a
    for _ in range(n):
        t = rest.astype(BF16)
        terms.append(t)
        rest = rest - t.astype(F32)
    return terms


def _dot_split(a, b, f=None):
    f = f or _dot
    a_hi, a_lo = _bf16_terms(a, 2)
    b_hi, b_lo = _bf16_terms(b, 2)
    return f(a_hi, b_hi) + (f(a_hi, b_lo) + f(a_lo, b_hi))


def _dot_pick(a, b, f=None, exact_lhs=False):
    f = f or _dot
    if exact_lhs:
        return sum(f(a.astype(BF16), t) for t in _bf16_terms(b, 3))
    return sum(f(t, b.astype(BF16)) for t in _bf16_terms(a, 3))


def _chunk_dots(L):
    if L % 16 == 0:
        cast = lambda f: (lambda a, b: f(a.astype(BF16), b.astype(BF16)))
        return cast(_dot), cast(_dot_nt), cast(_dot_tn)
    full = lambda f: (lambda a, b: _dot_split(a, b, f))
    return full(_dot), full(_dot_nt), full(_dot_tn)


def _mlstm_group(bbs, q_ref, k_ref, v_ref, o_ref, sc_ref, sr_ref, nw_ref, h_ref, c_ref, n_ref, m_ref, L):
    nn, nt, tn = _chunk_dots(L)
    rows, cols = _tri(L)
    lower = rows >= cols
    tril = lower.astype(F32)
    triu = (rows <= cols).astype(F32)
    lane4 = _iota((1, MI_H), 1)
    scs = {bb: sc_ref[bb] for bb in bbs}
    srs = {bb: sr_ref[bb, 0] for bb in bbs}
    b_col = {bb: _dot_pick(tril, scs[bb][:, 4:8], exact_lhs=True) for bb in bbs}
    b_row = {bb: _dot_pick(srs[bb][4:8, :], triu) for bb in bbs}
    m_vec = {bb: m_ref[bb] for bb in bbs}
    chains = [(bb, h) for bb in bbs for h in range(MI_H)]
    q = {c: q_ref[c[0], c[1]] for c in chains}
    k = {c: k_ref[c[0], c[1]] * (MI_DQK ** -0.5) for c in chains}
    v = {c: v_ref[c[0], :, c[1] * MI_DV:(c[1] + 1) * MI_DV] for c in chains}
    qk = {c: nt(q[c], k[c]) for c in chains}
    c_prev = {c: c_ref[c[0], c[1]] for c in chains}
    qc = {c: nn(q[c], c_prev[c]) for c in chains}
    s, a_inter, m_t, m_new, a_prev, kw = {}, {}, {}, {}, {}, {}
    for c in chains:
        bb, h = c
        bc = b_col[bb][:, h:h + 1]
        m_prev = m_vec[bb][:, h:h + 1]
        d = jnp.where(lower, bc - b_row[bb][h:h + 1, :] + srs[bb][h:h + 1, :], NEG_BIG)
        inter = bc + m_prev
        m_t[c] = jnp.maximum(inter, jnp.max(d, axis=1, keepdims=True))
        s[c] = qk[c] * jnp.exp(d - m_t[c])
        a_inter[c] = jnp.exp(inter - m_t[c])
        b_last = bc[L - 1:L, :]
        g_col = b_last - bc + scs[bb][:, h:h + 1]
        m_new[c] = jnp.maximum(b_last + m_prev, jnp.max(g_col, axis=0, keepdims=True))
        a_prev[c] = jnp.exp(b_last + m_prev - m_new[c])
        kw[c] = k[c] * jnp.exp(g_col - m_new[c])
    sv = {c: nn(s[c], v[c]) for c in chains}
    kv = {c: tn(kw[c], v[c]) for c in chains}
    for c in chains:
        bb, h = c
        n_prev = n_ref[bb, h]
        num = sv[c] + a_inter[c] * qc[c]
        den = (jnp.sum(s[c], axis=1, keepdims=True)
               + a_inter[c] * jnp.sum(q[c] * n_prev, axis=1, keepdims=True))
        hh = num / jnp.maximum(jnp.abs(den), jnp.exp(-m_t[c]))
        c_ref[bb, h] = a_prev[c] * c_prev[c] + kv[c]
        n_ref[bb, h] = a_prev[c] * n_prev + jnp.sum(kw[c], axis=0, keepdims=True)
        m_vec[bb] = jnp.where(lane4 == h, m_new[c], m_vec[bb])
        hn = hh * lax.rsqrt(jnp.mean(hh * hh, axis=-1, keepdims=True) + EPS) * nw_ref[h:h + 1, :]
        gate = _sigmoid(o_ref[bb, :, h * MI_DV:(h + 1) * MI_DV])
        h_ref[bb, :, h * MI_DV:(h + 1) * MI_DV] = hn * gate
    for bb in bbs:
        m_ref[bb] = m_vec[bb]


def _mlstm_body(q_ref, k_ref, v_ref, o_ref, sc_ref, sr_ref, c0_ref, n0_ref, m0_ref, nw_ref,
                h_ref, c_ref, n_ref, m_ref, *, L, Bb):
    @pl.when(pl.program_id(1) == 0)
    def _():
        c_ref[...] = c0_ref[...]
        n_ref[...] = n0_ref[...]
        m_ref[...] = m0_ref[...]

    _mlstm_group(list(range(Bb)), q_ref, k_ref, v_ref, o_ref, sc_ref, sr_ref, nw_ref, h_ref, c_ref, n_ref, m_ref, L)


def mlstm(q, k, v, o, sc, sr, c0, n0, m0, norm_w, L, Bb):
    B, H, T, _ = q.shape
    nc = T // L
    hv = H * MI_DV
    bmap = lambda b, c: (b, 0, 0, 0)
    return pl.pallas_call(
        functools.partial(_mlstm_body, L=L, Bb=Bb),
        grid=(B // Bb, nc),
        in_specs=[pl.BlockSpec((Bb, H, L, MI_DQK), lambda b, c: (b, 0, c, 0)),
                  pl.BlockSpec((Bb, H, L, MI_DQK), lambda b, c: (b, 0, c, 0)),
                  pl.BlockSpec((Bb, L, hv), lambda b, c: (b, c, 0)),
                  pl.BlockSpec((Bb, L, hv), lambda b, c: (b, c, 0)),
                  pl.BlockSpec((Bb, L, LANE), lambda b, c: (b, c, 0)),
                  pl.BlockSpec((Bb, 1, 8, L), lambda b, c: (b, c, 0, 0)),
                  pl.BlockSpec((Bb, H, MI_DQK, MI_DV), bmap),
                  pl.BlockSpec((Bb, H, 1, MI_DQK), bmap),
                  pl.BlockSpec((Bb, 1, H), lambda b, c: (b, 0, 0)),
                  pl.BlockSpec((H, MI_DV), lambda b, c: (0, 0))],
        out_specs=[pl.BlockSpec((Bb, L, hv), lambda b, c: (b, c, 0)),
                   pl.BlockSpec((Bb, H, MI_DQK, MI_DV), bmap),
                   pl.BlockSpec((Bb, H, 1, MI_DQK), bmap),
                   pl.BlockSpec((Bb, 1, H), lambda b, c: (b, 0, 0))],
        out_shape=[jax.ShapeDtypeStruct((B, T, hv), F32),
                   jax.ShapeDtypeStruct((B, H, MI_DQK, MI_DV), F32),
                   jax.ShapeDtypeStruct((B, H, 1, MI_DQK), F32),
                   jax.ShapeDtypeStruct((B, 1, H), F32)],
        compiler_params=_cparams(("parallel", "arbitrary")),
        name="mlstm",
    )(q, k, v, o, sc, sr, c0, n0, m0, norm_w)


def _gdn_group(bbs, x_ref, z_ref, sc_ref, sr_ref, cw_ref, nw_ref, o_ref, s_ref, conv_ref, xp_scr, L):
    base = 8 - (GD_CONV - 1)
    nn, nt, tn = _chunk_dots(L)
    rows, cols = _tri(L)
    incl = rows >= cols
    strict = rows > cols
    eye = (rows == cols).astype(F32)
    tril = incl.astype(F32)
    triu = (rows <= cols).astype(F32)
    kw = GD_H * GD_DK
    ys, scs, gcols, grows = {}, {}, {}, {}
    for bb in bbs:
        xp_scr[bb, 8:8 + L, :] = x_ref[bb]
        y = xp_scr[bb, base:base + L, :] * cw_ref[0:1, :]
        for j in range(1, GD_CONV):
            y = y + xp_scr[bb, base + j:base + j + L, :] * cw_ref[j:j + 1, :]
        tail = xp_scr[bb, L + base:L + 8, :]
        xp_scr[bb, base:8, :] = tail
        conv_ref[bb] = tail
        ys[bb] = _silu(y)
        scs[bb] = sc_ref[bb]
        gcols[bb] = _dot_pick(tril, scs[bb][:, 12:16], exact_lhs=True)
        grows[bb] = _dot_pick(sr_ref[bb, 0][0:4, :], triu)
    chains = [(bb, h) for bb in bbs for h in range(GD_H)]
    q, k, vb, kb, dec, gcc, egc = {}, {}, {}, {}, {}, {}, {}
    for c in chains:
        bb, h = c
        y = ys[bb]
        qh = y[:, h * GD_DK:(h + 1) * GD_DK]
        kh = y[:, kw + h * GD_DK:kw + (h + 1) * GD_DK]
        vh = y[:, 2 * kw + h * GD_DV:2 * kw + (h + 1) * GD_DV]
        q[c] = qh * lax.rsqrt(jnp.sum(qh * qh, axis=-1, keepdims=True) + EPS) * (GD_DK ** -0.5)
        k[c] = kh * lax.rsqrt(jnp.sum(kh * kh, axis=-1, keepdims=True) + EPS)
        beta = scs[bb][:, 8 + h:9 + h]
        gcc[c] = gcols[bb][:, h:h + 1]
        dec[c] = jnp.exp(jnp.where(incl, gcc[c] - grows[bb][h:h + 1, :], NEG_BIG))
        egc[c] = jnp.exp(gcc[c])
        kb[c] = k[c] * beta
        vb[c] = vh * beta
    pw = {c: -(nt(kb[c], k[c]) * jnp.where(strict, dec[c], 0.0)) for c in chains}
    attn = {c: nt(q[c], k[c]) * dec[c] for c in chains}
    tinv = {c: eye + pw[c] for c in chains}
    for _ in range(int(math.log2(L)) - 1):
        pw = {c: _dot_split(pw[c], pw[c]) for c in chains}
        tinv = {c: tinv[c] + _dot_split(tinv[c], pw[c]) for c in chains}
    u = {c: _dot_split(tinv[c], vb[c]) for c in chains}
    w = {c: _dot_split(tinv[c], kb[c] * egc[c]) for c in chains}
    s_prev = {c: s_ref[c[0], c[1]] for c in chains}
    v_new = {c: u[c] - nn(w[c], s_prev[c]) for c in chains}
    o = {c: nn(q[c] * egc[c], s_prev[c]) + nn(attn[c], v_new[c]) for c in chains}
    for c in chains:
        bb, h = c
        g_last = gcc[c][L - 1:L, :]
        s_ref[bb, h] = jnp.exp(g_last) * s_prev[c] + tn(k[c] * jnp.exp(g_last - gcc[c]), v_new[c])
        on = o[c] * lax.rsqrt(jnp.mean(o[c] * o[c], axis=-1, keepdims=True) + EPS) * nw_ref[...]
        o_ref[bb, :, h * GD_DV:(h + 1) * GD_DV] = on * _silu(z_ref[bb, :, h * GD_DV:(h + 1) * GD_DV])


def _gdn_body(x_ref, z_ref, sc_ref, sr_ref, s0_ref, conv0_ref, cw_ref, nw_ref,
              o_ref, s_ref, conv_ref, xp_scr, *, L, Bb):
    @pl.when(pl.program_id(1) == 0)
    def _():
        s_ref[...] = s0_ref[...]
        xp_scr[:, 8 - (GD_CONV - 1):8, :] = conv0_ref[...]

    _gdn_group(list(range(Bb)), x_ref, z_ref, sc_ref, sr_ref, cw_ref, nw_ref, o_ref, s_ref, conv_ref, xp_scr, L)


def gdn(x, z, sc, sr, s0, conv0, conv_w, norm_w, L, Bb):
    B, T, ch = x.shape
    H = GD_H
    nc = T // L
    hv = H * GD_DV
    bmap = lambda b, c: (b, 0, 0, 0)
    return pl.pallas_call(
        functools.partial(_gdn_body, L=L, Bb=Bb),
        grid=(B // Bb, nc),
        in_specs=[pl.BlockSpec((Bb, L, ch), lambda b, c: (b, c, 0)),
                  pl.BlockSpec((Bb, L, hv), lambda b, c: (b, c, 0)),
                  pl.BlockSpec((Bb, L, LANE), lambda b, c: (b, c, 0)),
                  pl.BlockSpec((Bb, 1, 8, L), lambda b, c: (b, c, 0, 0)),
                  pl.BlockSpec((Bb, H, GD_DK, GD_DV), bmap),
                  pl.BlockSpec((Bb, GD_CONV - 1, ch), lambda b, c: (b, 0, 0)),
                  pl.BlockSpec((GD_CONV, ch), lambda b, c: (0, 0)),
                  pl.BlockSpec((1, GD_DV), lambda b, c: (0, 0))],
        out_specs=[pl.BlockSpec((Bb, L, hv), lambda b, c: (b, c, 0)),
                   pl.BlockSpec((Bb, H, GD_DK, GD_DV), bmap),
                   pl.BlockSpec((Bb, GD_CONV - 1, ch), lambda b, c: (b, 0, 0))],
        out_shape=[jax.ShapeDtypeStruct((B, T, hv), F32),
                   jax.ShapeDtypeStruct((B, H, GD_DK, GD_DV), F32),
                   jax.ShapeDtypeStruct((B, GD_CONV - 1, ch), F32)],
        scratch_shapes=[pltpu.VMEM((Bb, L + 8, ch), F32)],
        compiler_params=_cparams(("parallel", "arbitrary")),
        name="gdn",
    )(x, z, sc, sr, s0, conv0, conv_w, norm_w.reshape(1, GD_DV))


def _compress_body(x_ref, pos_ref, w1_ref, w2_ref, o_ref):
    Bb, _, R, half = x_ref.shape
    x = x_ref[:, 0].reshape(Bb * R, half).astype(F32)
    ua = _dot((x + pos_ref[0, 0:1, :]).astype(BF16), w1_ref[0, 0])
    ub = _dot((x + pos_ref[0, 1:2, :]).astype(BF16), w1_ref[0, 1])
    h = _silu(ua + pltpu.roll(ub, Bb * R - 1, 0))
    o_ref[:, 0, 0:R, :] = _dot(h.astype(BF16), w2_ref[0]).reshape(Bb, R, HD)
    rp = o_ref.shape[2]
    if rp > R:
        o_ref[:, 0, R:rp, :] = jnp.zeros((Bb, rp - R, HD), F32)


def nsa_compress(xr, R, pos, w1, w2, Bb):
    B = xr.shape[0]
    half = CMP_STRIDE * HD
    rp = -(-R // LANE) * LANE
    return pl.pallas_call(
        _compress_body,
        grid=(4, B // Bb),
        in_specs=[pl.BlockSpec((Bb, 1, R, half), lambda c, b: (b, c, 0, 0)),
                  pl.BlockSpec((1, 2, half), lambda c, b: (c // 2, 0, 0)),
                  pl.BlockSpec((1, 2, half, CMP_HIDDEN), lambda c, b: (c // 2, 0, 0, 0)),
                  pl.BlockSpec((1, CMP_HIDDEN, HD), lambda c, b: (c // 2, 0, 0))],
        out_specs=pl.BlockSpec((Bb, 1, rp, HD), lambda c, b: (b, c, 0, 0)),
        out_shape=jax.ShapeDtypeStruct((B, 4, rp, HD), F32),
        compiler_params=_cparams(("parallel", "parallel")),
        name="nsa_compress",
    )(xr, pos.reshape(2, 2, half), w1.reshape(2, 2, half, CMP_HIDDEN).astype(BF16), w2.astype(BF16))


def _cumsum_body(x_ref, o_ref):
    Bb, H, T = x_ref.shape
    rows, cols = _tri(LANE)
    triu = (rows <= cols).astype(F32)
    carry = jnp.zeros((Bb * H, 1), F32)
    for c in range(T // LANE):
        seg = x_ref[:, :, c * LANE:(c + 1) * LANE].reshape(Bb * H, LANE)
        loc = _dot_pick(seg, triu) + carry
        o_ref[:, :, c * LANE:(c + 1) * LANE] = loc.reshape(Bb, H, LANE)
        carry = loc[:, LANE - 1:LANE]


def cumsum_lanes(x, Bb):
    B, H, T = x.shape
    return pl.pallas_call(
        _cumsum_body,
        grid=(B // Bb,),
        in_specs=[pl.BlockSpec((Bb, H, T), lambda b: (b, 0, 0))],
        out_specs=pl.BlockSpec((Bb, H, T), lambda b: (b, 0, 0)),
        out_shape=jax.ShapeDtypeStruct((B, H, T), F32),
        compiler_params=_cparams(("parallel",)),
        name="cumsum",
    )(x)


def _flash_tile(s_blocks, v, m_scr, l_scr, acc_scr):
    dv = acc_scr.shape[-1]
    m_prev = m_scr[...]
    mx = s_blocks[0]
    for sb in s_blocks[1:]:
        mx = jnp.maximum(mx, sb)
    m_new = jnp.maximum(m_prev, jnp.max(mx, axis=1, keepdims=True))
    p_blocks = [jnp.exp(sb - m_new) for sb in s_blocks]
    sm = p_blocks[0]
    for pb in p_blocks[1:]:
        sm = sm + pb
    alpha = jnp.exp(m_prev - m_new)
    l_scr[...] = alpha * l_scr[...] + jnp.sum(sm, axis=1, keepdims=True)
    p = (jnp.concatenate(p_blocks, axis=1) if len(p_blocks) > 1 else p_blocks[0]).astype(BF16)
    acc_scr[...] = alpha[:, :dv] * acc_scr[...] + _dot(p, v)
    m_scr[...] = m_new


def _flash_reset(m_scr, l_scr, acc_scr):
    m_scr[...] = jnp.full_like(m_scr, NEG_BIG)
    l_scr[...] = jnp.zeros_like(l_scr)
    acc_scr[...] = jnp.zeros_like(acc_scr)


def _lane_blocks(s):
    return [s[:, i * LANE:(i + 1) * LANE] for i in range(s.shape[1] // LANE)]


def _fox_body(q_ref, k_ref, v_ref, fq_ref, fk_ref, o_ref, m_scr, l_scr, acc_scr, *, tq, tk):
    qi = pl.program_id(1)
    j = pl.program_id(2)
    top = (qi * tq + tq - 1) // tk

    @pl.when(j == 0)
    def _():
        _flash_reset(m_scr, l_scr, acc_scr)

    def tile(diag):
        fq = fq_ref[0]
        fk = fk_ref[0]
        if diag:
            mask = (top * tk + _iota((tq, tk), 1)) <= (qi * tq + _iota((tq, tk), 0))
        for h in range(FOX_H):
            s = _dot_nt(q_ref[0, h], k_ref[0, h]) + fq[:, h:h + 1] - fk[h:h + 1, :]
            if diag:
                s = jnp.where(mask, s, NEG_BIG)
            _flash_tile(_lane_blocks(s), v_ref[0, h], m_scr.at[h], l_scr.at[h], acc_scr.at[h])

    @pl.when(j == 0)
    def _():
        tile(True)

    @pl.when(jnp.logical_and(j > 0, j <= top))
    def _():
        tile(False)

    @pl.when(j == pl.num_programs(2) - 1)
    def _():
        for h in range(FOX_H):
            o_ref[0, h] = acc_scr[h] / jnp.maximum(l_scr[h][:, :HD], 1e-30)


def fox_prompt(q, kv, fq, fk, tq, tk):
    B, H, T, _ = q.shape
    kmap = lambda i, j: jnp.maximum((i * tq + tq - 1) // tk - j, 0)
    return pl.pallas_call(
        functools.partial(_fox_body, tq=tq, tk=tk),
        grid=(B, T // tq, T // tk),
        in_specs=[pl.BlockSpec((1, H, tq, HD), lambda b, i, j: (b, 0, i, 0)),
                  pl.BlockSpec((1, H, tk, HD), lambda b, i, j: (b, 0, kmap(i, j), 0)),
                  pl.BlockSpec((1, H, tk, HD), lambda b, i, j: (b, 1, kmap(i, j), 0)),
                  pl.BlockSpec((1, tq, H), lambda b, i, j: (b, i, 0)),
                  pl.BlockSpec((1, H, tk), lambda b, i, j: (b, 0, kmap(i, j)))],
        out_specs=pl.BlockSpec((1, H, tq, HD), lambda b, i, j: (b, 0, i, 0)),
        out_shape=jax.ShapeDtypeStruct((B, H, T, HD), F32),
        scratch_shapes=[pltpu.VMEM((H, tq, LANE), F32), pltpu.VMEM((H, tq, LANE), F32), pltpu.VMEM((H, tq, HD), F32)],
        compiler_params=_cparams(("parallel", "parallel", "arbitrary")),
        name="fox_prompt",
    )(q, kv, kv, fq, fk)


def _fox_decode_body(pt_ref, q_ref, newkv_ref, newlf_ref, *refs, n_pages, tn):
    kv_refs = refs[:n_pages]
    lf_refs = refs[n_pages:2 * n_pages]
    o_ref = refs[2 * n_pages]
    R = FOX_H * tn
    W = PAGE * FOX_H
    q = q_ref[0]
    qb = q.astype(BF16)
    row_h = _iota((R, W), 0) // tn
    head_ok = (_iota((R, W), 1) % FOX_H) == row_h
    rows, cols = _tri(PAGE)
    triu = (rows <= cols).astype(F32)
    spread = (_iota((PAGE, W), 0) == _iota((PAGE, W), 1) // FOX_H).astype(F32)
    expand = lambda a: jnp.concatenate([jnp.broadcast_to(a[h:h + 1, :], (tn, a.shape[1])) for h in range(FOX_H)], 0)
    carry_c = jnp.zeros((FOX_H, 1), F32)
    carry_r = jnp.zeros((1, FOX_H), F32)
    s_tiles = []
    for pg in range(n_pages):
        lf = lf_refs[pg][0]
        f_t = _dot_pick(lf, triu, _dot_tn) + carry_c
        carry_c = f_t[:, PAGE - 1:PAGE]
        carry_r = carry_r + jnp.sum(lf, axis=0, keepdims=True)
        k = kv_refs[pg][0, :, 0:FOX_H, :].reshape(W, HD).astype(BF16)
        s = _dot_nt(qb, k) - expand(_dot_pick(f_t, spread))
        s_tiles.append(jnp.where(head_ok, s, NEG_BIG))
    lfn = newlf_ref[0]
    r8, c8 = _tri(tn)
    fq_c = _dot_pick((r8 >= c8).astype(F32), lfn, exact_lhs=True) + carry_r
    fq_t = _dot_pick(lfn, (r8 <= c8).astype(F32), _dot_tn) + carry_c
    fq_rows = jnp.concatenate([fq_c[:, h:h + 1] for h in range(FOX_H)], axis=0)
    wn = tn * FOX_H
    kn = newkv_ref[0, :, 0:FOX_H, :].reshape(wn, HD)
    vn = newkv_ref[0, :, FOX_H:2 * FOX_H, :].reshape(wn, HD)
    spread_n = (_iota((tn, wn), 0) == _iota((tn, wn), 1) // FOX_H).astype(F32)
    col_n = _iota((R, wn), 1)
    row_n = _iota((R, wn), 0)
    ok_n = jnp.logical_and(col_n % FOX_H == row_n // tn, col_n // FOX_H <= row_n % tn)
    s_new = _dot_split(qb.astype(F32), kn, _dot_nt) - expand(_dot_pick(fq_t, spread_n))
    s_new = jnp.where(ok_n, s_new + fq_rows, NEG_BIG)
    s_tiles = [s + fq_rows for s in s_tiles]
    m = jnp.max(s_new, axis=1, keepdims=True)
    for s in s_tiles:
        m = jnp.maximum(m, jnp.max(s, axis=1, keepdims=True))
    p_new = jnp.where(ok_n, jnp.exp(s_new - m), 0.0)
    l = jnp.sum(p_new, axis=1, keepdims=True)
    acc = _dot_split(p_new, vn)
    for pg, s in enumerate(s_tiles):
        p = jnp.exp(s - m)
        l = l + jnp.sum(p, axis=1, keepdims=True)
        acc = acc + _dot(p.astype(BF16), kv_refs[pg][0, :, FOX_H:2 * FOX_H, :].reshape(W, HD).astype(BF16))
    o_ref[0] = acc / jnp.maximum(l, 1e-30)


def fox_decode(page_table, q_rows, newkv, newlf, kv_pool, lf_pool):
    B, R, _ = q_rows.shape
    tn = R // FOX_H
    n_pages = page_table.shape[1]
    page_map = lambda pg: (lambda b, pt: (pt[b, pg], 0, 0, 0))
    grid_spec = pltpu.PrefetchScalarGridSpec(
        num_scalar_prefetch=1,
        grid=(B,),
        in_specs=[pl.BlockSpec((1, R, HD), lambda b, pt: (b, 0, 0)),
                  pl.BlockSpec((1, tn, 2 * FOX_H, HD), lambda b, pt: (b, 0, 0, 0)),
                  pl.BlockSpec((1, tn, FOX_H), lambda b, pt: (b, 0, 0))]
                 + [pl.BlockSpec((1, PAGE, 2 * FOX_H, HD), page_map(pg)) for pg in range(n_pages)]
                 + [pl.BlockSpec((1, PAGE, FOX_H), functools.partial(lambda b, pt, pg: (pt[b, pg], 0, 0), pg=pg))
                    for pg in range(n_pages)],
        out_specs=pl.BlockSpec((1, R, HD), lambda b, pt: (b, 0, 0)),
    )
    return pl.pallas_call(
        functools.partial(_fox_decode_body, n_pages=n_pages, tn=tn),
        grid_spec=grid_spec,
        out_shape=jax.ShapeDtypeStruct((B, R, HD), F32),
        compiler_params=_cparams(("arbitrary",)),
        name="fox_decode",
    )(page_table, q_rows, newkv, newlf, *([kv_pool] * n_pages), *([lf_pool] * n_pages))


def _t5_bucket(dist):
    n = jnp.maximum(dist, 0)
    nf = jnp.maximum(n, 1).astype(F32)
    large = BUCKET_EXACT + (jnp.log(nf / BUCKET_EXACT) / math.log(MAX_DISTANCE / BUCKET_EXACT)
                            * (N_BUCKETS - BUCKET_EXACT)).astype(jnp.int32)
    return jnp.where(n < BUCKET_EXACT, n, jnp.minimum(large, N_BUCKETS - 1))


def _bias_from_bucket(bucket, tbl_ref, head):
    out = jnp.zeros(bucket.shape, F32)
    for kk in range(N_BUCKETS):
        out = jnp.where(bucket == kk, tbl_ref[kk, head], out)
    return out


FOX_TQ, FOX_TK = 256, 512
RECURRENT_BATCH = 4
NSA_TK = 2 * LANE
NSA_FAR_GROUP = 4
NSA_BIAS_TILES = (BUCKET_SAT_DIST + NSA_TK + LANE - 1) // LANE


def _nsa_body(tbl_ref, q_ref, gate_ref, kc_ref, vc_ref, ks_ref, vs_ref, kw_ref, vw_ref, smap_ref, o_ref,
              bias_scr, score_scr, m_scr, l_scr, acc_scr, *, tq, q_pos0, win_pos0, n_sel, ncp, tw):
    g = pl.program_id(1)
    qi = pl.program_id(2)
    q0 = q_pos0 + qi * tq
    R = NSA_R * tq
    scale = HD ** -0.5
    last_bias = tuple(tbl_ref[N_BUCKETS - 1, g * NSA_R + r] for r in range(NSA_R))

    @pl.when(qi == 0)
    def _():
        ii = _iota((tq, NSA_TK), 0)
        jj = _iota((tq, NSA_TK), 1)
        for dd in range(NSA_BIAS_TILES):
            bucket = _t5_bucket(ii - jj + dd * LANE)
            for r in range(NSA_R):
                bias_scr[dd, r * tq:(r + 1) * tq, :] = (
                    _bias_from_bucket(bucket, tbl_ref, g * NSA_R + r) - last_bias[r])
        bias_scr[NSA_BIAS_TILES] = jnp.zeros((R, NSA_TK), F32)

    t_col = q0 + _iota((tq, 1), 0)

    bias_tiles, mask_tiles = [], []
    for nt in range(ncp // LANE):
        c_end = (nt * LANE + _iota((tq, LANE), 1)) * CMP_STRIDE + (CMP_BLOCK - 1)
        dist = t_col - c_end
        max_dist = q0 + tq - 1 - (nt * LANE * CMP_STRIDE + CMP_BLOCK - 1)
        min_dist = q0 - ((nt * LANE + LANE - 1) * CMP_STRIDE + CMP_BLOCK - 1)
        special = jnp.logical_and(max_dist >= 0, min_dist < BUCKET_SAT_DIST)

        def general(dist=dist):
            bucket = _t5_bucket(dist)
            return jnp.stack([_bias_from_bucket(bucket, tbl_ref, g * NSA_R + r) - last_bias[r]
                              for r in range(NSA_R)])

        def saturated():
            return jnp.zeros((NSA_R, tq, LANE), F32)

        bias_tiles.append(lax.cond(special, general, saturated))
        mask_tiles.append(dist >= 0)
    mask_c = jnp.concatenate(mask_tiles, axis=1) if len(mask_tiles) > 1 else mask_tiles[0]
    kc = kc_ref[0, 0].astype(BF16)
    vc = vc_ref[0, 0].astype(BF16)
    pcsum = jnp.zeros((tq, ncp), F32)
    o_c = []
    for r in range(NSA_R):
        bias_r = jnp.concatenate([b[r] for b in bias_tiles], axis=1) if len(bias_tiles) > 1 else bias_tiles[0][r]
        s = _dot_nt(q_ref[0, r].astype(BF16), kc) * scale + bias_r
        s = jnp.where(mask_c, s, NEG_BIG)
        m = jnp.max(s, axis=1, keepdims=True)
        p = jnp.where(mask_c, jnp.exp(s - m), 0.0)
        pc = p / jnp.maximum(jnp.sum(p, axis=1, keepdims=True), 1e-30)
        o_c.append(_dot(pc.astype(BF16), vc))
        pcsum = pcsum + pc

    ps_t = _dot_pick(smap_ref[...], pcsum, _dot_nt, exact_lhs=True)
    j_col = _iota((LANE, 1), 0)
    t_row = q0 + _iota((1, tq), 1)
    cur = lax.shift_right_logical(t_row, int(math.log2(SEL_BLOCK)))
    score = jnp.where(j_col * SEL_BLOCK <= t_row, ps_t, -1.0)
    score = jnp.where(j_col == cur - 1, FORCE_SCORE, score)
    score = jnp.where(j_col == cur, FORCE_SCORE, score)
    score = jnp.where(j_col == 0, FORCE_SCORE, score)
    score = jnp.where(j_col < n_sel, score, -3e38)
    score_scr[...] = score

    def rank_body(jp, rank):
        row = score_scr[pl.ds(jp, 1), :]
        tie = jnp.where(j_col > jp, 1.0, 0.0)
        return rank + jnp.where(row > score, 1.0, jnp.where(row == score, tie, 0.0))

    n_rank = jnp.minimum((q0 + tq - 1) // SEL_BLOCK + 1, LANE)
    rank = lax.fori_loop(0, n_rank, rank_body, jnp.zeros((LANE, tq), F32))
    sel_t = jnp.where(rank < SEL_TOPN, 1.0, 0.0).astype(BF16)
    eye = (_iota((tq, tq), 0) == _iota((tq, tq), 1)).astype(BF16)
    sel = _dot_nt(eye, sel_t).astype(BF16)

    qs = (q_ref[0].reshape(R, HD) * scale).astype(BF16)
    t_tile = q0 + _iota((tq, NSA_TK), 0)
    c_tile = _iota((tq, NSA_TK), 1)
    log2_blk = int(math.log2(SEL_BLOCK))

    def stack(a):
        return jnp.concatenate([a] * NSA_R, axis=0)

    def bias_tile(offset):
        return bias_scr[jnp.minimum(lax.shift_right_logical(offset, 7), NSA_BIAS_TILES)]

    def sel_tile(k0, width, near):
        k = ks_ref[0, 0, pl.ds(k0, width), :]
        v = vs_ref[0, 0, pl.ds(k0, width), :]
        blk = lax.shift_right_logical(k0 + _iota((LANE, width), 1), log2_blk)
        expand = jnp.where(_iota((LANE, width), 0) == blk, 1.0, 0.0).astype(BF16)
        chosen = _dot(sel, expand)
        s = _dot_nt(qs, k)
        if near:
            chosen = jnp.where(t_tile - (k0 + c_tile) >= 0, chosen, 0.0)
            s = s + bias_tile(q0 - k0)
        s = jnp.where(stack(chosen) > 0.5, s, NEG_BIG)
        _flash_tile(_lane_blocks(s), v, m_scr, l_scr, acc_scr)

    _flash_reset(m_scr, l_scr, acc_scr)
    kt_top = (q0 + tq - 1) // NSA_TK
    sel_tile(pl.multiple_of(kt_top * NSA_TK, NSA_TK), NSA_TK, True)

    @pl.when(kt_top >= 1)
    def _():
        sel_tile(pl.multiple_of((kt_top - 1) * NSA_TK, NSA_TK), NSA_TK, True)

    n_far = jnp.maximum(kt_top - 1, 0)
    n_groups = n_far // NSA_FAR_GROUP

    def sel_far_group(gi, carry):
        sel_tile(pl.multiple_of(gi * (NSA_FAR_GROUP * NSA_TK), NSA_FAR_GROUP * NSA_TK), NSA_FAR_GROUP * NSA_TK, False)
        return carry

    def sel_far(kt, carry):
        sel_tile(pl.multiple_of(kt * NSA_TK, NSA_TK), NSA_TK, False)
        return carry

    lax.fori_loop(0, n_groups, sel_far_group, 0)
    lax.fori_loop(n_groups * NSA_FAR_GROUP, n_far, sel_far, 0)
    o_s = acc_scr[...] / jnp.maximum(l_scr[...][:, :HD], 1e-30)

    def win_tile(i, carry):
        k0 = pl.multiple_of((wt_top - i) * NSA_TK, NSA_TK)
        k = kw_ref[0, 0, pl.ds(k0, NSA_TK), :]
        v = vw_ref[0, 0, pl.ds(k0, NSA_TK), :]
        dist = t_tile - (win_pos0 + k0 + c_tile)
        ok = jnp.where(dist >= 0, jnp.where(dist < WINDOW, 1.0, 0.0), 0.0)
        s = _dot_nt(qs, k) + bias_tile(q0 - win_pos0 - k0)
        s = jnp.where(stack(ok) > 0.5, s, NEG_BIG)
        _flash_tile(_lane_blocks(s), v, m_scr, l_scr, acc_scr)
        return carry

    _flash_reset(m_scr, l_scr, acc_scr)
    wt_top = (jnp.minimum(q0 + tq - win_pos0, tw) - 1) // NSA_TK
    wt_lo = jnp.maximum(q0 - (WINDOW - 1) - win_pos0, 0) // NSA_TK
    lax.fori_loop(0, wt_top - wt_lo + 1, win_tile, 0)
    o_w = acc_scr[...] / jnp.maximum(l_scr[...][:, :HD], 1e-30)

    gates = gate_ref[0, 0]
    for r in range(NSA_R):
        o_ref[0, r] = (gates[:, 3 * r:3 * r + 1] * o_c[r]
                       + gates[:, 3 * r + 1:3 * r + 2] * o_s[r * tq:(r + 1) * tq]
                       + gates[:, 3 * r + 2:3 * r + 3] * o_w[r * tq:(r + 1) * tq])


def _selection_overlap_t(ncp):
    c_start = np.arange(ncp)[None, :] * CMP_STRIDE
    s_start = np.arange(LANE)[:, None] * SEL_BLOCK
    return ((c_start < s_start + SEL_BLOCK) & (c_start + CMP_BLOCK > s_start)).astype(np.float32)


def nsa_attend(tbl, q, gates, kcvc, sel_arr, sel_off, win_arr, win_off, *, tq, q_pos0, win_pos0, n_sel):
    B, _, Tq, _ = q.shape
    ncp = kcvc.shape[2]
    tks = sel_arr.shape[2]
    tw = win_arr.shape[2]
    R = NSA_R * tq
    smap = jnp.asarray(_selection_overlap_t(ncp))
    kv_spec = lambda rows, off: pl.BlockSpec((1, 1, rows, HD), lambda b, g, i: (b, off + g, 0, 0))
    return pl.pallas_call(
        functools.partial(_nsa_body, tq=tq, q_pos0=q_pos0, win_pos0=win_pos0, n_sel=n_sel, ncp=ncp, tw=tw),
        grid=(B, NSA_G, Tq // tq),
        in_specs=[pl.BlockSpec(memory_space=pltpu.SMEM),
                  pl.BlockSpec((1, NSA_R, tq, HD), lambda b, g, i: (b, g, i, 0)),
                  pl.BlockSpec((1, 1, tq, 3 * NSA_R), lambda b, g, i: (b, g, i, 0)),
                  kv_spec(ncp, 0), kv_spec(ncp, 2),
                  kv_spec(tks, sel_off), kv_spec(tks, sel_off + 2),
                  kv_spec(tw, win_off), kv_spec(tw, win_off + 2),
                  pl.BlockSpec((LANE, ncp), lambda b, g, i: (0, 0))],
        out_specs=pl.BlockSpec((1, NSA_R, tq, HD), lambda b, g, i: (b, g, i, 0)),
        out_shape=jax.ShapeDtypeStruct((B, NSA_G * NSA_R, Tq, HD), F32),
        scratch_shapes=[pltpu.VMEM((NSA_BIAS_TILES + 1, R, NSA_TK), F32), pltpu.VMEM((LANE, tq), F32),
                        pltpu.VMEM((R, LANE), F32), pltpu.VMEM((R, LANE), F32), pltpu.VMEM((R, HD), F32)],
        compiler_params=_cparams(("parallel", "parallel", "arbitrary")),
        name="nsa_attend",
    )(tbl, q, gates, kcvc, kcvc, sel_arr, sel_arr, win_arr, win_arr, smap)


def _nsa_gather_body(pt_ref, new_ref, *refs, n_pages):
    pages = refs[:n_pages]
    cmp_ref, sel_ref = refs[n_pages:]
    ngrp = 2 * NSA_G
    per_pos = 2 * ngrp
    rows16 = PAGE // CMP_STRIDE
    for pg in range(n_pages):
        for j in range(ngrp):
            sel_ref[0, j, pg * PAGE:(pg + 1) * PAGE, :] = (
                pages[pg][0, pl.ds(ngrp + j, PAGE, stride=per_pos), :].astype(BF16))
            for p in range(CMP_STRIDE):
                cmp_ref[0, j, pg * rows16:(pg + 1) * rows16, p * HD:(p + 1) * HD] = (
                    pages[pg][0, pl.ds(p * per_pos + j, rows16, stride=CMP_STRIDE * per_pos), :])
    tn = new_ref.shape[1]
    tail = sel_ref.shape[2] - n_pages * PAGE
    for j in range(ngrp):
        new = new_ref[0, :, (ngrp + j) * HD:(ngrp + j + 1) * HD]
        tile = jnp.concatenate([new, jnp.zeros((tail - tn, HD), F32)], axis=0)
        sel_ref[0, j, n_pages * PAGE:, :] = tile.astype(BF16)


def nsa_gather(page_table, new_rows, pool):
    B, tn, width = new_rows.shape
    n_pages = page_table.shape[1]
    ngrp = 2 * NSA_G
    sel_rows = -(-(n_pages * PAGE + tn) // NSA_TK) * NSA_TK
    cmp_rows = n_pages * PAGE // CMP_STRIDE
    grid_spec = pltpu.PrefetchScalarGridSpec(
        num_scalar_prefetch=1,
        grid=(B,),
        in_specs=[pl.BlockSpec((1, tn, width), lambda b, pt: (b, 0, 0))]
                 + [pl.BlockSpec((1, PAGE * 2 * ngrp, HD),
                                 functools.partial(lambda b, pt, pg: (pt[b, pg], 0, 0), pg=pg))
                    for pg in range(n_pages)],
        out_specs=[pl.BlockSpec((1, ngrp, cmp_rows, CMP_STRIDE * HD), lambda b, pt: (b, 0, 0, 0)),
                   pl.BlockSpec((1, ngrp, sel_rows, HD), lambda b, pt: (b, 0, 0, 0))],
    )
    return pl.pallas_call(
        functools.partial(_nsa_gather_body, n_pages=n_pages),
        grid_spec=grid_spec,
        out_shape=[jax.ShapeDtypeStruct((B, ngrp, cmp_rows, CMP_STRIDE * HD), F32),
                   jax.ShapeDtypeStruct((B, ngrp, sel_rows, HD), BF16)],
        compiler_params=_cparams(("arbitrary",)),
        name="nsa_gather",
    )(page_table, new_rows, *([pool] * n_pages))


def _row_tile(m):
    return 512 if m % 512 == 0 else m


def _small_params(entries):
    sp = jnp.zeros((8, LANE), F32)
    for off, bias, act, log_scale in entries:
        n = bias.shape[0]
        sp = sp.at[0, off:off + n].set(bias.astype(F32))
        sp = sp.at[1, off:off + n].set(act)
        if log_scale is not None:
            sp = sp.at[2, off:off + n].set(log_scale.astype(F32))
    return sp


EVEN_WIDTHS = (256, 256, 512, 512, GD_CH, 512, LANE)


def _even_weights(w_in):
    s = np.cumsum((0, 256, 256, 512, 512, 4, 4, 512, 512, 512, 512, 4, 4))
    col = lambda i: w_in[:, s[i]:s[i + 1]]
    small = jnp.concatenate([col(4), col(5), col(10), col(11)], axis=1)
    small = jnp.pad(small, ((0, 0), (0, LANE - small.shape[1])))
    return jnp.concatenate([col(0), col(1), col(2), col(3), col(6), col(7), col(8), col(9), small], axis=1).astype(BF16)


def _chunk_rows(small, B, T, L, lanes):
    r = small.reshape(B, T // L, L, LANE)[..., lanes[0]:lanes[1]]
    r = jnp.swapaxes(r, 2, 3)
    return jnp.pad(r, ((0, 0), (0, 0), (0, 8 - r.shape[2]), (0, 0)))


def even_layer(x, p, past, L, Bb):
    B, T, D = x.shape
    M = B * T
    tm = _row_tile(M)
    sp = _small_params([(0, p['mi_b_i'], ACT_ID, None), (4, p['mi_b_f'], ACT_LOGSIG, None),
                        (8, jnp.zeros((4,), F32), ACT_SIG, None), (12, p['gd_dt_bias'], ACT_DECAY, p['gd_a_log'])])
    mq, mk, mv, mo, gx, gz, small = norm_proj(x.reshape(M, D), p['norm_mix'], _even_weights(p['w_in']), sp,
                                              EVEN_WIDTHS, tm)
    heads = lambda a: jnp.transpose(a.reshape(B, T, MI_H, MI_DQK), (0, 2, 1, 3))
    sc = small.reshape(B, T, LANE)
    if past is None:
        c0 = jnp.zeros((B, MI_H, MI_DQK, MI_DV), F32)
        n0 = jnp.zeros((B, MI_H, 1, MI_DQK), F32)
        m0 = jnp.zeros((B, 1, MI_H), F32)
        s0 = jnp.zeros((B, GD_H, GD_DK, GD_DV), F32)
        conv0 = jnp.zeros((B, GD_CONV - 1, GD_CH), F32)
    else:
        c0, n0, m0, s0, conv0 = past
        n0 = n0.reshape(B, MI_H, 1, MI_DQK)
        m0 = m0.reshape(B, 1, MI_H)
    hm, c1, n1, m1 = mlstm(heads(mq), heads(mk), mv.reshape(B, T, -1), mo.reshape(B, T, -1), sc,
                           _chunk_rows(small, B, T, L, (0, 8)), c0, n0, m0,
                           p['mi_norm'].reshape(MI_H, MI_DV), L, Bb)
    og, s1, conv1 = gdn(gx.reshape(B, T, GD_CH), gz.reshape(B, T, -1), sc,
                        _chunk_rows(small, B, T, L, (12, 16)), s0, conv0, p['gd_conv_w'], p['gd_norm'], L, Bb)
    y = out_proj_residual(x.reshape(M, D), hm.reshape(M, -1), og.reshape(M, -1), p['w_out'].astype(BF16), tm)
    return y.reshape(B, T, D), (c1, n1.reshape(B, MI_H, MI_DQK), m1.reshape(B, MI_H), s1, conv1)


NSA_QW = NSA_G * NSA_R * HD
NSA_KVW = 6 * NSA_G * HD
NSA_CACHE_W = 4 * NSA_G * HD
FOX_W = FOX_H * HD
N_GATE = 3 * NSA_G * NSA_R
ODD_WIDTHS = (NSA_QW, NSA_KVW, FOX_W, 2 * FOX_W, LANE)


def _odd_weights(w_in):
    s = np.cumsum((0, NSA_QW, NSA_KVW, N_GATE, FOX_W, FOX_W, FOX_W, FOX_H))
    col = lambda i: w_in[:, s[i]:s[i + 1]]
    small = jnp.concatenate([col(2), col(6)], axis=1)
    small = jnp.pad(small, ((0, 0), (0, LANE - small.shape[1])))
    return jnp.concatenate([col(0), col(1), col(3), col(4), col(5), small], axis=1).astype(BF16)


def _heads(a, B, T, n):
    return jnp.transpose(a.reshape(B, T, n, HD), (0, 2, 1, 3))


def _unheads(a):
    B, n, T, _ = a.shape
    return jnp.transpose(a, (0, 2, 1, 3)).reshape(B * T, n * HD)


def odd_layer(x, p, rel_bias, w_buf, past, page_table):
    B, T, D = x.shape
    M = B * T
    tm = _row_tile(M)
    sp = _small_params([(0, jnp.zeros((N_GATE,), F32), ACT_SIG, None), (N_GATE, p['fox_b_f'], ACT_LOGSIG, None)])
    nq, nkv, fq, fkv, small = norm_proj(x.reshape(M, D), p['norm_mix'], _odd_weights(p['w_in']), sp, ODD_WIDTHS, tm)
    new_nsa = nkv[:, :NSA_CACHE_W].reshape(B, T, 4, NSA_G, HD)
    new_win = nkv[:, NSA_CACHE_W:].reshape(B, T, 2, NSA_G, HD)
    new_fox = fkv.reshape(B, T, 2, FOX_H, HD)
    logf = small[:, N_GATE:N_GATE + FOX_H].reshape(B, T, FOX_H)
    q_heads = _heads(nq, B, T, NSA_G * NSA_R)
    gates = jnp.transpose(small[:, :N_GATE].reshape(B, T, NSA_G, 3 * NSA_R), (0, 2, 1, 3))
    cmp_args = (p['nsa_cmp_pos'], p['nsa_cmp_w1'], p['nsa_cmp_w2'])
    rows16 = CMP_STRIDE * HD
    if past is None:
        groups = _heads(nkv, B, T, 6 * NSA_G)
        kcvc = nsa_compress(groups.reshape(B, 6 * NSA_G, T // CMP_STRIDE, rows16), T // CMP_STRIDE, *cmp_args, 1)
        arr = groups.astype(BF16)
        o_n = nsa_attend(rel_bias, q_heads, gates, kcvc, arr, 2 * NSA_G, arr, 4 * NSA_G,
                         tq=min(T, LANE), q_pos0=0, win_pos0=0, n_sel=-(-T // SEL_BLOCK))
        f_t = cumsum_lanes(jnp.transpose(logf, (0, 2, 1)), B)
        o_f = fox_prompt(_heads(fq * HD ** -0.5, B, T, FOX_H).astype(BF16), _heads(fkv, B, T, 2 * FOX_H).astype(BF16),
                         jnp.transpose(f_t, (0, 2, 1)), f_t, min(T, FOX_TQ), min(T, FOX_TK))
        o_f = _unheads(o_f)
        win_prev = jnp.zeros((B, WINDOW, 2, NSA_G, HD), F32)
    else:
        nsa_pool, win_prev, fox_pool, logf_pool = past
        n_pool = nsa_pool.shape[0]
        n_pages = page_table.shape[1]
        start = n_pages * PAGE
        cmp_rows, sel_arr = nsa_gather(page_table, nkv[:, :NSA_CACHE_W].reshape(B, T, NSA_CACHE_W),
                                       nsa_pool.reshape(n_pool, PAGE * 4 * NSA_G, HD))
        kcvc = nsa_compress(cmp_rows, start // CMP_STRIDE, *cmp_args, math.gcd(B, 8))
        wp = win_prev.shape[1]
        win_all = jnp.concatenate([win_prev.reshape(B, wp, 2 * NSA_G * HD), nkv[:, NSA_CACHE_W:].reshape(B, T, -1)], 1)
        tw = -(-(wp + T) // NSA_TK) * NSA_TK
        win_arr = _heads(jnp.pad(win_all, ((0, 0), (0, tw - wp - T), (0, 0))), B, tw, 2 * NSA_G).astype(BF16)
        o_n = nsa_attend(rel_bias, q_heads, gates, kcvc, sel_arr, 0, win_arr, 0,
                         tq=T, q_pos0=start, win_pos0=start - wp, n_sel=-(-(start + T) // SEL_BLOCK))
        q_rows = _heads(fq * HD ** -0.5, B, T, FOX_H).reshape(B, FOX_H * T, HD)
        o_f = fox_decode(page_table, q_rows, fkv.reshape(B, T, 2 * FOX_H, HD), logf,
                         fox_pool.reshape(n_pool, PAGE, 2 * FOX_H, HD), logf_pool)
        o_f = _unheads(o_f.reshape(B, FOX_H, T, HD))
    win_state = jnp.concatenate([win_prev, new_win], axis=1)[:, -w_buf:]
    y = out_proj_residual(x.reshape(M, D), _unheads(o_n), o_f, p['w_out'].astype(BF16), tm)
    return y.reshape(B, T, D), (new_nsa, win_state, new_fox, logf)


def _trunk(x, past, page_table, P, w_buf, L, Bb):
    B, T, D = x.shape
    pe = dict(norm_mix=P['norm_mix'][0], w_in=P['w_in_even'][0], w_out=P['w_out_even'][0], mi_b_i=P['mi_b_i'][0],
              mi_b_f=P['mi_b_f'][0], mi_norm=P['mi_norm'][0], gd_conv_w=P['gd_conv_w'][0], gd_a_log=P['gd_a_log'][0],
              gd_dt_bias=P['gd_dt_bias'][0], gd_norm=P['gd_norm'][0])
    po = dict(norm_mix=P['norm_mix'][1], w_in=P['w_in_odd'][0], w_out=P['w_out_odd'][0],
              nsa_cmp_pos=P['nsa_cmp_pos'][0], nsa_cmp_w1=P['nsa_cmp_w1'][0], nsa_cmp_w2=P['nsa_cmp_w2'][0],
              fox_b_f=P['fox_b_f'][0])
    tm = _row_tile(B * T)
    mlp = lambda x, layer, final: mlp_residual(
        x.reshape(B * T, D), P['norm_mlp'][layer], P['w_up'][layer].astype(BF16), P['w_down'][layer].astype(BF16),
        P['norm_final'], final, tm, 1024).reshape(B, T, D)
    even_past = None if past is None else tuple(past[k][0] for k in ('mc', 'mn', 'mm', 'gs', 'gc'))
    odd_past = None if past is None else tuple(past[k][0] for k in ('nsa_kv', 'nsa_win', 'fox_kv', 'fox_logf'))
    x, st_e = even_layer(x, pe, even_past, L, Bb)
    x = mlp(x, 0, False)
    x, st_o = odd_layer(x, po, P['rel_bias'], w_buf, odd_past, page_table)
    y = mlp(x, 1, True)
    return y, tuple(a[None] for a in st_e + st_o)


def kernel(x_prompt, x_sample, state_mlstm_c, state_mlstm_n, state_mlstm_m, state_gdn_s, state_gdn_conv,
           cache_nsa_kv, state_nsa_win, cache_fox_kv, cache_fox_logf, page_table,
           norm_mix, norm_mlp, norm_final, w_up, w_down,
           w_in_even, w_out_even, mi_b_i, mi_b_f, mi_norm, gd_conv_w, gd_a_log, gd_dt_bias, gd_norm,
           w_in_odd, w_out_odd, nsa_cmp_pos, nsa_cmp_w1, nsa_cmp_w2, fox_b_f, rel_bias):
    P = dict(norm_mix=norm_mix, norm_mlp=norm_mlp, norm_final=norm_final, w_up=w_up, w_down=w_down,
             w_in_even=w_in_even, w_out_even=w_out_even, mi_b_i=mi_b_i, mi_b_f=mi_b_f, mi_norm=mi_norm,
             gd_conv_w=gd_conv_w, gd_a_log=gd_a_log, gd_dt_bias=gd_dt_bias, gd_norm=gd_norm,
             w_in_odd=w_in_odd, w_out_odd=w_out_odd, nsa_cmp_pos=nsa_cmp_pos, nsa_cmp_w1=nsa_cmp_w1,
             nsa_cmp_w2=nsa_cmp_w2, fox_b_f=fox_b_f, rel_bias=rel_bias)
    w_buf = state_nsa_win.shape[2]
    b_p, t_p = x_prompt.shape[:2]
    y_p, st_p = _trunk(x_prompt, None, None, P, w_buf, math.gcd(t_p, 64), math.gcd(b_p, RECURRENT_BATCH))
    past = dict(mc=state_mlstm_c, mn=state_mlstm_n, mm=state_mlstm_m, gs=state_gdn_s, gc=state_gdn_conv,
                nsa_kv=cache_nsa_kv, nsa_win=state_nsa_win, fox_kv=cache_fox_kv, fox_logf=cache_fox_logf)
    b_s, t_s = x_sample.shape[:2]
    y_s, st_s = _trunk(x_sample, past, page_table, P, w_buf, math.gcd(t_s, 64), math.gcd(b_s, RECURRENT_BATCH))
    return (y_p, y_s) + st_p + st_s
```

```python
import functools
import math

import jax
import jax.numpy as jnp
import numpy as np
from jax import lax
from jax.experimental import pallas as pl
from jax.experimental.pallas import tpu as pltpu

F32 = jnp.float32
BF16 = jnp.bfloat16
HI = lax.Precision.HIGHEST

D_MODEL = 1024
D_FF = 4 * D_MODEL
EPS = 1e-6
NEG_BIG = -1e30
PAGE = 128

MI_H, MI_DQK, MI_DV = 4, 64, 128
GD_H, GD_DK, GD_DV, GD_CONV = 4, 128, 128, 4
GD_CH = 3 * GD_H * GD_DK
NSA_G, NSA_R, HD = 2, 4, 64
FOX_H = 8
CMP_BLOCK, CMP_STRIDE, CMP_HIDDEN = 32, 16, 256
SEL_BLOCK, SEL_TOPN, WINDOW = 64, 16, 512
FORCE_SCORE = 1e4
N_BUCKETS, MAX_DISTANCE = 32, 128
BUCKET_EXACT = N_BUCKETS // 2
BUCKET_SAT_DIST = 113
LANE = 128
VMEM_LIMIT = 56 * 1024 * 1024


def _cparams(sem):
    return pltpu.CompilerParams(dimension_semantics=sem, vmem_limit_bytes=VMEM_LIMIT)


def _dot(a, b, precision=None):
    return jnp.dot(a, b, preferred_element_type=F32, precision=precision)


def _dot_nt(a, b, precision=None):
    return lax.dot_general(a, b, (((1,), (1,)), ((), ())), preferred_element_type=F32, precision=precision)


def _dot_tn(a, b, precision=None):
    return lax.dot_general(a, b, (((0,), (0,)), ((), ())), preferred_element_type=F32, precision=precision)


def _softplus(x):
    return jnp.maximum(x, 0.0) + jnp.log1p(jnp.exp(-jnp.abs(x)))


def _sigmoid(x):
    return 1.0 / (1.0 + jnp.exp(-x))


def _silu(x):
    return x * _sigmoid(x)


def _iota(shape, dim):
    return lax.broadcasted_iota(jnp.int32, shape, dim)


ACT_ID, ACT_LOGSIG, ACT_SIG, ACT_DECAY = 0.0, 1.0, 2.0, 3.0


def _proj_body(x_ref, g_ref, w_ref, sp_ref, *out_refs, widths):
    x = x_ref[...]
    hn = (x * lax.rsqrt(jnp.mean(x * x, axis=-1, keepdims=True) + EPS) * g_ref[...]).astype(BF16)
    off = 0
    for i, (o_ref, n) in enumerate(zip(out_refs, widths)):
        r = _dot(hn, w_ref[:, off:off + n])
        if i == len(widths) - 1:
            z = r + sp_ref[0:1, :]
            mode = sp_ref[1:2, :]
            decay = -jnp.exp(sp_ref[2:3, :]) * _softplus(z)
            r = jnp.where(mode == ACT_LOGSIG, -_softplus(-z),
                          jnp.where(mode == ACT_SIG, _sigmoid(z),
                                    jnp.where(mode == ACT_DECAY, decay, z)))
        o_ref[...] = r.astype(o_ref.dtype)
        off += n


def norm_proj(x, g, w_bf16, small_params, widths, tm):
    m, d = x.shape
    n_total = sum(widths)
    assert w_bf16.shape == (d, n_total) and m % tm == 0
    out_shape = [jax.ShapeDtypeStruct((m, n), F32) for n in widths]
    return pl.pallas_call(
        functools.partial(_proj_body, widths=tuple(widths)),
        grid=(m // tm,),
        in_specs=[pl.BlockSpec((tm, d), lambda i: (i, 0)),
                  pl.BlockSpec((1, d), lambda i: (0, 0)),
                  pl.BlockSpec((d, n_total), lambda i: (0, 0)),
                  pl.BlockSpec((8, LANE), lambda i: (0, 0))],
        out_specs=[pl.BlockSpec((tm, n), lambda i: (i, 0)) for n in widths],
        out_shape=out_shape,
        compiler_params=_cparams(("parallel",)),
        name="norm_proj",
    )(x, g.reshape(1, d), w_bf16, small_params)


def _outproj_body(x_ref, a1_ref, a2_ref, w_ref, o_ref):
    k1 = a1_ref.shape[1]
    y = _dot(a1_ref[...].astype(BF16), w_ref[0:k1, :]) + _dot(a2_ref[...].astype(BF16), w_ref[k1:, :])
    o_ref[...] = x_ref[...] + y


def out_proj_residual(x, a1, a2, w_bf16, tm):
    m, d = x.shape
    k1, k2 = a1.shape[1], a2.shape[1]
    return pl.pallas_call(
        _outproj_body,
        grid=(m // tm,),
        in_specs=[pl.BlockSpec((tm, d), lambda i: (i, 0)),
                  pl.BlockSpec((tm, k1), lambda i: (i, 0)),
                  pl.BlockSpec((tm, k2), lambda i: (i, 0)),
                  pl.BlockSpec((k1 + k2, d), lambda i: (0, 0))],
        out_specs=pl.BlockSpec((tm, d), lambda i: (i, 0)),
        out_shape=jax.ShapeDtypeStruct((m, d), F32),
        compiler_params=_cparams(("parallel",)),
        name="out_proj",
    )(x, a1, a2, w_bf16)


def _mlp_body(x_ref, g_ref, wu_ref, wd_ref, gf_ref, o_ref, hn_scr, acc_scr, *, final_norm):
    j = pl.program_id(1)

    @pl.when(j == 0)
    def _():
        x = x_ref[...]
        hn_scr[...] = (x * lax.rsqrt(jnp.mean(x * x, axis=-1, keepdims=True) + EPS) * g_ref[...]).astype(BF16)
        acc_scr[...] = jnp.zeros_like(acc_scr)

    u = jnp.maximum(_dot(hn_scr[...], wu_ref[...]), 0.0)
    acc_scr[...] += _dot((u * u).astype(BF16), wd_ref[...])

    @pl.when(j == pl.num_programs(1) - 1)
    def _():
        y = x_ref[...] + acc_scr[...]
        if final_norm:
            y = y * lax.rsqrt(jnp.mean(y * y, axis=-1, keepdims=True) + EPS) * gf_ref[...]
        o_ref[...] = y


def mlp_residual(x, g, w_up_bf16, w_down_bf16, g_final, final_norm, tm, tf):
    m, d = x.shape
    f = w_up_bf16.shape[1]
    return pl.pallas_call(
        functools.partial(_mlp_body, final_norm=final_norm),
        grid=(m // tm, f // tf),
        in_specs=[pl.BlockSpec((tm, d), lambda i, j: (i, 0)),
                  pl.BlockSpec((1, d), lambda i, j: (0, 0)),
                  pl.BlockSpec((d, tf), lambda i, j: (0, j)),
                  pl.BlockSpec((tf, d), lambda i, j: (j, 0)),
                  pl.BlockSpec((1, d), lambda i, j: (0, 0))],
        out_specs=pl.BlockSpec((tm, d), lambda i, j: (i, 0)),
        out_shape=jax.ShapeDtypeStruct((m, d), F32),
        scratch_shapes=[pltpu.VMEM((tm, d), BF16), pltpu.VMEM((tm, d), F32)],
        compiler_params=_cparams(("parallel", "arbitrary")),
        name="mlp",
    )(x, g.reshape(1, d), w_up_bf16, w_down_bf16, g_final.reshape(1, d))


def _tri(n):
    r = _iota((n, n), 0)
    c = _iota((n, n), 1)
    return r, c


def _bf16_terms(a, n):
    terms, rest = [], a
    for _ in range(n):
        t = rest.astype(BF16)
        terms.append(t)
        rest = rest - t.astype(F32)
    return terms


def _dot_split(a, b, f=None):
    f = f or _dot
    a_hi, a_lo = _bf16_terms(a, 2)
    b_hi, b_lo = _bf16_terms(b, 2)
    return f(a_hi, b_hi) + (f(a_hi, b_lo) + f(a_lo, b_hi))


def _dot_pick(a, b, f=None, exact_lhs=False):
    f = f or _dot
    if exact_lhs:
        return sum(f(a.astype(BF16), t) for t in _bf16_terms(b, 3))
    return sum(f(t, b.astype(BF16)) for t in _bf16_terms(a, 3))


def _chunk_dots(L):
    if L % 16 == 0:
        cast = lambda f: (lambda a, b: f(a.astype(BF16), b.astype(BF16)))
        return cast(_dot), cast(_dot_nt), cast(_dot_tn)
    full = lambda f: (lambda a, b: _dot_split(a, b, f))
    return full(_dot), full(_dot_nt), full(_dot_tn)


def _mlstm_group(bbs, q_ref, k_ref, v_ref, o_ref, sc_ref, sr_ref, nw_ref, h_ref, c_ref, n_ref, m_ref, L):
    nn, nt, tn = _chunk_dots(L)
    rows, cols = _tri(L)
    lower = rows >= cols
    tril = lower.astype(F32)
    triu = (rows <= cols).astype(F32)
    lane4 = _iota((1, MI_H), 1)
    scs = {bb: sc_ref[bb] for bb in bbs}
    srs = {bb: sr_ref[bb, 0] for bb in bbs}
    b_col = {bb: _dot_pick(tril, scs[bb][:, 4:8], exact_lhs=True) for bb in bbs}
    b_row = {bb: _dot_pick(srs[bb][4:8, :], triu) for bb in bbs}
    m_vec = {bb: m_ref[bb] for bb in bbs}
    chains = [(bb, h) for bb in bbs for h in range(MI_H)]
    q = {c: q_ref[c[0], c[1]] for c in chains}
    k = {c: k_ref[c[0], c[1]] * (MI_DQK ** -0.5) for c in chains}
    v = {c: v_ref[c[0], :, c[1] * MI_DV:(c[1] + 1) * MI_DV] for c in chains}
    qk = {c: nt(q[c], k[c]) for c in chains}
    c_prev = {c: c_ref[c[0], c[1]] for c in chains}
    qc = {c: nn(q[c], c_prev[c]) for c in chains}
    s, a_inter, m_t, m_new, a_prev, kw = {}, {}, {}, {}, {}, {}
    for c in chains:
        bb, h = c
        bc = b_col[bb][:, h:h + 1]
        m_prev = m_vec[bb][:, h:h + 1]
        d = jnp.where(lower, bc - b_row[bb][h:h + 1, :] + srs[bb][h:h + 1, :], NEG_BIG)
        inter = bc + m_prev
        m_t[c] = jnp.maximum(inter, jnp.max(d, axis=1, keepdims=True))
        s[c] = qk[c] * jnp.exp(d - m_t[c])
        a_inter[c] = jnp.exp(inter - m_t[c])
        b_last = bc[L - 1:L, :]
        g_col = b_last - bc + scs[bb][:, h:h + 1]
        m_new[c] = jnp.maximum(b_last + m_prev, jnp.max(g_col, axis=0, keepdims=True))
        a_prev[c] = jnp.exp(b_last + m_prev - m_new[c])
        kw[c] = k[c] * jnp.exp(g_col - m_new[c])
    sv = {c: nn(s[c], v[c]) for c in chains}
    kv = {c: tn(kw[c], v[c]) for c in chains}
    for c in chains:
        bb, h = c
        n_prev = n_ref[bb, h]
        num = sv[c] + a_inter[c] * qc[c]
        den = (jnp.sum(s[c], axis=1, keepdims=True)
               + a_inter[c] * jnp.sum(q[c] * n_prev, axis=1, keepdims=True))
        hh = num / jnp.maximum(jnp.abs(den), jnp.exp(-m_t[c]))
        c_ref[bb, h] = a_prev[c] * c_prev[c] + kv[c]
        n_ref[bb, h] = a_prev[c] * n_prev + jnp.sum(kw[c], axis=0, keepdims=True)
        m_vec[bb] = jnp.where(lane4 == h, m_new[c], m_vec[bb])
        hn = hh * lax.rsqrt(jnp.mean(hh * hh, axis=-1, keepdims=True) + EPS) * nw_ref[h:h + 1, :]
        gate = _sigmoid(o_ref[bb, :, h * MI_DV:(h + 1) * MI_DV])
        h_ref[bb, :, h * MI_DV:(h + 1) * MI_DV] = hn * gate
    for bb in bbs:
        m_ref[bb] = m_vec[bb]


def _mlstm_body(q_ref, k_ref, v_ref, o_ref, sc_ref, sr_ref, c0_ref, n0_ref, m0_ref, nw_ref,
                h_ref, c_ref, n_ref, m_ref, *, L, Bb):
    @pl.when(pl.program_id(1) == 0)
    def _():
        c_ref[...] = c0_ref[...]
        n_ref[...] = n0_ref[...]
        m_ref[...] = m0_ref[...]

    _mlstm_group(list(range(Bb)), q_ref, k_ref, v_ref, o_ref, sc_ref, sr_ref, nw_ref, h_ref, c_ref, n_ref, m_ref, L)


def mlstm(q, k, v, o, sc, sr, c0, n0, m0, norm_w, L, Bb):
    B, H, T, _ = q.shape
    nc = T // L
    hv = H * MI_DV
    bmap = lambda b, c: (b, 0, 0, 0)
    return pl.pallas_call(
        functools.partial(_mlstm_body, L=L, Bb=Bb),
        grid=(B // Bb, nc),
        in_specs=[pl.BlockSpec((Bb, H, L, MI_DQK), lambda b, c: (b, 0, c, 0)),
                  pl.BlockSpec((Bb, H, L, MI_DQK), lambda b, c: (b, 0, c, 0)),
                  pl.BlockSpec((Bb, L, hv), lambda b, c: (b, c, 0)),
                  pl.BlockSpec((Bb, L, hv), lambda b, c: (b, c, 0)),
                  pl.BlockSpec((Bb, L, LANE), lambda b, c: (b, c, 0)),
                  pl.BlockSpec((Bb, 1, 8, L), lambda b, c: (b, c, 0, 0)),
                  pl.BlockSpec((Bb, H, MI_DQK, MI_DV), bmap),
                  pl.BlockSpec((Bb, H, 1, MI_DQK), bmap),
                  pl.BlockSpec((Bb, 1, H), lambda b, c: (b, 0, 0)),
                  pl.BlockSpec((H, MI_DV), lambda b, c: (0, 0))],
        out_specs=[pl.BlockSpec((Bb, L, hv), lambda b, c: (b, c, 0)),
                   pl.BlockSpec((Bb, H, MI_DQK, MI_DV), bmap),
                   pl.BlockSpec((Bb, H, 1, MI_DQK), bmap),
                   pl.BlockSpec((Bb, 1, H), lambda b, c: (b, 0, 0))],
        out_shape=[jax.ShapeDtypeStruct((B, T, hv), F32),
                   jax.ShapeDtypeStruct((B, H, MI_DQK, MI_DV), F32),
                   jax.ShapeDtypeStruct((B, H, 1, MI_DQK), F32),
                   jax.ShapeDtypeStruct((B, 1, H), F32)],
        compiler_params=_cparams(("parallel", "arbitrary")),
        name="mlstm",
    )(q, k, v, o, sc, sr, c0, n0, m0, norm_w)


def _gdn_group(bbs, x_ref, z_ref, sc_ref, sr_ref, cw_ref, nw_ref, o_ref, s_ref, conv_ref, xp_scr, L):
    base = 8 - (GD_CONV - 1)
    nn, nt, tn = _chunk_dots(L)
    rows, cols = _tri(L)
    incl = rows >= cols
    strict = rows > cols
    eye = (rows == cols).astype(F32)
    tril = incl.astype(F32)
    triu = (rows <= cols).astype(F32)
    kw = GD_H * GD_DK
    ys, scs, gcols, grows = {}, {}, {}, {}
    for bb in bbs:
        xp_scr[bb, 8:8 + L, :] = x_ref[bb]
        y = xp_scr[bb, base:base + L, :] * cw_ref[0:1, :]
        for j in range(1, GD_CONV):
            y = y + xp_scr[bb, base + j:base + j + L, :] * cw_ref[j:j + 1, :]
        tail = xp_scr[bb, L + base:L + 8, :]
        xp_scr[bb, base:8, :] = tail
        conv_ref[bb] = tail
        ys[bb] = _silu(y)
        scs[bb] = sc_ref[bb]
        gcols[bb] = _dot_pick(tril, scs[bb][:, 12:16], exact_lhs=True)
        grows[bb] = _dot_pick(sr_ref[bb, 0][0:4, :], triu)
    chains = [(bb, h) for bb in bbs for h in range(GD_H)]
    q, k, vb, kb, dec, gcc, egc = {}, {}, {}, {}, {}, {}, {}
    for c in chains:
        bb, h = c
        y = ys[bb]
        qh = y[:, h * GD_DK:(h + 1) * GD_DK]
        kh = y[:, kw + h * GD_DK:kw + (h + 1) * GD_DK]
        vh = y[:, 2 * kw + h * GD_DV:2 * kw + (h + 1) * GD_DV]
        q[c] = qh * lax.rsqrt(jnp.sum(qh * qh, axis=-1, keepdims=True) + EPS) * (GD_DK ** -0.5)
        k[c] = kh * lax.rsqrt(jnp.sum(kh * kh, axis=-1, keepdims=True) + EPS)
        beta = scs[bb][:, 8 + h:9 + h]
        gcc[c] = gcols[bb][:, h:h + 1]
        dec[c] = jnp.exp(jnp.where(incl, gcc[c] - grows[bb][h:h + 1, :], NEG_BIG))
        egc[c] = jnp.exp(gcc[c])
        kb[c] = k[c] * beta
        vb[c] = vh * beta
    pw = {c: -(nt(kb[c], k[c]) * jnp.where(strict, dec[c], 0.0)) for c in chains}
    attn = {c: nt(q[c], k[c]) * dec[c] for c in chains}
    tinv = {c: eye + pw[c] for c in chains}
    for _ in range(int(math.log2(L)) - 1):
        pw = {c: _dot_split(pw[c], pw[c]) for c in chains}
        tinv = {c: tinv[c] + _dot_split(tinv[c], pw[c]) for c in chains}
    u = {c: _dot_split(tinv[c], vb[c]) for c in chains}
    w = {c: _dot_split(tinv[c], kb[c] * egc[c]) for c in chains}
    s_prev = {c: s_ref[c[0], c[1]] for c in chains}
    v_new = {c: u[c] - nn(w[c], s_prev[c]) for c in chains}
    o = {c: nn(q[c] * egc[c], s_prev[c]) + nn(attn[c], v_new[c]) for c in chains}
    for c in chains:
        bb, h = c
        g_last = gcc[c][L - 1:L, :]
        s_ref[bb, h] = jnp.exp(g_last) * s_prev[c] + tn(k[c] * jnp.exp(g_last - gcc[c]), v_new[c])
        on = o[c] * lax.rsqrt(jnp.mean(o[c] * o[c], axis=-1, keepdims=True) + EPS) * nw_ref[...]
        o_ref[bb, :, h * GD_DV:(h + 1) * GD_DV] = on * _silu(z_ref[bb, :, h * GD_DV:(h + 1) * GD_DV])


def _gdn_body(x_ref, z_ref, sc_ref, sr_ref, s0_ref, conv0_ref, cw_ref, nw_ref,
              o_ref, s_ref, conv_ref, xp_scr, *, L, Bb):
    @pl.when(pl.program_id(1) == 0)
    def _():
        s_ref[...] = s0_ref[...]
        xp_scr[:, 8 - (GD_CONV - 1):8, :] = conv0_ref[...]

    _gdn_group(list(range(Bb)), x_ref, z_ref, sc_ref, sr_ref, cw_ref, nw_ref, o_ref, s_ref, conv_ref, xp_scr, L)


def gdn(x, z, sc, sr, s0, conv0, conv_w, norm_w, L, Bb):
    B, T, ch = x.shape
    H = GD_H
    nc = T // L
    hv = H * GD_DV
    bmap = lambda b, c: (b, 0, 0, 0)
    return pl.pallas_call(
        functools.partial(_gdn_body, L=L, Bb=Bb),
        grid=(B // Bb, nc),
        in_specs=[pl.BlockSpec((Bb, L, ch), lambda b, c: (b, c, 0)),
                  pl.BlockSpec((Bb, L, hv), lambda b, c: (b, c, 0)),
                  pl.BlockSpec((Bb, L, LANE), lambda b, c: (b, c, 0)),
                  pl.BlockSpec((Bb, 1, 8, L), lambda b, c: (b, c, 0, 0)),
                  pl.BlockSpec((Bb, H, GD_DK, GD_DV), bmap),
                  pl.BlockSpec((Bb, GD_CONV - 1, ch), lambda b, c: (b, 0, 0)),
                  pl.BlockSpec((GD_CONV, ch), lambda b, c: (0, 0)),
                  pl.BlockSpec((1, GD_DV), lambda b, c: (0, 0))],
        out_specs=[pl.BlockSpec((Bb, L, hv), lambda b, c: (b, c, 0)),
                   pl.BlockSpec((Bb, H, GD_DK, GD_DV), bmap),
                   pl.BlockSpec((Bb, GD_CONV - 1, ch), lambda b, c: (b, 0, 0))],
        out_shape=[jax.ShapeDtypeStruct((B, T, hv), F32),
                   jax.ShapeDtypeStruct((B, H, GD_DK, GD_DV), F32),
                   jax.ShapeDtypeStruct((B, GD_CONV - 1, ch), F32)],
        scratch_shapes=[pltpu.VMEM((Bb, L + 8, ch), F32)],
        compiler_params=_cparams(("parallel", "arbitrary")),
        name="gdn",
    )(x, z, sc, sr, s0, conv0, conv_w, norm_w.reshape(1, GD_DV))


def _compress_body(x_ref, pos_ref, w1_ref, w2_ref, o_ref):
    Bb, _, R, half = x_ref.shape
    x = x_ref[:, 0].reshape(Bb * R, half).astype(F32)
    ua = _dot((x + pos_ref[0, 0:1, :]).astype(BF16), w1_ref[0, 0])
    ub = _dot((x + pos_ref[0, 1:2, :]).astype(BF16), w1_ref[0, 1])
    h = _silu(ua + pltpu.roll(ub, Bb * R - 1, 0))
    o_ref[:, 0, 0:R, :] = _dot(h.astype(BF16), w2_ref[0]).reshape(Bb, R, HD)
    rp = o_ref.shape[2]
    if rp > R:
        o_ref[:, 0, R:rp, :] = jnp.zeros((Bb, rp - R, HD), F32)


def nsa_compress(xr, R, pos, w1, w2, Bb):
    B = xr.shape[0]
    half = CMP_STRIDE * HD
    rp = -(-R // LANE) * LANE
    return pl.pallas_call(
        _compress_body,
        grid=(4, B // Bb),
        in_specs=[pl.BlockSpec((Bb, 1, R, half), lambda c, b: (b, c, 0, 0)),
                  pl.BlockSpec((1, 2, half), lambda c, b: (c // 2, 0, 0)),
                  pl.BlockSpec((1, 2, half, CMP_HIDDEN), lambda c, b: (c // 2, 0, 0, 0)),
                  pl.BlockSpec((1, CMP_HIDDEN, HD), lambda c, b: (c // 2, 0, 0))],
        out_specs=pl.BlockSpec((Bb, 1, rp, HD), lambda c, b: (b, c, 0, 0)),
        out_shape=jax.ShapeDtypeStruct((B, 4, rp, HD), F32),
        compiler_params=_cparams(("parallel", "parallel")),
        name="nsa_compress",
    )(xr, pos.reshape(2, 2, half), w1.reshape(2, 2, half, CMP_HIDDEN).astype(BF16), w2.astype(BF16))


def _cumsum_body(x_ref, o_ref):
    Bb, H, T = x_ref.shape
    rows, cols = _tri(LANE)
    triu = (rows <= cols).astype(F32)
    carry = jnp.zeros((Bb * H, 1), F32)
    for c in range(T // LANE):
        seg = x_ref[:, :, c * LANE:(c + 1) * LANE].reshape(Bb * H, LANE)
        loc = _dot_pick(seg, triu) + carry
        o_ref[:, :, c * LANE:(c + 1) * LANE] = loc.reshape(Bb, H, LANE)
        carry = loc[:, LANE - 1:LANE]


def cumsum_lanes(x, Bb):
    B, H, T = x.shape
    return pl.pallas_call(
        _cumsum_body,
        grid=(B // Bb,),
        in_specs=[pl.BlockSpec((Bb, H, T), lambda b: (b, 0, 0))],
        out_specs=pl.BlockSpec((Bb, H, T), lambda b: (b, 0, 0)),
        out_shape=jax.ShapeDtypeStruct((B, H, T), F32),
        compiler_params=_cparams(("parallel",)),
        name="cumsum",
    )(x)


def _flash_tile(s_blocks, v, m_scr, l_scr, acc_scr):
    dv = acc_scr.shape[-1]
    m_prev = m_scr[...]
    mx = s_blocks[0]
    for sb in s_blocks[1:]:
        mx = jnp.maximum(mx, sb)
    m_new = jnp.maximum(m_prev, jnp.max(mx, axis=1, keepdims=True))
    p_blocks = [jnp.exp(sb - m_new) for sb in s_blocks]
    sm = p_blocks[0]
    for pb in p_blocks[1:]:
        sm = sm + pb
    alpha = jnp.exp(m_prev - m_new)
    l_scr[...] = alpha * l_scr[...] + jnp.sum(sm, axis=1, keepdims=True)
    p = (jnp.concatenate(p_blocks, axis=1) if len(p_blocks) > 1 else p_blocks[0]).astype(BF16)
    acc_scr[...] = alpha[:, :dv] * acc_scr[...] + _dot(p, v)
    m_scr[...] = m_new


def _flash_reset(m_scr, l_scr, acc_scr):
    m_scr[...] = jnp.full_like(m_scr, NEG_BIG)
    l_scr[...] = jnp.zeros_like(l_scr)
    acc_scr[...] = jnp.zeros_like(acc_scr)


def _lane_blocks(s):
    return [s[:, i * LANE:(i + 1) * LANE] for i in range(s.shape[1] // LANE)]


def _fox_body(q_ref, k_ref, v_ref, fq_ref, fk_ref, o_ref, m_scr, l_scr, acc_scr, *, tq, tk):
    qi = pl.program_id(1)
    j = pl.program_id(2)
    top = (qi * tq + tq - 1) // tk

    @pl.when(j == 0)
    def _():
        _flash_reset(m_scr, l_scr, acc_scr)

    def tile(diag):
        fq = fq_ref[0]
        fk = fk_ref[0]
        if diag:
            mask = (top * tk + _iota((tq, tk), 1)) <= (qi * tq + _iota((tq, tk), 0))
        for h in range(FOX_H):
            s = _dot_nt(q_ref[0, h], k_ref[0, h]) + fq[:, h:h + 1] - fk[h:h + 1, :]
            if diag:
                s = jnp.where(mask, s, NEG_BIG)
            _flash_tile(_lane_blocks(s), v_ref[0, h], m_scr.at[h], l_scr.at[h], acc_scr.at[h])

    @pl.when(j == 0)
    def _():
        tile(True)

    @pl.when(jnp.logical_and(j > 0, j <= top))
    def _():
        tile(False)

    @pl.when(j == pl.num_programs(2) - 1)
    def _():
        for h in range(FOX_H):
            o_ref[0, h] = acc_scr[h] / jnp.maximum(l_scr[h][:, :HD], 1e-30)


def fox_prompt(q, kv, fq, fk, tq, tk):
    B, H, T, _ = q.shape
    kmap = lambda i, j: jnp.maximum((i * tq + tq - 1) // tk - j, 0)
    return pl.pallas_call(
        functools.partial(_fox_body, tq=tq, tk=tk),
        grid=(B, T // tq, T // tk),
        in_specs=[pl.BlockSpec((1, H, tq, HD), lambda b, i, j: (b, 0, i, 0)),
                  pl.BlockSpec((1, H, tk, HD), lambda b, i, j: (b, 0, kmap(i, j), 0)),
                  pl.BlockSpec((1, H, tk, HD), lambda b, i, j: (b, 1, kmap(i, j), 0)),
                  pl.BlockSpec((1, tq, H), lambda b, i, j: (b, i, 0)),
                  pl.BlockSpec((1, H, tk), lambda b, i, j: (b, 0, kmap(i, j)))],
        out_specs=pl.BlockSpec((1, H, tq, HD), lambda b, i, j: (b, 0, i, 0)),
        out_shape=jax.ShapeDtypeStruct((B, H, T, HD), F32),
        scratch_shapes=[pltpu.VMEM((H, tq, LANE), F32), pltpu.VMEM((H, tq, LANE), F32), pltpu.VMEM((H, tq, HD), F32)],
        compiler_params=_cparams(("parallel", "parallel", "arbitrary")),
        name="fox_prompt",
    )(q, kv, kv, fq, fk)


def _fox_decode_body(pt_ref, q_ref, newkv_ref, newlf_ref, *refs, n_pages, tn):
    kv_refs = refs[:n_pages]
    lf_refs = refs[n_pages:2 * n_pages]
    o_ref = refs[2 * n_pages]
    hw = FOX_H * HD
    R = FOX_H * tn
    q = q_ref[0]
    qrep = jnp.concatenate([q] * FOX_H, axis=0)
    blockmask = (_iota((R, hw), 0) // tn) == (_iota((R, hw), 1) // HD)
    qbd = jnp.where(blockmask, qrep, 0.0).astype(BF16)
    rows, cols = _tri(PAGE)
    triu = (rows <= cols).astype(F32)
    ones = jnp.ones((8, PAGE), F32)
    expand = lambda a: jnp.concatenate([jnp.broadcast_to(a[h:h + 1, :], (tn, a.shape[1])) for h in range(FOX_H)], 0)
    carry_c = jnp.zeros((FOX_H, 1), F32)
    carry_r = jnp.zeros((1, FOX_H), F32)
    s_tiles = []
    for pg in range(n_pages):
        lf = lf_refs[pg][0]
        f_t = _dot_pick(lf, triu) + carry_c
        carry_c = f_t[:, PAGE - 1:PAGE]
        carry_r = carry_r + _dot_pick(ones, lf, _dot_nt, exact_lhs=True)[0:1, :]
        s_tiles.append(_dot(qbd, kv_refs[pg][0, 0:hw, :].astype(BF16)) - expand(f_t))
    lfn = newlf_ref[0]
    r8, c8 = _tri(tn)
    fq_c = _dot_pick((r8 >= c8).astype(F32), lfn, exact_lhs=True) + carry_r
    fq_t = _dot_pick(lfn, (r8 <= c8).astype(F32), _dot_tn) + carry_c
    fq_rows = jnp.concatenate([fq_c[:, h:h + 1] for h in range(FOX_H)], axis=0)
    kn = newkv_ref[0, :, 0:hw]
    s_new = _dot_split(qbd.astype(F32), kn, _dot_nt) - expand(fq_t)
    causal = _iota((R, tn), 1) <= (_iota((R, tn), 0) % tn)
    s_new = jnp.where(causal, s_new + fq_rows, NEG_BIG)
    s_tiles = [s + fq_rows for s in s_tiles]
    m = jnp.max(s_new, axis=1, keepdims=True)
    for s in s_tiles:
        m = jnp.maximum(m, jnp.max(s, axis=1, keepdims=True))
    p_new = jnp.where(causal, jnp.exp(s_new - m), 0.0)
    l = jnp.sum(p_new, axis=1, keepdims=True)
    acc = _dot_split(p_new, newkv_ref[0, :, hw:2 * hw])
    for pg, s in enumerate(s_tiles):
        p = jnp.exp(s - m)
        l = l + jnp.sum(p, axis=1, keepdims=True)
        acc = acc + _dot_nt(p.astype(BF16), kv_refs[pg][0, hw:2 * hw, :].astype(BF16))
    acc = acc / jnp.maximum(l, 1e-30)
    o_ref[0] = jnp.concatenate([acc[h * tn:(h + 1) * tn, h * HD:(h + 1) * HD] for h in range(FOX_H)], axis=1)


def fox_decode(page_table, q, newkv, newlf, kv_pool_t, lf_pool_t):
    B, tn, hw = q.shape
    n_pages = page_table.shape[1]
    page_spec = lambda rows, pg: pl.BlockSpec((1, rows, PAGE), lambda b, pt: (pt[b, pg], 0, 0))
    grid_spec = pltpu.PrefetchScalarGridSpec(
        num_scalar_prefetch=1,
        grid=(B,),
        in_specs=[pl.BlockSpec((1, tn, hw), lambda b, pt: (b, 0, 0)),
                  pl.BlockSpec((1, tn, 2 * hw), lambda b, pt: (b, 0, 0)),
                  pl.BlockSpec((1, tn, FOX_H), lambda b, pt: (b, 0, 0))]
                 + [page_spec(2 * hw, pg) for pg in range(n_pages)]
                 + [page_spec(FOX_H, pg) for pg in range(n_pages)],
        out_specs=pl.BlockSpec((1, tn, hw), lambda b, pt: (b, 0, 0)),
    )
    return pl.pallas_call(
        functools.partial(_fox_decode_body, n_pages=n_pages, tn=tn),
        grid_spec=grid_spec,
        out_shape=jax.ShapeDtypeStruct((B, tn, hw), F32),
        compiler_params=_cparams(("arbitrary",)),
        name="fox_decode",
    )(page_table, q, newkv, newlf, *([kv_pool_t] * n_pages), *([lf_pool_t] * n_pages))


def _t5_bucket(dist):
    n = jnp.maximum(dist, 0)
    nf = jnp.maximum(n, 1).astype(F32)
    large = BUCKET_EXACT + (jnp.log(nf / BUCKET_EXACT) / math.log(MAX_DISTANCE / BUCKET_EXACT)
                            * (N_BUCKETS - BUCKET_EXACT)).astype(jnp.int32)
    return jnp.where(n < BUCKET_EXACT, n, jnp.minimum(large, N_BUCKETS - 1))


def _bias_from_bucket(bucket, tbl_ref, head):
    out = jnp.zeros(bucket.shape, F32)
    for kk in range(N_BUCKETS):
        out = jnp.where(bucket == kk, tbl_ref[kk, head], out)
    return out


FOX_TQ, FOX_TK = 256, 512
RECURRENT_BATCH = 4
NSA_TK = 2 * LANE
NSA_FAR_GROUP = 4
NSA_BIAS_TILES = (BUCKET_SAT_DIST + NSA_TK + LANE - 1) // LANE


def _nsa_body(tbl_ref, q_ref, gate_ref, kc_ref, vc_ref, ks_ref, vs_ref, kw_ref, vw_ref, smap_ref, o_ref,
              bias_scr, score_scr, m_scr, l_scr, acc_scr, *, tq, q_pos0, win_pos0, n_sel, ncp, tw):
    g = pl.program_id(1)
    qi = pl.program_id(2)
    q0 = q_pos0 + qi * tq
    R = NSA_R * tq
    scale = HD ** -0.5
    last_bias = tuple(tbl_ref[N_BUCKETS - 1, g * NSA_R + r] for r in range(NSA_R))

    @pl.when(qi == 0)
    def _():
        ii = _iota((tq, NSA_TK), 0)
        jj = _iota((tq, NSA_TK), 1)
        for dd in range(NSA_BIAS_TILES):
            bucket = _t5_bucket(ii - jj + dd * LANE)
            for r in range(NSA_R):
                bias_scr[dd, r * tq:(r + 1) * tq, :] = (
                    _bias_from_bucket(bucket, tbl_ref, g * NSA_R + r) - last_bias[r])
        bias_scr[NSA_BIAS_TILES] = jnp.zeros((R, NSA_TK), F32)

    t_col = q0 + _iota((tq, 1), 0)

    bias_tiles, mask_tiles = [], []
    for nt in range(ncp // LANE):
        c_end = (nt * LANE + _iota((tq, LANE), 1)) * CMP_STRIDE + (CMP_BLOCK - 1)
        dist = t_col - c_end
        max_dist = q0 + tq - 1 - (nt * LANE * CMP_STRIDE + CMP_BLOCK - 1)
        min_dist = q0 - ((nt * LANE + LANE - 1) * CMP_STRIDE + CMP_BLOCK - 1)
        special = jnp.logical_and(max_dist >= 0, min_dist < BUCKET_SAT_DIST)

        def general(dist=dist):
            bucket = _t5_bucket(dist)
            return jnp.stack([_bias_from_bucket(bucket, tbl_ref, g * NSA_R + r) - last_bias[r]
                              for r in range(NSA_R)])

        def saturated():
            return jnp.zeros((NSA_R, tq, LANE), F32)

        bias_tiles.append(lax.cond(special, general, saturated))
        mask_tiles.append(dist >= 0)
    mask_c = jnp.concatenate(mask_tiles, axis=1) if len(mask_tiles) > 1 else mask_tiles[0]
    kc = kc_ref[0, 0].astype(BF16)
    vc = vc_ref[0, 0].astype(BF16)
    pcsum = jnp.zeros((tq, ncp), F32)
    o_c = []
    for r in range(NSA_R):
        bias_r = jnp.concatenate([b[r] for b in bias_tiles], axis=1) if len(bias_tiles) > 1 else bias_tiles[0][r]
        s = _dot_nt(q_ref[0, r].astype(BF16), kc) * scale + bias_r
        s = jnp.where(mask_c, s, NEG_BIG)
        m = jnp.max(s, axis=1, keepdims=True)
        p = jnp.where(mask_c, jnp.exp(s - m), 0.0)
        pc = p / jnp.maximum(jnp.sum(p, axis=1, keepdims=True), 1e-30)
        o_c.append(_dot(pc.astype(BF16), vc))
        pcsum = pcsum + pc

    ps_t = _dot_pick(smap_ref[...], pcsum, _dot_nt, exact_lhs=True)
    j_col = _iota((LANE, 1), 0)
    t_row = q0 + _iota((1, tq), 1)
    cur = lax.shift_right_logical(t_row, int(math.log2(SEL_BLOCK)))
    score = jnp.where(j_col * SEL_BLOCK <= t_row, ps_t, -1.0)
    score = jnp.where(j_col == cur - 1, FORCE_SCORE, score)
    score = jnp.where(j_col == cur, FORCE_SCORE, score)
    score = jnp.where(j_col == 0, FORCE_SCORE, score)
    score = jnp.where(j_col < n_sel, score, -3e38)
    score_scr[...] = score

    def rank_body(jp, rank):
        row = score_scr[pl.ds(jp, 1), :]
        tie = jnp.where(j_col > jp, 1.0, 0.0)
        return rank + jnp.where(row > score, 1.0, jnp.where(row == score, tie, 0.0))

    n_rank = jnp.minimum((q0 + tq - 1) // SEL_BLOCK + 1, LANE)
    rank = lax.fori_loop(0, n_rank, rank_body, jnp.zeros((LANE, tq), F32))
    sel_t = jnp.where(rank < SEL_TOPN, 1.0, 0.0).astype(BF16)
    eye = (_iota((tq, tq), 0) == _iota((tq, tq), 1)).astype(BF16)
    sel = _dot_nt(eye, sel_t).astype(BF16)

    qs = (q_ref[0].reshape(R, HD) * scale).astype(BF16)
    t_tile = q0 + _iota((tq, NSA_TK), 0)
    c_tile = _iota((tq, NSA_TK), 1)
    log2_blk = int(math.log2(SEL_BLOCK))

    def stack(a):
        return jnp.concatenate([a] * NSA_R, axis=0)

    def bias_tile(offset):
        return bias_scr[jnp.minimum(lax.shift_right_logical(offset, 7), NSA_BIAS_TILES)]

    def sel_tile(k0, width, near):
        k = ks_ref[0, 0, pl.ds(k0, width), :]
        v = vs_ref[0, 0, pl.ds(k0, width), :]
        blk = lax.shift_right_logical(k0 + _iota((LANE, width), 1), log2_blk)
        expand = jnp.where(_iota((LANE, width), 0) == blk, 1.0, 0.0).astype(BF16)
        chosen = _dot(sel, expand)
        s = _dot_nt(qs, k)
        if near:
            chosen = jnp.where(t_tile - (k0 + c_tile) >= 0, chosen, 0.0)
            s = s + bias_tile(q0 - k0)
        s = jnp.where(stack(chosen) > 0.5, s, NEG_BIG)
        _flash_tile(_lane_blocks(s), v, m_scr, l_scr, acc_scr)

    _flash_reset(m_scr, l_scr, acc_scr)
    kt_top = (q0 + tq - 1) // NSA_TK
    sel_tile(pl.multiple_of(kt_top * NSA_TK, NSA_TK), NSA_TK, True)

    @pl.when(kt_top >= 1)
    def _():
        sel_tile(pl.multiple_of((kt_top - 1) * NSA_TK, NSA_TK), NSA_TK, True)

    n_far = jnp.maximum(kt_top - 1, 0)
    n_groups = n_far // NSA_FAR_GROUP

    def sel_far_group(gi, carry):
        sel_tile(pl.multiple_of(gi * (NSA_FAR_GROUP * NSA_TK), NSA_FAR_GROUP * NSA_TK), NSA_FAR_GROUP * NSA_TK, False)
        return carry

    def sel_far(kt, carry):
        sel_tile(pl.multiple_of(kt * NSA_TK, NSA_TK), NSA_TK, False)
        return carry

    lax.fori_loop(0, n_groups, sel_far_group, 0)
    lax.fori_loop(n_groups * NSA_FAR_GROUP, n_far, sel_far, 0)
    o_s = acc_scr[...] / jnp.maximum(l_scr[...][:, :HD], 1e-30)

    def win_tile(i, carry):
        k0 = pl.multiple_of((wt_top - i) * NSA_TK, NSA_TK)
        k = kw_ref[0, 0, pl.ds(k0, NSA_TK), :]
        v = vw_ref[0, 0, pl.ds(k0, NSA_TK), :]
        dist = t_tile - (win_pos0 + k0 + c_tile)
        ok = jnp.where(dist >= 0, jnp.where(dist < WINDOW, 1.0, 0.0), 0.0)
        s = _dot_nt(qs, k) + bias_tile(q0 - win_pos0 - k0)
        s = jnp.where(stack(ok) > 0.5, s, NEG_BIG)
        _flash_tile(_lane_blocks(s), v, m_scr, l_scr, acc_scr)
        return carry

    _flash_reset(m_scr, l_scr, acc_scr)
    wt_top = (jnp.minimum(q0 + tq - win_pos0, tw) - 1) // NSA_TK
    wt_lo = jnp.maximum(q0 - (WINDOW - 1) - win_pos0, 0) // NSA_TK
    lax.fori_loop(0, wt_top - wt_lo + 1, win_tile, 0)
    o_w = acc_scr[...] / jnp.maximum(l_scr[...][:, :HD], 1e-30)

    gates = gate_ref[0, 0]
    for r in range(NSA_R):
        o_ref[0, r] = (gates[:, 3 * r:3 * r + 1] * o_c[r]
                       + gates[:, 3 * r + 1:3 * r + 2] * o_s[r * tq:(r + 1) * tq]
                       + gates[:, 3 * r + 2:3 * r + 3] * o_w[r * tq:(r + 1) * tq])


def _selection_overlap_t(ncp):
    c_start = np.arange(ncp)[None, :] * CMP_STRIDE
    s_start = np.arange(LANE)[:, None] * SEL_BLOCK
    return ((c_start < s_start + SEL_BLOCK) & (c_start + CMP_BLOCK > s_start)).astype(np.float32)


def nsa_attend(tbl, q, gates, kcvc, sel_arr, sel_off, win_arr, win_off, *, tq, q_pos0, win_pos0, n_sel):
    B, _, Tq, _ = q.shape
    ncp = kcvc.shape[2]
    tks = sel_arr.shape[2]
    tw = win_arr.shape[2]
    R = NSA_R * tq
    smap = jnp.asarray(_selection_overlap_t(ncp))
    kv_spec = lambda rows, off: pl.BlockSpec((1, 1, rows, HD), lambda b, g, i: (b, off + g, 0, 0))
    return pl.pallas_call(
        functools.partial(_nsa_body, tq=tq, q_pos0=q_pos0, win_pos0=win_pos0, n_sel=n_sel, ncp=ncp, tw=tw),
        grid=(B, NSA_G, Tq // tq),
        in_specs=[pl.BlockSpec(memory_space=pltpu.SMEM),
                  pl.BlockSpec((1, NSA_R, tq, HD), lambda b, g, i: (b, g, i, 0)),
                  pl.BlockSpec((1, 1, tq, 3 * NSA_R), lambda b, g, i: (b, g, i, 0)),
                  kv_spec(ncp, 0), kv_spec(ncp, 2),
                  kv_spec(tks, sel_off), kv_spec(tks, sel_off + 2),
                  kv_spec(tw, win_off), kv_spec(tw, win_off + 2),
                  pl.BlockSpec((LANE, ncp), lambda b, g, i: (0, 0))],
        out_specs=pl.BlockSpec((1, NSA_R, tq, HD), lambda b, g, i: (b, g, i, 0)),
        out_shape=jax.ShapeDtypeStruct((B, NSA_G * NSA_R, Tq, HD), F32),
        scratch_shapes=[pltpu.VMEM((NSA_BIAS_TILES + 1, R, NSA_TK), F32), pltpu.VMEM((LANE, tq), F32),
                        pltpu.VMEM((R, LANE), F32), pltpu.VMEM((R, LANE), F32), pltpu.VMEM((R, HD), F32)],
        compiler_params=_cparams(("parallel", "parallel", "arbitrary")),
        name="nsa_attend",
    )(tbl, q, gates, kcvc, kcvc, sel_arr, sel_arr, win_arr, win_arr, smap)


def _nsa_gather_body(pt_ref, new_ref, *refs, n_pages):
    pages = refs[:n_pages]
    cmp_ref, sel_ref, xt_scr = refs[n_pages:]
    ngrp = 2 * NSA_G
    half = ngrp * HD
    rows16 = PAGE // CMP_STRIDE
    for pg in range(n_pages):
        xt = pages[pg][0].T
        for j in range(ngrp):
            sel_ref[0, j, pg * PAGE:(pg + 1) * PAGE, :] = xt[:, half + j * HD:half + (j + 1) * HD].astype(BF16)
        for slab in range(half // LANE):
            xt_scr[slab] = xt[:, slab * LANE:(slab + 1) * LANE]
            for p in range(CMP_STRIDE):
                rows = xt_scr[slab, pl.ds(p, rows16, stride=CMP_STRIDE), :]
                for jj in range(LANE // HD):
                    cmp_ref[0, slab * (LANE // HD) + jj, pg * rows16:(pg + 1) * rows16, p * HD:(p + 1) * HD] = (
                        rows[:, jj * HD:(jj + 1) * HD])
    tn = new_ref.shape[1]
    tail = sel_ref.shape[2] - n_pages * PAGE
    for j in range(ngrp):
        new = new_ref[0, :, (ngrp + j) * HD:(ngrp + j + 1) * HD]
        tile = jnp.concatenate([new, jnp.zeros((tail - tn, HD), F32)], axis=0)
        sel_ref[0, j, n_pages * PAGE:, :] = tile.astype(BF16)


def nsa_gather(page_table, new_rows, pool_t):
    B, tn, width = new_rows.shape
    n_pages = page_table.shape[1]
    ngrp = 2 * NSA_G
    sel_rows = -(-(n_pages * PAGE + tn) // NSA_TK) * NSA_TK
    cmp_rows = n_pages * PAGE // CMP_STRIDE
    grid_spec = pltpu.PrefetchScalarGridSpec(
        num_scalar_prefetch=1,
        grid=(B,),
        in_specs=[pl.BlockSpec((1, tn, width), lambda b, pt: (b, 0, 0))]
                 + [pl.BlockSpec((1, 2 * ngrp * HD, PAGE),
                                 functools.partial(lambda b, pt, pg: (pt[b, pg], 0, 0), pg=pg))
                    for pg in range(n_pages)],
        out_specs=[pl.BlockSpec((1, ngrp, cmp_rows, CMP_STRIDE * HD), lambda b, pt: (b, 0, 0, 0)),
                   pl.BlockSpec((1, ngrp, sel_rows, HD), lambda b, pt: (b, 0, 0, 0))],
        scratch_shapes=[pltpu.VMEM((ngrp * HD // LANE, PAGE, LANE), F32)],
    )
    return pl.pallas_call(
        functools.partial(_nsa_gather_body, n_pages=n_pages),
        grid_spec=grid_spec,
        out_shape=[jax.ShapeDtypeStruct((B, ngrp, cmp_rows, CMP_STRIDE * HD), F32),
                   jax.ShapeDtypeStruct((B, ngrp, sel_rows, HD), BF16)],
        compiler_params=_cparams(("arbitrary",)),
        name="nsa_gather",
    )(page_table, new_rows, *([pool_t] * n_pages))


def _row_tile(m):
    return 512 if m % 512 == 0 else m


def _small_params(entries):
    sp = jnp.zeros((8, LANE), F32)
    for off, bias, act, log_scale in entries:
        n = bias.shape[0]
        sp = sp.at[0, off:off + n].set(bias.astype(F32))
        sp = sp.at[1, off:off + n].set(act)
        if log_scale is not None:
            sp = sp.at[2, off:off + n].set(log_scale.astype(F32))
    return sp


EVEN_WIDTHS = (256, 256, 512, 512, GD_CH, 512, LANE)


def _even_weights(w_in):
    s = np.cumsum((0, 256, 256, 512, 512, 4, 4, 512, 512, 512, 512, 4, 4))
    col = lambda i: w_in[:, s[i]:s[i + 1]]
    small = jnp.concatenate([col(4), col(5), col(10), col(11)], axis=1)
    small = jnp.pad(small, ((0, 0), (0, LANE - small.shape[1])))
    return jnp.concatenate([col(0), col(1), col(2), col(3), col(6), col(7), col(8), col(9), small], axis=1).astype(BF16)


def _chunk_rows(small, B, T, L, lanes):
    r = small.reshape(B, T // L, L, LANE)[..., lanes[0]:lanes[1]]
    r = jnp.swapaxes(r, 2, 3)
    return jnp.pad(r, ((0, 0), (0, 0), (0, 8 - r.shape[2]), (0, 0)))


def even_layer(x, p, past, L, Bb):
    B, T, D = x.shape
    M = B * T
    tm = _row_tile(M)
    sp = _small_params([(0, p['mi_b_i'], ACT_ID, None), (4, p['mi_b_f'], ACT_LOGSIG, None),
                        (8, jnp.zeros((4,), F32), ACT_SIG, None), (12, p['gd_dt_bias'], ACT_DECAY, p['gd_a_log'])])
    mq, mk, mv, mo, gx, gz, small = norm_proj(x.reshape(M, D), p['norm_mix'], _even_weights(p['w_in']), sp,
                                              EVEN_WIDTHS, tm)
    heads = lambda a: jnp.transpose(a.reshape(B, T, MI_H, MI_DQK), (0, 2, 1, 3))
    sc = small.reshape(B, T, LANE)
    if past is None:
        c0 = jnp.zeros((B, MI_H, MI_DQK, MI_DV), F32)
        n0 = jnp.zeros((B, MI_H, 1, MI_DQK), F32)
        m0 = jnp.zeros((B, 1, MI_H), F32)
        s0 = jnp.zeros((B, GD_H, GD_DK, GD_DV), F32)
        conv0 = jnp.zeros((B, GD_CONV - 1, GD_CH), F32)
    else:
        c0, n0, m0, s0, conv0 = past
        n0 = n0.reshape(B, MI_H, 1, MI_DQK)
        m0 = m0.reshape(B, 1, MI_H)
    hm, c1, n1, m1 = mlstm(heads(mq), heads(mk), mv.reshape(B, T, -1), mo.reshape(B, T, -1), sc,
                           _chunk_rows(small, B, T, L, (0, 8)), c0, n0, m0,
                           p['mi_norm'].reshape(MI_H, MI_DV), L, Bb)
    og, s1, conv1 = gdn(gx.reshape(B, T, GD_CH), gz.reshape(B, T, -1), sc,
                        _chunk_rows(small, B, T, L, (12, 16)), s0, conv0, p['gd_conv_w'], p['gd_norm'], L, Bb)
    y = out_proj_residual(x.reshape(M, D), hm.reshape(M, -1), og.reshape(M, -1), p['w_out'].astype(BF16), tm)
    return y.reshape(B, T, D), (c1, n1.reshape(B, MI_H, MI_DQK), m1.reshape(B, MI_H), s1, conv1)


NSA_QW = NSA_G * NSA_R * HD
NSA_KVW = 6 * NSA_G * HD
NSA_CACHE_W = 4 * NSA_G * HD
FOX_W = FOX_H * HD
N_GATE = 3 * NSA_G * NSA_R
ODD_WIDTHS = (NSA_QW, NSA_KVW, FOX_W, 2 * FOX_W, LANE)


def _odd_weights(w_in):
    s = np.cumsum((0, NSA_QW, NSA_KVW, N_GATE, FOX_W, FOX_W, FOX_W, FOX_H))
    col = lambda i: w_in[:, s[i]:s[i + 1]]
    small = jnp.concatenate([col(2), col(6)], axis=1)
    small = jnp.pad(small, ((0, 0), (0, LANE - small.shape[1])))
    return jnp.concatenate([col(0), col(1), col(3), col(4), col(5), small], axis=1).astype(BF16)


def _heads(a, B, T, n):
    return jnp.transpose(a.reshape(B, T, n, HD), (0, 2, 1, 3))


def _unheads(a):
    B, n, T, _ = a.shape
    return jnp.transpose(a, (0, 2, 1, 3)).reshape(B * T, n * HD)


def odd_layer(x, p, rel_bias, w_buf, past, page_table):
    B, T, D = x.shape
    M = B * T
    tm = _row_tile(M)
    sp = _small_params([(0, jnp.zeros((N_GATE,), F32), ACT_SIG, None), (N_GATE, p['fox_b_f'], ACT_LOGSIG, None)])
    nq, nkv, fq, fkv, small = norm_proj(x.reshape(M, D), p['norm_mix'], _odd_weights(p['w_in']), sp, ODD_WIDTHS, tm)
    new_nsa = nkv[:, :NSA_CACHE_W].reshape(B, T, 4, NSA_G, HD)
    new_win = nkv[:, NSA_CACHE_W:].reshape(B, T, 2, NSA_G, HD)
    new_fox = fkv.reshape(B, T, 2, FOX_H, HD)
    logf = small[:, N_GATE:N_GATE + FOX_H].reshape(B, T, FOX_H)
    q_heads = _heads(nq, B, T, NSA_G * NSA_R)
    gates = jnp.transpose(small[:, :N_GATE].reshape(B, T, NSA_G, 3 * NSA_R), (0, 2, 1, 3))
    cmp_args = (p['nsa_cmp_pos'], p['nsa_cmp_w1'], p['nsa_cmp_w2'])
    rows16 = CMP_STRIDE * HD
    if past is None:
        groups = _heads(nkv, B, T, 6 * NSA_G)
        kcvc = nsa_compress(groups.reshape(B, 6 * NSA_G, T // CMP_STRIDE, rows16), T // CMP_STRIDE, *cmp_args, 1)
        arr = groups.astype(BF16)
        o_n = nsa_attend(rel_bias, q_heads, gates, kcvc, arr, 2 * NSA_G, arr, 4 * NSA_G,
                         tq=min(T, LANE), q_pos0=0, win_pos0=0, n_sel=-(-T // SEL_BLOCK))
        f_t = cumsum_lanes(jnp.transpose(logf, (0, 2, 1)), B)
        o_f = fox_prompt(_heads(fq * HD ** -0.5, B, T, FOX_H).astype(BF16), _heads(fkv, B, T, 2 * FOX_H).astype(BF16),
                         jnp.transpose(f_t, (0, 2, 1)), f_t, min(T, FOX_TQ), min(T, FOX_TK))
        o_f = _unheads(o_f)
        win_prev = jnp.zeros((B, WINDOW, 2, NSA_G, HD), F32)
    else:
        nsa_pool, win_prev, fox_pool, logf_pool = past
        n_pool = nsa_pool.shape[0]
        n_pages = page_table.shape[1]
        start = n_pages * PAGE
        feature_major = lambda pool: jnp.moveaxis(pool.reshape(n_pool, PAGE, -1), 1, 2)
        cmp_rows, sel_arr = nsa_gather(page_table, nkv[:, :NSA_CACHE_W].reshape(B, T, NSA_CACHE_W),
                                       feature_major(nsa_pool))
        kcvc = nsa_compress(cmp_rows, start // CMP_STRIDE, *cmp_args, math.gcd(B, 8))
        wp = win_prev.shape[1]
        win_all = jnp.concatenate([win_prev.reshape(B, wp, 2 * NSA_G * HD), nkv[:, NSA_CACHE_W:].reshape(B, T, -1)], 1)
        tw = -(-(wp + T) // NSA_TK) * NSA_TK
        win_arr = _heads(jnp.pad(win_all, ((0, 0), (0, tw - wp - T), (0, 0))), B, tw, 2 * NSA_G).astype(BF16)
        o_n = nsa_attend(rel_bias, q_heads, gates, kcvc, sel_arr, 0, win_arr, 0,
                         tq=T, q_pos0=start, win_pos0=start - wp, n_sel=-(-(start + T) // SEL_BLOCK))
        o_f = fox_decode(page_table, (fq * HD ** -0.5).reshape(B, T, FOX_W), fkv.reshape(B, T, 2 * FOX_W), logf,
                         feature_major(fox_pool), feature_major(logf_pool))
        o_f = o_f.reshape(M, FOX_W)
    win_state = jnp.concatenate([win_prev, new_win], axis=1)[:, -w_buf:]
    y = out_proj_residual(x.reshape(M, D), _unheads(o_n), o_f, p['w_out'].astype(BF16), tm)
    return y.reshape(B, T, D), (new_nsa, win_state, new_fox, logf)


def _trunk(x, past, page_table, P, w_buf, L, Bb):
    B, T, D = x.shape
    pe = dict(norm_mix=P['norm_mix'][0], w_in=P['w_in_even'][0], w_out=P['w_out_even'][0], mi_b_i=P['mi_b_i'][0],
              mi_b_f=P['mi_b_f'][0], mi_norm=P['mi_norm'][0], gd_conv_w=P['gd_conv_w'][0], gd_a_log=P['gd_a_log'][0],
              gd_dt_bias=P['gd_dt_bias'][0], gd_norm=P['gd_norm'][0])
    po = dict(norm_mix=P['norm_mix'][1], w_in=P['w_in_odd'][0], w_out=P['w_out_odd'][0],
              nsa_cmp_pos=P['nsa_cmp_pos'][0], nsa_cmp_w1=P['nsa_cmp_w1'][0], nsa_cmp_w2=P['nsa_cmp_w2'][0],
              fox_b_f=P['fox_b_f'][0])
    tm = _row_tile(B * T)
    mlp = lambda x, layer, final: mlp_residual(
        x.reshape(B * T, D), P['norm_mlp'][layer], P['w_up'][layer].astype(BF16), P['w_down'][layer].astype(BF16),
        P['norm_final'], final, tm, 1024).reshape(B, T, D)
    even_past = None if past is None else tuple(past[k][0] for k in ('mc', 'mn', 'mm', 'gs', 'gc'))
    odd_past = None if past is None else tuple(past[k][0] for k in ('nsa_kv', 'nsa_win', 'fox_kv', 'fox_logf'))
    x, st_e = even_layer(x, pe, even_past, L, Bb)
    x = mlp(x, 0, False)
    x, st_o = odd_layer(x, po, P['rel_bias'], w_buf, odd_past, page_table)
    y = mlp(x, 1, True)
    return y, tuple(a[None] for a in st_e + st_o)


def kernel(x_prompt, x_sample, state_mlstm_c, state_mlstm_n, state_mlstm_m, state_gdn_s, state_gdn_conv,
           cache_nsa_kv, state_nsa_win, cache_fox_kv, cache_fox_logf, page_table,
           norm_mix, norm_mlp, norm_final, w_up, w_down,
           w_in_even, w_out_even, mi_b_i, mi_b_f, mi_norm, gd_conv_w, gd_a_log, gd_dt_bias, gd_norm,
           w_in_odd, w_out_odd, nsa_cmp_pos, nsa_cmp_w1, nsa_cmp_w2, fox_b_f, rel_bias):
    P = dict(norm_mix=norm_mix, norm_mlp=norm_mlp, norm_final=norm_final, w_up=w_up, w_down=w_down,
             w_in_even=w_in_even, w_out_even=w_out_even, mi_b_i=mi_b_i, mi_b_f=mi_b_f, mi_norm=mi_norm,
             gd_conv_w=gd_conv_w, gd_a_log=gd_a_log, gd_dt_bias=gd_dt_bias, gd_norm=gd_norm,
             w_in_odd=w_in_odd, w_out_odd=w_out_odd, nsa_cmp_pos=nsa_cmp_pos, nsa_cmp_w1=nsa_cmp_w1,
             nsa_cmp_w2=nsa_cmp_w2, fox_b_f=fox_b_f, rel_bias=rel_bias)
    w_buf = state_nsa_win.shape[2]
    b_p, t_p = x_prompt.shape[:2]
    y_p, st_p = _trunk(x_prompt, None, None, P, w_buf, math.gcd(t_p, 64), math.gcd(b_p, RECURRENT_BATCH))
    past = dict(mc=state_mlstm_c, mn=state_mlstm_n, mm=state_mlstm_m, gs=state_gdn_s, gc=state_gdn_conv,
                nsa_kv=cache_nsa_kv, nsa_win=state_nsa_win, fox_kv=cache_fox_kv, fox_logf=cache_fox_logf)
    b_s, t_s = x_sample.shape[:2]
    y_s, st_s = _trunk(x_sample, past, page_table, P, w_buf, math.gcd(t_s, 64), math.gcd(b_s, RECURRENT_BATCH))
    return (y_p, y_s) + st_p + st_s
```

```python
import functools
import math

import jax
import jax.numpy as jnp
import numpy as np
from jax import lax
from jax.experimental import pallas as pl
from jax.experimental.pallas import tpu as pltpu

F32 = jnp.float32
BF16 = jnp.bfloat16
HI = lax.Precision.HIGHEST

D_MODEL = 1024
D_FF = 4 * D_MODEL
EPS = 1e-6
NEG_BIG = -1e30
PAGE = 128

MI_H, MI_DQK, MI_DV = 4, 64, 128
GD_H, GD_DK, GD_DV, GD_CONV = 4, 128, 128, 4
GD_CH = 3 * GD_H * GD_DK
NSA_G, NSA_R, HD = 2, 4, 64
FOX_H = 8
CMP_BLOCK, CMP_STRIDE, CMP_HIDDEN = 32, 16, 256
SEL_BLOCK, SEL_TOPN, WINDOW = 64, 16, 512
FORCE_SCORE = 1e4
N_BUCKETS, MAX_DISTANCE = 32, 128
BUCKET_EXACT = N_BUCKETS // 2
BUCKET_SAT_DIST = 113
LANE = 128
VMEM_LIMIT = 56 * 1024 * 1024


def _cparams(sem):
    return pltpu.CompilerParams(dimension_semantics=sem, vmem_limit_bytes=VMEM_LIMIT)


def _dot(a, b, precision=None):
    return jnp.dot(a, b, preferred_element_type=F32, precision=precision)


def _dot_nt(a, b, precision=None):
    return lax.dot_general(a, b, (((1,), (1,)), ((), ())), preferred_element_type=F32, precision=precision)


def _dot_tn(a, b, precision=None):
    return lax.dot_general(a, b, (((0,), (0,)), ((), ())), preferred_element_type=F32, precision=precision)


def _softplus(x):
    return jnp.maximum(x, 0.0) + jnp.log1p(jnp.exp(-jnp.abs(x)))


def _sigmoid(x):
    return 1.0 / (1.0 + jnp.exp(-x))


def _silu(x):
    return x * _sigmoid(x)


def _iota(shape, dim):
    return lax.broadcasted_iota(jnp.int32, shape, dim)


ACT_ID, ACT_LOGSIG, ACT_SIG, ACT_DECAY = 0.0, 1.0, 2.0, 3.0


def _proj_body(x_ref, g_ref, w_ref, sp_ref, *out_refs, widths):
    x = x_ref[...]
    hn = (x * lax.rsqrt(jnp.mean(x * x, axis=-1, keepdims=True) + EPS) * g_ref[...]).astype(BF16)
    off = 0
    for i, (o_ref, n) in enumerate(zip(out_refs, widths)):
        r = _dot(hn, w_ref[:, off:off + n])
        if i == len(widths) - 1:
            z = r + sp_ref[0:1, :]
            mode = sp_ref[1:2, :]
            decay = -jnp.exp(sp_ref[2:3, :]) * _softplus(z)
            r = jnp.where(mode == ACT_LOGSIG, -_softplus(-z),
                          jnp.where(mode == ACT_SIG, _sigmoid(z),
                                    jnp.where(mode == ACT_DECAY, decay, z)))
        o_ref[...] = r.astype(o_ref.dtype)
        off += n


def norm_proj(x, g, w_bf16, small_params, widths, tm):
    m, d = x.shape
    n_total = sum(widths)
    assert w_bf16.shape == (d, n_total) and m % tm == 0
    out_shape = [jax.ShapeDtypeStruct((m, n), F32) for n in widths]
    return pl.pallas_call(
        functools.partial(_proj_body, widths=tuple(widths)),
        grid=(m // tm,),
        in_specs=[pl.BlockSpec((tm, d), lambda i: (i, 0)),
                  pl.BlockSpec((1, d), lambda i: (0, 0)),
                  pl.BlockSpec((d, n_total), lambda i: (0, 0)),
                  pl.BlockSpec((8, LANE), lambda i: (0, 0))],
        out_specs=[pl.BlockSpec((tm, n), lambda i: (i, 0)) for n in widths],
        out_shape=out_shape,
        compiler_params=_cparams(("parallel",)),
        name="norm_proj",
    )(x, g.reshape(1, d), w_bf16, small_params)


def _outproj_body(x_ref, a1_ref, a2_ref, w_ref, o_ref):
    k1 = a1_ref.shape[1]
    y = _dot(a1_ref[...].astype(BF16), w_ref[0:k1, :]) + _dot(a2_ref[...].astype(BF16), w_ref[k1:, :])
    o_ref[...] = x_ref[...] + y


def out_proj_residual(x, a1, a2, w_bf16, tm):
    m, d = x.shape
    k1, k2 = a1.shape[1], a2.shape[1]
    return pl.pallas_call(
        _outproj_body,
        grid=(m // tm,),
        in_specs=[pl.BlockSpec((tm, d), lambda i: (i, 0)),
                  pl.BlockSpec((tm, k1), lambda i: (i, 0)),
                  pl.BlockSpec((tm, k2), lambda i: (i, 0)),
                  pl.BlockSpec((k1 + k2, d), lambda i: (0, 0))],
        out_specs=pl.BlockSpec((tm, d), lambda i: (i, 0)),
        out_shape=jax.ShapeDtypeStruct((m, d), F32),
        compiler_params=_cparams(("parallel",)),
        name="out_proj",
    )(x, a1, a2, w_bf16)


def _mlp_body(x_ref, g_ref, wu_ref, wd_ref, gf_ref, o_ref, hn_scr, acc_scr, *, final_norm):
    j = pl.program_id(1)

    @pl.when(j == 0)
    def _():
        x = x_ref[...]
        hn_scr[...] = (x * lax.rsqrt(jnp.mean(x * x, axis=-1, keepdims=True) + EPS) * g_ref[...]).astype(BF16)
        acc_scr[...] = jnp.zeros_like(acc_scr)

    u = jnp.maximum(_dot(hn_scr[...], wu_ref[...]), 0.0)
    acc_scr[...] += _dot((u * u).astype(BF16), wd_ref[...])

    @pl.when(j == pl.num_programs(1) - 1)
    def _():
        y = x_ref[...] + acc_scr[...]
        if final_norm:
            y = y * lax.rsqrt(jnp.mean(y * y, axis=-1, keepdims=True) + EPS) * gf_ref[...]
        o_ref[...] = y


def mlp_residual(x, g, w_up_bf16, w_down_bf16, g_final, final_norm, tm, tf):
    m, d = x.shape
    f = w_up_bf16.shape[1]
    return pl.pallas_call(
        functools.partial(_mlp_body, final_norm=final_norm),
        grid=(m // tm, f // tf),
        in_specs=[pl.BlockSpec((tm, d), lambda i, j: (i, 0)),
                  pl.BlockSpec((1, d), lambda i, j: (0, 0)),
                  pl.BlockSpec((d, tf), lambda i, j: (0, j)),
                  pl.BlockSpec((tf, d), lambda i, j: (j, 0)),
                  pl.BlockSpec((1, d), lambda i, j: (0, 0))],
        out_specs=pl.BlockSpec((tm, d), lambda i, j: (i, 0)),
        out_shape=jax.ShapeDtypeStruct((m, d), F32),
        scratch_shapes=[pltpu.VMEM((tm, d), BF16), pltpu.VMEM((tm, d), F32)],
        compiler_params=_cparams(("parallel", "arbitrary")),
        name="mlp",
    )(x, g.reshape(1, d), w_up_bf16, w_down_bf16, g_final.reshape(1, d))


def _tri(n):
    r = _iota((n, n), 0)
    c = _iota((n, n), 1)
    return r, c


def _bf16_terms(a, n):
    terms, rest = [], a
    for _ in range(n):
        t = rest.astype(BF16)
        terms.append(t)
        rest = rest - t.astype(F32)
    return terms


def _dot_split(a, b, f=None):
    f = f or _dot
    a_hi, a_lo = _bf16_terms(a, 2)
    b_hi, b_lo = _bf16_terms(b, 2)
    return f(a_hi, b_hi) + (f(a_hi, b_lo) + f(a_lo, b_hi))


def _dot_pick(a, b, f=None, exact_lhs=False):
    f = f or _dot
    if exact_lhs:
        return sum(f(a.astype(BF16), t) for t in _bf16_terms(b, 3))
    return sum(f(t, b.astype(BF16)) for t in _bf16_terms(a, 3))


def _chunk_dots(L):
    if L % 16 == 0:
        cast = lambda f: (lambda a, b: f(a.astype(BF16), b.astype(BF16)))
        return cast(_dot), cast(_dot_nt), cast(_dot_tn)
    full = lambda f: (lambda a, b: _dot_split(a, b, f))
    return full(_dot), full(_dot_nt), full(_dot_tn)


def _mlstm_group(bbs, q_ref, k_ref, v_ref, o_ref, sc_ref, sr_ref, nw_ref, h_ref, c_ref, n_ref, m_ref, L):
    nn, nt, tn = _chunk_dots(L)
    rows, cols = _tri(L)
    lower = rows >= cols
    tril = lower.astype(F32)
    triu = (rows <= cols).astype(F32)
    lane4 = _iota((1, MI_H), 1)
    scs = {bb: sc_ref[bb] for bb in bbs}
    srs = {bb: sr_ref[bb, 0] for bb in bbs}
    b_col = {bb: _dot_pick(tril, scs[bb][:, 4:8], exact_lhs=True) for bb in bbs}
    b_row = {bb: _dot_pick(srs[bb][4:8, :], triu) for bb in bbs}
    m_vec = {bb: m_ref[bb] for bb in bbs}
    chains = [(bb, h) for bb in bbs for h in range(MI_H)]
    q = {c: q_ref[c[0], c[1]] for c in chains}
    k = {c: k_ref[c[0], c[1]] * (MI_DQK ** -0.5) for c in chains}
    v = {c: v_ref[c[0], :, c[1] * MI_DV:(c[1] + 1) * MI_DV] for c in chains}
    qk = {c: nt(q[c], k[c]) for c in chains}
    c_prev = {c: c_ref[c[0], c[1]] for c in chains}
    qc = {c: nn(q[c], c_prev[c]) for c in chains}
    s, a_inter, m_t, m_new, a_prev, kw = {}, {}, {}, {}, {}, {}
    for c in chains:
        bb, h = c
        bc = b_col[bb][:, h:h + 1]
        m_prev = m_vec[bb][:, h:h + 1]
        d = jnp.where(lower, bc - b_row[bb][h:h + 1, :] + srs[bb][h:h + 1, :], NEG_BIG)
        inter = bc + m_prev
        m_t[c] = jnp.maximum(inter, jnp.max(d, axis=1, keepdims=True))
        s[c] = qk[c] * jnp.exp(d - m_t[c])
        a_inter[c] = jnp.exp(inter - m_t[c])
        b_last = bc[L - 1:L, :]
        g_col = b_last - bc + scs[bb][:, h:h + 1]
        m_new[c] = jnp.maximum(b_last + m_prev, jnp.max(g_col, axis=0, keepdims=True))
        a_prev[c] = jnp.exp(b_last + m_prev - m_new[c])
        kw[c] = k[c] * jnp.exp(g_col - m_new[c])
    sv = {c: nn(s[c], v[c]) for c in chains}
    kv = {c: tn(kw[c], v[c]) for c in chains}
    for c in chains:
        bb, h = c
        n_prev = n_ref[bb, h]
        num = sv[c] + a_inter[c] * qc[c]
        den = (jnp.sum(s[c], axis=1, keepdims=True)
               + a_inter[c] * jnp.sum(q[c] * n_prev, axis=1, keepdims=True))
        hh = num / jnp.maximum(jnp.abs(den), jnp.exp(-m_t[c]))
        c_ref[bb, h] = a_prev[c] * c_prev[c] + kv[c]
        n_ref[bb, h] = a_prev[c] * n_prev + jnp.sum(kw[c], axis=0, keepdims=True)
        m_vec[bb] = jnp.where(lane4 == h, m_new[c], m_vec[bb])
        hn = hh * lax.rsqrt(jnp.mean(hh * hh, axis=-1, keepdims=True) + EPS) * nw_ref[h:h + 1, :]
        gate = _sigmoid(o_ref[bb, :, h * MI_DV:(h + 1) * MI_DV])
        h_ref[bb, :, h * MI_DV:(h + 1) * MI_DV] = hn * gate
    for bb in bbs:
        m_ref[bb] = m_vec[bb]


def _mlstm_body(q_ref, k_ref, v_ref, o_ref, sc_ref, sr_ref, c0_ref, n0_ref, m0_ref, nw_ref,
                h_ref, c_ref, n_ref, m_ref, *, L, Bb):
    @pl.when(pl.program_id(1) == 0)
    def _():
        c_ref[...] = c0_ref[...]
        n_ref[...] = n0_ref[...]
        m_ref[...] = m0_ref[...]

    _mlstm_group(list(range(Bb)), q_ref, k_ref, v_ref, o_ref, sc_ref, sr_ref, nw_ref, h_ref, c_ref, n_ref, m_ref, L)


def mlstm(q, k, v, o, sc, sr, c0, n0, m0, norm_w, L, Bb):
    B, H, T, _ = q.shape
    nc = T // L
    hv = H * MI_DV
    bmap = lambda b, c: (b, 0, 0, 0)
    return pl.pallas_call(
        functools.partial(_mlstm_body, L=L, Bb=Bb),
        grid=(B // Bb, nc),
        in_specs=[pl.BlockSpec((Bb, H, L, MI_DQK), lambda b, c: (b, 0, c, 0)),
                  pl.BlockSpec((Bb, H, L, MI_DQK), lambda b, c: (b, 0, c, 0)),
                  pl.BlockSpec((Bb, L, hv), lambda b, c: (b, c, 0)),
                  pl.BlockSpec((Bb, L, hv), lambda b, c: (b, c, 0)),
                  pl.BlockSpec((Bb, L, LANE), lambda b, c: (b, c, 0)),
                  pl.BlockSpec((Bb, 1, 8, L), lambda b, c: (b, c, 0, 0)),
                  pl.BlockSpec((Bb, H, MI_DQK, MI_DV), bmap),
                  pl.BlockSpec((Bb, H, 1, MI_DQK), bmap),
                  pl.BlockSpec((Bb, 1, H), lambda b, c: (b, 0, 0)),
                  pl.BlockSpec((H, MI_DV), lambda b, c: (0, 0))],
        out_specs=[pl.BlockSpec((Bb, L, hv), lambda b, c: (b, c, 0)),
                   pl.BlockSpec((Bb, H, MI_DQK, MI_DV), bmap),
                   pl.BlockSpec((Bb, H, 1, MI_DQK), bmap),
                   pl.BlockSpec((Bb, 1, H), lambda b, c: (b, 0, 0))],
        out_shape=[jax.ShapeDtypeStruct((B, T, hv), F32),
                   jax.ShapeDtypeStruct((B, H, MI_DQK, MI_DV), F32),
                   jax.ShapeDtypeStruct((B, H, 1, MI_DQK), F32),
                   jax.ShapeDtypeStruct((B, 1, H), F32)],
        compiler_params=_cparams(("parallel", "arbitrary")),
        name="mlstm",
    )(q, k, v, o, sc, sr, c0, n0, m0, norm_w)


def _gdn_group(bbs, x_ref, z_ref, sc_ref, sr_ref, cw_ref, nw_ref, o_ref, s_ref, conv_ref, xp_scr, L):
    base = 8 - (GD_CONV - 1)
    nn, nt, tn = _chunk_dots(L)
    rows, cols = _tri(L)
    incl = rows >= cols
    strict = rows > cols
    eye = (rows == cols).astype(F32)
    tril = incl.astype(F32)
    triu = (rows <= cols).astype(F32)
    kw = GD_H * GD_DK
    ys, scs, gcols, grows = {}, {}, {}, {}
    for bb in bbs:
        xp_scr[bb, 8:8 + L, :] = x_ref[bb]
        y = xp_scr[bb, base:base + L, :] * cw_ref[0:1, :]
        for j in range(1, GD_CONV):
            y = y + xp_scr[bb, base + j:base + j + L, :] * cw_ref[j:j + 1, :]
        tail = xp_scr[bb, L + base:L + 8, :]
        xp_scr[bb, base:8, :] = tail
        conv_ref[bb] = tail
        ys[bb] = _silu(y)
        scs[bb] = sc_ref[bb]
        gcols[bb] = _dot_pick(tril, scs[bb][:, 12:16], exact_lhs=True)
        grows[bb] = _dot_pick(sr_ref[bb, 0][0:4, :], triu)
    chains = [(bb, h) for bb in bbs for h in range(GD_H)]
    q, k, vb, kb, dec, gcc, egc = {}, {}, {}, {}, {}, {}, {}
    for c in chains:
        bb, h = c
        y = ys[bb]
        qh = y[:, h * GD_DK:(h + 1) * GD_DK]
        kh = y[:, kw + h * GD_DK:kw + (h + 1) * GD_DK]
        vh = y[:, 2 * kw + h * GD_DV:2 * kw + (h + 1) * GD_DV]
        q[c] = qh * lax.rsqrt(jnp.sum(qh * qh, axis=-1, keepdims=True) + EPS) * (GD_DK ** -0.5)
        k[c] = kh * lax.rsqrt(jnp.sum(kh * kh, axis=-1, keepdims=True) + EPS)
        beta = scs[bb][:, 8 + h:9 + h]
        gcc[c] = gcols[bb][:, h:h + 1]
        dec[c] = jnp.exp(jnp.where(incl, gcc[c] - grows[bb][h:h + 1, :], NEG_BIG))
        egc[c] = jnp.exp(gcc[c])
        kb[c] = k[c] * beta
        vb[c] = vh * beta
    pw = {c: -(nt(kb[c], k[c]) * jnp.where(strict, dec[c], 0.0)) for c in chains}
    attn = {c: nt(q[c], k[c]) * dec[c] for c in chains}
    tinv = {c: eye + pw[c] for c in chains}
    for _ in range(int(math.log2(L)) - 1):
        pw = {c: _dot_split(pw[c], pw[c]) for c in chains}
        tinv = {c: tinv[c] + _dot_split(tinv[c], pw[c]) for c in chains}
    u = {c: _dot_split(tinv[c], vb[c]) for c in chains}
    w = {c: _dot_split(tinv[c], kb[c] * egc[c]) for c in chains}
    s_prev = {c: s_ref[c[0], c[1]] for c in chains}
    v_new = {c: u[c] - nn(w[c], s_prev[c]) for c in chains}
    o = {c: nn(q[c] * egc[c], s_prev[c]) + nn(attn[c], v_new[c]) for c in chains}
    for c in chains:
        bb, h = c
        g_last = gcc[c][L - 1:L, :]
        s_ref[bb, h] = jnp.exp(g_last) * s_prev[c] + tn(k[c] * jnp.exp(g_last - gcc[c]), v_new[c])
        on = o[c] * lax.rsqrt(jnp.mean(o[c] * o[c], axis=-1, keepdims=True) + EPS) * nw_ref[...]
        o_ref[bb, :, h * GD_DV:(h + 1) * GD_DV] = on * _silu(z_ref[bb, :, h * GD_DV:(h + 1) * GD_DV])


def _gdn_body(x_ref, z_ref, sc_ref, sr_ref, s0_ref, conv0_ref, cw_ref, nw_ref,
              o_ref, s_ref, conv_ref, xp_scr, *, L, Bb):
    @pl.when(pl.program_id(1) == 0)
    def _():
        s_ref[...] = s0_ref[...]
        xp_scr[:, 8 - (GD_CONV - 1):8, :] = conv0_ref[...]

    _gdn_group(list(range(Bb)), x_ref, z_ref, sc_ref, sr_ref, cw_ref, nw_ref, o_ref, s_ref, conv_ref, xp_scr, L)


def gdn(x, z, sc, sr, s0, conv0, conv_w, norm_w, L, Bb):
    B, T, ch = x.shape
    H = GD_H
    nc = T // L
    hv = H * GD_DV
    bmap = lambda b, c: (b, 0, 0, 0)
    return pl.pallas_call(
        functools.partial(_gdn_body, L=L, Bb=Bb),
        grid=(B // Bb, nc),
        in_specs=[pl.BlockSpec((Bb, L, ch), lambda b, c: (b, c, 0)),
                  pl.BlockSpec((Bb, L, hv), lambda b, c: (b, c, 0)),
                  pl.BlockSpec((Bb, L, LANE), lambda b, c: (b, c, 0)),
                  pl.BlockSpec((Bb, 1, 8, L), lambda b, c: (b, c, 0, 0)),
                  pl.BlockSpec((Bb, H, GD_DK, GD_DV), bmap),
                  pl.BlockSpec((Bb, GD_CONV - 1, ch), lambda b, c: (b, 0, 0)),
                  pl.BlockSpec((GD_CONV, ch), lambda b, c: (0, 0)),
                  pl.BlockSpec((1, GD_DV), lambda b, c: (0, 0))],
        out_specs=[pl.BlockSpec((Bb, L, hv), lambda b, c: (b, c, 0)),
                   pl.BlockSpec((Bb, H, GD_DK, GD_DV), bmap),
                   pl.BlockSpec((Bb, GD_CONV - 1, ch), lambda b, c: (b, 0, 0))],
        out_shape=[jax.ShapeDtypeStruct((B, T, hv), F32),
                   jax.ShapeDtypeStruct((B, H, GD_DK, GD_DV), F32),
                   jax.ShapeDtypeStruct((B, GD_CONV - 1, ch), F32)],
        scratch_shapes=[pltpu.VMEM((Bb, L + 8, ch), F32)],
        compiler_params=_cparams(("parallel", "arbitrary")),
        name="gdn",
    )(x, z, sc, sr, s0, conv0, conv_w, norm_w.reshape(1, GD_DV))


def _compress_body(x_ref, pos_ref, w1_ref, w2_ref, o_ref):
    Bb, _, R, half = x_ref.shape
    x = x_ref[:, 0].reshape(Bb * R, half).astype(F32)
    ua = _dot((x + pos_ref[0, 0:1, :]).astype(BF16), w1_ref[0, 0])
    ub = _dot((x + pos_ref[0, 1:2, :]).astype(BF16), w1_ref[0, 1])
    h = _silu(ua + pltpu.roll(ub, Bb * R - 1, 0))
    o_ref[:, 0, 0:R, :] = _dot(h.astype(BF16), w2_ref[0]).reshape(Bb, R, HD)
    rp = o_ref.shape[2]
    if rp > R:
        o_ref[:, 0, R:rp, :] = jnp.zeros((Bb, rp - R, HD), F32)


def nsa_compress(xr, R, pos, w1, w2, Bb):
    B = xr.shape[0]
    half = CMP_STRIDE * HD
    rp = -(-R // LANE) * LANE
    return pl.pallas_call(
        _compress_body,
        grid=(4, B // Bb),
        in_specs=[pl.BlockSpec((Bb, 1, R, half), lambda c, b: (b, c, 0, 0)),
                  pl.BlockSpec((1, 2, half), lambda c, b: (c // 2, 0, 0)),
                  pl.BlockSpec((1, 2, half, CMP_HIDDEN), lambda c, b: (c // 2, 0, 0, 0)),
                  pl.BlockSpec((1, CMP_HIDDEN, HD), lambda c, b: (c // 2, 0, 0))],
        out_specs=pl.BlockSpec((Bb, 1, rp, HD), lambda c, b: (b, c, 0, 0)),
        out_shape=jax.ShapeDtypeStruct((B, 4, rp, HD), F32),
        compiler_params=_cparams(("parallel", "parallel")),
        name="nsa_compress",
    )(xr, pos.reshape(2, 2, half), w1.reshape(2, 2, half, CMP_HIDDEN).astype(BF16), w2.astype(BF16))


def _cumsum_body(x_ref, o_ref):
    Bb, H, T = x_ref.shape
    rows, cols = _tri(LANE)
    triu = (rows <= cols).astype(F32)
    carry = jnp.zeros((Bb * H, 1), F32)
    for c in range(T // LANE):
        seg = x_ref[:, :, c * LANE:(c + 1) * LANE].reshape(Bb * H, LANE)
        loc = _dot_pick(seg, triu) + carry
        o_ref[:, :, c * LANE:(c + 1) * LANE] = loc.reshape(Bb, H, LANE)
        carry = loc[:, LANE - 1:LANE]


def cumsum_lanes(x, Bb):
    B, H, T = x.shape
    return pl.pallas_call(
        _cumsum_body,
        grid=(B // Bb,),
        in_specs=[pl.BlockSpec((Bb, H, T), lambda b: (b, 0, 0))],
        out_specs=pl.BlockSpec((Bb, H, T), lambda b: (b, 0, 0)),
        out_shape=jax.ShapeDtypeStruct((B, H, T), F32),
        compiler_params=_cparams(("parallel",)),
        name="cumsum",
    )(x)


def _flash_tile(s_blocks, v, m_scr, l_scr, acc_scr):
    dv = acc_scr.shape[-1]
    m_prev = m_scr[...]
    mx = s_blocks[0]
    for sb in s_blocks[1:]:
        mx = jnp.maximum(mx, sb)
    m_new = jnp.maximum(m_prev, jnp.max(mx, axis=1, keepdims=True))
    p_blocks = [jnp.exp(sb - m_new) for sb in s_blocks]
    sm = p_blocks[0]
    for pb in p_blocks[1:]:
        sm = sm + pb
    alpha = jnp.exp(m_prev - m_new)
    l_scr[...] = alpha * l_scr[...] + jnp.sum(sm, axis=1, keepdims=True)
    p = (jnp.concatenate(p_blocks, axis=1) if len(p_blocks) > 1 else p_blocks[0]).astype(BF16)
    acc_scr[...] = alpha[:, :dv] * acc_scr[...] + _dot(p, v)
    m_scr[...] = m_new


def _flash_reset(m_scr, l_scr, acc_scr):
    m_scr[...] = jnp.full_like(m_scr, NEG_BIG)
    l_scr[...] = jnp.zeros_like(l_scr)
    acc_scr[...] = jnp.zeros_like(acc_scr)


def _lane_blocks(s):
    return [s[:, i * LANE:(i + 1) * LANE] for i in range(s.shape[1] // LANE)]


def _fox_body(q_ref, k_ref, v_ref, fq_ref, fk_ref, o_ref, m_scr, l_scr, acc_scr, *, tq, tk):
    qi = pl.program_id(1)
    j = pl.program_id(2)
    top = (qi * tq + tq - 1) // tk

    @pl.when(j == 0)
    def _():
        _flash_reset(m_scr, l_scr, acc_scr)

    def tile(diag):
        fq = fq_ref[0]
        fk = fk_ref[0]
        if diag:
            mask = (top * tk + _iota((tq, tk), 1)) <= (qi * tq + _iota((tq, tk), 0))
        qk = [_dot_nt(q_ref[0, h], k_ref[0, h]) for h in range(FOX_H)]
        for h in range(FOX_H):
            s = qk[h] + fq[:, h:h + 1] - fk[h:h + 1, :]
            if diag:
                s = jnp.where(mask, s, NEG_BIG)
            _flash_tile(_lane_blocks(s), v_ref[0, h], m_scr.at[h], l_scr.at[h], acc_scr.at[h])

    @pl.when(j == 0)
    def _():
        tile(True)

    @pl.when(jnp.logical_and(j > 0, j <= top))
    def _():
        tile(False)

    @pl.when(j == pl.num_programs(2) - 1)
    def _():
        for h in range(FOX_H):
            o_ref[0, h] = acc_scr[h] / jnp.maximum(l_scr[h][:, :HD], 1e-30)


def fox_prompt(q, kv, fq, fk, tq, tk):
    B, H, T, _ = q.shape
    kmap = lambda i, j: jnp.maximum((i * tq + tq - 1) // tk - j, 0)
    return pl.pallas_call(
        functools.partial(_fox_body, tq=tq, tk=tk),
        grid=(B, T // tq, T // tk),
        in_specs=[pl.BlockSpec((1, H, tq, HD), lambda b, i, j: (b, 0, i, 0)),
                  pl.BlockSpec((1, H, tk, HD), lambda b, i, j: (b, 0, kmap(i, j), 0)),
                  pl.BlockSpec((1, H, tk, HD), lambda b, i, j: (b, 1, kmap(i, j), 0)),
                  pl.BlockSpec((1, tq, H), lambda b, i, j: (b, i, 0)),
                  pl.BlockSpec((1, H, tk), lambda b, i, j: (b, 0, kmap(i, j)))],
        out_specs=pl.BlockSpec((1, H, tq, HD), lambda b, i, j: (b, 0, i, 0)),
        out_shape=jax.ShapeDtypeStruct((B, H, T, HD), F32),
        scratch_shapes=[pltpu.VMEM((H, tq, LANE), F32), pltpu.VMEM((H, tq, LANE), F32), pltpu.VMEM((H, tq, HD), F32)],
        compiler_params=_cparams(("parallel", "parallel", "arbitrary")),
        name="fox_prompt",
    )(q, kv, kv, fq, fk)


def _fox_decode_body(pt_ref, q_ref, newkv_ref, newlf_ref, *refs, n_pages, tn):
    kv_refs = refs[:n_pages]
    lf_refs = refs[n_pages:2 * n_pages]
    o_ref = refs[2 * n_pages]
    hw = FOX_H * HD
    R = FOX_H * tn
    q = q_ref[0]
    qrep = jnp.concatenate([q] * FOX_H, axis=0)
    blockmask = (_iota((R, hw), 0) // tn) == (_iota((R, hw), 1) // HD)
    qbd = jnp.where(blockmask, qrep, 0.0).astype(BF16)
    rows, cols = _tri(PAGE)
    triu = (rows <= cols).astype(F32)
    ones = jnp.ones((8, PAGE), F32)
    expand = lambda a: jnp.concatenate([jnp.broadcast_to(a[h:h + 1, :], (tn, a.shape[1])) for h in range(FOX_H)], 0)
    carry_c = jnp.zeros((FOX_H, 1), F32)
    carry_r = jnp.zeros((1, FOX_H), F32)
    s_tiles = []
    for pg in range(n_pages):
        lf = lf_refs[pg][0]
        f_t = _dot_pick(lf, triu) + carry_c
        carry_c = f_t[:, PAGE - 1:PAGE]
        carry_r = carry_r + _dot_pick(ones, lf, _dot_nt, exact_lhs=True)[0:1, :]
        s_tiles.append(_dot(qbd, kv_refs[pg][0, 0:hw, :].astype(BF16)) - expand(f_t))
    lfn = newlf_ref[0]
    r8, c8 = _tri(tn)
    fq_c = _dot_pick((r8 >= c8).astype(F32), lfn, exact_lhs=True) + carry_r
    fq_t = _dot_pick(lfn, (r8 <= c8).astype(F32), _dot_tn) + carry_c
    fq_rows = jnp.concatenate([fq_c[:, h:h + 1] for h in range(FOX_H)], axis=0)
    kn = newkv_ref[0, :, 0:hw]
    s_new = _dot_split(qbd.astype(F32), kn, _dot_nt) - expand(fq_t)
    causal = _iota((R, tn), 1) <= (_iota((R, tn), 0) % tn)
    s_new = jnp.where(causal, s_new + fq_rows, NEG_BIG)
    s_tiles = [s + fq_rows for s in s_tiles]
    m = jnp.max(s_new, axis=1, keepdims=True)
    for s in s_tiles:
        m = jnp.maximum(m, jnp.max(s, axis=1, keepdims=True))
    p_new = jnp.where(causal, jnp.exp(s_new - m), 0.0)
    l = jnp.sum(p_new, axis=1, keepdims=True)
    acc = _dot_split(p_new, newkv_ref[0, :, hw:2 * hw])
    for pg, s in enumerate(s_tiles):
        p = jnp.exp(s - m)
        l = l + jnp.sum(p, axis=1, keepdims=True)
        acc = acc + _dot_nt(p.astype(BF16), kv_refs[pg][0, hw:2 * hw, :].astype(BF16))
    acc = acc / jnp.maximum(l, 1e-30)
    o_ref[0] = jnp.concatenate([acc[h * tn:(h + 1) * tn, h * HD:(h + 1) * HD] for h in range(FOX_H)], axis=1)


def fox_decode(page_table, q, newkv, newlf, kv_pool_t, lf_pool_t):
    B, tn, hw = q.shape
    n_pages = page_table.shape[1]
    page_spec = lambda rows, pg: pl.BlockSpec((1, rows, PAGE), lambda b, pt: (pt[b, pg], 0, 0))
    grid_spec = pltpu.PrefetchScalarGridSpec(
        num_scalar_prefetch=1,
        grid=(B,),
        in_specs=[pl.BlockSpec((1, tn, hw), lambda b, pt: (b, 0, 0)),
                  pl.BlockSpec((1, tn, 2 * hw), lambda b, pt: (b, 0, 0)),
                  pl.BlockSpec((1, tn, FOX_H), lambda b, pt: (b, 0, 0))]
                 + [page_spec(2 * hw, pg) for pg in range(n_pages)]
                 + [page_spec(FOX_H, pg) for pg in range(n_pages)],
        out_specs=pl.BlockSpec((1, tn, hw), lambda b, pt: (b, 0, 0)),
    )
    return pl.pallas_call(
        functools.partial(_fox_decode_body, n_pages=n_pages, tn=tn),
        grid_spec=grid_spec,
        out_shape=jax.ShapeDtypeStruct((B, tn, hw), F32),
        compiler_params=_cparams(("arbitrary",)),
        name="fox_decode",
    )(page_table, q, newkv, newlf, *([kv_pool_t] * n_pages), *([lf_pool_t] * n_pages))


def _t5_bucket(dist):
    n = jnp.maximum(dist, 0)
    nf = jnp.maximum(n, 1).astype(F32)
    large = BUCKET_EXACT + (jnp.log(nf / BUCKET_EXACT) / math.log(MAX_DISTANCE / BUCKET_EXACT)
                            * (N_BUCKETS - BUCKET_EXACT)).astype(jnp.int32)
    return jnp.where(n < BUCKET_EXACT, n, jnp.minimum(large, N_BUCKETS - 1))


def _bias_from_bucket(bucket, tbl_ref, head):
    out = jnp.zeros(bucket.shape, F32)
    for kk in range(N_BUCKETS):
        out = jnp.where(bucket == kk, tbl_ref[kk, head], out)
    return out


FOX_TQ, FOX_TK = 256, 512
RECURRENT_BATCH = 4
NSA_TK = 2 * LANE
NSA_FAR_GROUP = 4
CMP_PAT_CENTER = LANE // 2
NSA_BIAS_TILES = (BUCKET_SAT_DIST + NSA_TK + LANE - 1) // LANE


def _nsa_body(tbl_ref, q_ref, gate_ref, kc_ref, vc_ref, ks_ref, vs_ref, kw_ref, vw_ref, smap_ref, o_ref,
              bias_scr, pat_scr, score_scr, m_scr, l_scr, acc_scr, *, tq, q_pos0, win_pos0, n_sel, ncp, tw, single):
    g = pl.program_id(1)
    qi = pl.program_id(2)
    q0 = q_pos0 if single else q_pos0 + qi * tq
    aligned = (lambda x, m: x) if single else pl.multiple_of
    lo, hi = (max, min) if single else (jnp.maximum, jnp.minimum)
    R = NSA_R * tq
    scale = HD ** -0.5
    last_bias = tuple(tbl_ref[N_BUCKETS - 1, g * NSA_R + r] for r in range(NSA_R))

    @pl.when(qi == 0)
    def _():
        ii = _iota((tq, NSA_TK), 0)
        jj = _iota((tq, NSA_TK), 1)
        for dd in range(NSA_BIAS_TILES):
            bucket = _t5_bucket(ii - jj + dd * LANE)
            for r in range(NSA_R):
                bias_scr[dd, r * tq:(r + 1) * tq, :] = (
                    _bias_from_bucket(bucket, tbl_ref, g * NSA_R + r) - last_bias[r])
        bias_scr[NSA_BIAS_TILES] = jnp.zeros((R, NSA_TK), F32)
        rel_blk = _iota((tq, LANE), 1) - CMP_PAT_CENTER
        bucket = _t5_bucket(_iota((tq, LANE), 0) - rel_blk * CMP_STRIDE - (CMP_BLOCK - 1))
        for r in range(NSA_R):
            pat_scr[r] = _bias_from_bucket(bucket, tbl_ref, g * NSA_R + r) - last_bias[r]

    t_col = q0 + _iota((tq, 1), 0)

    bias_tiles, mask_tiles = [], []
    for nt in range(ncp // LANE):
        c_end = (nt * LANE + _iota((tq, LANE), 1)) * CMP_STRIDE + (CMP_BLOCK - 1)
        dist = t_col - c_end
        max_dist = q0 + tq - 1 - (nt * LANE * CMP_STRIDE + CMP_BLOCK - 1)
        min_dist = q0 - ((nt * LANE + LANE - 1) * CMP_STRIDE + CMP_BLOCK - 1)
        special = jnp.logical_and(max_dist >= 0, min_dist < BUCKET_SAT_DIST)

        off = nt * LANE - q0 // CMP_STRIDE + CMP_PAT_CENTER

        def general(off=off):
            lanes = _iota((tq, LANE), 1) + off
            inside = jnp.logical_and(lanes >= 0, lanes < LANE)
            shift = (LANE - off % LANE) % LANE
            return jnp.stack([jnp.where(inside, pltpu.roll(pat_scr[r], shift, 1), 0.0) for r in range(NSA_R)])

        def saturated():
            return jnp.zeros((NSA_R, tq, LANE), F32)

        bias_tiles.append(lax.cond(special, general, saturated))
        mask_tiles.append(dist >= 0)
    mask_c = jnp.concatenate(mask_tiles, axis=1) if len(mask_tiles) > 1 else mask_tiles[0]
    kc = kc_ref[0, 0].astype(BF16)
    vc = vc_ref[0, 0].astype(BF16)
    pcsum = jnp.zeros((tq, ncp), F32)
    qk_c = [_dot_nt((q_ref[0, r] * scale).astype(BF16), kc) for r in range(NSA_R)]
    pcs = []
    for r in range(NSA_R):
        bias_r = jnp.concatenate([b[r] for b in bias_tiles], axis=1) if len(bias_tiles) > 1 else bias_tiles[0][r]
        s = jnp.where(mask_c, qk_c[r] + bias_r, NEG_BIG)
        m = jnp.max(s, axis=1, keepdims=True)
        p = jnp.where(mask_c, jnp.exp(s - m), 0.0)
        pc = p / jnp.maximum(jnp.sum(p, axis=1, keepdims=True), 1e-30)
        pcs.append(pc.astype(BF16))
        pcsum = pcsum + pc
    o_c = [_dot(pc, vc) for pc in pcs]

    ps_t = _dot_pick(smap_ref[...], pcsum, _dot_nt, exact_lhs=True)
    j_col = _iota((LANE, 1), 0)
    t_row = q0 + _iota((1, tq), 1)
    cur = lax.shift_right_logical(t_row, int(math.log2(SEL_BLOCK)))
    score = jnp.where(j_col * SEL_BLOCK <= t_row, ps_t, -1.0)
    score = jnp.where(j_col == cur - 1, FORCE_SCORE, score)
    score = jnp.where(j_col == cur, FORCE_SCORE, score)
    score = jnp.where(j_col == 0, FORCE_SCORE, score)
    score = jnp.where(j_col < n_sel, score, -3e38)
    score_scr[...] = score

    def rank_body(jp, rank):
        row = score_scr[pl.ds(jp, 1), :]
        tie = jnp.where(j_col > jp, 1.0, 0.0)
        return rank + jnp.where(row > score, 1.0, jnp.where(row == score, tie, 0.0))

    n_rank = jnp.minimum((q0 + tq - 1) // SEL_BLOCK + 1, LANE)
    rank = lax.fori_loop(0, n_rank, rank_body, jnp.zeros((LANE, tq), F32))
    sel_t = jnp.where(rank < SEL_TOPN, 1.0, 0.0).astype(BF16)
    eye = (_iota((tq, tq), 0) == _iota((tq, tq), 1)).astype(BF16)
    sel = _dot_nt(eye, sel_t).astype(BF16)

    qs = (q_ref[0].reshape(R, HD) * scale).astype(BF16)
    t_tile = q0 + _iota((tq, NSA_TK), 0)
    c_tile = _iota((tq, NSA_TK), 1)
    log2_blk = int(math.log2(SEL_BLOCK))

    def stack(a):
        return jnp.concatenate([a] * NSA_R, axis=0)

    def bias_tile(offset):
        return bias_scr[hi(offset // LANE, NSA_BIAS_TILES)]

    def sel_tile(k0, width, near):
        k = ks_ref[0, 0, pl.ds(k0, width), :]
        v = vs_ref[0, 0, pl.ds(k0, width), :]
        blk = lax.shift_right_logical(k0 + _iota((LANE, width), 1), log2_blk)
        expand = jnp.where(_iota((LANE, width), 0) == blk, 1.0, 0.0).astype(BF16)
        chosen = _dot(sel, expand)
        s = _dot_nt(qs, k)
        if near:
            chosen = jnp.where(t_tile - (k0 + c_tile) >= 0, chosen, 0.0)
            s = s + bias_tile(q0 - k0)
        s = jnp.where(stack(chosen) > 0.5, s, NEG_BIG)
        _flash_tile(_lane_blocks(s), v, m_scr, l_scr, acc_scr)

    _flash_reset(m_scr, l_scr, acc_scr)
    kt_top = (q0 + tq - 1) // NSA_TK
    sel_tile(aligned(kt_top * NSA_TK, NSA_TK), NSA_TK, True)
    if single:
        if kt_top >= 1:
            sel_tile((kt_top - 1) * NSA_TK, NSA_TK, True)
    else:
        @pl.when(kt_top >= 1)
        def _():
            sel_tile(pl.multiple_of((kt_top - 1) * NSA_TK, NSA_TK), NSA_TK, True)

    n_far = lo(kt_top - 1, 0)
    n_groups = n_far // NSA_FAR_GROUP

    def sel_far_group(gi, carry):
        sel_tile(pl.multiple_of(gi * (NSA_FAR_GROUP * NSA_TK), NSA_FAR_GROUP * NSA_TK), NSA_FAR_GROUP * NSA_TK, False)
        return carry

    def sel_far(kt, carry):
        sel_tile(pl.multiple_of(kt * NSA_TK, NSA_TK), NSA_TK, False)
        return carry

    if single:
        if n_far:
            sel_tile(0, n_far * NSA_TK, False)
    else:
        lax.fori_loop(0, n_groups, sel_far_group, 0)
        lax.fori_loop(n_groups * NSA_FAR_GROUP, n_far, sel_far, 0)
    o_s = acc_scr[...] / jnp.maximum(l_scr[...][:, :HD], 1e-30)

    def win_tile(i, carry):
        k0 = aligned((wt_top - i) * NSA_TK, NSA_TK)
        k = kw_ref[0, 0, pl.ds(k0, NSA_TK), :]
        v = vw_ref[0, 0, pl.ds(k0, NSA_TK), :]
        dist = t_tile - (win_pos0 + k0 + c_tile)
        ok = jnp.where(dist >= 0, jnp.where(dist < WINDOW, 1.0, 0.0), 0.0)
        s = _dot_nt(qs, k) + bias_tile(q0 - win_pos0 - k0)
        s = jnp.where(stack(ok) > 0.5, s, NEG_BIG)
        _flash_tile(_lane_blocks(s), v, m_scr, l_scr, acc_scr)
        return carry

    _flash_reset(m_scr, l_scr, acc_scr)
    wt_top = (hi(q0 + tq - win_pos0, tw) - 1) // NSA_TK
    wt_lo = lo(q0 - (WINDOW - 1) - win_pos0, 0) // NSA_TK
    if single:
        for i in range(wt_top - wt_lo + 1):
            win_tile(i, 0)
    else:
        lax.fori_loop(0, wt_top - wt_lo + 1, win_tile, 0)
    o_w = acc_scr[...] / jnp.maximum(l_scr[...][:, :HD], 1e-30)

    gates = gate_ref[0, 0]
    for r in range(NSA_R):
        o_ref[0, r] = (gates[:, 3 * r:3 * r + 1] * o_c[r]
                       + gates[:, 3 * r + 1:3 * r + 2] * o_s[r * tq:(r + 1) * tq]
                       + gates[:, 3 * r + 2:3 * r + 3] * o_w[r * tq:(r + 1) * tq])


def _selection_overlap_t(ncp):
    c_start = np.arange(ncp)[None, :] * CMP_STRIDE
    s_start = np.arange(LANE)[:, None] * SEL_BLOCK
    return ((c_start < s_start + SEL_BLOCK) & (c_start + CMP_BLOCK > s_start)).astype(np.float32)


def nsa_attend(tbl, q, gates, kcvc, sel_arr, sel_off, win_arr, win_off, *, tq, q_pos0, win_pos0, n_sel):
    B, _, Tq, _ = q.shape
    ncp = kcvc.shape[2]
    tks = sel_arr.shape[2]
    tw = win_arr.shape[2]
    R = NSA_R * tq
    smap = jnp.asarray(_selection_overlap_t(ncp))
    kv_spec = lambda rows, off: pl.BlockSpec((1, 1, rows, HD), lambda b, g, i: (b, off + g, 0, 0))
    return pl.pallas_call(
        functools.partial(_nsa_body, tq=tq, q_pos0=q_pos0, win_pos0=win_pos0, n_sel=n_sel, ncp=ncp, tw=tw,
                          single=(Tq == tq)),
        grid=(B, NSA_G, Tq // tq),
        in_specs=[pl.BlockSpec(memory_space=pltpu.SMEM),
                  pl.BlockSpec((1, NSA_R, tq, HD), lambda b, g, i: (b, g, i, 0)),
                  pl.BlockSpec((1, 1, tq, 3 * NSA_R), lambda b, g, i: (b, g, i, 0)),
                  kv_spec(ncp, 0), kv_spec(ncp, 2),
                  kv_spec(tks, sel_off), kv_spec(tks, sel_off + 2),
                  kv_spec(tw, win_off), kv_spec(tw, win_off + 2),
                  pl.BlockSpec((LANE, ncp), lambda b, g, i: (0, 0))],
        out_specs=pl.BlockSpec((1, NSA_R, tq, HD), lambda b, g, i: (b, g, i, 0)),
        out_shape=jax.ShapeDtypeStruct((B, NSA_G * NSA_R, Tq, HD), F32),
        scratch_shapes=[pltpu.VMEM((NSA_BIAS_TILES + 1, R, NSA_TK), F32), pltpu.VMEM((NSA_R, tq, LANE), F32),
                        pltpu.VMEM((LANE, tq), F32),
                        pltpu.VMEM((R, LANE), F32), pltpu.VMEM((R, LANE), F32), pltpu.VMEM((R, HD), F32)],
        compiler_params=_cparams(("parallel", "parallel", "arbitrary")),
        name="nsa_attend",
    )(tbl, q, gates, kcvc, kcvc, sel_arr, sel_arr, win_arr, win_arr, smap)


def _nsa_gather_body(pt_ref, new_ref, *refs, n_pages):
    pages = refs[:n_pages]
    cmp_ref, sel_ref, xt_scr = refs[n_pages:]
    ngrp = 2 * NSA_G
    half = ngrp * HD
    rows16 = PAGE // CMP_STRIDE
    for pg in range(n_pages):
        xt = pages[pg][0].T
        for j in range(ngrp):
            sel_ref[0, j, pg * PAGE:(pg + 1) * PAGE, :] = xt[:, half + j * HD:half + (j + 1) * HD].astype(BF16)
        for slab in range(half // LANE):
            xt_scr[slab] = xt[:, slab * LANE:(slab + 1) * LANE]
            for p in range(CMP_STRIDE):
                rows = xt_scr[slab, pl.ds(p, rows16, stride=CMP_STRIDE), :]
                for jj in range(LANE // HD):
                    cmp_ref[0, slab * (LANE // HD) + jj, pg * rows16:(pg + 1) * rows16, p * HD:(p + 1) * HD] = (
                        rows[:, jj * HD:(jj + 1) * HD])
    tn = new_ref.shape[1]
    tail = sel_ref.shape[2] - n_pages * PAGE
    for j in range(ngrp):
        new = new_ref[0, :, (ngrp + j) * HD:(ngrp + j + 1) * HD]
        tile = jnp.concatenate([new, jnp.zeros((tail - tn, HD), F32)], axis=0)
        sel_ref[0, j, n_pages * PAGE:, :] = tile.astype(BF16)


def nsa_gather(page_table, new_rows, pool_t):
    B, tn, width = new_rows.shape
    n_pages = page_table.shape[1]
    ngrp = 2 * NSA_G
    sel_rows = -(-(n_pages * PAGE + tn) // NSA_TK) * NSA_TK
    cmp_rows = n_pages * PAGE // CMP_STRIDE
    grid_spec = pltpu.PrefetchScalarGridSpec(
        num_scalar_prefetch=1,
        grid=(B,),
        in_specs=[pl.BlockSpec((1, tn, width), lambda b, pt: (b, 0, 0))]
                 + [pl.BlockSpec((1, 2 * ngrp * HD, PAGE),
                                 functools.partial(lambda b, pt, pg: (pt[b, pg], 0, 0), pg=pg))
                    for pg in range(n_pages)],
        out_specs=[pl.BlockSpec((1, ngrp, cmp_rows, CMP_STRIDE * HD), lambda b, pt: (b, 0, 0, 0)),
                   pl.BlockSpec((1, ngrp, sel_rows, HD), lambda b, pt: (b, 0, 0, 0))],
        scratch_shapes=[pltpu.VMEM((ngrp * HD // LANE, PAGE, LANE), F32)],
    )
    return pl.pallas_call(
        functools.partial(_nsa_gather_body, n_pages=n_pages),
        grid_spec=grid_spec,
        out_shape=[jax.ShapeDtypeStruct((B, ngrp, cmp_rows, CMP_STRIDE * HD), F32),
                   jax.ShapeDtypeStruct((B, ngrp, sel_rows, HD), BF16)],
        compiler_params=_cparams(("arbitrary",)),
        name="nsa_gather",
    )(page_table, new_rows, *([pool_t] * n_pages))


def _row_tile(m):
    return 512 if m % 512 == 0 else m


def _small_params(entries):
    sp = jnp.zeros((8, LANE), F32)
    for off, bias, act, log_scale in entries:
        n = bias.shape[0]
        sp = sp.at[0, off:off + n].set(bias.astype(F32))
        sp = sp.at[1, off:off + n].set(act)
        if log_scale is not None:
            sp = sp.at[2, off:off + n].set(log_scale.astype(F32))
    return sp


EVEN_WIDTHS = (256, 256, 512, 512, GD_CH, 512, LANE)


def _even_weights(w_in):
    s = np.cumsum((0, 256, 256, 512, 512, 4, 4, 512, 512, 512, 512, 4, 4))
    col = lambda i: w_in[:, s[i]:s[i + 1]]
    small = jnp.concatenate([col(4), col(5), col(10), col(11)], axis=1)
    small = jnp.pad(small, ((0, 0), (0, LANE - small.shape[1])))
    return jnp.concatenate([col(0), col(1), col(2), col(3), col(6), col(7), col(8), col(9), small], axis=1).astype(BF16)


def _chunk_rows(small, B, T, L, lanes):
    r = small.reshape(B, T // L, L, LANE)[..., lanes[0]:lanes[1]]
    r = jnp.swapaxes(r, 2, 3)
    return jnp.pad(r, ((0, 0), (0, 0), (0, 8 - r.shape[2]), (0, 0)))


def even_layer(x, p, past, L, Bb):
    B, T, D = x.shape
    M = B * T
    tm = _row_tile(M)
    sp = _small_params([(0, p['mi_b_i'], ACT_ID, None), (4, p['mi_b_f'], ACT_LOGSIG, None),
                        (8, jnp.zeros((4,), F32), ACT_SIG, None), (12, p['gd_dt_bias'], ACT_DECAY, p['gd_a_log'])])
    mq, mk, mv, mo, gx, gz, small = norm_proj(x.reshape(M, D), p['norm_mix'], _even_weights(p['w_in']), sp,
                                              EVEN_WIDTHS, tm)
    heads = lambda a: jnp.transpose(a.reshape(B, T, MI_H, MI_DQK), (0, 2, 1, 3))
    sc = small.reshape(B, T, LANE)
    if past is None:
        c0 = jnp.zeros((B, MI_H, MI_DQK, MI_DV), F32)
        n0 = jnp.zeros((B, MI_H, 1, MI_DQK), F32)
        m0 = jnp.zeros((B, 1, MI_H), F32)
        s0 = jnp.zeros((B, GD_H, GD_DK, GD_DV), F32)
        conv0 = jnp.zeros((B, GD_CONV - 1, GD_CH), F32)
    else:
        c0, n0, m0, s0, conv0 = past
        n0 = n0.reshape(B, MI_H, 1, MI_DQK)
        m0 = m0.reshape(B, 1, MI_H)
    hm, c1, n1, m1 = mlstm(heads(mq), heads(mk), mv.reshape(B, T, -1), mo.reshape(B, T, -1), sc,
                           _chunk_rows(small, B, T, L, (0, 8)), c0, n0, m0,
                           p['mi_norm'].reshape(MI_H, MI_DV), L, Bb)
    og, s1, conv1 = gdn(gx.reshape(B, T, GD_CH), gz.reshape(B, T, -1), sc,
                        _chunk_rows(small, B, T, L, (12, 16)), s0, conv0, p['gd_conv_w'], p['gd_norm'], L, Bb)
    y = out_proj_residual(x.reshape(M, D), hm.reshape(M, -1), og.reshape(M, -1), p['w_out'].astype(BF16), tm)
    return y.reshape(B, T, D), (c1, n1.reshape(B, MI_H, MI_DQK), m1.reshape(B, MI_H), s1, conv1)


NSA_QW = NSA_G * NSA_R * HD
NSA_KVW = 6 * NSA_G * HD
NSA_CACHE_W = 4 * NSA_G * HD
FOX_W = FOX_H * HD
N_GATE = 3 * NSA_G * NSA_R
ODD_WIDTHS = (NSA_QW, NSA_KVW, FOX_W, 2 * FOX_W, LANE)


def _odd_weights(w_in):
    s = np.cumsum((0, NSA_QW, NSA_KVW, N_GATE, FOX_W, FOX_W, FOX_W, FOX_H))
    col = lambda i: w_in[:, s[i]:s[i + 1]]
    small = jnp.concatenate([col(2), col(6)], axis=1)
    small = jnp.pad(small, ((0, 0), (0, LANE - small.shape[1])))
    return jnp.concatenate([col(0), col(1), col(3), col(4), col(5), small], axis=1).astype(BF16)


def _heads(a, B, T, n):
    return jnp.transpose(a.reshape(B, T, n, HD), (0, 2, 1, 3))


def _unheads(a):
    B, n, T, _ = a.shape
    return jnp.transpose(a, (0, 2, 1, 3)).reshape(B * T, n * HD)


def odd_layer(x, p, rel_bias, w_buf, past, page_table):
    B, T, D = x.shape
    M = B * T
    tm = _row_tile(M)
    sp = _small_params([(0, jnp.zeros((N_GATE,), F32), ACT_SIG, None), (N_GATE, p['fox_b_f'], ACT_LOGSIG, None)])
    nq, nkv, fq, fkv, small = norm_proj(x.reshape(M, D), p['norm_mix'], _odd_weights(p['w_in']), sp, ODD_WIDTHS, tm)
    new_nsa = nkv[:, :NSA_CACHE_W].reshape(B, T, 4, NSA_G, HD)
    new_win = nkv[:, NSA_CACHE_W:].reshape(B, T, 2, NSA_G, HD)
    new_fox = fkv.reshape(B, T, 2, FOX_H, HD)
    logf = small[:, N_GATE:N_GATE + FOX_H].reshape(B, T, FOX_H)
    q_heads = _heads(nq, B, T, NSA_G * NSA_R)
    gates = jnp.transpose(small[:, :N_GATE].reshape(B, T, NSA_G, 3 * NSA_R), (0, 2, 1, 3))
    cmp_args = (p['nsa_cmp_pos'], p['nsa_cmp_w1'], p['nsa_cmp_w2'])
    rows16 = CMP_STRIDE * HD
    if past is None:
        groups = _heads(nkv, B, T, 6 * NSA_G)
        kcvc = nsa_compress(groups.reshape(B, 6 * NSA_G, T // CMP_STRIDE, rows16), T // CMP_STRIDE, *cmp_args, 1)
        arr = groups.astype(BF16)
        o_n = nsa_attend(rel_bias, q_heads, gates, kcvc, arr, 2 * NSA_G, arr, 4 * NSA_G,
                         tq=min(T, LANE), q_pos0=0, win_pos0=0, n_sel=-(-T // SEL_BLOCK))
        f_t = cumsum_lanes(jnp.transpose(logf, (0, 2, 1)), B)
        o_f = fox_prompt(_heads(fq * HD ** -0.5, B, T, FOX_H).astype(BF16), _heads(fkv, B, T, 2 * FOX_H).astype(BF16),
                         jnp.transpose(f_t, (0, 2, 1)), f_t, min(T, FOX_TQ), min(T, FOX_TK))
        o_f = _unheads(o_f)
        win_prev = jnp.zeros((B, WINDOW, 2, NSA_G, HD), F32)
    else:
        nsa_pool, win_prev, fox_pool, logf_pool = past
        n_pool = nsa_pool.shape[0]
        n_pages = page_table.shape[1]
        start = n_pages * PAGE
        feature_major = lambda pool: jnp.moveaxis(pool.reshape(n_pool, PAGE, -1), 1, 2)
        cmp_rows, sel_arr = nsa_gather(page_table, nkv[:, :NSA_CACHE_W].reshape(B, T, NSA_CACHE_W),
                                       feature_major(nsa_pool))
        kcvc = nsa_compress(cmp_rows, start // CMP_STRIDE, *cmp_args, math.gcd(B, 8))
        wp = win_prev.shape[1]
        win_all = jnp.concatenate([win_prev.reshape(B, wp, 2 * NSA_G * HD), nkv[:, NSA_CACHE_W:].reshape(B, T, -1)], 1)
        tw = -(-(wp + T) // NSA_TK) * NSA_TK
        win_arr = _heads(jnp.pad(win_all, ((0, 0), (0, tw - wp - T), (0, 0))), B, tw, 2 * NSA_G).astype(BF16)
        o_n = nsa_attend(rel_bias, q_heads, gates, kcvc, sel_arr, 0, win_arr, 0,
                         tq=T, q_pos0=start, win_pos0=start - wp, n_sel=-(-(start + T) // SEL_BLOCK))
        o_f = fox_decode(page_table, (fq * HD ** -0.5).reshape(B, T, FOX_W), fkv.reshape(B, T, 2 * FOX_W), logf,
                         feature_major(fox_pool), feature_major(logf_pool))
        o_f = o_f.reshape(M, FOX_W)
    win_state = jnp.concatenate([win_prev, new_win], axis=1)[:, -w_buf:]
    y = out_proj_residual(x.reshape(M, D), _unheads(o_n), o_f, p['w_out'].astype(BF16), tm)
    return y.reshape(B, T, D), (new_nsa, win_state, new_fox, logf)


def _trunk(x, past, page_table, P, w_buf, L, Bb):
    B, T, D = x.shape
    pe = dict(norm_mix=P['norm_mix'][0], w_in=P['w_in_even'][0], w_out=P['w_out_even'][0], mi_b_i=P['mi_b_i'][0],
              mi_b_f=P['mi_b_f'][0], mi_norm=P['mi_norm'][0], gd_conv_w=P['gd_conv_w'][0], gd_a_log=P['gd_a_log'][0],
              gd_dt_bias=P['gd_dt_bias'][0], gd_norm=P['gd_norm'][0])
    po = dict(norm_mix=P['norm_mix'][1], w_in=P['w_in_odd'][0], w_out=P['w_out_odd'][0],
              nsa_cmp_pos=P['nsa_cmp_pos'][0], nsa_cmp_w1=P['nsa_cmp_w1'][0], nsa_cmp_w2=P['nsa_cmp_w2'][0],
              fox_b_f=P['fox_b_f'][0])
    tm = _row_tile(B * T)
    mlp = lambda x, layer, final: mlp_residual(
        x.reshape(B * T, D), P['norm_mlp'][layer], P['w_up'][layer].astype(BF16), P['w_down'][layer].astype(BF16),
        P['norm_final'], final, tm, 1024).reshape(B, T, D)
    even_past = None if past is None else tuple(past[k][0] for k in ('mc', 'mn', 'mm', 'gs', 'gc'))
    odd_past = None if past is None else tuple(past[k][0] for k in ('nsa_kv', 'nsa_win', 'fox_kv', 'fox_logf'))
    x, st_e = even_layer(x, pe, even_past, L, Bb)
    x = mlp(x, 0, False)
    x, st_o = odd_layer(x, po, P['rel_bias'], w_buf, odd_past, page_table)
    y = mlp(x, 1, True)
    return y, tuple(a[None] for a in st_e + st_o)


def kernel(x_prompt, x_sample, state_mlstm_c, state_mlstm_n, state_mlstm_m, state_gdn_s, state_gdn_conv,
           cache_nsa_kv, state_nsa_win, cache_fox_kv, cache_fox_logf, page_table,
           norm_mix, norm_mlp, norm_final, w_up, w_down,
           w_in_even, w_out_even, mi_b_i, mi_b_f, mi_norm, gd_conv_w, gd_a_log, gd_dt_bias, gd_norm,
           w_in_odd, w_out_odd, nsa_cmp_pos, nsa_cmp_w1, nsa_cmp_w2, fox_b_f, rel_bias):
    P = dict(norm_mix=norm_mix, norm_mlp=norm_mlp, norm_final=norm_final, w_up=w_up, w_down=w_down,
             w_in_even=w_in_even, w_out_even=w_out_even, mi_b_i=mi_b_i, mi_b_f=mi_b_f, mi_norm=mi_norm,
             gd_conv_w=gd_conv_w, gd_a_log=gd_a_log, gd_dt_bias=gd_dt_bias, gd_norm=gd_norm,
             w_in_odd=w_in_odd, w_out_odd=w_out_odd, nsa_cmp_pos=nsa_cmp_pos, nsa_cmp_w1=nsa_cmp_w1,
             nsa_cmp_w2=nsa_cmp_w2, fox_b_f=fox_b_f, rel_bias=rel_bias)
    w_buf = state_nsa_win.shape[2]
    b_p, t_p = x_prompt.shape[:2]
    y_p, st_p = _trunk(x_prompt, None, None, P, w_buf, math.gcd(t_p, 64), math.gcd(b_p, RECURRENT_BATCH))
    past = dict(mc=state_mlstm_c, mn=state_mlstm_n, mm=state_mlstm_m, gs=state_gdn_s, gc=state_gdn_conv,
                nsa_kv=cache_nsa_kv, nsa_win=state_nsa_win, fox_kv=cache_fox_kv, fox_logf=cache_fox_logf)
    b_s, t_s = x_sample.shape[:2]
    y_s, st_s = _trunk(x_sample, past, page_table, P, w_buf, math.gcd(t_s, 64), math.gcd(b_s, RECURRENT_BATCH))
    return (y_p, y_s) + st_p + st_s
```

```python
import functools
import math

import jax
import jax.numpy as jnp
import numpy as np
from jax import lax
from jax.experimental import pallas as pl
from jax.experimental.pallas import tpu as pltpu

F32 = jnp.float32
BF16 = jnp.bfloat16
HI = lax.Precision.HIGHEST

D_MODEL = 1024
D_FF = 4 * D_MODEL
EPS = 1e-6
NEG_BIG = -1e30
PAGE = 128

MI_H, MI_DQK, MI_DV = 4, 64, 128
GD_H, GD_DK, GD_DV, GD_CONV = 4, 128, 128, 4
GD_CH = 3 * GD_H * GD_DK
NSA_G, NSA_R, HD = 2, 4, 64
FOX_H = 8
CMP_BLOCK, CMP_STRIDE, CMP_HIDDEN = 32, 16, 256
SEL_BLOCK, SEL_TOPN, WINDOW = 64, 16, 512
FORCE_SCORE = 1e4
N_BUCKETS, MAX_DISTANCE = 32, 128
BUCKET_EXACT = N_BUCKETS // 2
BUCKET_SAT_DIST = 113
LANE = 128
VMEM_LIMIT = 56 * 1024 * 1024


def _cparams(sem):
    return pltpu.CompilerParams(dimension_semantics=sem, vmem_limit_bytes=VMEM_LIMIT)


def _dot(a, b, precision=None):
    return jnp.dot(a, b, preferred_element_type=F32, precision=precision)


def _dot_nt(a, b, precision=None):
    return lax.dot_general(a, b, (((1,), (1,)), ((), ())), preferred_element_type=F32, precision=precision)


def _dot_tn(a, b, precision=None):
    return lax.dot_general(a, b, (((0,), (0,)), ((), ())), preferred_element_type=F32, precision=precision)


def _softplus(x):
    return jnp.maximum(x, 0.0) + jnp.log1p(jnp.exp(-jnp.abs(x)))


def _sigmoid(x):
    return 1.0 / (1.0 + jnp.exp(-x))


def _silu(x):
    return x * _sigmoid(x)


def _iota(shape, dim):
    return lax.broadcasted_iota(jnp.int32, shape, dim)


ACT_ID, ACT_LOGSIG, ACT_SIG, ACT_DECAY = 0.0, 1.0, 2.0, 3.0


def _proj_body(x_ref, g_ref, w_ref, sp_ref, *out_refs, widths):
    x = x_ref[...]
    hn = (x * lax.rsqrt(jnp.mean(x * x, axis=-1, keepdims=True) + EPS) * g_ref[...]).astype(BF16)
    off = 0
    for i, (o_ref, n) in enumerate(zip(out_refs, widths)):
        r = _dot(hn, w_ref[:, off:off + n])
        if i == len(widths) - 1:
            z = r + sp_ref[0:1, :]
            mode = sp_ref[1:2, :]
            decay = -jnp.exp(sp_ref[2:3, :]) * _softplus(z)
            r = jnp.where(mode == ACT_LOGSIG, -_softplus(-z),
                          jnp.where(mode == ACT_SIG, _sigmoid(z),
                                    jnp.where(mode == ACT_DECAY, decay, z)))
        o_ref[...] = r.astype(o_ref.dtype)
        off += n


def norm_proj(x, g, w_bf16, small_params, widths, tm):
    m, d = x.shape
    n_total = sum(widths)
    assert w_bf16.shape == (d, n_total) and m % tm == 0
    out_shape = [jax.ShapeDtypeStruct((m, n), F32) for n in widths]
    return pl.pallas_call(
        functools.partial(_proj_body, widths=tuple(widths)),
        grid=(m // tm,),
        in_specs=[pl.BlockSpec((tm, d), lambda i: (i, 0)),
                  pl.BlockSpec((1, d), lambda i: (0, 0)),
                  pl.BlockSpec((d, n_total), lambda i: (0, 0)),
                  pl.BlockSpec((8, LANE), lambda i: (0, 0))],
        out_specs=[pl.BlockSpec((tm, n), lambda i: (i, 0)) for n in widths],
        out_shape=out_shape,
        compiler_params=_cparams(("parallel",)),
        name="norm_proj",
    )(x, g.reshape(1, d), w_bf16, small_params)


def _outproj_body(x_ref, a1_ref, a2_ref, w_ref, o_ref):
    k1 = a1_ref.shape[1]
    y = _dot(a1_ref[...].astype(BF16), w_ref[0:k1, :]) + _dot(a2_ref[...].astype(BF16), w_ref[k1:, :])
    o_ref[...] = x_ref[...] + y


def out_proj_residual(x, a1, a2, w_bf16, tm):
    m, d = x.shape
    k1, k2 = a1.shape[1], a2.shape[1]
    return pl.pallas_call(
        _outproj_body,
        grid=(m // tm,),
        in_specs=[pl.BlockSpec((tm, d), lambda i: (i, 0)),
                  pl.BlockSpec((tm, k1), lambda i: (i, 0)),
                  pl.BlockSpec((tm, k2), lambda i: (i, 0)),
                  pl.BlockSpec((k1 + k2, d), lambda i: (0, 0))],
        out_specs=pl.BlockSpec((tm, d), lambda i: (i, 0)),
        out_shape=jax.ShapeDtypeStruct((m, d), F32),
        compiler_params=_cparams(("parallel",)),
        name="out_proj",
    )(x, a1, a2, w_bf16)


def _mlp_body(x_ref, g_ref, wu_ref, wd_ref, gf_ref, o_ref, hn_scr, acc_scr, *, final_norm):
    j = pl.program_id(1)

    @pl.when(j == 0)
    def _():
        x = x_ref[...]
        hn_scr[...] = (x * lax.rsqrt(jnp.mean(x * x, axis=-1, keepdims=True) + EPS) * g_ref[...]).astype(BF16)
        acc_scr[...] = jnp.zeros_like(acc_scr)

    u = jnp.maximum(_dot(hn_scr[...], wu_ref[...]), 0.0)
    acc_scr[...] += _dot((u * u).astype(BF16), wd_ref[...])

    @pl.when(j == pl.num_programs(1) - 1)
    def _():
        y = x_ref[...] + acc_scr[...]
        if final_norm:
            y = y * lax.rsqrt(jnp.mean(y * y, axis=-1, keepdims=True) + EPS) * gf_ref[...]
        o_ref[...] = y


def mlp_residual(x, g, w_up_bf16, w_down_bf16, g_final, final_norm, tm, tf):
    m, d = x.shape
    f = w_up_bf16.shape[1]
    return pl.pallas_call(
        functools.partial(_mlp_body, final_norm=final_norm),
        grid=(m // tm, f // tf),
        in_specs=[pl.BlockSpec((tm, d), lambda i, j: (i, 0)),
                  pl.BlockSpec((1, d), lambda i, j: (0, 0)),
                  pl.BlockSpec((d, tf), lambda i, j: (0, j)),
                  pl.BlockSpec((tf, d), lambda i, j: (j, 0)),
                  pl.BlockSpec((1, d), lambda i, j: (0, 0))],
        out_specs=pl.BlockSpec((tm, d), lambda i, j: (i, 0)),
        out_shape=jax.ShapeDtypeStruct((m, d), F32),
        scratch_shapes=[pltpu.VMEM((tm, d), BF16), pltpu.VMEM((tm, d), F32)],
        compiler_params=_cparams(("parallel", "arbitrary")),
        name="mlp",
    )(x, g.reshape(1, d), w_up_bf16, w_down_bf16, g_final.reshape(1, d))


def _tri(n):
    r = _iota((n, n), 0)
    c = _iota((n, n), 1)
    return r, c


def _bf16_terms(a, n):
    terms, rest = [], a
    for _ in range(n):
        t = rest.astype(BF16)
        terms.append(t)
        rest = rest - t.astype(F32)
    return terms


def _dot_split(a, b, f=None):
    f = f or _dot
    a_hi, a_lo = _bf16_terms(a, 2)
    b_hi, b_lo = _bf16_terms(b, 2)
    return f(a_hi, b_hi) + (f(a_hi, b_lo) + f(a_lo, b_hi))


def _dot_pick(a, b, f=None, exact_lhs=False):
    f = f or _dot
    if exact_lhs:
        return sum(f(a.astype(BF16), t) for t in _bf16_terms(b, 3))
    return sum(f(t, b.astype(BF16)) for t in _bf16_terms(a, 3))


def _chunk_dots(L):
    if L % 16 == 0:
        cast = lambda f: (lambda a, b: f(a.astype(BF16), b.astype(BF16)))
        return cast(_dot), cast(_dot_nt), cast(_dot_tn)
    full = lambda f: (lambda a, b: _dot_split(a, b, f))
    return full(_dot), full(_dot_nt), full(_dot_tn)


def _mlstm_group(bbs, q_ref, k_ref, v_ref, o_ref, sc_ref, sr_ref, nw_ref, h_ref, c_ref, n_ref, m_ref, L):
    nn, nt, tn = _chunk_dots(L)
    rows, cols = _tri(L)
    lower = rows >= cols
    tril = lower.astype(F32)
    triu = (rows <= cols).astype(F32)
    lane4 = _iota((1, MI_H), 1)
    scs = {bb: sc_ref[bb] for bb in bbs}
    srs = {bb: sr_ref[bb, 0] for bb in bbs}
    b_col = {bb: _dot_pick(tril, scs[bb][:, 4:8], exact_lhs=True) for bb in bbs}
    b_row = {bb: _dot_pick(srs[bb][4:8, :], triu) for bb in bbs}
    m_vec = {bb: m_ref[bb] for bb in bbs}
    chains = [(bb, h) for bb in bbs for h in range(MI_H)]
    q = {c: q_ref[c[0], c[1]] for c in chains}
    k = {c: k_ref[c[0], c[1]] * (MI_DQK ** -0.5) for c in chains}
    v = {c: v_ref[c[0], :, c[1] * MI_DV:(c[1] + 1) * MI_DV] for c in chains}
    qk = {c: nt(q[c], k[c]) for c in chains}
    c_prev = {c: c_ref[c[0], c[1]] for c in chains}
    qc = {c: nn(q[c], c_prev[c]) for c in chains}
    s, a_inter, m_t, m_new, a_prev, kw = {}, {}, {}, {}, {}, {}
    for c in chains:
        bb, h = c
        bc = b_col[bb][:, h:h + 1]
        m_prev = m_vec[bb][:, h:h + 1]
        d = jnp.where(lower, bc - b_row[bb][h:h + 1, :] + srs[bb][h:h + 1, :], NEG_BIG)
        inter = bc + m_prev
        m_t[c] = jnp.maximum(inter, jnp.max(d, axis=1, keepdims=True))
        s[c] = qk[c] * jnp.exp(d - m_t[c])
        a_inter[c] = jnp.exp(inter - m_t[c])
        b_last = bc[L - 1:L, :]
        g_col = b_last - bc + scs[bb][:, h:h + 1]
        m_new[c] = jnp.maximum(b_last + m_prev, jnp.max(g_col, axis=0, keepdims=True))
        a_prev[c] = jnp.exp(b_last + m_prev - m_new[c])
        kw[c] = k[c] * jnp.exp(g_col - m_new[c])
    sv = {c: nn(s[c], v[c]) for c in chains}
    kv = {c: tn(kw[c], v[c]) for c in chains}
    for c in chains:
        bb, h = c
        n_prev = n_ref[bb, h]
        num = sv[c] + a_inter[c] * qc[c]
        den = (jnp.sum(s[c], axis=1, keepdims=True)
               + a_inter[c] * jnp.sum(q[c] * n_prev, axis=1, keepdims=True))
        hh = num / jnp.maximum(jnp.abs(den), jnp.exp(-m_t[c]))
        c_ref[bb, h] = a_prev[c] * c_prev[c] + kv[c]
        n_ref[bb, h] = a_prev[c] * n_prev + jnp.sum(kw[c], axis=0, keepdims=True)
        m_vec[bb] = jnp.where(lane4 == h, m_new[c], m_vec[bb])
        hn = hh * lax.rsqrt(jnp.mean(hh * hh, axis=-1, keepdims=True) + EPS) * nw_ref[h:h + 1, :]
        gate = _sigmoid(o_ref[bb, :, h * MI_DV:(h + 1) * MI_DV])
        h_ref[bb, :, h * MI_DV:(h + 1) * MI_DV] = hn * gate
    for bb in bbs:
        m_ref[bb] = m_vec[bb]


def _mlstm_body(q_ref, k_ref, v_ref, o_ref, sc_ref, sr_ref, c0_ref, n0_ref, m0_ref, nw_ref,
                h_ref, c_ref, n_ref, m_ref, *, L, Bb):
    @pl.when(pl.program_id(1) == 0)
    def _():
        c_ref[...] = c0_ref[...]
        n_ref[...] = n0_ref[...]
        m_ref[...] = m0_ref[...]

    _mlstm_group(list(range(Bb)), q_ref, k_ref, v_ref, o_ref, sc_ref, sr_ref, nw_ref, h_ref, c_ref, n_ref, m_ref, L)


def mlstm(q, k, v, o, sc, sr, c0, n0, m0, norm_w, L, Bb):
    B, H, T, _ = q.shape
    nc = T // L
    hv = H * MI_DV
    bmap = lambda b, c: (b, 0, 0, 0)
    return pl.pallas_call(
        functools.partial(_mlstm_body, L=L, Bb=Bb),
        grid=(B // Bb, nc),
        in_specs=[pl.BlockSpec((Bb, H, L, MI_DQK), lambda b, c: (b, 0, c, 0)),
                  pl.BlockSpec((Bb, H, L, MI_DQK), lambda b, c: (b, 0, c, 0)),
                  pl.BlockSpec((Bb, L, hv), lambda b, c: (b, c, 0)),
                  pl.BlockSpec((Bb, L, hv), lambda b, c: (b, c, 0)),
                  pl.BlockSpec((Bb, L, LANE), lambda b, c: (b, c, 0)),
                  pl.BlockSpec((Bb, 1, 8, L), lambda b, c: (b, c, 0, 0)),
                  pl.BlockSpec((Bb, H, MI_DQK, MI_DV), bmap),
                  pl.BlockSpec((Bb, H, 1, MI_DQK), bmap),
                  pl.BlockSpec((Bb, 1, H), lambda b, c: (b, 0, 0)),
                  pl.BlockSpec((H, MI_DV), lambda b, c: (0, 0))],
        out_specs=[pl.BlockSpec((Bb, L, hv), lambda b, c: (b, c, 0)),
                   pl.BlockSpec((Bb, H, MI_DQK, MI_DV), bmap),
                   pl.BlockSpec((Bb, H, 1, MI_DQK), bmap),
                   pl.BlockSpec((Bb, 1, H), lambda b, c: (b, 0, 0))],
        out_shape=[jax.ShapeDtypeStruct((B, T, hv), F32),
                   jax.ShapeDtypeStruct((B, H, MI_DQK, MI_DV), F32),
                   jax.ShapeDtypeStruct((B, H, 1, MI_DQK), F32),
                   jax.ShapeDtypeStruct((B, 1, H), F32)],
        compiler_params=_cparams(("parallel", "arbitrary")),
        name="mlstm",
    )(q, k, v, o, sc, sr, c0, n0, m0, norm_w)


def _gdn_group(bbs, x_ref, z_ref, sc_ref, sr_ref, cw_ref, nw_ref, o_ref, s_ref, conv_ref, xp_scr, L):
    base = 8 - (GD_CONV - 1)
    nn, nt, tn = _chunk_dots(L)
    rows, cols = _tri(L)
    incl = rows >= cols
    strict = rows > cols
    eye = (rows == cols).astype(F32)
    tril = incl.astype(F32)
    triu = (rows <= cols).astype(F32)
    kw = GD_H * GD_DK
    ys, scs, gcols, grows = {}, {}, {}, {}
    for bb in bbs:
        xp_scr[bb, 8:8 + L, :] = x_ref[bb]
        y = xp_scr[bb, base:base + L, :] * cw_ref[0:1, :]
        for j in range(1, GD_CONV):
            y = y + xp_scr[bb, base + j:base + j + L, :] * cw_ref[j:j + 1, :]
        tail = xp_scr[bb, L + base:L + 8, :]
        xp_scr[bb, base:8, :] = tail
        conv_ref[bb] = tail
        ys[bb] = _silu(y)
        scs[bb] = sc_ref[bb]
        gcols[bb] = _dot_pick(tril, scs[bb][:, 12:16], exact_lhs=True)
        grows[bb] = _dot_pick(sr_ref[bb, 0][0:4, :], triu)
    chains = [(bb, h) for bb in bbs for h in range(GD_H)]
    q, k, vb, kb, dec, gcc, egc = {}, {}, {}, {}, {}, {}, {}
    for c in chains:
        bb, h = c
        y = ys[bb]
        qh = y[:, h * GD_DK:(h + 1) * GD_DK]
        kh = y[:, kw + h * GD_DK:kw + (h + 1) * GD_DK]
        vh = y[:, 2 * kw + h * GD_DV:2 * kw + (h + 1) * GD_DV]
        q[c] = qh * lax.rsqrt(jnp.sum(qh * qh, axis=-1, keepdims=True) + EPS) * (GD_DK ** -0.5)
        k[c] = kh * lax.rsqrt(jnp.sum(kh * kh, axis=-1, keepdims=True) + EPS)
        beta = scs[bb][:, 8 + h:9 + h]
        gcc[c] = gcols[bb][:, h:h + 1]
        dec[c] = jnp.exp(jnp.where(incl, gcc[c] - grows[bb][h:h + 1, :], NEG_BIG))
        egc[c] = jnp.exp(gcc[c])
        kb[c] = k[c] * beta
        vb[c] = vh * beta
    pw = {c: -(nt(kb[c], k[c]) * jnp.where(strict, dec[c], 0.0)) for c in chains}
    attn = {c: nt(q[c], k[c]) * dec[c] for c in chains}
    tinv = {c: eye + pw[c] for c in chains}
    for _ in range(int(math.log2(L)) - 1):
        pw = {c: _dot_split(pw[c], pw[c]) for c in chains}
        tinv = {c: tinv[c] + _dot_split(tinv[c], pw[c]) for c in chains}
    u = {c: _dot_split(tinv[c], vb[c]) for c in chains}
    w = {c: _dot_split(tinv[c], kb[c] * egc[c]) for c in chains}
    s_prev = {c: s_ref[c[0], c[1]] for c in chains}
    v_new = {c: u[c] - nn(w[c], s_prev[c]) for c in chains}
    o = {c: nn(q[c] * egc[c], s_prev[c]) + nn(attn[c], v_new[c]) for c in chains}
    for c in chains:
        bb, h = c
        g_last = gcc[c][L - 1:L, :]
        s_ref[bb, h] = jnp.exp(g_last) * s_prev[c] + tn(k[c] * jnp.exp(g_last - gcc[c]), v_new[c])
        on = o[c] * lax.rsqrt(jnp.mean(o[c] * o[c], axis=-1, keepdims=True) + EPS) * nw_ref[...]
        o_ref[bb, :, h * GD_DV:(h + 1) * GD_DV] = on * _silu(z_ref[bb, :, h * GD_DV:(h + 1) * GD_DV])


def _gdn_body(x_ref, z_ref, sc_ref, sr_ref, s0_ref, conv0_ref, cw_ref, nw_ref,
              o_ref, s_ref, conv_ref, xp_scr, *, L, Bb):
    @pl.when(pl.program_id(1) == 0)
    def _():
        s_ref[...] = s0_ref[...]
        xp_scr[:, 8 - (GD_CONV - 1):8, :] = conv0_ref[...]

    _gdn_group(list(range(Bb)), x_ref, z_ref, sc_ref, sr_ref, cw_ref, nw_ref, o_ref, s_ref, conv_ref, xp_scr, L)


def gdn(x, z, sc, sr, s0, conv0, conv_w, norm_w, L, Bb):
    B, T, ch = x.shape
    H = GD_H
    nc = T // L
    hv = H * GD_DV
    bmap = lambda b, c: (b, 0, 0, 0)
    return pl.pallas_call(
        functools.partial(_gdn_body, L=L, Bb=Bb),
        grid=(B // Bb, nc),
        in_specs=[pl.BlockSpec((Bb, L, ch), lambda b, c: (b, c, 0)),
                  pl.BlockSpec((Bb, L, hv), lambda b, c: (b, c, 0)),
                  pl.BlockSpec((Bb, L, LANE), lambda b, c: (b, c, 0)),
                  pl.BlockSpec((Bb, 1, 8, L), lambda b, c: (b, c, 0, 0)),
                  pl.BlockSpec((Bb, H, GD_DK, GD_DV), bmap),
                  pl.BlockSpec((Bb, GD_CONV - 1, ch), lambda b, c: (b, 0, 0)),
                  pl.BlockSpec((GD_CONV, ch), lambda b, c: (0, 0)),
                  pl.BlockSpec((1, GD_DV), lambda b, c: (0, 0))],
        out_specs=[pl.BlockSpec((Bb, L, hv), lambda b, c: (b, c, 0)),
                   pl.BlockSpec((Bb, H, GD_DK, GD_DV), bmap),
                   pl.BlockSpec((Bb, GD_CONV - 1, ch), lambda b, c: (b, 0, 0))],
        out_shape=[jax.ShapeDtypeStruct((B, T, hv), F32),
                   jax.ShapeDtypeStruct((B, H, GD_DK, GD_DV), F32),
                   jax.ShapeDtypeStruct((B, GD_CONV - 1, ch), F32)],
        scratch_shapes=[pltpu.VMEM((Bb, L + 8, ch), F32)],
        compiler_params=_cparams(("parallel", "arbitrary")),
        name="gdn",
    )(x, z, sc, sr, s0, conv0, conv_w, norm_w.reshape(1, GD_DV))


def _compress_body(x_ref, pos_ref, w1_ref, w2_ref, o_ref):
    Bb, _, R, half = x_ref.shape
    x = x_ref[:, 0].reshape(Bb * R, half).astype(F32)
    ua = _dot((x + pos_ref[0, 0:1, :]).astype(BF16), w1_ref[0, 0])
    ub = _dot((x + pos_ref[0, 1:2, :]).astype(BF16), w1_ref[0, 1])
    h = _silu(ua + pltpu.roll(ub, Bb * R - 1, 0))
    o_ref[:, 0, 0:R, :] = _dot(h.astype(BF16), w2_ref[0]).reshape(Bb, R, HD)
    rp = o_ref.shape[2]
    if rp > R:
        o_ref[:, 0, R:rp, :] = jnp.zeros((Bb, rp - R, HD), F32)


def nsa_compress(xr, R, pos, w1, w2, Bb):
    B = xr.shape[0]
    half = CMP_STRIDE * HD
    rp = -(-R // LANE) * LANE
    return pl.pallas_call(
        _compress_body,
        grid=(4, B // Bb),
        in_specs=[pl.BlockSpec((Bb, 1, R, half), lambda c, b: (b, c, 0, 0)),
                  pl.BlockSpec((1, 2, half), lambda c, b: (c // 2, 0, 0)),
                  pl.BlockSpec((1, 2, half, CMP_HIDDEN), lambda c, b: (c // 2, 0, 0, 0)),
                  pl.BlockSpec((1, CMP_HIDDEN, HD), lambda c, b: (c // 2, 0, 0))],
        out_specs=pl.BlockSpec((Bb, 1, rp, HD), lambda c, b: (b, c, 0, 0)),
        out_shape=jax.ShapeDtypeStruct((B, 4, rp, HD), F32),
        compiler_params=_cparams(("parallel", "parallel")),
        name="nsa_compress",
    )(xr, pos.reshape(2, 2, half), w1.reshape(2, 2, half, CMP_HIDDEN).astype(BF16), w2.astype(BF16))


def _cumsum_body(x_ref, o_ref):
    Bb, H, T = x_ref.shape
    rows, cols = _tri(LANE)
    triu = (rows <= cols).astype(F32)
    carry = jnp.zeros((Bb * H, 1), F32)
    for c in range(T // LANE):
        seg = x_ref[:, :, c * LANE:(c + 1) * LANE].reshape(Bb * H, LANE)
        loc = _dot_pick(seg, triu) + carry
        o_ref[:, :, c * LANE:(c + 1) * LANE] = loc.reshape(Bb, H, LANE)
        carry = loc[:, LANE - 1:LANE]


def cumsum_lanes(x, Bb):
    B, H, T = x.shape
    return pl.pallas_call(
        _cumsum_body,
        grid=(B // Bb,),
        in_specs=[pl.BlockSpec((Bb, H, T), lambda b: (b, 0, 0))],
        out_specs=pl.BlockSpec((Bb, H, T), lambda b: (b, 0, 0)),
        out_shape=jax.ShapeDtypeStruct((B, H, T), F32),
        compiler_params=_cparams(("parallel",)),
        name="cumsum",
    )(x)


def _flash_tile(s_blocks, v, m_scr, l_scr, acc_scr, exp=jnp.exp):
    dv = acc_scr.shape[-1]
    m_prev = m_scr[...]
    mx = s_blocks[0]
    for sb in s_blocks[1:]:
        mx = jnp.maximum(mx, sb)
    m_new = jnp.maximum(m_prev, jnp.max(mx, axis=1, keepdims=True))
    p_blocks = [exp(sb - m_new) for sb in s_blocks]
    sm = p_blocks[0]
    for pb in p_blocks[1:]:
        sm = sm + pb
    alpha = exp(m_prev - m_new)
    l_scr[...] = alpha * l_scr[...] + jnp.sum(sm, axis=1, keepdims=True)
    p = (jnp.concatenate(p_blocks, axis=1) if len(p_blocks) > 1 else p_blocks[0]).astype(BF16)
    acc_scr[...] = alpha[:, :dv] * acc_scr[...] + _dot(p, v)
    m_scr[...] = m_new


def _flash_reset(m_scr, l_scr, acc_scr):
    m_scr[...] = jnp.full_like(m_scr, NEG_BIG)
    l_scr[...] = jnp.zeros_like(l_scr)
    acc_scr[...] = jnp.zeros_like(acc_scr)


def _lane_blocks(s):
    return [s[:, i * LANE:(i + 1) * LANE] for i in range(s.shape[1] // LANE)]


def _fox_body(q_ref, k_ref, v_ref, o_ref, m_scr, l_scr, acc_scr, *, tq, tk):
    qi = pl.program_id(1)
    j = pl.program_id(2)
    top = (qi * tq + tq - 1) // tk

    @pl.when(j == 0)
    def _():
        _flash_reset(m_scr, l_scr, acc_scr)

    def tile(diag):
        if diag:
            mask = (top * tk + _iota((tq, tk), 1)) <= (qi * tq + _iota((tq, tk), 0))
        for h in range(FOX_H):
            s = _dot_nt(q_ref[0, h], k_ref[0, h])
            if diag:
                s = jnp.where(mask, s, NEG_BIG)
            _flash_tile(_lane_blocks(s), v_ref[0, h], m_scr.at[h], l_scr.at[h], acc_scr.at[h], exp=jnp.exp2)

    @pl.when(j == 0)
    def _():
        tile(True)

    @pl.when(jnp.logical_and(j > 0, j <= top))
    def _():
        tile(False)

    @pl.when(j == pl.num_programs(2) - 1)
    def _():
        for h in range(FOX_H):
            o_ref[0, h] = acc_scr[h] / jnp.maximum(l_scr[h][:, :HD], 1e-30)


LOG2E = 1.0 / math.log(2.0)


def fox_operands(q_heads, k_heads, f_heads):
    def top_bits(x):
        return lax.bitcast_convert_type(lax.bitcast_convert_type(x, jnp.uint32) & jnp.uint32(0xFFFF0000), F32)

    terms, rest = [], f_heads * LOG2E
    for _ in range(3):
        t = top_bits(rest)
        terms.append(t.astype(BF16))
        rest = rest - t
    f_terms = jnp.stack(terms, axis=-1)
    ones = jnp.ones_like(f_terms)
    pad = jnp.zeros(q_heads.shape[:3] + (LANE - HD - 6,), BF16)
    q_aug = jnp.concatenate([(q_heads * (HD ** -0.5 * LOG2E)).astype(BF16), f_terms, ones, pad], axis=-1)
    k_aug = jnp.concatenate([k_heads.astype(BF16), ones, -f_terms, pad], axis=-1)
    return q_aug, k_aug


def fox_prompt(q_aug, k_aug, kv, tq, tk):
    B, H, T, _ = q_aug.shape
    kmap = lambda i, j: jnp.maximum((i * tq + tq - 1) // tk - j, 0)
    return pl.pallas_call(
        functools.partial(_fox_body, tq=tq, tk=tk),
        grid=(B, T // tq, T // tk),
        in_specs=[pl.BlockSpec((1, H, tq, LANE), lambda b, i, j: (b, 0, i, 0)),
                  pl.BlockSpec((1, H, tk, LANE), lambda b, i, j: (b, 0, kmap(i, j), 0)),
                  pl.BlockSpec((1, H, tk, HD), lambda b, i, j: (b, 1, kmap(i, j), 0))],
        out_specs=pl.BlockSpec((1, H, tq, HD), lambda b, i, j: (b, 0, i, 0)),
        out_shape=jax.ShapeDtypeStruct((B, H, T, HD), F32),
        scratch_shapes=[pltpu.VMEM((H, tq, LANE), F32), pltpu.VMEM((H, tq, LANE), F32), pltpu.VMEM((H, tq, HD), F32)],
        compiler_params=_cparams(("parallel", "parallel", "arbitrary")),
        name="fox_prompt",
    )(q_aug, k_aug, kv)


def _fox_decode_body(pt_ref, q_ref, newkv_ref, newlf_ref, *refs, n_pages, tn):
    kv_refs = refs[:n_pages]
    lf_refs = refs[n_pages:2 * n_pages]
    o_ref = refs[2 * n_pages]
    hw = FOX_H * HD
    R = FOX_H * tn
    q = q_ref[0]
    qrep = jnp.concatenate([q] * FOX_H, axis=0)
    blockmask = (_iota((R, hw), 0) // tn) == (_iota((R, hw), 1) // HD)
    qbd = jnp.where(blockmask, qrep, 0.0).astype(BF16)
    rows, cols = _tri(PAGE)
    triu = (rows <= cols).astype(F32)
    ones = jnp.ones((8, PAGE), F32)
    expand = lambda a: jnp.concatenate([jnp.broadcast_to(a[h:h + 1, :], (tn, a.shape[1])) for h in range(FOX_H)], 0)
    carry_c = jnp.zeros((FOX_H, 1), F32)
    carry_r = jnp.zeros((1, FOX_H), F32)
    s_tiles = []
    for pg in range(n_pages):
        lf = lf_refs[pg][0]
        f_t = _dot_pick(lf, triu) + carry_c
        carry_c = f_t[:, PAGE - 1:PAGE]
        carry_r = carry_r + _dot_pick(ones, lf, _dot_nt, exact_lhs=True)[0:1, :]
        s_tiles.append(_dot(qbd, kv_refs[pg][0, 0:hw, :].astype(BF16)) - expand(f_t))
    lfn = newlf_ref[0]
    r8, c8 = _tri(tn)
    fq_c = _dot_pick((r8 >= c8).astype(F32), lfn, exact_lhs=True) + carry_r
    fq_t = _dot_pick(lfn, (r8 <= c8).astype(F32), _dot_tn) + carry_c
    fq_rows = jnp.concatenate([fq_c[:, h:h + 1] for h in range(FOX_H)], axis=0)
    kn = newkv_ref[0, :, 0:hw]
    s_new = _dot_split(qbd.astype(F32), kn, _dot_nt) - expand(fq_t)
    causal = _iota((R, tn), 1) <= (_iota((R, tn), 0) % tn)
    s_new = jnp.where(causal, s_new + fq_rows, NEG_BIG)
    s_tiles = [s + fq_rows for s in s_tiles]
    m = jnp.max(s_new, axis=1, keepdims=True)
    for s in s_tiles:
        m = jnp.maximum(m, jnp.max(s, axis=1, keepdims=True))
    p_new = jnp.where(causal, jnp.exp(s_new - m), 0.0)
    l = jnp.sum(p_new, axis=1, keepdims=True)
    acc = _dot_split(p_new, newkv_ref[0, :, hw:2 * hw])
    for pg, s in enumerate(s_tiles):
        p = jnp.exp(s - m)
        l = l + jnp.sum(p, axis=1, keepdims=True)
        acc = acc + _dot_nt(p.astype(BF16), kv_refs[pg][0, hw:2 * hw, :].astype(BF16))
    acc = acc / jnp.maximum(l, 1e-30)
    o_ref[0] = jnp.concatenate([acc[h * tn:(h + 1) * tn, h * HD:(h + 1) * HD] for h in range(FOX_H)], axis=1)


def fox_decode(page_table, q, newkv, newlf, kv_pool_t, lf_pool_t):
    B, tn, hw = q.shape
    n_pages = page_table.shape[1]
    page_spec = lambda rows, pg: pl.BlockSpec((1, rows, PAGE), lambda b, pt: (pt[b, pg], 0, 0))
    grid_spec = pltpu.PrefetchScalarGridSpec(
        num_scalar_prefetch=1,
        grid=(B,),
        in_specs=[pl.BlockSpec((1, tn, hw), lambda b, pt: (b, 0, 0)),
                  pl.BlockSpec((1, tn, 2 * hw), lambda b, pt: (b, 0, 0)),
                  pl.BlockSpec((1, tn, FOX_H), lambda b, pt: (b, 0, 0))]
                 + [page_spec(2 * hw, pg) for pg in range(n_pages)]
                 + [page_spec(FOX_H, pg) for pg in range(n_pages)],
        out_specs=pl.BlockSpec((1, tn, hw), lambda b, pt: (b, 0, 0)),
    )
    return pl.pallas_call(
        functools.partial(_fox_decode_body, n_pages=n_pages, tn=tn),
        grid_spec=grid_spec,
        out_shape=jax.ShapeDtypeStruct((B, tn, hw), F32),
        compiler_params=_cparams(("arbitrary",)),
        name="fox_decode",
    )(page_table, q, newkv, newlf, *([kv_pool_t] * n_pages), *([lf_pool_t] * n_pages))


def _t5_bucket(dist):
    n = jnp.maximum(dist, 0)
    nf = jnp.maximum(n, 1).astype(F32)
    large = BUCKET_EXACT + (jnp.log(nf / BUCKET_EXACT) / math.log(MAX_DISTANCE / BUCKET_EXACT)
                            * (N_BUCKETS - BUCKET_EXACT)).astype(jnp.int32)
    return jnp.where(n < BUCKET_EXACT, n, jnp.minimum(large, N_BUCKETS - 1))


def _bias_from_bucket(bucket, tbl_ref, head):
    out = jnp.zeros(bucket.shape, F32)
    for kk in range(N_BUCKETS):
        out = jnp.where(bucket == kk, tbl_ref[kk, head], out)
    return out


FOX_TQ, FOX_TK = 256, 512
RECURRENT_BATCH = 4
NSA_TQ = 2 * LANE
NSA_TK = 2 * LANE
NSA_FAR_GROUP = 4
CMP_PAT_CENTER = LANE // 2
NSA_BIAS_TILES = (BUCKET_SAT_DIST + NSA_TK + LANE - 1) // LANE


def _nsa_body(tbl_ref, q_ref, gate_ref, kc_ref, vc_ref, ks_ref, vs_ref, kw_ref, vw_ref, smap_ref, o_ref,
              bias_scr, pat_scr, score_scr, m_scr, l_scr, acc_scr, *, tq, q_pos0, win_pos0, n_sel, ncp, tw, single):
    g = pl.program_id(1)
    qi = pl.program_id(2)
    q0 = q_pos0 if single else q_pos0 + qi * tq
    aligned = (lambda x, m: x) if single else pl.multiple_of
    lo, hi = (max, min) if single else (jnp.maximum, jnp.minimum)
    R = NSA_R * tq
    scale = HD ** -0.5
    last_bias = tuple(tbl_ref[N_BUCKETS - 1, g * NSA_R + r] for r in range(NSA_R))

    @pl.when(qi == 0)
    def _():
        ii = _iota((tq, NSA_TK), 0)
        jj = _iota((tq, NSA_TK), 1)
        for dd in range(NSA_BIAS_TILES):
            bucket = _t5_bucket(ii - jj + dd * LANE)
            for r in range(NSA_R):
                bias_scr[dd, r * tq:(r + 1) * tq, :] = (
                    _bias_from_bucket(bucket, tbl_ref, g * NSA_R + r) - last_bias[r])
        bias_scr[NSA_BIAS_TILES] = jnp.zeros((R, NSA_TK), F32)
        rel_blk = _iota((tq, LANE), 1) - CMP_PAT_CENTER
        bucket = _t5_bucket(_iota((tq, LANE), 0) - rel_blk * CMP_STRIDE - (CMP_BLOCK - 1))
        for r in range(NSA_R):
            pat_scr[r] = _bias_from_bucket(bucket, tbl_ref, g * NSA_R + r) - last_bias[r]

    t_col = q0 + _iota((tq, 1), 0)

    bias_tiles, mask_tiles = [], []
    for nt in range(ncp // LANE):
        c_end = (nt * LANE + _iota((tq, LANE), 1)) * CMP_STRIDE + (CMP_BLOCK - 1)
        dist = t_col - c_end
        max_dist = q0 + tq - 1 - (nt * LANE * CMP_STRIDE + CMP_BLOCK - 1)
        min_dist = q0 - ((nt * LANE + LANE - 1) * CMP_STRIDE + CMP_BLOCK - 1)
        special = jnp.logical_and(max_dist >= 0, min_dist < BUCKET_SAT_DIST)

        off = nt * LANE - q0 // CMP_STRIDE + CMP_PAT_CENTER

        def general(off=off):
            lanes = _iota((tq, LANE), 1) + off
            inside = jnp.logical_and(lanes >= 0, lanes < LANE)
            shift = (LANE - off % LANE) % LANE
            return jnp.stack([jnp.where(inside, pltpu.roll(pat_scr[r], shift, 1), 0.0) for r in range(NSA_R)])

        def saturated():
            return jnp.zeros((NSA_R, tq, LANE), F32)

        bias_tiles.append(lax.cond(special, general, saturated))
        mask_tiles.append(dist >= 0)
    mask_c = jnp.concatenate(mask_tiles, axis=1) if len(mask_tiles) > 1 else mask_tiles[0]
    kc = kc_ref[0, 0].astype(BF16)
    vc = vc_ref[0, 0].astype(BF16)
    pcsum = jnp.zeros((tq, ncp), F32)
    qk_c = [_dot_nt((q_ref[0, r] * scale).astype(BF16), kc) for r in range(NSA_R)]
    pcs = []
    for r in range(NSA_R):
        bias_r = jnp.concatenate([b[r] for b in bias_tiles], axis=1) if len(bias_tiles) > 1 else bias_tiles[0][r]
        s = jnp.where(mask_c, qk_c[r] + bias_r, NEG_BIG)
        m = jnp.max(s, axis=1, keepdims=True)
        p = jnp.where(mask_c, jnp.exp(s - m), 0.0)
        pc = p / jnp.maximum(jnp.sum(p, axis=1, keepdims=True), 1e-30)
        pcs.append(pc.astype(BF16))
        pcsum = pcsum + pc
    o_c = [_dot(pc, vc) for pc in pcs]

    ps_t = _dot_pick(smap_ref[...], pcsum, _dot_nt, exact_lhs=True)
    j_col = _iota((LANE, 1), 0)
    t_row = q0 + _iota((1, tq), 1)
    cur = lax.shift_right_logical(t_row, int(math.log2(SEL_BLOCK)))
    score = jnp.where(j_col * SEL_BLOCK <= t_row, ps_t, -1.0)
    score = jnp.where(j_col == cur - 1, FORCE_SCORE, score)
    score = jnp.where(j_col == cur, FORCE_SCORE, score)
    score = jnp.where(j_col == 0, FORCE_SCORE, score)
    score = jnp.where(j_col < n_sel, score, -3e38)
    score_scr[...] = score

    def rank_body(jp, rank):
        row = score_scr[pl.ds(jp, 1), :]
        tie = jnp.where(j_col > jp, 1.0, 0.0)
        return rank + jnp.where(row > score, 1.0, jnp.where(row == score, tie, 0.0))

    n_rank = jnp.minimum((q0 + tq - 1) // SEL_BLOCK + 1, LANE)
    rank = lax.fori_loop(0, n_rank, rank_body, jnp.zeros((LANE, tq), F32))
    sel_t = jnp.where(rank < SEL_TOPN, 1.0, 0.0).astype(BF16)
    eye = (_iota((tq, tq), 0) == _iota((tq, tq), 1)).astype(BF16)
    sel = _dot_nt(eye, sel_t).astype(BF16)

    qs = (q_ref[0].reshape(R, HD) * scale).astype(BF16)
    log2_blk = int(math.log2(SEL_BLOCK))

    def stack(a):
        return jnp.concatenate([a] * NSA_R, axis=0)

    def rel_dist(pos0, width):
        return (q0 + _iota((tq, width), 0)) - (pos0 + _iota((tq, width), 1))

    def near_bias(offset, width):
        tiles = [bias_scr[hi(lo(offset - j * NSA_TK, 0) // LANE, NSA_BIAS_TILES)] for j in range(width // NSA_TK)]
        return jnp.concatenate(tiles, axis=1) if len(tiles) > 1 else tiles[0]

    def sel_tile(k0, width, near):
        k = ks_ref[0, 0, pl.ds(k0, width), :]
        v = vs_ref[0, 0, pl.ds(k0, width), :]
        blk = lax.shift_right_logical(k0 + _iota((LANE, width), 1), log2_blk)
        expand = jnp.where(_iota((LANE, width), 0) == blk, 1.0, 0.0).astype(BF16)
        chosen = _dot(sel, expand)
        s = _dot_nt(qs, k)
        if near:
            chosen = jnp.where(rel_dist(k0, width) >= 0, chosen, 0.0)
            s = s + near_bias(q0 - k0, width)
        s = jnp.where(stack(chosen) > 0.5, s, NEG_BIG)
        _flash_tile(_lane_blocks(s), v, m_scr, l_scr, acc_scr)

    _flash_reset(m_scr, l_scr, acc_scr)
    kt_top = (q0 + tq - 1) // NSA_TK
    near_w = min(2 * NSA_TK, ks_ref.shape[2])
    sel_tile(aligned(lo(kt_top - 1, 0) * NSA_TK, NSA_TK), near_w, True)

    n_far = lo(kt_top - 1, 0)
    n_groups = n_far // NSA_FAR_GROUP

    def sel_far_group(gi, carry):
        sel_tile(pl.multiple_of(gi * (NSA_FAR_GROUP * NSA_TK), NSA_FAR_GROUP * NSA_TK), NSA_FAR_GROUP * NSA_TK, False)
        return carry

    def sel_far(kt, carry):
        sel_tile(pl.multiple_of(kt * NSA_TK, NSA_TK), NSA_TK, False)
        return carry

    if single:
        if n_far:
            sel_tile(0, n_far * NSA_TK, False)
    else:
        lax.fori_loop(0, n_groups, sel_far_group, 0)
        lax.fori_loop(n_groups * NSA_FAR_GROUP, n_far, sel_far, 0)
    o_s = acc_scr[...] / jnp.maximum(l_scr[...][:, :HD], 1e-30)

    n_win = -(-(WINDOW - 1 + tq) // NSA_TK) + (0 if tq % NSA_TK == 0 else 1)
    win_w = min(n_win * NSA_TK, tw)
    wt_top = (hi(q0 + tq - win_pos0, tw) - 1) // NSA_TK
    k0 = aligned(hi(lo(wt_top + 1 - win_w // NSA_TK, 0), (tw - win_w) // NSA_TK) * NSA_TK, NSA_TK)
    _flash_reset(m_scr, l_scr, acc_scr)
    dist = rel_dist(win_pos0 + k0, win_w)
    ok = jnp.where(dist >= 0, jnp.where(dist < WINDOW, 1.0, 0.0), 0.0)
    s = _dot_nt(qs, kw_ref[0, 0, pl.ds(k0, win_w), :]) + near_bias(q0 - win_pos0 - k0, win_w)
    s = jnp.where(stack(ok) > 0.5, s, NEG_BIG)
    _flash_tile(_lane_blocks(s), vw_ref[0, 0, pl.ds(k0, win_w), :], m_scr, l_scr, acc_scr)
    o_w = acc_scr[...] / jnp.maximum(l_scr[...][:, :HD], 1e-30)

    gates = gate_ref[0, 0]
    for r in range(NSA_R):
        o_ref[0, r] = (gates[:, 3 * r:3 * r + 1] * o_c[r]
                       + gates[:, 3 * r + 1:3 * r + 2] * o_s[r * tq:(r + 1) * tq]
                       + gates[:, 3 * r + 2:3 * r + 3] * o_w[r * tq:(r + 1) * tq])


def _selection_overlap_t(ncp):
    c_start = np.arange(ncp)[None, :] * CMP_STRIDE
    s_start = np.arange(LANE)[:, None] * SEL_BLOCK
    return ((c_start < s_start + SEL_BLOCK) & (c_start + CMP_BLOCK > s_start)).astype(np.float32)


def nsa_attend(tbl, q, gates, kcvc, sel_arr, sel_off, win_arr, win_off, *, tq, q_pos0, win_pos0, n_sel):
    B, _, Tq, _ = q.shape
    ncp = kcvc.shape[2]
    tks = sel_arr.shape[2]
    tw = win_arr.shape[2]
    R = NSA_R * tq
    smap = jnp.asarray(_selection_overlap_t(ncp))
    kv_spec = lambda rows, off: pl.BlockSpec((1, 1, rows, HD), lambda b, g, i: (b, off + g, 0, 0))
    return pl.pallas_call(
        functools.partial(_nsa_body, tq=tq, q_pos0=q_pos0, win_pos0=win_pos0, n_sel=n_sel, ncp=ncp, tw=tw,
                          single=(Tq == tq)),
        grid=(B, NSA_G, Tq // tq),
        in_specs=[pl.BlockSpec(memory_space=pltpu.SMEM),
                  pl.BlockSpec((1, NSA_R, tq, HD), lambda b, g, i: (b, g, i, 0)),
                  pl.BlockSpec((1, 1, tq, 3 * NSA_R), lambda b, g, i: (b, g, i, 0)),
                  kv_spec(ncp, 0), kv_spec(ncp, 2),
                  kv_spec(tks, sel_off), kv_spec(tks, sel_off + 2),
                  kv_spec(tw, win_off), kv_spec(tw, win_off + 2),
                  pl.BlockSpec((LANE, ncp), lambda b, g, i: (0, 0))],
        out_specs=pl.BlockSpec((1, NSA_R, tq, HD), lambda b, g, i: (b, g, i, 0)),
        out_shape=jax.ShapeDtypeStruct((B, NSA_G * NSA_R, Tq, HD), F32),
        scratch_shapes=[pltpu.VMEM((NSA_BIAS_TILES + 1, R, NSA_TK), F32), pltpu.VMEM((NSA_R, tq, LANE), F32),
                        pltpu.VMEM((LANE, tq), F32),
                        pltpu.VMEM((R, LANE), F32), pltpu.VMEM((R, LANE), F32), pltpu.VMEM((R, HD), F32)],
        compiler_params=_cparams(("parallel", "parallel", "arbitrary")),
        name="nsa_attend",
    )(tbl, q, gates, kcvc, kcvc, sel_arr, sel_arr, win_arr, win_arr, smap)


def _nsa_gather_body(pt_ref, new_ref, *refs, n_pages):
    pages = refs[:n_pages]
    cmp_ref, sel_ref, xt_scr = refs[n_pages:]
    ngrp = 2 * NSA_G
    half = ngrp * HD
    rows16 = PAGE // CMP_STRIDE
    for pg in range(n_pages):
        xt = pages[pg][0].T
        for j in range(ngrp):
            sel_ref[0, j, pg * PAGE:(pg + 1) * PAGE, :] = xt[:, half + j * HD:half + (j + 1) * HD].astype(BF16)
        for slab in range(half // LANE):
            xt_scr[slab] = xt[:, slab * LANE:(slab + 1) * LANE]
            for p in range(CMP_STRIDE):
                rows = xt_scr[slab, pl.ds(p, rows16, stride=CMP_STRIDE), :]
                for jj in range(LANE // HD):
                    cmp_ref[0, slab * (LANE // HD) + jj, pg * rows16:(pg + 1) * rows16, p * HD:(p + 1) * HD] = (
                        rows[:, jj * HD:(jj + 1) * HD])
    tn = new_ref.shape[1]
    tail = sel_ref.shape[2] - n_pages * PAGE
    for j in range(ngrp):
        new = new_ref[0, :, (ngrp + j) * HD:(ngrp + j + 1) * HD]
        tile = jnp.concatenate([new, jnp.zeros((tail - tn, HD), F32)], axis=0)
        sel_ref[0, j, n_pages * PAGE:, :] = tile.astype(BF16)


def nsa_gather(page_table, new_rows, pool_t):
    B, tn, width = new_rows.shape
    n_pages = page_table.shape[1]
    ngrp = 2 * NSA_G
    sel_rows = -(-(n_pages * PAGE + tn) // NSA_TK) * NSA_TK
    cmp_rows = n_pages * PAGE // CMP_STRIDE
    grid_spec = pltpu.PrefetchScalarGridSpec(
        num_scalar_prefetch=1,
        grid=(B,),
        in_specs=[pl.BlockSpec((1, tn, width), lambda b, pt: (b, 0, 0))]
                 + [pl.BlockSpec((1, 2 * ngrp * HD, PAGE),
                                 functools.partial(lambda b, pt, pg: (pt[b, pg], 0, 0), pg=pg))
                    for pg in range(n_pages)],
        out_specs=[pl.BlockSpec((1, ngrp, cmp_rows, CMP_STRIDE * HD), lambda b, pt: (b, 0, 0, 0)),
                   pl.BlockSpec((1, ngrp, sel_rows, HD), lambda b, pt: (b, 0, 0, 0))],
        scratch_shapes=[pltpu.VMEM((ngrp * HD // LANE, PAGE, LANE), F32)],
    )
    return pl.pallas_call(
        functools.partial(_nsa_gather_body, n_pages=n_pages),
        grid_spec=grid_spec,
        out_shape=[jax.ShapeDtypeStruct((B, ngrp, cmp_rows, CMP_STRIDE * HD), F32),
                   jax.ShapeDtypeStruct((B, ngrp, sel_rows, HD), BF16)],
        compiler_params=_cparams(("arbitrary",)),
        name="nsa_gather",
    )(page_table, new_rows, *([pool_t] * n_pages))


def _row_tile(m):
    return 512 if m % 512 == 0 else m


def _small_params(entries):
    sp = jnp.zeros((8, LANE), F32)
    for off, bias, act, log_scale in entries:
        n = bias.shape[0]
        sp = sp.at[0, off:off + n].set(bias.astype(F32))
        sp = sp.at[1, off:off + n].set(act)
        if log_scale is not None:
            sp = sp.at[2, off:off + n].set(log_scale.astype(F32))
    return sp


EVEN_WIDTHS = (256, 256, 512, 512, GD_CH, 512, LANE)


def _even_weights(w_in):
    s = np.cumsum((0, 256, 256, 512, 512, 4, 4, 512, 512, 512, 512, 4, 4))
    col = lambda i: w_in[:, s[i]:s[i + 1]]
    small = jnp.concatenate([col(4), col(5), col(10), col(11)], axis=1)
    small = jnp.pad(small, ((0, 0), (0, LANE - small.shape[1])))
    return jnp.concatenate([col(0), col(1), col(2), col(3), col(6), col(7), col(8), col(9), small], axis=1).astype(BF16)


def _chunk_rows(small, B, T, L, lanes):
    r = small.reshape(B, T // L, L, LANE)[..., lanes[0]:lanes[1]]
    r = jnp.swapaxes(r, 2, 3)
    return jnp.pad(r, ((0, 0), (0, 0), (0, 8 - r.shape[2]), (0, 0)))


def even_layer(x, p, past, L, Bb):
    B, T, D = x.shape
    M = B * T
    tm = _row_tile(M)
    sp = _small_params([(0, p['mi_b_i'], ACT_ID, None), (4, p['mi_b_f'], ACT_LOGSIG, None),
                        (8, jnp.zeros((4,), F32), ACT_SIG, None), (12, p['gd_dt_bias'], ACT_DECAY, p['gd_a_log'])])
    mq, mk, mv, mo, gx, gz, small = norm_proj(x.reshape(M, D), p['norm_mix'], _even_weights(p['w_in']), sp,
                                              EVEN_WIDTHS, tm)
    heads = lambda a: jnp.transpose(a.reshape(B, T, MI_H, MI_DQK), (0, 2, 1, 3))
    sc = small.reshape(B, T, LANE)
    if past is None:
        c0 = jnp.zeros((B, MI_H, MI_DQK, MI_DV), F32)
        n0 = jnp.zeros((B, MI_H, 1, MI_DQK), F32)
        m0 = jnp.zeros((B, 1, MI_H), F32)
        s0 = jnp.zeros((B, GD_H, GD_DK, GD_DV), F32)
        conv0 = jnp.zeros((B, GD_CONV - 1, GD_CH), F32)
    else:
        c0, n0, m0, s0, conv0 = past
        n0 = n0.reshape(B, MI_H, 1, MI_DQK)
        m0 = m0.reshape(B, 1, MI_H)
    hm, c1, n1, m1 = mlstm(heads(mq), heads(mk), mv.reshape(B, T, -1), mo.reshape(B, T, -1), sc,
                           _chunk_rows(small, B, T, L, (0, 8)), c0, n0, m0,
                           p['mi_norm'].reshape(MI_H, MI_DV), L, Bb)
    og, s1, conv1 = gdn(gx.reshape(B, T, GD_CH), gz.reshape(B, T, -1), sc,
                        _chunk_rows(small, B, T, L, (12, 16)), s0, conv0, p['gd_conv_w'], p['gd_norm'], L, Bb)
    y = out_proj_residual(x.reshape(M, D), hm.reshape(M, -1), og.reshape(M, -1), p['w_out'].astype(BF16), tm)
    return y.reshape(B, T, D), (c1, n1.reshape(B, MI_H, MI_DQK), m1.reshape(B, MI_H), s1, conv1)


NSA_QW = NSA_G * NSA_R * HD
NSA_KVW = 6 * NSA_G * HD
NSA_CACHE_W = 4 * NSA_G * HD
FOX_W = FOX_H * HD
N_GATE = 3 * NSA_G * NSA_R
ODD_WIDTHS = (NSA_QW, NSA_KVW, FOX_W, 2 * FOX_W, LANE)


def _odd_weights(w_in):
    s = np.cumsum((0, NSA_QW, NSA_KVW, N_GATE, FOX_W, FOX_W, FOX_W, FOX_H))
    col = lambda i: w_in[:, s[i]:s[i + 1]]
    small = jnp.concatenate([col(2), col(6)], axis=1)
    small = jnp.pad(small, ((0, 0), (0, LANE - small.shape[1])))
    return jnp.concatenate([col(0), col(1), col(3), col(4), col(5), small], axis=1).astype(BF16)


def _heads(a, B, T, n):
    return jnp.transpose(a.reshape(B, T, n, HD), (0, 2, 1, 3))


def _unheads(a):
    B, n, T, _ = a.shape
    return jnp.transpose(a, (0, 2, 1, 3)).reshape(B * T, n * HD)


def odd_layer(x, p, rel_bias, w_buf, past, page_table):
    B, T, D = x.shape
    M = B * T
    tm = _row_tile(M)
    sp = _small_params([(0, jnp.zeros((N_GATE,), F32), ACT_SIG, None), (N_GATE, p['fox_b_f'], ACT_LOGSIG, None)])
    nq, nkv, fq, fkv, small = norm_proj(x.reshape(M, D), p['norm_mix'], _odd_weights(p['w_in']), sp, ODD_WIDTHS, tm)
    new_nsa = nkv[:, :NSA_CACHE_W].reshape(B, T, 4, NSA_G, HD)
    new_win = nkv[:, NSA_CACHE_W:].reshape(B, T, 2, NSA_G, HD)
    new_fox = fkv.reshape(B, T, 2, FOX_H, HD)
    logf = small[:, N_GATE:N_GATE + FOX_H].reshape(B, T, FOX_H)
    q_heads = _heads(nq, B, T, NSA_G * NSA_R)
    gates = jnp.transpose(small[:, :N_GATE].reshape(B, T, NSA_G, 3 * NSA_R), (0, 2, 1, 3))
    cmp_args = (p['nsa_cmp_pos'], p['nsa_cmp_w1'], p['nsa_cmp_w2'])
    rows16 = CMP_STRIDE * HD
    if past is None:
        groups = _heads(nkv, B, T, 6 * NSA_G)
        kcvc = nsa_compress(groups.reshape(B, 6 * NSA_G, T // CMP_STRIDE, rows16), T // CMP_STRIDE, *cmp_args, 1)
        arr = groups.astype(BF16)
        o_n = nsa_attend(rel_bias, q_heads, gates, kcvc, arr, 2 * NSA_G, arr, 4 * NSA_G,
                         tq=min(T, NSA_TQ), q_pos0=0, win_pos0=0, n_sel=-(-T // SEL_BLOCK))
        f_t = cumsum_lanes(jnp.transpose(logf, (0, 2, 1)), B)
        kv_heads = _heads(fkv, B, T, 2 * FOX_H)
        q_aug, k_aug = fox_operands(_heads(fq, B, T, FOX_H), kv_heads[:, :FOX_H], f_t)
        o_f = fox_prompt(q_aug, k_aug, kv_heads.astype(BF16), min(T, FOX_TQ), min(T, FOX_TK))
        o_f = _unheads(o_f)
        win_prev = jnp.zeros((B, WINDOW, 2, NSA_G, HD), F32)
    else:
        nsa_pool, win_prev, fox_pool, logf_pool = past
        n_pool = nsa_pool.shape[0]
        n_pages = page_table.shape[1]
        start = n_pages * PAGE
        feature_major = lambda pool: jnp.moveaxis(pool.reshape(n_pool, PAGE, -1), 1, 2)
        cmp_rows, sel_arr = nsa_gather(page_table, nkv[:, :NSA_CACHE_W].reshape(B, T, NSA_CACHE_W),
                                       feature_major(nsa_pool))
        kcvc = nsa_compress(cmp_rows, start // CMP_STRIDE, *cmp_args, math.gcd(B, 8))
        wp = win_prev.shape[1]
        win_all = jnp.concatenate([win_prev.reshape(B, wp, 2 * NSA_G * HD), nkv[:, NSA_CACHE_W:].reshape(B, T, -1)], 1)
        tw = -(-(wp + T) // NSA_TK) * NSA_TK
        win_arr = _heads(jnp.pad(win_all, ((0, 0), (0, tw - wp - T), (0, 0))), B, tw, 2 * NSA_G).astype(BF16)
        o_n = nsa_attend(rel_bias, q_heads, gates, kcvc, sel_arr, 0, win_arr, 0,
                         tq=T, q_pos0=start, win_pos0=start - wp, n_sel=-(-(start + T) // SEL_BLOCK))
        o_f = fox_decode(page_table, (fq * HD ** -0.5).reshape(B, T, FOX_W), fkv.reshape(B, T, 2 * FOX_W), logf,
                         feature_major(fox_pool), feature_major(logf_pool))
        o_f = o_f.reshape(M, FOX_W)
    win_state = jnp.concatenate([win_prev, new_win], axis=1)[:, -w_buf:]
    y = out_proj_residual(x.reshape(M, D), _unheads(o_n), o_f, p['w_out'].astype(BF16), tm)
    return y.reshape(B, T, D), (new_nsa, win_state, new_fox, logf)


def _trunk(x, past, page_table, P, w_buf, L, Bb):
    B, T, D = x.shape
    pe = dict(norm_mix=P['norm_mix'][0], w_in=P['w_in_even'][0], w_out=P['w_out_even'][0], mi_b_i=P['mi_b_i'][0],
              mi_b_f=P['mi_b_f'][0], mi_norm=P['mi_norm'][0], gd_conv_w=P['gd_conv_w'][0], gd_a_log=P['gd_a_log'][0],
              gd_dt_bias=P['gd_dt_bias'][0], gd_norm=P['gd_norm'][0])
    po = dict(norm_mix=P['norm_mix'][1], w_in=P['w_in_odd'][0], w_out=P['w_out_odd'][0],
              nsa_cmp_pos=P['nsa_cmp_pos'][0], nsa_cmp_w1=P['nsa_cmp_w1'][0], nsa_cmp_w2=P['nsa_cmp_w2'][0],
              fox_b_f=P['fox_b_f'][0])
    tm = _row_tile(B * T)
    mlp = lambda x, layer, final: mlp_residual(
        x.reshape(B * T, D), P['norm_mlp'][layer], P['w_up'][layer].astype(BF16), P['w_down'][layer].astype(BF16),
        P['norm_final'], final, tm, 1024).reshape(B, T, D)
    even_past = None if past is None else tuple(past[k][0] for k in ('mc', 'mn', 'mm', 'gs', 'gc'))
    odd_past = None if past is None else tuple(past[k][0] for k in ('nsa_kv', 'nsa_win', 'fox_kv', 'fox_logf'))
    x, st_e = even_layer(x, pe, even_past, L, Bb)
    x = mlp(x, 0, False)
    x, st_o = odd_layer(x, po, P['rel_bias'], w_buf, odd_past, page_table)
    y = mlp(x, 1, True)
    return y, tuple(a[None] for a in st_e + st_o)


def kernel(x_prompt, x_sample, state_mlstm_c, state_mlstm_n, state_mlstm_m, state_gdn_s, state_gdn_conv,
           cache_nsa_kv, state_nsa_win, cache_fox_kv, cache_fox_logf, page_table,
           norm_mix, norm_mlp, norm_final, w_up, w_down,
           w_in_even, w_out_even, mi_b_i, mi_b_f, mi_norm, gd_conv_w, gd_a_log, gd_dt_bias, gd_norm,
           w_in_odd, w_out_odd, nsa_cmp_pos, nsa_cmp_w1, nsa_cmp_w2, fox_b_f, rel_bias):
    P = dict(norm_mix=norm_mix, norm_mlp=norm_mlp, norm_final=norm_final, w_up=w_up, w_down=w_down,
             w_in_even=w_in_even, w_out_even=w_out_even, mi_b_i=mi_b_i, mi_b_f=mi_b_f, mi_norm=mi_norm,
             gd_conv_w=gd_conv_w, gd_a_log=gd_a_log, gd_dt_bias=gd_dt_bias, gd_norm=gd_norm,
             w_in_odd=w_in_odd, w_out_odd=w_out_odd, nsa_cmp_pos=nsa_cmp_pos, nsa_cmp_w1=nsa_cmp_w1,
             nsa_cmp_w2=nsa_cmp_w2, fox_b_f=fox_b_f, rel_bias=rel_bias)
    w_buf = state_nsa_win.shape[2]
    b_p, t_p = x_prompt.shape[:2]
    y_p, st_p = _trunk(x_prompt, None, None, P, w_buf, math.gcd(t_p, 64), math.gcd(b_p, RECURRENT_BATCH))
    past = dict(mc=state_mlstm_c, mn=state_mlstm_n, mm=state_mlstm_m, gs=state_gdn_s, gc=state_gdn_conv,
                nsa_kv=cache_nsa_kv, nsa_win=state_nsa_win, fox_kv=cache_fox_kv, fox_logf=cache_fox_logf)
    b_s, t_s = x_sample.shape[:2]
    y_s, st_s = _trunk(x_sample, past, page_table, P, w_buf, math.gcd(t_s, 64), math.gcd(b_s, RECURRENT_BATCH))
    return (y_p, y_s) + st_p + st_s
```

```python
import functools
import math

import jax
import jax.numpy as jnp
import numpy as np
from jax import lax
from jax.experimental import pallas as pl
from jax.experimental.pallas import tpu as pltpu

F32 = jnp.float32
BF16 = jnp.bfloat16
HI = lax.Precision.HIGHEST

D_MODEL = 1024
D_FF = 4 * D_MODEL
EPS = 1e-6
NEG_BIG = -1e30
PAGE = 128

MI_H, MI_DQK, MI_DV = 4, 64, 128
GD_H, GD_DK, GD_DV, GD_CONV = 4, 128, 128, 4
GD_CH = 3 * GD_H * GD_DK
NSA_G, NSA_R, HD = 2, 4, 64
FOX_H = 8
CMP_BLOCK, CMP_STRIDE, CMP_HIDDEN = 32, 16, 256
SEL_BLOCK, SEL_TOPN, WINDOW = 64, 16, 512
FORCE_SCORE = 1e4
N_BUCKETS, MAX_DISTANCE = 32, 128
BUCKET_EXACT = N_BUCKETS // 2
BUCKET_SAT_DIST = 113
LANE = 128
VMEM_LIMIT = 56 * 1024 * 1024


def _cparams(sem):
    return pltpu.CompilerParams(dimension_semantics=sem, vmem_limit_bytes=VMEM_LIMIT)


def _dot(a, b, precision=None):
    return jnp.dot(a, b, preferred_element_type=F32, precision=precision)


def _dot_nt(a, b, precision=None):
    return lax.dot_general(a, b, (((1,), (1,)), ((), ())), preferred_element_type=F32, precision=precision)


def _dot_tn(a, b, precision=None):
    return lax.dot_general(a, b, (((0,), (0,)), ((), ())), preferred_element_type=F32, precision=precision)


def _softplus(x):
    return jnp.maximum(x, 0.0) + jnp.log1p(jnp.exp(-jnp.abs(x)))


def _sigmoid(x):
    return 1.0 / (1.0 + jnp.exp(-x))


def _silu(x):
    return x * _sigmoid(x)


def _iota(shape, dim):
    return lax.broadcasted_iota(jnp.int32, shape, dim)


ACT_ID, ACT_LOGSIG, ACT_SIG, ACT_DECAY = 0.0, 1.0, 2.0, 3.0


def _proj_body(x_ref, g_ref, w_ref, sp_ref, *out_refs, widths):
    x = x_ref[...]
    hn = (x * lax.rsqrt(jnp.mean(x * x, axis=-1, keepdims=True) + EPS) * g_ref[...]).astype(BF16)
    off = 0
    for i, (o_ref, n) in enumerate(zip(out_refs, widths)):
        r = _dot(hn, w_ref[:, off:off + n])
        if i == len(widths) - 1:
            z = r + sp_ref[0:1, :]
            mode = sp_ref[1:2, :]
            decay = -jnp.exp(sp_ref[2:3, :]) * _softplus(z)
            r = jnp.where(mode == ACT_LOGSIG, -_softplus(-z),
                          jnp.where(mode == ACT_SIG, _sigmoid(z),
                                    jnp.where(mode == ACT_DECAY, decay, z)))
        o_ref[...] = r.astype(o_ref.dtype)
        off += n


def norm_proj(x, g, w_bf16, small_params, widths, tm):
    m, d = x.shape
    n_total = sum(widths)
    assert w_bf16.shape == (d, n_total) and m % tm == 0
    out_shape = [jax.ShapeDtypeStruct((m, n), F32) for n in widths]
    return pl.pallas_call(
        functools.partial(_proj_body, widths=tuple(widths)),
        grid=(m // tm,),
        in_specs=[pl.BlockSpec((tm, d), lambda i: (i, 0)),
                  pl.BlockSpec((1, d), lambda i: (0, 0)),
                  pl.BlockSpec((d, n_total), lambda i: (0, 0)),
                  pl.BlockSpec((8, LANE), lambda i: (0, 0))],
        out_specs=[pl.BlockSpec((tm, n), lambda i: (i, 0)) for n in widths],
        out_shape=out_shape,
        compiler_params=_cparams(("parallel",)),
        name="norm_proj",
    )(x, g.reshape(1, d), w_bf16, small_params)


def _outproj_body(x_ref, a1_ref, a2_ref, w_ref, o_ref):
    k1 = a1_ref.shape[1]
    y = _dot(a1_ref[...].astype(BF16), w_ref[0:k1, :]) + _dot(a2_ref[...].astype(BF16), w_ref[k1:, :])
    o_ref[...] = x_ref[...] + y


def out_proj_residual(x, a1, a2, w_bf16, tm):
    m, d = x.shape
    k1, k2 = a1.shape[1], a2.shape[1]
    return pl.pallas_call(
        _outproj_body,
        grid=(m // tm,),
        in_specs=[pl.BlockSpec((tm, d), lambda i: (i, 0)),
                  pl.BlockSpec((tm, k1), lambda i: (i, 0)),
                  pl.BlockSpec((tm, k2), lambda i: (i, 0)),
                  pl.BlockSpec((k1 + k2, d), lambda i: (0, 0))],
        out_specs=pl.BlockSpec((tm, d), lambda i: (i, 0)),
        out_shape=jax.ShapeDtypeStruct((m, d), F32),
        compiler_params=_cparams(("parallel",)),
        name="out_proj",
    )(x, a1, a2, w_bf16)


def _mlp_body(x_ref, g_ref, wu_ref, wd_ref, gf_ref, o_ref, hn_scr, acc_scr, *, final_norm):
    j = pl.program_id(1)

    @pl.when(j == 0)
    def _():
        x = x_ref[...]
        hn_scr[...] = (x * lax.rsqrt(jnp.mean(x * x, axis=-1, keepdims=True) + EPS) * g_ref[...]).astype(BF16)
        acc_scr[...] = jnp.zeros_like(acc_scr)

    u = jnp.maximum(_dot(hn_scr[...], wu_ref[...]), 0.0)
    acc_scr[...] += _dot((u * u).astype(BF16), wd_ref[...])

    @pl.when(j == pl.num_programs(1) - 1)
    def _():
        y = x_ref[...] + acc_scr[...]
        if final_norm:
            y = y * lax.rsqrt(jnp.mean(y * y, axis=-1, keepdims=True) + EPS) * gf_ref[...]
        o_ref[...] = y


def mlp_residual(x, g, w_up_bf16, w_down_bf16, g_final, final_norm, tm, tf):
    m, d = x.shape
    f = w_up_bf16.shape[1]
    return pl.pallas_call(
        functools.partial(_mlp_body, final_norm=final_norm),
        grid=(m // tm, f // tf),
        in_specs=[pl.BlockSpec((tm, d), lambda i, j: (i, 0)),
                  pl.BlockSpec((1, d), lambda i, j: (0, 0)),
                  pl.BlockSpec((d, tf), lambda i, j: (0, j)),
                  pl.BlockSpec((tf, d), lambda i, j: (j, 0)),
                  pl.BlockSpec((1, d), lambda i, j: (0, 0))],
        out_specs=pl.BlockSpec((tm, d), lambda i, j: (i, 0)),
        out_shape=jax.ShapeDtypeStruct((m, d), F32),
        scratch_shapes=[pltpu.VMEM((tm, d), BF16), pltpu.VMEM((tm, d), F32)],
        compiler_params=_cparams(("parallel", "arbitrary")),
        name="mlp",
    )(x, g.reshape(1, d), w_up_bf16, w_down_bf16, g_final.reshape(1, d))


def _tri(n):
    r = _iota((n, n), 0)
    c = _iota((n, n), 1)
    return r, c


def _bf16_terms(a, n):
    terms, rest = [], a
    for _ in range(n):
        t = rest.astype(BF16)
        terms.append(t)
        rest = rest - t.astype(F32)
    return terms


def _dot_split(a, b, f=None):
    f = f or _dot
    a_hi, a_lo = _bf16_terms(a, 2)
    b_hi, b_lo = _bf16_terms(b, 2)
    return f(a_hi, b_hi) + (f(a_hi, b_lo) + f(a_lo, b_hi))


def _dot_pick(a, b, f=None, exact_lhs=False):
    f = f or _dot
    if exact_lhs:
        return sum(f(a.astype(BF16), t) for t in _bf16_terms(b, 3))
    return sum(f(t, b.astype(BF16)) for t in _bf16_terms(a, 3))


def _chunk_dots(L):
    if L % 16 == 0:
        cast = lambda f: (lambda a, b: f(a.astype(BF16), b.astype(BF16)))
        return cast(_dot), cast(_dot_nt), cast(_dot_tn)
    full = lambda f: (lambda a, b: _dot_split(a, b, f))
    return full(_dot), full(_dot_nt), full(_dot_tn)


def _mlstm_group(bbs, q_ref, k_ref, v_ref, o_ref, sc_ref, sr_ref, nw_ref, h_ref, c_ref, n_ref, m_ref, L):
    nn, nt, tn = _chunk_dots(L)
    rows, cols = _tri(L)
    lower = rows >= cols
    tril = lower.astype(F32)
    triu = (rows <= cols).astype(F32)
    lane4 = _iota((1, MI_H), 1)
    scs = {bb: sc_ref[bb] for bb in bbs}
    srs = {bb: sr_ref[bb, 0] for bb in bbs}
    b_col = {bb: _dot_pick(tril, scs[bb][:, 4:8], exact_lhs=True) for bb in bbs}
    b_row = {bb: _dot_pick(srs[bb][4:8, :], triu) for bb in bbs}
    m_vec = {bb: m_ref[bb] for bb in bbs}
    chains = [(bb, h) for bb in bbs for h in range(MI_H)]
    q = {c: q_ref[c[0], c[1]] for c in chains}
    k = {c: k_ref[c[0], c[1]] * (MI_DQK ** -0.5) for c in chains}
    v = {c: v_ref[c[0], :, c[1] * MI_DV:(c[1] + 1) * MI_DV] for c in chains}
    qk = {c: nt(q[c], k[c]) for c in chains}
    c_prev = {c: c_ref[c[0], c[1]] for c in chains}
    qc = {c: nn(q[c], c_prev[c]) for c in chains}
    s, a_inter, m_t, m_new, a_prev, kw = {}, {}, {}, {}, {}, {}
    for c in chains:
        bb, h = c
        bc = b_col[bb][:, h:h + 1]
        m_prev = m_vec[bb][:, h:h + 1]
        d = jnp.where(lower, bc - b_row[bb][h:h + 1, :] + srs[bb][h:h + 1, :], NEG_BIG)
        inter = bc + m_prev
        m_t[c] = jnp.maximum(inter, jnp.max(d, axis=1, keepdims=True))
        s[c] = qk[c] * jnp.exp(d - m_t[c])
        a_inter[c] = jnp.exp(inter - m_t[c])
        b_last = bc[L - 1:L, :]
        g_col = b_last - bc + scs[bb][:, h:h + 1]
        m_new[c] = jnp.maximum(b_last + m_prev, jnp.max(g_col, axis=0, keepdims=True))
        a_prev[c] = jnp.exp(b_last + m_prev - m_new[c])
        kw[c] = k[c] * jnp.exp(g_col - m_new[c])
    sv = {c: nn(s[c], v[c]) for c in chains}
    kv = {c: tn(kw[c], v[c]) for c in chains}
    for c in chains:
        bb, h = c
        n_prev = n_ref[bb, h]
        num = sv[c] + a_inter[c] * qc[c]
        den = (jnp.sum(s[c], axis=1, keepdims=True)
               + a_inter[c] * jnp.sum(q[c] * n_prev, axis=1, keepdims=True))
        hh = num / jnp.maximum(jnp.abs(den), jnp.exp(-m_t[c]))
        c_ref[bb, h] = a_prev[c] * c_prev[c] + kv[c]
        n_ref[bb, h] = a_prev[c] * n_prev + jnp.sum(kw[c], axis=0, keepdims=True)
        m_vec[bb] = jnp.where(lane4 == h, m_new[c], m_vec[bb])
        hn = hh * lax.rsqrt(jnp.mean(hh * hh, axis=-1, keepdims=True) + EPS) * nw_ref[h:h + 1, :]
        gate = _sigmoid(o_ref[bb, :, h * MI_DV:(h + 1) * MI_DV])
        h_ref[bb, :, h * MI_DV:(h + 1) * MI_DV] = hn * gate
    for bb in bbs:
        m_ref[bb] = m_vec[bb]


def _mlstm_body(q_ref, k_ref, v_ref, o_ref, sc_ref, sr_ref, c0_ref, n0_ref, m0_ref, nw_ref,
                h_ref, c_ref, n_ref, m_ref, *, L, Bb):
    @pl.when(pl.program_id(1) == 0)
    def _():
        c_ref[...] = c0_ref[...]
        n_ref[...] = n0_ref[...]
        m_ref[...] = m0_ref[...]

    _mlstm_group(list(range(Bb)), q_ref, k_ref, v_ref, o_ref, sc_ref, sr_ref, nw_ref, h_ref, c_ref, n_ref, m_ref, L)


def mlstm(q, k, v, o, sc, sr, c0, n0, m0, norm_w, L, Bb):
    B, H, T, _ = q.shape
    nc = T // L
    hv = H * MI_DV
    bmap = lambda b, c: (b, 0, 0, 0)
    return pl.pallas_call(
        functools.partial(_mlstm_body, L=L, Bb=Bb),
        grid=(B // Bb, nc),
        in_specs=[pl.BlockSpec((Bb, H, L, MI_DQK), lambda b, c: (b, 0, c, 0)),
                  pl.BlockSpec((Bb, H, L, MI_DQK), lambda b, c: (b, 0, c, 0)),
                  pl.BlockSpec((Bb, L, hv), lambda b, c: (b, c, 0)),
                  pl.BlockSpec((Bb, L, hv), lambda b, c: (b, c, 0)),
                  pl.BlockSpec((Bb, L, LANE), lambda b, c: (b, c, 0)),
                  pl.BlockSpec((Bb, 1, 8, L), lambda b, c: (b, c, 0, 0)),
                  pl.BlockSpec((Bb, H, MI_DQK, MI_DV), bmap),
                  pl.BlockSpec((Bb, H, 1, MI_DQK), bmap),
                  pl.BlockSpec((Bb, 1, H), lambda b, c: (b, 0, 0)),
                  pl.BlockSpec((H, MI_DV), lambda b, c: (0, 0))],
        out_specs=[pl.BlockSpec((Bb, L, hv), lambda b, c: (b, c, 0)),
                   pl.BlockSpec((Bb, H, MI_DQK, MI_DV), bmap),
                   pl.BlockSpec((Bb, H, 1, MI_DQK), bmap),
                   pl.BlockSpec((Bb, 1, H), lambda b, c: (b, 0, 0))],
        out_shape=[jax.ShapeDtypeStruct((B, T, hv), F32),
                   jax.ShapeDtypeStruct((B, H, MI_DQK, MI_DV), F32),
                   jax.ShapeDtypeStruct((B, H, 1, MI_DQK), F32),
                   jax.ShapeDtypeStruct((B, 1, H), F32)],
        compiler_params=_cparams(("parallel", "arbitrary")),
        name="mlstm",
    )(q, k, v, o, sc, sr, c0, n0, m0, norm_w)


def _gdn_group(bbs, x_ref, z_ref, sc_ref, sr_ref, cw_ref, nw_ref, o_ref, s_ref, conv_ref, xp_scr, L):
    base = 8 - (GD_CONV - 1)
    nn, nt, tn = _chunk_dots(L)
    rows, cols = _tri(L)
    incl = rows >= cols
    strict = rows > cols
    eye = (rows == cols).astype(F32)
    tril = incl.astype(F32)
    triu = (rows <= cols).astype(F32)
    kw = GD_H * GD_DK
    ys, scs, gcols, grows = {}, {}, {}, {}
    for bb in bbs:
        xp_scr[bb, 8:8 + L, :] = x_ref[bb]
        y = xp_scr[bb, base:base + L, :] * cw_ref[0:1, :]
        for j in range(1, GD_CONV):
            y = y + xp_scr[bb, base + j:base + j + L, :] * cw_ref[j:j + 1, :]
        tail = xp_scr[bb, L + base:L + 8, :]
        xp_scr[bb, base:8, :] = tail
        conv_ref[bb] = tail
        ys[bb] = _silu(y)
        scs[bb] = sc_ref[bb]
        gcols[bb] = _dot_pick(tril, scs[bb][:, 12:16], exact_lhs=True)
        grows[bb] = _dot_pick(sr_ref[bb, 0][0:4, :], triu)
    chains = [(bb, h) for bb in bbs for h in range(GD_H)]
    q, k, vb, kb, dec, gcc, egc = {}, {}, {}, {}, {}, {}, {}
    for c in chains:
        bb, h = c
        y = ys[bb]
        qh = y[:, h * GD_DK:(h + 1) * GD_DK]
        kh = y[:, kw + h * GD_DK:kw + (h + 1) * GD_DK]
        vh = y[:, 2 * kw + h * GD_DV:2 * kw + (h + 1) * GD_DV]
        q[c] = qh * lax.rsqrt(jnp.sum(qh * qh, axis=-1, keepdims=True) + EPS) * (GD_DK ** -0.5)
        k[c] = kh * lax.rsqrt(jnp.sum(kh * kh, axis=-1, keepdims=True) + EPS)
        beta = scs[bb][:, 8 + h:9 + h]
        gcc[c] = gcols[bb][:, h:h + 1]
        dec[c] = jnp.exp(jnp.where(incl, gcc[c] - grows[bb][h:h + 1, :], NEG_BIG))
        egc[c] = jnp.exp(gcc[c])
        kb[c] = k[c] * beta
        vb[c] = vh * beta
    pw = {c: -(nt(kb[c], k[c]) * jnp.where(strict, dec[c], 0.0)) for c in chains}
    attn = {c: nt(q[c], k[c]) * dec[c] for c in chains}
    tinv = {c: eye + pw[c] for c in chains}
    for _ in range(int(math.log2(L)) - 1):
        pw = {c: _dot_split(pw[c], pw[c]) for c in chains}
        tinv = {c: tinv[c] + _dot_split(tinv[c], pw[c]) for c in chains}
    u = {c: _dot_split(tinv[c], vb[c]) for c in chains}
    w = {c: _dot_split(tinv[c], kb[c] * egc[c]) for c in chains}
    s_prev = {c: s_ref[c[0], c[1]] for c in chains}
    v_new = {c: u[c] - nn(w[c], s_prev[c]) for c in chains}
    o = {c: nn(q[c] * egc[c], s_prev[c]) + nn(attn[c], v_new[c]) for c in chains}
    for c in chains:
        bb, h = c
        g_last = gcc[c][L - 1:L, :]
        s_ref[bb, h] = jnp.exp(g_last) * s_prev[c] + tn(k[c] * jnp.exp(g_last - gcc[c]), v_new[c])
        on = o[c] * lax.rsqrt(jnp.mean(o[c] * o[c], axis=-1, keepdims=True) + EPS) * nw_ref[...]
        o_ref[bb, :, h * GD_DV:(h + 1) * GD_DV] = on * _silu(z_ref[bb, :, h * GD_DV:(h + 1) * GD_DV])


def _gdn_body(x_ref, z_ref, sc_ref, sr_ref, s0_ref, conv0_ref, cw_ref, nw_ref,
              o_ref, s_ref, conv_ref, xp_scr, *, L, Bb):
    @pl.when(pl.program_id(1) == 0)
    def _():
        s_ref[...] = s0_ref[...]
        xp_scr[:, 8 - (GD_CONV - 1):8, :] = conv0_ref[...]

    _gdn_group(list(range(Bb)), x_ref, z_ref, sc_ref, sr_ref, cw_ref, nw_ref, o_ref, s_ref, conv_ref, xp_scr, L)


def gdn(x, z, sc, sr, s0, conv0, conv_w, norm_w, L, Bb):
    B, T, ch = x.shape
    H = GD_H
    nc = T // L
    hv = H * GD_DV
    bmap = lambda b, c: (b, 0, 0, 0)
    return pl.pallas_call(
        functools.partial(_gdn_body, L=L, Bb=Bb),
        grid=(B // Bb, nc),
        in_specs=[pl.BlockSpec((Bb, L, ch), lambda b, c: (b, c, 0)),
                  pl.BlockSpec((Bb, L, hv), lambda b, c: (b, c, 0)),
                  pl.BlockSpec((Bb, L, LANE), lambda b, c: (b, c, 0)),
                  pl.BlockSpec((Bb, 1, 8, L), lambda b, c: (b, c, 0, 0)),
                  pl.BlockSpec((Bb, H, GD_DK, GD_DV), bmap),
                  pl.BlockSpec((Bb, GD_CONV - 1, ch), lambda b, c: (b, 0, 0)),
                  pl.BlockSpec((GD_CONV, ch), lambda b, c: (0, 0)),
                  pl.BlockSpec((1, GD_DV), lambda b, c: (0, 0))],
        out_specs=[pl.BlockSpec((Bb, L, hv), lambda b, c: (b, c, 0)),
                   pl.BlockSpec((Bb, H, GD_DK, GD_DV), bmap),
                   pl.BlockSpec((Bb, GD_CONV - 1, ch), lambda b, c: (b, 0, 0))],
        out_shape=[jax.ShapeDtypeStruct((B, T, hv), F32),
                   jax.ShapeDtypeStruct((B, H, GD_DK, GD_DV), F32),
                   jax.ShapeDtypeStruct((B, GD_CONV - 1, ch), F32)],
        scratch_shapes=[pltpu.VMEM((Bb, L + 8, ch), F32)],
        compiler_params=_cparams(("parallel", "arbitrary")),
        name="gdn",
    )(x, z, sc, sr, s0, conv0, conv_w, norm_w.reshape(1, GD_DV))


def _compress_body(x_ref, pos_ref, w1_ref, w2_ref, o_ref):
    Bb, _, R, half = x_ref.shape
    x = x_ref[:, 0].reshape(Bb * R, half).astype(F32)
    ua = _dot((x + pos_ref[0, 0:1, :]).astype(BF16), w1_ref[0, 0])
    ub = _dot((x + pos_ref[0, 1:2, :]).astype(BF16), w1_ref[0, 1])
    h = _silu(ua + pltpu.roll(ub, Bb * R - 1, 0))
    o_ref[:, 0, 0:R, :] = _dot(h.astype(BF16), w2_ref[0]).reshape(Bb, R, HD)
    rp = o_ref.shape[2]
    if rp > R:
        o_ref[:, 0, R:rp, :] = jnp.zeros((Bb, rp - R, HD), F32)


def nsa_compress(xr, R, pos, w1, w2, Bb):
    B = xr.shape[0]
    half = CMP_STRIDE * HD
    rp = -(-R // LANE) * LANE
    return pl.pallas_call(
        _compress_body,
        grid=(4, B // Bb),
        in_specs=[pl.BlockSpec((Bb, 1, R, half), lambda c, b: (b, c, 0, 0)),
                  pl.BlockSpec((1, 2, half), lambda c, b: (c // 2, 0, 0)),
                  pl.BlockSpec((1, 2, half, CMP_HIDDEN), lambda c, b: (c // 2, 0, 0, 0)),
                  pl.BlockSpec((1, CMP_HIDDEN, HD), lambda c, b: (c // 2, 0, 0))],
        out_specs=pl.BlockSpec((Bb, 1, rp, HD), lambda c, b: (b, c, 0, 0)),
        out_shape=jax.ShapeDtypeStruct((B, 4, rp, HD), F32),
        compiler_params=_cparams(("parallel", "parallel")),
        name="nsa_compress",
    )(xr, pos.reshape(2, 2, half), w1.reshape(2, 2, half, CMP_HIDDEN).astype(BF16), w2.astype(BF16))


def _cumsum_body(x_ref, o_ref):
    Bb, H, T = x_ref.shape
    rows, cols = _tri(LANE)
    triu = (rows <= cols).astype(F32)
    carry = jnp.zeros((Bb * H, 1), F32)
    for c in range(T // LANE):
        seg = x_ref[:, :, c * LANE:(c + 1) * LANE].reshape(Bb * H, LANE)
        loc = _dot_pick(seg, triu) + carry
        o_ref[:, :, c * LANE:(c + 1) * LANE] = loc.reshape(Bb, H, LANE)
        carry = loc[:, LANE - 1:LANE]


def cumsum_lanes(x, Bb):
    B, H, T = x.shape
    return pl.pallas_call(
        _cumsum_body,
        grid=(B // Bb,),
        in_specs=[pl.BlockSpec((Bb, H, T), lambda b: (b, 0, 0))],
        out_specs=pl.BlockSpec((Bb, H, T), lambda b: (b, 0, 0)),
        out_shape=jax.ShapeDtypeStruct((B, H, T), F32),
        compiler_params=_cparams(("parallel",)),
        name="cumsum",
    )(x)


def _flash_tile(s_blocks, v, m_scr, l_scr, acc_scr, exp=jnp.exp):
    dv = acc_scr.shape[-1]
    m_prev = m_scr[...]
    mx = s_blocks[0]
    for sb in s_blocks[1:]:
        mx = jnp.maximum(mx, sb)
    m_new = jnp.maximum(m_prev, jnp.max(mx, axis=1, keepdims=True))
    p_blocks = [exp(sb - m_new) for sb in s_blocks]
    sm = p_blocks[0]
    for pb in p_blocks[1:]:
        sm = sm + pb
    alpha = exp(m_prev - m_new)
    l_scr[...] = alpha * l_scr[...] + jnp.sum(sm, axis=1, keepdims=True)
    p = (jnp.concatenate(p_blocks, axis=1) if len(p_blocks) > 1 else p_blocks[0]).astype(BF16)
    acc_scr[...] = alpha[:, :dv] * acc_scr[...] + _dot(p, v)
    m_scr[...] = m_new


def _flash_tiles(items, exp=jnp.exp):
    m_prev = [m[...] for _, _, m, _, _ in items]
    m_new = []
    for (s_blocks, _, _, _, _), mp in zip(items, m_prev):
        mx = s_blocks[0]
        for sb in s_blocks[1:]:
            mx = jnp.maximum(mx, sb)
        m_new.append(jnp.maximum(mp, jnp.max(mx, axis=1, keepdims=True)))
    p_blocks = [[exp(sb - mn) for sb in it[0]] for it, mn in zip(items, m_new)]
    sums = []
    for pb in p_blocks:
        sm = pb[0]
        for x in pb[1:]:
            sm = sm + x
        sums.append(jnp.sum(sm, axis=1, keepdims=True))
    pv = [_dot((jnp.concatenate(pb, axis=1) if len(pb) > 1 else pb[0]).astype(BF16), it[1])
          for pb, it in zip(p_blocks, items)]
    for (_, _, m_ref, l_ref, acc_ref), mp, mn, sm, x in zip(items, m_prev, m_new, sums, pv):
        alpha = exp(mp - mn)
        l_ref[...] = alpha * l_ref[...] + sm
        acc_ref[...] = alpha[:, :acc_ref.shape[-1]] * acc_ref[...] + x
        m_ref[...] = mn


def _flash_reset(m_scr, l_scr, acc_scr):
    m_scr[...] = jnp.full_like(m_scr, NEG_BIG)
    l_scr[...] = jnp.zeros_like(l_scr)
    acc_scr[...] = jnp.zeros_like(acc_scr)


def _lane_blocks(s):
    return [s[:, i * LANE:(i + 1) * LANE] for i in range(s.shape[1] // LANE)]


def _fox_body(q_ref, k_ref, v_ref, o_ref, m_scr, l_scr, acc_scr, *, tq, tk):
    qi = pl.program_id(1)
    j = pl.program_id(2)
    top = (qi * tq + tq - 1) // tk

    @pl.when(j == 0)
    def _():
        _flash_reset(m_scr, l_scr, acc_scr)

    def tile(diag):
        if diag:
            mask = (top * tk + _iota((tq, tk), 1)) <= (qi * tq + _iota((tq, tk), 0))
        items = []
        for h in range(FOX_H):
            s = _dot_nt(q_ref[0, h], k_ref[0, h])
            if diag:
                s = jnp.where(mask, s, NEG_BIG)
            items.append((_lane_blocks(s), v_ref[0, h], m_scr.at[h], l_scr.at[h], acc_scr.at[h]))
        _flash_tiles(items, exp=jnp.exp2)

    @pl.when(j == 0)
    def _():
        tile(True)

    @pl.when(jnp.logical_and(j > 0, j <= top))
    def _():
        tile(False)

    @pl.when(j == pl.num_programs(2) - 1)
    def _():
        for h in range(FOX_H):
            o_ref[0, h] = acc_scr[h] / jnp.maximum(l_scr[h][:, :HD], 1e-30)


LOG2E = 1.0 / math.log(2.0)


def fox_operands(q_heads, k_heads, f_heads):
    def top_bits(x):
        return lax.bitcast_convert_type(lax.bitcast_convert_type(x, jnp.uint32) & jnp.uint32(0xFFFF0000), F32)

    terms, rest = [], f_heads * LOG2E
    for _ in range(3):
        t = top_bits(rest)
        terms.append(t.astype(BF16))
        rest = rest - t
    f_terms = jnp.stack(terms, axis=-1)
    ones = jnp.ones_like(f_terms)
    pad = jnp.zeros(q_heads.shape[:3] + (LANE - HD - 6,), BF16)
    q_aug = jnp.concatenate([(q_heads * (HD ** -0.5 * LOG2E)).astype(BF16), f_terms, ones, pad], axis=-1)
    k_aug = jnp.concatenate([k_heads.astype(BF16), ones, -f_terms, pad], axis=-1)
    return q_aug, k_aug


def fox_prompt(q_aug, k_aug, kv, tq, tk):
    B, H, T, _ = q_aug.shape
    kmap = lambda i, j: jnp.maximum((i * tq + tq - 1) // tk - j, 0)
    return pl.pallas_call(
        functools.partial(_fox_body, tq=tq, tk=tk),
        grid=(B, T // tq, T // tk),
        in_specs=[pl.BlockSpec((1, H, tq, LANE), lambda b, i, j: (b, 0, i, 0)),
                  pl.BlockSpec((1, H, tk, LANE), lambda b, i, j: (b, 0, kmap(i, j), 0)),
                  pl.BlockSpec((1, H, tk, HD), lambda b, i, j: (b, 1, kmap(i, j), 0))],
        out_specs=pl.BlockSpec((1, H, tq, HD), lambda b, i, j: (b, 0, i, 0)),
        out_shape=jax.ShapeDtypeStruct((B, H, T, HD), F32),
        scratch_shapes=[pltpu.VMEM((H, tq, LANE), F32), pltpu.VMEM((H, tq, LANE), F32), pltpu.VMEM((H, tq, HD), F32)],
        compiler_params=_cparams(("parallel", "parallel", "arbitrary")),
        name="fox_prompt",
    )(q_aug, k_aug, kv)


def _fox_decode_body(pt_ref, q_ref, newkv_ref, newlf_ref, *refs, n_pages, tn):
    kv_refs = refs[:n_pages]
    lf_refs = refs[n_pages:2 * n_pages]
    o_ref = refs[2 * n_pages]
    hw = FOX_H * HD
    R = FOX_H * tn
    q = q_ref[0]
    qrep = jnp.concatenate([q] * FOX_H, axis=0)
    blockmask = (_iota((R, hw), 0) // tn) == (_iota((R, hw), 1) // HD)
    qbd = jnp.where(blockmask, qrep, 0.0).astype(BF16)
    rows, cols = _tri(PAGE)
    triu = (rows <= cols).astype(F32)
    ones = jnp.ones((8, PAGE), F32)
    expand = lambda a: jnp.concatenate([jnp.broadcast_to(a[h:h + 1, :], (tn, a.shape[1])) for h in range(FOX_H)], 0)
    carry_c = jnp.zeros((FOX_H, 1), F32)
    carry_r = jnp.zeros((1, FOX_H), F32)
    s_tiles = []
    for pg in range(n_pages):
        lf = lf_refs[pg][0]
        f_t = _dot_pick(lf, triu) + carry_c
        carry_c = f_t[:, PAGE - 1:PAGE]
        carry_r = carry_r + _dot_pick(ones, lf, _dot_nt, exact_lhs=True)[0:1, :]
        s_tiles.append(_dot(qbd, kv_refs[pg][0, 0:hw, :].astype(BF16)) - expand(f_t))
    lfn = newlf_ref[0]
    r8, c8 = _tri(tn)
    fq_c = _dot_pick((r8 >= c8).astype(F32), lfn, exact_lhs=True) + carry_r
    fq_t = _dot_pick(lfn, (r8 <= c8).astype(F32), _dot_tn) + carry_c
    fq_rows = jnp.concatenate([fq_c[:, h:h + 1] for h in range(FOX_H)], axis=0)
    kn = newkv_ref[0, :, 0:hw]
    s_new = _dot_split(qbd.astype(F32), kn, _dot_nt) - expand(fq_t)
    causal = _iota((R, tn), 1) <= (_iota((R, tn), 0) % tn)
    s_new = jnp.where(causal, s_new + fq_rows, NEG_BIG)
    s_tiles = [s + fq_rows for s in s_tiles]
    m = jnp.max(s_new, axis=1, keepdims=True)
    for s in s_tiles:
        m = jnp.maximum(m, jnp.max(s, axis=1, keepdims=True))
    p_new = jnp.where(causal, jnp.exp(s_new - m), 0.0)
    l = jnp.sum(p_new, axis=1, keepdims=True)
    acc = _dot_split(p_new, newkv_ref[0, :, hw:2 * hw])
    for pg, s in enumerate(s_tiles):
        p = jnp.exp(s - m)
        l = l + jnp.sum(p, axis=1, keepdims=True)
        acc = acc + _dot_nt(p.astype(BF16), kv_refs[pg][0, hw:2 * hw, :].astype(BF16))
    acc = acc / jnp.maximum(l, 1e-30)
    o_ref[0] = jnp.concatenate([acc[h * tn:(h + 1) * tn, h * HD:(h + 1) * HD] for h in range(FOX_H)], axis=1)


def fox_decode(page_table, q, newkv, newlf, kv_pool_t, lf_pool_t):
    B, tn, hw = q.shape
    n_pages = page_table.shape[1]
    page_spec = lambda rows, pg: pl.BlockSpec((1, rows, PAGE), lambda b, pt: (pt[b, pg], 0, 0))
    grid_spec = pltpu.PrefetchScalarGridSpec(
        num_scalar_prefetch=1,
        grid=(B,),
        in_specs=[pl.BlockSpec((1, tn, hw), lambda b, pt: (b, 0, 0)),
                  pl.BlockSpec((1, tn, 2 * hw), lambda b, pt: (b, 0, 0)),
                  pl.BlockSpec((1, tn, FOX_H), lambda b, pt: (b, 0, 0))]
                 + [page_spec(2 * hw, pg) for pg in range(n_pages)]
                 + [page_spec(FOX_H, pg) for pg in range(n_pages)],
        out_specs=pl.BlockSpec((1, tn, hw), lambda b, pt: (b, 0, 0)),
    )
    return pl.pallas_call(
        functools.partial(_fox_decode_body, n_pages=n_pages, tn=tn),
        grid_spec=grid_spec,
        out_shape=jax.ShapeDtypeStruct((B, tn, hw), F32),
        compiler_params=_cparams(("arbitrary",)),
        name="fox_decode",
    )(page_table, q, newkv, newlf, *([kv_pool_t] * n_pages), *([lf_pool_t] * n_pages))


def _t5_bucket(dist):
    n = jnp.maximum(dist, 0)
    nf = jnp.maximum(n, 1).astype(F32)
    large = BUCKET_EXACT + (jnp.log(nf / BUCKET_EXACT) / math.log(MAX_DISTANCE / BUCKET_EXACT)
                            * (N_BUCKETS - BUCKET_EXACT)).astype(jnp.int32)
    return jnp.where(n < BUCKET_EXACT, n, jnp.minimum(large, N_BUCKETS - 1))


def _bias_from_bucket(bucket, tbl_ref, head):
    out = jnp.zeros(bucket.shape, F32)
    for kk in range(N_BUCKETS):
        out = jnp.where(bucket == kk, tbl_ref[kk, head], out)
    return out


FOX_TQ, FOX_TK = 256, 512
RECURRENT_BATCH = 4
NSA_TQ = 2 * LANE
NSA_TK = 2 * LANE
NSA_FAR_GROUP = 4
CMP_PAT_CENTER = LANE // 2
NSA_BIAS_TILES = (BUCKET_SAT_DIST + NSA_TK + LANE - 1) // LANE


def _nsa_body(tbl_ref, q_ref, gate_ref, kc_ref, vc_ref, ks_ref, vs_ref, kw_ref, vw_ref, smap_ref, o_ref,
              bias_scr, pat_scr, score_scr, m_scr, l_scr, acc_scr, *, tq, q_pos0, win_pos0, n_sel, ncp, tw, single):
    g = pl.program_id(1)
    qi = pl.program_id(2)
    q0 = q_pos0 if single else q_pos0 + qi * tq
    aligned = (lambda x, m: x) if single else pl.multiple_of
    lo, hi = (max, min) if single else (jnp.maximum, jnp.minimum)
    R = NSA_R * tq
    scale = HD ** -0.5
    last_bias = tuple(tbl_ref[N_BUCKETS - 1, g * NSA_R + r] for r in range(NSA_R))

    @pl.when(qi == 0)
    def _():
        ii = _iota((tq, NSA_TK), 0)
        jj = _iota((tq, NSA_TK), 1)
        for dd in range(NSA_BIAS_TILES):
            bucket = _t5_bucket(ii - jj + dd * LANE)
            for r in range(NSA_R):
                bias_scr[dd, r * tq:(r + 1) * tq, :] = (
                    _bias_from_bucket(bucket, tbl_ref, g * NSA_R + r) - last_bias[r])
        bias_scr[NSA_BIAS_TILES] = jnp.zeros((R, NSA_TK), F32)
        rel_blk = _iota((tq, LANE), 1) - CMP_PAT_CENTER
        bucket = _t5_bucket(_iota((tq, LANE), 0) - rel_blk * CMP_STRIDE - (CMP_BLOCK - 1))
        for r in range(NSA_R):
            pat_scr[r] = _bias_from_bucket(bucket, tbl_ref, g * NSA_R + r) - last_bias[r]

    t_col = q0 + _iota((tq, 1), 0)

    bias_tiles, mask_tiles = [], []
    for nt in range(ncp // LANE):
        c_end = (nt * LANE + _iota((tq, LANE), 1)) * CMP_STRIDE + (CMP_BLOCK - 1)
        dist = t_col - c_end
        max_dist = q0 + tq - 1 - (nt * LANE * CMP_STRIDE + CMP_BLOCK - 1)
        min_dist = q0 - ((nt * LANE + LANE - 1) * CMP_STRIDE + CMP_BLOCK - 1)
        special = jnp.logical_and(max_dist >= 0, min_dist < BUCKET_SAT_DIST)

        off = nt * LANE - q0 // CMP_STRIDE + CMP_PAT_CENTER

        def general(off=off):
            lanes = _iota((tq, LANE), 1) + off
            inside = jnp.logical_and(lanes >= 0, lanes < LANE)
            shift = (LANE - off % LANE) % LANE
            return jnp.stack([jnp.where(inside, pltpu.roll(pat_scr[r], shift, 1), 0.0) for r in range(NSA_R)])

        def saturated():
            return jnp.zeros((NSA_R, tq, LANE), F32)

        bias_tiles.append(lax.cond(special, general, saturated))
        mask_tiles.append(dist >= 0)
    mask_c = jnp.concatenate(mask_tiles, axis=1) if len(mask_tiles) > 1 else mask_tiles[0]
    kc = kc_ref[0, 0].astype(BF16)
    vc = vc_ref[0, 0].astype(BF16)
    pcsum = jnp.zeros((tq, ncp), F32)
    qk_c = [_dot_nt((q_ref[0, r] * scale).astype(BF16), kc) for r in range(NSA_R)]
    pcs = []
    for r in range(NSA_R):
        bias_r = jnp.concatenate([b[r] for b in bias_tiles], axis=1) if len(bias_tiles) > 1 else bias_tiles[0][r]
        s = jnp.where(mask_c, qk_c[r] + bias_r, NEG_BIG)
        m = jnp.max(s, axis=1, keepdims=True)
        p = jnp.where(mask_c, jnp.exp(s - m), 0.0)
        pc = p / jnp.maximum(jnp.sum(p, axis=1, keepdims=True), 1e-30)
        pcs.append(pc.astype(BF16))
        pcsum = pcsum + pc
    o_c = [_dot(pc, vc) for pc in pcs]

    ps_t = _dot_pick(smap_ref[...], pcsum, _dot_nt, exact_lhs=True)
    j_col = _iota((LANE, 1), 0)
    t_row = q0 + _iota((1, tq), 1)
    cur = lax.shift_right_logical(t_row, int(math.log2(SEL_BLOCK)))
    score = jnp.where(j_col * SEL_BLOCK <= t_row, ps_t, -1.0)
    score = jnp.where(j_col == cur - 1, FORCE_SCORE, score)
    score = jnp.where(j_col == cur, FORCE_SCORE, score)
    score = jnp.where(j_col == 0, FORCE_SCORE, score)
    score = jnp.where(j_col < n_sel, score, -3e38)
    score_scr[...] = score

    def rank_body(jp, rank):
        row = score_scr[pl.ds(jp, 1), :]
        tie = jnp.where(j_col > jp, 1.0, 0.0)
        return rank + jnp.where(row > score, 1.0, jnp.where(row == score, tie, 0.0))

    n_rank = jnp.minimum((q0 + tq - 1) // SEL_BLOCK + 1, LANE)
    rank = lax.fori_loop(0, n_rank, rank_body, jnp.zeros((LANE, tq), F32))
    sel_t = jnp.where(rank < SEL_TOPN, 1.0, 0.0).astype(BF16)
    eye = (_iota((tq, tq), 0) == _iota((tq, tq), 1)).astype(BF16)
    sel = _dot_nt(eye, sel_t).astype(BF16)

    qs = (q_ref[0].reshape(R, HD) * scale).astype(BF16)
    log2_blk = int(math.log2(SEL_BLOCK))

    def stack(a):
        return jnp.concatenate([a] * NSA_R, axis=0)

    def rel_dist(pos0, width):
        return (q0 + _iota((tq, width), 0)) - (pos0 + _iota((tq, width), 1))

    def near_bias(offset, width):
        tiles = [bias_scr[hi(lo(offset - j * NSA_TK, 0) // LANE, NSA_BIAS_TILES)] for j in range(width // NSA_TK)]
        return jnp.concatenate(tiles, axis=1) if len(tiles) > 1 else tiles[0]

    def sel_tile(k0, width, near):
        k = ks_ref[0, 0, pl.ds(k0, width), :]
        v = vs_ref[0, 0, pl.ds(k0, width), :]
        blk = lax.shift_right_logical(k0 + _iota((LANE, width), 1), log2_blk)
        expand = jnp.where(_iota((LANE, width), 0) == blk, 1.0, 0.0).astype(BF16)
        chosen = _dot(sel, expand)
        s = _dot_nt(qs, k)
        if near:
            chosen = jnp.where(rel_dist(k0, width) >= 0, chosen, 0.0)
            s = s + near_bias(q0 - k0, width)
        s = jnp.where(stack(chosen) > 0.5, s, NEG_BIG)
        _flash_tile(_lane_blocks(s), v, m_scr, l_scr, acc_scr)

    _flash_reset(m_scr, l_scr, acc_scr)
    kt_top = (q0 + tq - 1) // NSA_TK
    near_w = min(2 * NSA_TK, ks_ref.shape[2])
    sel_tile(aligned(lo(kt_top - 1, 0) * NSA_TK, NSA_TK), near_w, True)

    n_far = lo(kt_top - 1, 0)
    n_groups = n_far // NSA_FAR_GROUP

    def sel_far_group(gi, carry):
        sel_tile(pl.multiple_of(gi * (NSA_FAR_GROUP * NSA_TK), NSA_FAR_GROUP * NSA_TK), NSA_FAR_GROUP * NSA_TK, False)
        return carry

    def sel_far(kt, carry):
        sel_tile(pl.multiple_of(kt * NSA_TK, NSA_TK), NSA_TK, False)
        return carry

    if single:
        if n_far:
            sel_tile(0, n_far * NSA_TK, False)
    else:
        lax.fori_loop(0, n_groups, sel_far_group, 0)
        lax.fori_loop(n_groups * NSA_FAR_GROUP, n_far, sel_far, 0)
    o_s = acc_scr[...] / jnp.maximum(l_scr[...][:, :HD], 1e-30)

    n_win = -(-(WINDOW - 1 + tq) // NSA_TK) + (0 if tq % NSA_TK == 0 else 1)
    win_w = min(n_win * NSA_TK, tw)
    wt_top = (hi(q0 + tq - win_pos0, tw) - 1) // NSA_TK
    k0 = aligned(hi(lo(wt_top + 1 - win_w // NSA_TK, 0), (tw - win_w) // NSA_TK) * NSA_TK, NSA_TK)
    _flash_reset(m_scr, l_scr, acc_scr)
    dist = rel_dist(win_pos0 + k0, win_w)
    ok = jnp.where(dist >= 0, jnp.where(dist < WINDOW, 1.0, 0.0), 0.0)
    s = _dot_nt(qs, kw_ref[0, 0, pl.ds(k0, win_w), :]) + near_bias(q0 - win_pos0 - k0, win_w)
    s = jnp.where(stack(ok) > 0.5, s, NEG_BIG)
    _flash_tile(_lane_blocks(s), vw_ref[0, 0, pl.ds(k0, win_w), :], m_scr, l_scr, acc_scr)
    o_w = acc_scr[...] / jnp.maximum(l_scr[...][:, :HD], 1e-30)

    gates = gate_ref[0, 0]
    for r in range(NSA_R):
        o_ref[0, r] = (gates[:, 3 * r:3 * r + 1] * o_c[r]
                       + gates[:, 3 * r + 1:3 * r + 2] * o_s[r * tq:(r + 1) * tq]
                       + gates[:, 3 * r + 2:3 * r + 3] * o_w[r * tq:(r + 1) * tq])


def _selection_overlap_t(ncp):
    c_start = np.arange(ncp)[None, :] * CMP_STRIDE
    s_start = np.arange(LANE)[:, None] * SEL_BLOCK
    return ((c_start < s_start + SEL_BLOCK) & (c_start + CMP_BLOCK > s_start)).astype(np.float32)


def nsa_attend(tbl, q, gates, kcvc, sel_arr, sel_off, win_arr, win_off, *, tq, q_pos0, win_pos0, n_sel):
    B, _, Tq, _ = q.shape
    ncp = kcvc.shape[2]
    tks = sel_arr.shape[2]
    tw = win_arr.shape[2]
    R = NSA_R * tq
    smap = jnp.asarray(_selection_overlap_t(ncp))
    kv_spec = lambda rows, off: pl.BlockSpec((1, 1, rows, HD), lambda b, g, i: (b, off + g, 0, 0))
    return pl.pallas_call(
        functools.partial(_nsa_body, tq=tq, q_pos0=q_pos0, win_pos0=win_pos0, n_sel=n_sel, ncp=ncp, tw=tw,
                          single=(Tq == tq)),
        grid=(B, NSA_G, Tq // tq),
        in_specs=[pl.BlockSpec(memory_space=pltpu.SMEM),
                  pl.BlockSpec((1, NSA_R, tq, HD), lambda b, g, i: (b, g, i, 0)),
                  pl.BlockSpec((1, 1, tq, 3 * NSA_R), lambda b, g, i: (b, g, i, 0)),
                  kv_spec(ncp, 0), kv_spec(ncp, 2),
                  kv_spec(tks, sel_off), kv_spec(tks, sel_off + 2),
                  kv_spec(tw, win_off), kv_spec(tw, win_off + 2),
                  pl.BlockSpec((LANE, ncp), lambda b, g, i: (0, 0))],
        out_specs=pl.BlockSpec((1, NSA_R, tq, HD), lambda b, g, i: (b, g, i, 0)),
        out_shape=jax.ShapeDtypeStruct((B, NSA_G * NSA_R, Tq, HD), F32),
        scratch_shapes=[pltpu.VMEM((NSA_BIAS_TILES + 1, R, NSA_TK), F32), pltpu.VMEM((NSA_R, tq, LANE), F32),
                        pltpu.VMEM((LANE, tq), F32),
                        pltpu.VMEM((R, LANE), F32), pltpu.VMEM((R, LANE), F32), pltpu.VMEM((R, HD), F32)],
        compiler_params=_cparams(("parallel", "parallel", "arbitrary")),
        name="nsa_attend",
    )(tbl, q, gates, kcvc, kcvc, sel_arr, sel_arr, win_arr, win_arr, smap)


def _nsa_gather_body(pt_ref, new_ref, *refs, n_pages):
    pages = refs[:n_pages]
    cmp_ref, sel_ref, xt_scr = refs[n_pages:]
    ngrp = 2 * NSA_G
    half = ngrp * HD
    rows16 = PAGE // CMP_STRIDE
    for pg in range(n_pages):
        xt = pages[pg][0].T
        for j in range(ngrp):
            sel_ref[0, j, pg * PAGE:(pg + 1) * PAGE, :] = xt[:, half + j * HD:half + (j + 1) * HD].astype(BF16)
        for slab in range(half // LANE):
            xt_scr[slab] = xt[:, slab * LANE:(slab + 1) * LANE]
            for p in range(CMP_STRIDE):
                rows = xt_scr[slab, pl.ds(p, rows16, stride=CMP_STRIDE), :]
                for jj in range(LANE // HD):
                    cmp_ref[0, slab * (LANE // HD) + jj, pg * rows16:(pg + 1) * rows16, p * HD:(p + 1) * HD] = (
                        rows[:, jj * HD:(jj + 1) * HD])
    tn = new_ref.shape[1]
    tail = sel_ref.shape[2] - n_pages * PAGE
    for j in range(ngrp):
        new = new_ref[0, :, (ngrp + j) * HD:(ngrp + j + 1) * HD]
        tile = jnp.concatenate([new, jnp.zeros((tail - tn, HD), F32)], axis=0)
        sel_ref[0, j, n_pages * PAGE:, :] = tile.astype(BF16)


def nsa_gather(page_table, new_rows, pool_t):
    B, tn, width = new_rows.shape
    n_pages = page_table.shape[1]
    ngrp = 2 * NSA_G
    sel_rows = -(-(n_pages * PAGE + tn) // NSA_TK) * NSA_TK
    cmp_rows = n_pages * PAGE // CMP_STRIDE
    grid_spec = pltpu.PrefetchScalarGridSpec(
        num_scalar_prefetch=1,
        grid=(B,),
        in_specs=[pl.BlockSpec((1, tn, width), lambda b, pt: (b, 0, 0))]
                 + [pl.BlockSpec((1, 2 * ngrp * HD, PAGE),
                                 functools.partial(lambda b, pt, pg: (pt[b, pg], 0, 0), pg=pg))
                    for pg in range(n_pages)],
        out_specs=[pl.BlockSpec((1, ngrp, cmp_rows, CMP_STRIDE * HD), lambda b, pt: (b, 0, 0, 0)),
                   pl.BlockSpec((1, ngrp, sel_rows, HD), lambda b, pt: (b, 0, 0, 0))],
        scratch_shapes=[pltpu.VMEM((ngrp * HD // LANE, PAGE, LANE), F32)],
    )
    return pl.pallas_call(
        functools.partial(_nsa_gather_body, n_pages=n_pages),
        grid_spec=grid_spec,
        out_shape=[jax.ShapeDtypeStruct((B, ngrp, cmp_rows, CMP_STRIDE * HD), F32),
                   jax.ShapeDtypeStruct((B, ngrp, sel_rows, HD), BF16)],
        compiler_params=_cparams(("arbitrary",)),
        name="nsa_gather",
    )(page_table, new_rows, *([pool_t] * n_pages))


def _row_tile(m):
    return 512 if m % 512 == 0 else m


def _small_params(entries):
    sp = jnp.zeros((8, LANE), F32)
    for off, bias, act, log_scale in entries:
        n = bias.shape[0]
        sp = sp.at[0, off:off + n].set(bias.astype(F32))
        sp = sp.at[1, off:off + n].set(act)
        if log_scale is not None:
            sp = sp.at[2, off:off + n].set(log_scale.astype(F32))
    return sp


EVEN_WIDTHS = (256, 256, 512, 512, GD_CH, 512, LANE)


def _even_weights(w_in):
    s = np.cumsum((0, 256, 256, 512, 512, 4, 4, 512, 512, 512, 512, 4, 4))
    col = lambda i: w_in[:, s[i]:s[i + 1]]
    small = jnp.concatenate([col(4), col(5), col(10), col(11)], axis=1)
    small = jnp.pad(small, ((0, 0), (0, LANE - small.shape[1])))
    return jnp.concatenate([col(0), col(1), col(2), col(3), col(6), col(7), col(8), col(9), small], axis=1).astype(BF16)


def _chunk_rows(small, B, T, L, lanes):
    r = small.reshape(B, T // L, L, LANE)[..., lanes[0]:lanes[1]]
    r = jnp.swapaxes(r, 2, 3)
    return jnp.pad(r, ((0, 0), (0, 0), (0, 8 - r.shape[2]), (0, 0)))


def even_layer(x, p, past, L, Bb):
    B, T, D = x.shape
    M = B * T
    tm = _row_tile(M)
    sp = _small_params([(0, p['mi_b_i'], ACT_ID, None), (4, p['mi_b_f'], ACT_LOGSIG, None),
                        (8, jnp.zeros((4,), F32), ACT_SIG, None), (12, p['gd_dt_bias'], ACT_DECAY, p['gd_a_log'])])
    mq, mk, mv, mo, gx, gz, small = norm_proj(x.reshape(M, D), p['norm_mix'], _even_weights(p['w_in']), sp,
                                              EVEN_WIDTHS, tm)
    heads = lambda a: jnp.transpose(a.reshape(B, T, MI_H, MI_DQK), (0, 2, 1, 3))
    sc = small.reshape(B, T, LANE)
    if past is None:
        c0 = jnp.zeros((B, MI_H, MI_DQK, MI_DV), F32)
        n0 = jnp.zeros((B, MI_H, 1, MI_DQK), F32)
        m0 = jnp.zeros((B, 1, MI_H), F32)
        s0 = jnp.zeros((B, GD_H, GD_DK, GD_DV), F32)
        conv0 = jnp.zeros((B, GD_CONV - 1, GD_CH), F32)
    else:
        c0, n0, m0, s0, conv0 = past
        n0 = n0.reshape(B, MI_H, 1, MI_DQK)
        m0 = m0.reshape(B, 1, MI_H)
    hm, c1, n1, m1 = mlstm(heads(mq), heads(mk), mv.reshape(B, T, -1), mo.reshape(B, T, -1), sc,
                           _chunk_rows(small, B, T, L, (0, 8)), c0, n0, m0,
                           p['mi_norm'].reshape(MI_H, MI_DV), L, Bb)
    og, s1, conv1 = gdn(gx.reshape(B, T, GD_CH), gz.reshape(B, T, -1), sc,
                        _chunk_rows(small, B, T, L, (12, 16)), s0, conv0, p['gd_conv_w'], p['gd_norm'], L, Bb)
    y = out_proj_residual(x.reshape(M, D), hm.reshape(M, -1), og.reshape(M, -1), p['w_out'].astype(BF16), tm)
    return y.reshape(B, T, D), (c1, n1.reshape(B, MI_H, MI_DQK), m1.reshape(B, MI_H), s1, conv1)


NSA_QW = NSA_G * NSA_R * HD
NSA_KVW = 6 * NSA_G * HD
NSA_CACHE_W = 4 * NSA_G * HD
FOX_W = FOX_H * HD
N_GATE = 3 * NSA_G * NSA_R
ODD_WIDTHS = (NSA_QW, NSA_KVW, FOX_W, 2 * FOX_W, LANE)


def _odd_weights(w_in):
    s = np.cumsum((0, NSA_QW, NSA_KVW, N_GATE, FOX_W, FOX_W, FOX_W, FOX_H))
    col = lambda i: w_in[:, s[i]:s[i + 1]]
    small = jnp.concatenate([col(2), col(6)], axis=1)
    small = jnp.pad(small, ((0, 0), (0, LANE - small.shape[1])))
    return jnp.concatenate([col(0), col(1), col(3), col(4), col(5), small], axis=1).astype(BF16)


def _heads(a, B, T, n):
    return jnp.transpose(a.reshape(B, T, n, HD), (0, 2, 1, 3))


def _unheads(a):
    B, n, T, _ = a.shape
    return jnp.transpose(a, (0, 2, 1, 3)).reshape(B * T, n * HD)


def odd_layer(x, p, rel_bias, w_buf, past, page_table):
    B, T, D = x.shape
    M = B * T
    tm = _row_tile(M)
    sp = _small_params([(0, jnp.zeros((N_GATE,), F32), ACT_SIG, None), (N_GATE, p['fox_b_f'], ACT_LOGSIG, None)])
    nq, nkv, fq, fkv, small = norm_proj(x.reshape(M, D), p['norm_mix'], _odd_weights(p['w_in']), sp, ODD_WIDTHS, tm)
    new_nsa = nkv[:, :NSA_CACHE_W].reshape(B, T, 4, NSA_G, HD)
    new_win = nkv[:, NSA_CACHE_W:].reshape(B, T, 2, NSA_G, HD)
    new_fox = fkv.reshape(B, T, 2, FOX_H, HD)
    logf = small[:, N_GATE:N_GATE + FOX_H].reshape(B, T, FOX_H)
    q_heads = _heads(nq, B, T, NSA_G * NSA_R)
    gates = jnp.transpose(small[:, :N_GATE].reshape(B, T, NSA_G, 3 * NSA_R), (0, 2, 1, 3))
    cmp_args = (p['nsa_cmp_pos'], p['nsa_cmp_w1'], p['nsa_cmp_w2'])
    rows16 = CMP_STRIDE * HD
    if past is None:
        groups = _heads(nkv, B, T, 6 * NSA_G)
        kcvc = nsa_compress(groups.reshape(B, 6 * NSA_G, T // CMP_STRIDE, rows16), T // CMP_STRIDE, *cmp_args, 1)
        arr = groups.astype(BF16)
        o_n = nsa_attend(rel_bias, q_heads, gates, kcvc, arr, 2 * NSA_G, arr, 4 * NSA_G,
                         tq=min(T, NSA_TQ), q_pos0=0, win_pos0=0, n_sel=-(-T // SEL_BLOCK))
        f_t = cumsum_lanes(jnp.transpose(logf, (0, 2, 1)), B)
        kv_heads = _heads(fkv, B, T, 2 * FOX_H)
        q_aug, k_aug = fox_operands(_heads(fq, B, T, FOX_H), kv_heads[:, :FOX_H], f_t)
        o_f = fox_prompt(q_aug, k_aug, kv_heads.astype(BF16), min(T, FOX_TQ), min(T, FOX_TK))
        o_f = _unheads(o_f)
        win_prev = jnp.zeros((B, WINDOW, 2, NSA_G, HD), F32)
    else:
        nsa_pool, win_prev, fox_pool, logf_pool = past
        n_pool = nsa_pool.shape[0]
        n_pages = page_table.shape[1]
        start = n_pages * PAGE
        feature_major = lambda pool: jnp.moveaxis(pool.reshape(n_pool, PAGE, -1), 1, 2)
        cmp_rows, sel_arr = nsa_gather(page_table, nkv[:, :NSA_CACHE_W].reshape(B, T, NSA_CACHE_W),
                                       feature_major(nsa_pool))
        kcvc = nsa_compress(cmp_rows, start // CMP_STRIDE, *cmp_args, math.gcd(B, 8))
        wp = win_prev.shape[1]
        win_all = jnp.concatenate([win_prev.reshape(B, wp, 2 * NSA_G * HD), nkv[:, NSA_CACHE_W:].reshape(B, T, -1)], 1)
        tw = -(-(wp + T) // NSA_TK) * NSA_TK
        win_arr = _heads(jnp.pad(win_all, ((0, 0), (0, tw - wp - T), (0, 0))), B, tw, 2 * NSA_G).astype(BF16)
        o_n = nsa_attend(rel_bias, q_heads, gates, kcvc, sel_arr, 0, win_arr, 0,
                         tq=T, q_pos0=start, win_pos0=start - wp, n_sel=-(-(start + T) // SEL_BLOCK))
        o_f = fox_decode(page_table, (fq * HD ** -0.5).reshape(B, T, FOX_W), fkv.reshape(B, T, 2 * FOX_W), logf,
                         feature_major(fox_pool), feature_major(logf_pool))
        o_f = o_f.reshape(M, FOX_W)
    win_state = jnp.concatenate([win_prev, new_win], axis=1)[:, -w_buf:]
    y = out_proj_residual(x.reshape(M, D), _unheads(o_n), o_f, p['w_out'].astype(BF16), tm)
    return y.reshape(B, T, D), (new_nsa, win_state, new_fox, logf)


def _trunk(x, past, page_table, P, w_buf, L, Bb):
    B, T, D = x.shape
    pe = dict(norm_mix=P['norm_mix'][0], w_in=P['w_in_even'][0], w_out=P['w_out_even'][0], mi_b_i=P['mi_b_i'][0],
              mi_b_f=P['mi_b_f'][0], mi_norm=P['mi_norm'][0], gd_conv_w=P['gd_conv_w'][0], gd_a_log=P['gd_a_log'][0],
              gd_dt_bias=P['gd_dt_bias'][0], gd_norm=P['gd_norm'][0])
    po = dict(norm_mix=P['norm_mix'][1], w_in=P['w_in_odd'][0], w_out=P['w_out_odd'][0],
              nsa_cmp_pos=P['nsa_cmp_pos'][0], nsa_cmp_w1=P['nsa_cmp_w1'][0], nsa_cmp_w2=P['nsa_cmp_w2'][0],
              fox_b_f=P['fox_b_f'][0])
    tm = _row_tile(B * T)
    mlp = lambda x, layer, final: mlp_residual(
        x.reshape(B * T, D), P['norm_mlp'][layer], P['w_up'][layer].astype(BF16), P['w_down'][layer].astype(BF16),
        P['norm_final'], final, tm, 1024).reshape(B, T, D)
    even_past = None if past is None else tuple(past[k][0] for k in ('mc', 'mn', 'mm', 'gs', 'gc'))
    odd_past = None if past is None else tuple(past[k][0] for k in ('nsa_kv', 'nsa_win', 'fox_kv', 'fox_logf'))
    x, st_e = even_layer(x, pe, even_past, L, Bb)
    x = mlp(x, 0, False)
    x, st_o = odd_layer(x, po, P['rel_bias'], w_buf, odd_past, page_table)
    y = mlp(x, 1, True)
    return y, tuple(a[None] for a in st_e + st_o)


def kernel(x_prompt, x_sample, state_mlstm_c, state_mlstm_n, state_mlstm_m, state_gdn_s, state_gdn_conv,
           cache_nsa_kv, state_nsa_win, cache_fox_kv, cache_fox_logf, page_table,
           norm_mix, norm_mlp, norm_final, w_up, w_down,
           w_in_even, w_out_even, mi_b_i, mi_b_f, mi_norm, gd_conv_w, gd_a_log, gd_dt_bias, gd_norm,
           w_in_odd, w_out_odd, nsa_cmp_pos, nsa_cmp_w1, nsa_cmp_w2, fox_b_f, rel_bias):
    P = dict(norm_mix=norm_mix, norm_mlp=norm_mlp, norm_final=norm_final, w_up=w_up, w_down=w_down,
             w_in_even=w_in_even, w_out_even=w_out_even, mi_b_i=mi_b_i, mi_b_f=mi_b_f, mi_norm=mi_norm,
             gd_conv_w=gd_conv_w, gd_a_log=gd_a_log, gd_dt_bias=gd_dt_bias, gd_norm=gd_norm,
             w_in_odd=w_in_odd, w_out_odd=w_out_odd, nsa_cmp_pos=nsa_cmp_pos, nsa_cmp_w1=nsa_cmp_w1,
             nsa_cmp_w2=nsa_cmp_w2, fox_b_f=fox_b_f, rel_bias=rel_bias)
    w_buf = state_nsa_win.shape[2]
    b_p, t_p = x_prompt.shape[:2]
    y_p, st_p = _trunk(x_prompt, None, None, P, w_buf, math.gcd(t_p, 64), math.gcd(b_p, RECURRENT_BATCH))
    past = dict(mc=state_mlstm_c, mn=state_mlstm_n, mm=state_mlstm_m, gs=state_gdn_s, gc=state_gdn_conv,
                nsa_kv=cache_nsa_kv, nsa_win=state_nsa_win, fox_kv=cache_fox_kv, fox_logf=cache_fox_logf)
    b_s, t_s = x_sample.shape[:2]
    y_s, st_s = _trunk(x_sample, past, page_table, P, w_buf, math.gcd(t_s, 64), math.gcd(b_s, RECURRENT_BATCH))
    return (y_p, y_s) + st_p + st_s
```

```python
import functools
import math

import jax
import jax.numpy as jnp
import numpy as np
from jax import lax
from jax.experimental import pallas as pl
from jax.experimental.pallas import tpu as pltpu

F32 = jnp.float32
BF16 = jnp.bfloat16
HI = lax.Precision.HIGHEST

D_MODEL = 1024
D_FF = 4 * D_MODEL
EPS = 1e-6
NEG_BIG = -1e30
PAGE = 128

MI_H, MI_DQK, MI_DV = 4, 64, 128
GD_H, GD_DK, GD_DV, GD_CONV = 4, 128, 128, 4
GD_CH = 3 * GD_H * GD_DK
NSA_G, NSA_R, HD = 2, 4, 64
FOX_H = 8
CMP_BLOCK, CMP_STRIDE, CMP_HIDDEN = 32, 16, 256
SEL_BLOCK, SEL_TOPN, WINDOW = 64, 16, 512
FORCE_SCORE = 1e4
N_BUCKETS, MAX_DISTANCE = 32, 128
BUCKET_EXACT = N_BUCKETS // 2
BUCKET_SAT_DIST = 113
LANE = 128
VMEM_LIMIT = 56 * 1024 * 1024


def _cparams(sem):
    return pltpu.CompilerParams(dimension_semantics=sem, vmem_limit_bytes=VMEM_LIMIT)


def _dot(a, b, precision=None):
    return jnp.dot(a, b, preferred_element_type=F32, precision=precision)


def _dot_nt(a, b, precision=None):
    return lax.dot_general(a, b, (((1,), (1,)), ((), ())), preferred_element_type=F32, precision=precision)


def _dot_tn(a, b, precision=None):
    return lax.dot_general(a, b, (((0,), (0,)), ((), ())), preferred_element_type=F32, precision=precision)


def _softplus(x):
    return jnp.maximum(x, 0.0) + jnp.log1p(jnp.exp(-jnp.abs(x)))


def _sigmoid(x):
    return 1.0 / (1.0 + jnp.exp(-x))


def _silu(x):
    return x * _sigmoid(x)


def _iota(shape, dim):
    return lax.broadcasted_iota(jnp.int32, shape, dim)


ACT_ID, ACT_LOGSIG, ACT_SIG, ACT_DECAY = 0.0, 1.0, 2.0, 3.0


def _proj_body(x_ref, g_ref, w_ref, sp_ref, *out_refs, widths, head_major):
    x = x_ref[...]
    hn = (x * lax.rsqrt(jnp.mean(x * x, axis=-1, keepdims=True) + EPS) * g_ref[...]).astype(BF16)
    off = 0
    for i, (o_ref, n) in enumerate(zip(out_refs, widths)):
        r = _dot(hn, w_ref[:, off:off + n])
        if i == len(widths) - 1:
            z = r + sp_ref[0:1, :]
            mode = sp_ref[1:2, :]
            decay = -jnp.exp(sp_ref[2:3, :]) * _softplus(z)
            r = jnp.where(mode == ACT_LOGSIG, -_softplus(-z),
                          jnp.where(mode == ACT_SIG, _sigmoid(z),
                                    jnp.where(mode == ACT_DECAY, decay, z)))
        if i in head_major:
            bt, _, tt, _ = o_ref.shape
            r3 = r.reshape(bt, tt, n)
            for j in range(n // HD):
                o_ref[:, j] = r3[:, :, j * HD:(j + 1) * HD]
        else:
            o_ref[...] = r.astype(o_ref.dtype)
        off += n


def norm_proj(x, g, w_bf16, small_params, widths, tm, rows_per_batch, head_major=()):
    m, d = x.shape
    n_total = sum(widths)
    T = rows_per_batch
    assert w_bf16.shape == (d, n_total) and m % tm == 0 and (tm % T == 0 or T % tm == 0)
    bt, tt = max(tm // T, 1), min(tm, T)
    per_b = T // tt
    out_shape, out_specs = [], []
    for i, n in enumerate(widths):
        if i in head_major:
            out_shape.append(jax.ShapeDtypeStruct((m // T, n // HD, T, HD), F32))
            out_specs.append(pl.BlockSpec((bt, n // HD, tt, HD),
                                          (lambda i: (i, 0, 0, 0)) if bt > 1 else
                                          (lambda i: (i // per_b, 0, i % per_b, 0))))
        else:
            out_shape.append(jax.ShapeDtypeStruct((m, n), F32))
            out_specs.append(pl.BlockSpec((tm, n), lambda i: (i, 0)))
    return pl.pallas_call(
        functools.partial(_proj_body, widths=tuple(widths), head_major=tuple(head_major)),
        grid=(m // tm,),
        in_specs=[pl.BlockSpec((tm, d), lambda i: (i, 0)),
                  pl.BlockSpec((1, d), lambda i: (0, 0)),
                  pl.BlockSpec((d, n_total), lambda i: (0, 0)),
                  pl.BlockSpec((8, LANE), lambda i: (0, 0))],
        out_specs=out_specs,
        out_shape=out_shape,
        compiler_params=_cparams(("parallel",)),
        name="norm_proj",
    )(x, g.reshape(1, d), w_bf16, small_params)


def _outproj_body(x_ref, a1_ref, a2_ref, w_ref, o_ref):
    k1 = a1_ref.shape[1]
    y = _dot(a1_ref[...].astype(BF16), w_ref[0:k1, :]) + _dot(a2_ref[...].astype(BF16), w_ref[k1:, :])
    o_ref[...] = x_ref[...] + y


def out_proj_residual(x, a1, a2, w_bf16, tm):
    m, d = x.shape
    k1, k2 = a1.shape[1], a2.shape[1]
    return pl.pallas_call(
        _outproj_body,
        grid=(m // tm,),
        in_specs=[pl.BlockSpec((tm, d), lambda i: (i, 0)),
                  pl.BlockSpec((tm, k1), lambda i: (i, 0)),
                  pl.BlockSpec((tm, k2), lambda i: (i, 0)),
                  pl.BlockSpec((k1 + k2, d), lambda i: (0, 0))],
        out_specs=pl.BlockSpec((tm, d), lambda i: (i, 0)),
        out_shape=jax.ShapeDtypeStruct((m, d), F32),
        compiler_params=_cparams(("parallel",)),
        name="out_proj",
    )(x, a1, a2, w_bf16)


def _mlp_body(x_ref, g_ref, wu_ref, wd_ref, gf_ref, o_ref, hn_scr, acc_scr, *, final_norm):
    j = pl.program_id(1)

    @pl.when(j == 0)
    def _():
        x = x_ref[...]
        hn_scr[...] = (x * lax.rsqrt(jnp.mean(x * x, axis=-1, keepdims=True) + EPS) * g_ref[...]).astype(BF16)
        acc_scr[...] = jnp.zeros_like(acc_scr)

    u = jnp.maximum(_dot(hn_scr[...], wu_ref[...]), 0.0)
    acc_scr[...] += _dot((u * u).astype(BF16), wd_ref[...])

    @pl.when(j == pl.num_programs(1) - 1)
    def _():
        y = x_ref[...] + acc_scr[...]
        if final_norm:
            y = y * lax.rsqrt(jnp.mean(y * y, axis=-1, keepdims=True) + EPS) * gf_ref[...]
        o_ref[...] = y


def mlp_residual(x, g, w_up_bf16, w_down_bf16, g_final, final_norm, tm, tf):
    m, d = x.shape
    f = w_up_bf16.shape[1]
    return pl.pallas_call(
        functools.partial(_mlp_body, final_norm=final_norm),
        grid=(m // tm, f // tf),
        in_specs=[pl.BlockSpec((tm, d), lambda i, j: (i, 0)),
                  pl.BlockSpec((1, d), lambda i, j: (0, 0)),
                  pl.BlockSpec((d, tf), lambda i, j: (0, j)),
                  pl.BlockSpec((tf, d), lambda i, j: (j, 0)),
                  pl.BlockSpec((1, d), lambda i, j: (0, 0))],
        out_specs=pl.BlockSpec((tm, d), lambda i, j: (i, 0)),
        out_shape=jax.ShapeDtypeStruct((m, d), F32),
        scratch_shapes=[pltpu.VMEM((tm, d), BF16), pltpu.VMEM((tm, d), F32)],
        compiler_params=_cparams(("parallel", "arbitrary")),
        name="mlp",
    )(x, g.reshape(1, d), w_up_bf16, w_down_bf16, g_final.reshape(1, d))


def _tri(n):
    r = _iota((n, n), 0)
    c = _iota((n, n), 1)
    return r, c


def _bf16_terms(a, n):
    terms, rest = [], a
    for _ in range(n):
        t = rest.astype(BF16)
        terms.append(t)
        rest = rest - t.astype(F32)
    return terms


def _dot_split(a, b, f=None):
    f = f or _dot
    a_hi, a_lo = _bf16_terms(a, 2)
    b_hi, b_lo = _bf16_terms(b, 2)
    return f(a_hi, b_hi) + (f(a_hi, b_lo) + f(a_lo, b_hi))


def _dot_pick(a, b, f=None, exact_lhs=False):
    f = f or _dot
    if exact_lhs:
        return sum(f(a.astype(BF16), t) for t in _bf16_terms(b, 3))
    return sum(f(t, b.astype(BF16)) for t in _bf16_terms(a, 3))


def _chunk_dots(L):
    if L % 16 == 0:
        cast = lambda f: (lambda a, b: f(a.astype(BF16), b.astype(BF16)))
        return cast(_dot), cast(_dot_nt), cast(_dot_tn)
    full = lambda f: (lambda a, b: _dot_split(a, b, f))
    return full(_dot), full(_dot_nt), full(_dot_tn)


def _mlstm_group(bbs, q_ref, k_ref, v_ref, o_ref, sc_ref, sr_ref, nw_ref, h_ref, c_ref, n_ref, m_ref, L):
    nn, nt, tn = _chunk_dots(L)
    rows, cols = _tri(L)
    lower = rows >= cols
    tril = lower.astype(F32)
    triu = (rows <= cols).astype(F32)
    lane4 = _iota((1, MI_H), 1)
    scs = {bb: sc_ref[bb] for bb in bbs}
    srs = {bb: sr_ref[bb, 0] for bb in bbs}
    b_col = {bb: _dot_pick(tril, scs[bb][:, 4:8], exact_lhs=True) for bb in bbs}
    b_row = {bb: _dot_pick(srs[bb][4:8, :], triu) for bb in bbs}
    m_vec = {bb: m_ref[bb] for bb in bbs}
    chains = [(bb, h) for bb in bbs for h in range(MI_H)]
    q = {c: q_ref[c[0], c[1]] for c in chains}
    k = {c: k_ref[c[0], c[1]] * (MI_DQK ** -0.5) for c in chains}
    v = {c: v_ref[c[0], :, c[1] * MI_DV:(c[1] + 1) * MI_DV] for c in chains}
    qk = {c: nt(q[c], k[c]) for c in chains}
    c_prev = {c: c_ref[c[0], c[1]] for c in chains}
    qc = {c: nn(q[c], c_prev[c]) for c in chains}
    s, a_inter, m_t, m_new, a_prev, kw = {}, {}, {}, {}, {}, {}
    for c in chains:
        bb, h = c
        bc = b_col[bb][:, h:h + 1]
        m_prev = m_vec[bb][:, h:h + 1]
        d = jnp.where(lower, bc - b_row[bb][h:h + 1, :] + srs[bb][h:h + 1, :], NEG_BIG)
        inter = bc + m_prev
        m_t[c] = jnp.maximum(inter, jnp.max(d, axis=1, keepdims=True))
        s[c] = qk[c] * jnp.exp(d - m_t[c])
        a_inter[c] = jnp.exp(inter - m_t[c])
        b_last = bc[L - 1:L, :]
        g_col = b_last - bc + scs[bb][:, h:h + 1]
        m_new[c] = jnp.maximum(b_last + m_prev, jnp.max(g_col, axis=0, keepdims=True))
        a_prev[c] = jnp.exp(b_last + m_prev - m_new[c])
        kw[c] = k[c] * jnp.exp(g_col - m_new[c])
    sv = {c: nn(s[c], v[c]) for c in chains}
    kv = {c: tn(kw[c], v[c]) for c in chains}
    for c in chains:
        bb, h = c
        n_prev = n_ref[bb, h]
        num = sv[c] + a_inter[c] * qc[c]
        den = (jnp.sum(s[c], axis=1, keepdims=True)
               + a_inter[c] * jnp.sum(q[c] * n_prev, axis=1, keepdims=True))
        hh = num / jnp.maximum(jnp.abs(den), jnp.exp(-m_t[c]))
        c_ref[bb, h] = a_prev[c] * c_prev[c] + kv[c]
        n_ref[bb, h] = a_prev[c] * n_prev + jnp.sum(kw[c], axis=0, keepdims=True)
        m_vec[bb] = jnp.where(lane4 == h, m_new[c], m_vec[bb])
        hn = hh * lax.rsqrt(jnp.mean(hh * hh, axis=-1, keepdims=True) + EPS) * nw_ref[h:h + 1, :]
        gate = _sigmoid(o_ref[bb, :, h * MI_DV:(h + 1) * MI_DV])
        h_ref[bb, :, h * MI_DV:(h + 1) * MI_DV] = hn * gate
    for bb in bbs:
        m_ref[bb] = m_vec[bb]


def _mlstm_body(q_ref, k_ref, v_ref, o_ref, sc_ref, sr_ref, c0_ref, n0_ref, m0_ref, nw_ref,
                h_ref, c_ref, n_ref, m_ref, *, L, Bb):
    @pl.when(pl.program_id(1) == 0)
    def _():
        c_ref[...] = c0_ref[...]
        n_ref[...] = n0_ref[...]
        m_ref[...] = m0_ref[...]

    _mlstm_group(list(range(Bb)), q_ref, k_ref, v_ref, o_ref, sc_ref, sr_ref, nw_ref, h_ref, c_ref, n_ref, m_ref, L)


def mlstm(q, k, v, o, sc, sr, c0, n0, m0, norm_w, L, Bb):
    B, H, T, _ = q.shape
    nc = T // L
    hv = H * MI_DV
    bmap = lambda b, c: (b, 0, 0, 0)
    return pl.pallas_call(
        functools.partial(_mlstm_body, L=L, Bb=Bb),
        grid=(B // Bb, nc),
        in_specs=[pl.BlockSpec((Bb, H, L, MI_DQK), lambda b, c: (b, 0, c, 0)),
                  pl.BlockSpec((Bb, H, L, MI_DQK), lambda b, c: (b, 0, c, 0)),
                  pl.BlockSpec((Bb, L, hv), lambda b, c: (b, c, 0)),
                  pl.BlockSpec((Bb, L, hv), lambda b, c: (b, c, 0)),
                  pl.BlockSpec((Bb, L, LANE), lambda b, c: (b, c, 0)),
                  pl.BlockSpec((Bb, 1, 8, L), lambda b, c: (b, c, 0, 0)),
                  pl.BlockSpec((Bb, H, MI_DQK, MI_DV), bmap),
                  pl.BlockSpec((Bb, H, 1, MI_DQK), bmap),
                  pl.BlockSpec((Bb, 1, H), lambda b, c: (b, 0, 0)),
                  pl.BlockSpec((H, MI_DV), lambda b, c: (0, 0))],
        out_specs=[pl.BlockSpec((Bb, L, hv), lambda b, c: (b, c, 0)),
                   pl.BlockSpec((Bb, H, MI_DQK, MI_DV), bmap),
                   pl.BlockSpec((Bb, H, 1, MI_DQK), bmap),
                   pl.BlockSpec((Bb, 1, H), lambda b, c: (b, 0, 0))],
        out_shape=[jax.ShapeDtypeStruct((B, T, hv), F32),
                   jax.ShapeDtypeStruct((B, H, MI_DQK, MI_DV), F32),
                   jax.ShapeDtypeStruct((B, H, 1, MI_DQK), F32),
                   jax.ShapeDtypeStruct((B, 1, H), F32)],
        compiler_params=_cparams(("parallel", "arbitrary")),
        name="mlstm",
    )(q, k, v, o, sc, sr, c0, n0, m0, norm_w)


def _gdn_group(bbs, x_ref, z_ref, sc_ref, sr_ref, cw_ref, nw_ref, o_ref, s_ref, conv_ref, xp_scr, L):
    base = 8 - (GD_CONV - 1)
    nn, nt, tn = _chunk_dots(L)
    rows, cols = _tri(L)
    incl = rows >= cols
    strict = rows > cols
    eye = (rows == cols).astype(F32)
    tril = incl.astype(F32)
    triu = (rows <= cols).astype(F32)
    kw = GD_H * GD_DK
    ys, scs, gcols, grows = {}, {}, {}, {}
    for bb in bbs:
        xp_scr[bb, 8:8 + L, :] = x_ref[bb]
        y = xp_scr[bb, base:base + L, :] * cw_ref[0:1, :]
        for j in range(1, GD_CONV):
            y = y + xp_scr[bb, base + j:base + j + L, :] * cw_ref[j:j + 1, :]
        tail = xp_scr[bb, L + base:L + 8, :]
        xp_scr[bb, base:8, :] = tail
        conv_ref[bb] = tail
        ys[bb] = _silu(y)
        scs[bb] = sc_ref[bb]
        gcols[bb] = _dot_pick(tril, scs[bb][:, 12:16], exact_lhs=True)
        grows[bb] = _dot_pick(sr_ref[bb, 0][0:4, :], triu)
    chains = [(bb, h) for bb in bbs for h in range(GD_H)]
    q, k, vb, kb, dec, gcc, egc = {}, {}, {}, {}, {}, {}, {}
    for c in chains:
        bb, h = c
        y = ys[bb]
        qh = y[:, h * GD_DK:(h + 1) * GD_DK]
        kh = y[:, kw + h * GD_DK:kw + (h + 1) * GD_DK]
        vh = y[:, 2 * kw + h * GD_DV:2 * kw + (h + 1) * GD_DV]
        q[c] = qh * lax.rsqrt(jnp.sum(qh * qh, axis=-1, keepdims=True) + EPS) * (GD_DK ** -0.5)
        k[c] = kh * lax.rsqrt(jnp.sum(kh * kh, axis=-1, keepdims=True) + EPS)
        beta = scs[bb][:, 8 + h:9 + h]
        gcc[c] = gcols[bb][:, h:h + 1]
        dec[c] = jnp.exp(jnp.where(incl, gcc[c] - grows[bb][h:h + 1, :], NEG_BIG))
        egc[c] = jnp.exp(gcc[c])
        kb[c] = k[c] * beta
        vb[c] = vh * beta
    pw = {c: -(nt(kb[c], k[c]) * jnp.where(strict, dec[c], 0.0)) for c in chains}
    attn = {c: nt(q[c], k[c]) * dec[c] for c in chains}
    tinv = {c: eye + pw[c] for c in chains}
    for _ in range(int(math.log2(L)) - 1):
        pw = {c: _dot_split(pw[c], pw[c]) for c in chains}
        tinv = {c: tinv[c] + _dot_split(tinv[c], pw[c]) for c in chains}
    u = {c: _dot_split(tinv[c], vb[c]) for c in chains}
    w = {c: _dot_split(tinv[c], kb[c] * egc[c]) for c in chains}
    s_prev = {c: s_ref[c[0], c[1]] for c in chains}
    v_new = {c: u[c] - nn(w[c], s_prev[c]) for c in chains}
    o = {c: nn(q[c] * egc[c], s_prev[c]) + nn(attn[c], v_new[c]) for c in chains}
    for c in chains:
        bb, h = c
        g_last = gcc[c][L - 1:L, :]
        s_ref[bb, h] = jnp.exp(g_last) * s_prev[c] + tn(k[c] * jnp.exp(g_last - gcc[c]), v_new[c])
        on = o[c] * lax.rsqrt(jnp.mean(o[c] * o[c], axis=-1, keepdims=True) + EPS) * nw_ref[...]
        o_ref[bb, :, h * GD_DV:(h + 1) * GD_DV] = on * _silu(z_ref[bb, :, h * GD_DV:(h + 1) * GD_DV])


def _gdn_body(x_ref, z_ref, sc_ref, sr_ref, s0_ref, conv0_ref, cw_ref, nw_ref,
              o_ref, s_ref, conv_ref, xp_scr, *, L, Bb):
    @pl.when(pl.program_id(1) == 0)
    def _():
        s_ref[...] = s0_ref[...]
        xp_scr[:, 8 - (GD_CONV - 1):8, :] = conv0_ref[...]

    _gdn_group(list(range(Bb)), x_ref, z_ref, sc_ref, sr_ref, cw_ref, nw_ref, o_ref, s_ref, conv_ref, xp_scr, L)


def gdn(x, z, sc, sr, s0, conv0, conv_w, norm_w, L, Bb):
    B, T, ch = x.shape
    H = GD_H
    nc = T // L
    hv = H * GD_DV
    bmap = lambda b, c: (b, 0, 0, 0)
    return pl.pallas_call(
        functools.partial(_gdn_body, L=L, Bb=Bb),
        grid=(B // Bb, nc),
        in_specs=[pl.BlockSpec((Bb, L, ch), lambda b, c: (b, c, 0)),
                  pl.BlockSpec((Bb, L, hv), lambda b, c: (b, c, 0)),
                  pl.BlockSpec((Bb, L, LANE), lambda b, c: (b, c, 0)),
                  pl.BlockSpec((Bb, 1, 8, L), lambda b, c: (b, c, 0, 0)),
                  pl.BlockSpec((Bb, H, GD_DK, GD_DV), bmap),
                  pl.BlockSpec((Bb, GD_CONV - 1, ch), lambda b, c: (b, 0, 0)),
                  pl.BlockSpec((GD_CONV, ch), lambda b, c: (0, 0)),
                  pl.BlockSpec((1, GD_DV), lambda b, c: (0, 0))],
        out_specs=[pl.BlockSpec((Bb, L, hv), lambda b, c: (b, c, 0)),
                   pl.BlockSpec((Bb, H, GD_DK, GD_DV), bmap),
                   pl.BlockSpec((Bb, GD_CONV - 1, ch), lambda b, c: (b, 0, 0))],
        out_shape=[jax.ShapeDtypeStruct((B, T, hv), F32),
                   jax.ShapeDtypeStruct((B, H, GD_DK, GD_DV), F32),
                   jax.ShapeDtypeStruct((B, GD_CONV - 1, ch), F32)],
        scratch_shapes=[pltpu.VMEM((Bb, L + 8, ch), F32)],
        compiler_params=_cparams(("parallel", "arbitrary")),
        name="gdn",
    )(x, z, sc, sr, s0, conv0, conv_w, norm_w.reshape(1, GD_DV))


def _compress_body(x_ref, pos_ref, w1_ref, w2_ref, o_ref):
    Bb, _, R, half = x_ref.shape
    x = x_ref[:, 0].reshape(Bb * R, half).astype(F32)
    ua = _dot((x + pos_ref[0, 0:1, :]).astype(BF16), w1_ref[0, 0])
    ub = _dot((x + pos_ref[0, 1:2, :]).astype(BF16), w1_ref[0, 1])
    h = _silu(ua + pltpu.roll(ub, Bb * R - 1, 0))
    o_ref[:, 0, 0:R, :] = _dot(h.astype(BF16), w2_ref[0]).reshape(Bb, R, HD)
    rp = o_ref.shape[2]
    if rp > R:
        o_ref[:, 0, R:rp, :] = jnp.zeros((Bb, rp - R, HD), F32)


def nsa_compress(xr, R, pos, w1, w2, Bb):
    B = xr.shape[0]
    half = CMP_STRIDE * HD
    rp = -(-R // LANE) * LANE
    return pl.pallas_call(
        _compress_body,
        grid=(4, B // Bb),
        in_specs=[pl.BlockSpec((Bb, 1, R, half), lambda c, b: (b, c, 0, 0)),
                  pl.BlockSpec((1, 2, half), lambda c, b: (c // 2, 0, 0)),
                  pl.BlockSpec((1, 2, half, CMP_HIDDEN), lambda c, b: (c // 2, 0, 0, 0)),
                  pl.BlockSpec((1, CMP_HIDDEN, HD), lambda c, b: (c // 2, 0, 0))],
        out_specs=pl.BlockSpec((Bb, 1, rp, HD), lambda c, b: (b, c, 0, 0)),
        out_shape=jax.ShapeDtypeStruct((B, 4, rp, HD), F32),
        compiler_params=_cparams(("parallel", "parallel")),
        name="nsa_compress",
    )(xr, pos.reshape(2, 2, half), w1.reshape(2, 2, half, CMP_HIDDEN).astype(BF16), w2.astype(BF16))


def _cumsum_body(x_ref, o_ref):
    Bb, H, T = x_ref.shape
    rows, cols = _tri(LANE)
    triu = (rows <= cols).astype(F32)
    carry = jnp.zeros((Bb * H, 1), F32)
    for c in range(T // LANE):
        seg = x_ref[:, :, c * LANE:(c + 1) * LANE].reshape(Bb * H, LANE)
        loc = _dot_pick(seg, triu) + carry
        o_ref[:, :, c * LANE:(c + 1) * LANE] = loc.reshape(Bb, H, LANE)
        carry = loc[:, LANE - 1:LANE]


def cumsum_lanes(x, Bb):
    B, H, T = x.shape
    return pl.pallas_call(
        _cumsum_body,
        grid=(B // Bb,),
        in_specs=[pl.BlockSpec((Bb, H, T), lambda b: (b, 0, 0))],
        out_specs=pl.BlockSpec((Bb, H, T), lambda b: (b, 0, 0)),
        out_shape=jax.ShapeDtypeStruct((B, H, T), F32),
        compiler_params=_cparams(("parallel",)),
        name="cumsum",
    )(x)


def _flash_tile(s_blocks, v, m_scr, l_scr, acc_scr, exp=jnp.exp):
    dv = acc_scr.shape[-1]
    m_prev = m_scr[...]
    mx = s_blocks[0]
    for sb in s_blocks[1:]:
        mx = jnp.maximum(mx, sb)
    m_new = jnp.maximum(m_prev, jnp.max(mx, axis=1, keepdims=True))
    p_blocks = [exp(sb - m_new) for sb in s_blocks]
    sm = p_blocks[0]
    for pb in p_blocks[1:]:
        sm = sm + pb
    alpha = exp(m_prev - m_new)
    l_scr[...] = alpha * l_scr[...] + jnp.sum(sm, axis=1, keepdims=True)
    p = (jnp.concatenate(p_blocks, axis=1) if len(p_blocks) > 1 else p_blocks[0]).astype(BF16)
    acc_scr[...] = alpha[:, :dv] * acc_scr[...] + _dot(p, v)
    m_scr[...] = m_new


def _flash_tiles(items, exp=jnp.exp):
    m_prev = [m[...] for _, _, m, _, _ in items]
    m_new = []
    for (s_blocks, _, _, _, _), mp in zip(items, m_prev):
        mx = s_blocks[0]
        for sb in s_blocks[1:]:
            mx = jnp.maximum(mx, sb)
        m_new.append(jnp.maximum(mp, jnp.max(mx, axis=1, keepdims=True)))
    p_blocks = [[exp(sb - mn) for sb in it[0]] for it, mn in zip(items, m_new)]
    sums = []
    for pb in p_blocks:
        sm = pb[0]
        for x in pb[1:]:
            sm = sm + x
        sums.append(jnp.sum(sm, axis=1, keepdims=True))
    pv = [_dot((jnp.concatenate(pb, axis=1) if len(pb) > 1 else pb[0]).astype(BF16), it[1])
          for pb, it in zip(p_blocks, items)]
    for (_, _, m_ref, l_ref, acc_ref), mp, mn, sm, x in zip(items, m_prev, m_new, sums, pv):
        alpha = exp(mp - mn)
        l_ref[...] = alpha * l_ref[...] + sm
        acc_ref[...] = alpha[:, :acc_ref.shape[-1]] * acc_ref[...] + x
        m_ref[...] = mn


def _flash_reset(m_scr, l_scr, acc_scr):
    m_scr[...] = jnp.full_like(m_scr, NEG_BIG)
    l_scr[...] = jnp.zeros_like(l_scr)
    acc_scr[...] = jnp.zeros_like(acc_scr)


def _lane_blocks(s):
    return [s[:, i * LANE:(i + 1) * LANE] for i in range(s.shape[1] // LANE)]


def _fox_body(q_ref, k_ref, v_ref, o_ref, m_scr, l_scr, acc_scr, *, tq, tk):
    qi = pl.program_id(1)
    j = pl.program_id(2)
    top = (qi * tq + tq - 1) // tk

    @pl.when(j == 0)
    def _():
        _flash_reset(m_scr, l_scr, acc_scr)

    def tile(diag):
        if diag:
            mask = (top * tk + _iota((tq, tk), 1)) <= (qi * tq + _iota((tq, tk), 0))
        items = []
        for h in range(FOX_H):
            s = _dot_nt(q_ref[0, h], k_ref[0, h])
            if diag:
                s = jnp.where(mask, s, NEG_BIG)
            items.append((_lane_blocks(s), v_ref[0, h], m_scr.at[h], l_scr.at[h], acc_scr.at[h]))
        _flash_tiles(items, exp=jnp.exp2)

    @pl.when(j == 0)
    def _():
        tile(True)

    @pl.when(jnp.logical_and(j > 0, j <= top))
    def _():
        tile(False)

    @pl.when(j == pl.num_programs(2) - 1)
    def _():
        for h in range(FOX_H):
            o_ref[0, h] = acc_scr[h] / jnp.maximum(l_scr[h][:, :HD], 1e-30)


LOG2E = 1.0 / math.log(2.0)


def fox_operands(q_heads, k_heads, f_heads):
    def top_bits(x):
        return lax.bitcast_convert_type(lax.bitcast_convert_type(x, jnp.uint32) & jnp.uint32(0xFFFF0000), F32)

    terms, rest = [], f_heads * LOG2E
    for _ in range(3):
        t = top_bits(rest)
        terms.append(t.astype(BF16))
        rest = rest - t
    f_terms = jnp.stack(terms, axis=-1)
    ones = jnp.ones_like(f_terms)
    pad = jnp.zeros(q_heads.shape[:3] + (LANE - HD - 6,), BF16)
    q_aug = jnp.concatenate([(q_heads * (HD ** -0.5 * LOG2E)).astype(BF16), f_terms, ones, pad], axis=-1)
    k_aug = jnp.concatenate([k_heads.astype(BF16), ones, -f_terms, pad], axis=-1)
    return q_aug, k_aug


def fox_prompt(q_aug, k_aug, kv, tq, tk):
    B, H, T, _ = q_aug.shape
    kmap = lambda i, j: jnp.maximum((i * tq + tq - 1) // tk - j, 0)
    return pl.pallas_call(
        functools.partial(_fox_body, tq=tq, tk=tk),
        grid=(B, T // tq, T // tk),
        in_specs=[pl.BlockSpec((1, H, tq, LANE), lambda b, i, j: (b, 0, i, 0)),
                  pl.BlockSpec((1, H, tk, LANE), lambda b, i, j: (b, 0, kmap(i, j), 0)),
                  pl.BlockSpec((1, H, tk, HD), lambda b, i, j: (b, 1, kmap(i, j), 0))],
        out_specs=pl.BlockSpec((1, H, tq, HD), lambda b, i, j: (b, 0, i, 0)),
        out_shape=jax.ShapeDtypeStruct((B, H, T, HD), F32),
        scratch_shapes=[pltpu.VMEM((H, tq, LANE), F32), pltpu.VMEM((H, tq, LANE), F32), pltpu.VMEM((H, tq, HD), F32)],
        compiler_params=_cparams(("parallel", "parallel", "arbitrary")),
        name="fox_prompt",
    )(q_aug, k_aug, kv)


def _fox_decode_body(pt_ref, q_ref, newkv_ref, newlf_ref, *refs, n_pages, tn):
    kv_refs = refs[:n_pages]
    lf_refs = refs[n_pages:2 * n_pages]
    o_ref = refs[2 * n_pages]
    hw = FOX_H * HD
    R = FOX_H * tn
    q = q_ref[0]
    qrep = jnp.concatenate([q] * FOX_H, axis=0)
    blockmask = (_iota((R, hw), 0) // tn) == (_iota((R, hw), 1) // HD)
    qbd = jnp.where(blockmask, qrep, 0.0).astype(BF16)
    rows, cols = _tri(PAGE)
    triu = (rows <= cols).astype(F32)
    ones = jnp.ones((8, PAGE), F32)
    expand = lambda a: jnp.concatenate([jnp.broadcast_to(a[h:h + 1, :], (tn, a.shape[1])) for h in range(FOX_H)], 0)
    carry_c = jnp.zeros((FOX_H, 1), F32)
    carry_r = jnp.zeros((1, FOX_H), F32)
    s_tiles = []
    for pg in range(n_pages):
        lf = lf_refs[pg][0]
        f_t = _dot_pick(lf, triu) + carry_c
        carry_c = f_t[:, PAGE - 1:PAGE]
        carry_r = carry_r + _dot_pick(ones, lf, _dot_nt, exact_lhs=True)[0:1, :]
        s_tiles.append(_dot(qbd, kv_refs[pg][0, 0:hw, :].astype(BF16)) - expand(f_t))
    lfn = newlf_ref[0]
    r8, c8 = _tri(tn)
    fq_c = _dot_pick((r8 >= c8).astype(F32), lfn, exact_lhs=True) + carry_r
    fq_t = _dot_pick(lfn, (r8 <= c8).astype(F32), _dot_tn) + carry_c
    fq_rows = jnp.concatenate([fq_c[:, h:h + 1] for h in range(FOX_H)], axis=0)
    kn = newkv_ref[0, :, 0:hw]
    s_new = _dot_split(qbd.astype(F32), kn, _dot_nt) - expand(fq_t)
    causal = _iota((R, tn), 1) <= (_iota((R, tn), 0) % tn)
    s_new = jnp.where(causal, s_new + fq_rows, NEG_BIG)
    s_tiles = [s + fq_rows for s in s_tiles]
    m = jnp.max(s_new, axis=1, keepdims=True)
    for s in s_tiles:
        m = jnp.maximum(m, jnp.max(s, axis=1, keepdims=True))
    p_new = jnp.where(causal, jnp.exp(s_new - m), 0.0)
    l = jnp.sum(p_new, axis=1, keepdims=True)
    acc = _dot_split(p_new, newkv_ref[0, :, hw:2 * hw])
    for pg, s in enumerate(s_tiles):
        p = jnp.exp(s - m)
        l = l + jnp.sum(p, axis=1, keepdims=True)
        acc = acc + _dot_nt(p.astype(BF16), kv_refs[pg][0, hw:2 * hw, :].astype(BF16))
    acc = acc / jnp.maximum(l, 1e-30)
    o_ref[0] = jnp.concatenate([acc[h * tn:(h + 1) * tn, h * HD:(h + 1) * HD] for h in range(FOX_H)], axis=1)


def fox_decode(page_table, q, newkv, newlf, kv_pool_t, lf_pool_t):
    B, tn, hw = q.shape
    n_pages = page_table.shape[1]
    page_spec = lambda rows, pg: pl.BlockSpec((1, rows, PAGE), lambda b, pt: (pt[b, pg], 0, 0))
    grid_spec = pltpu.PrefetchScalarGridSpec(
        num_scalar_prefetch=1,
        grid=(B,),
        in_specs=[pl.BlockSpec((1, tn, hw), lambda b, pt: (b, 0, 0)),
                  pl.BlockSpec((1, tn, 2 * hw), lambda b, pt: (b, 0, 0)),
                  pl.BlockSpec((1, tn, FOX_H), lambda b, pt: (b, 0, 0))]
                 + [page_spec(2 * hw, pg) for pg in range(n_pages)]
                 + [page_spec(FOX_H, pg) for pg in range(n_pages)],
        out_specs=pl.BlockSpec((1, tn, hw), lambda b, pt: (b, 0, 0)),
    )
    return pl.pallas_call(
        functools.partial(_fox_decode_body, n_pages=n_pages, tn=tn),
        grid_spec=grid_spec,
        out_shape=jax.ShapeDtypeStruct((B, tn, hw), F32),
        compiler_params=_cparams(("arbitrary",)),
        name="fox_decode",
    )(page_table, q, newkv, newlf, *([kv_pool_t] * n_pages), *([lf_pool_t] * n_pages))


def _t5_bucket(dist):
    n = jnp.maximum(dist, 0)
    nf = jnp.maximum(n, 1).astype(F32)
    large = BUCKET_EXACT + (jnp.log(nf / BUCKET_EXACT) / math.log(MAX_DISTANCE / BUCKET_EXACT)
                            * (N_BUCKETS - BUCKET_EXACT)).astype(jnp.int32)
    return jnp.where(n < BUCKET_EXACT, n, jnp.minimum(large, N_BUCKETS - 1))


def _bias_from_bucket(bucket, tbl_ref, head):
    out = jnp.zeros(bucket.shape, F32)
    for kk in range(N_BUCKETS):
        out = jnp.where(bucket == kk, tbl_ref[kk, head], out)
    return out


FOX_TQ, FOX_TK = 512, 512
RECURRENT_BATCH = 4
NSA_TQ = 2 * LANE
NSA_TK = 2 * LANE
NSA_FAR_GROUP = 4
CMP_PAT_CENTER = LANE // 2
NSA_BIAS_TILES = (BUCKET_SAT_DIST + NSA_TK + LANE - 1) // LANE


def _nsa_body(tbl_ref, q_ref, gate_ref, kc_ref, vc_ref, ks_ref, vs_ref, kw_ref, vw_ref, smap_ref, o_ref,
              bias_scr, pat_scr, score_scr, m_scr, l_scr, acc_scr, *, tq, q_pos0, win_pos0, n_sel, ncp, tw, single):
    g = pl.program_id(1)
    qi = pl.program_id(2)
    q0 = q_pos0 if single else q_pos0 + qi * tq
    aligned = (lambda x, m: x) if single else pl.multiple_of
    lo, hi = (max, min) if single else (jnp.maximum, jnp.minimum)
    R = NSA_R * tq
    scale = HD ** -0.5
    last_bias = tuple(tbl_ref[N_BUCKETS - 1, g * NSA_R + r] for r in range(NSA_R))

    @pl.when(qi == 0)
    def _():
        ii = _iota((tq, NSA_TK), 0)
        jj = _iota((tq, NSA_TK), 1)
        for dd in range(NSA_BIAS_TILES):
            bucket = _t5_bucket(ii - jj + dd * LANE)
            for r in range(NSA_R):
                bias_scr[dd, r * tq:(r + 1) * tq, :] = (
                    _bias_from_bucket(bucket, tbl_ref, g * NSA_R + r) - last_bias[r])
        bias_scr[NSA_BIAS_TILES] = jnp.zeros((R, NSA_TK), F32)
        rel_blk = _iota((tq, LANE), 1) - CMP_PAT_CENTER
        bucket = _t5_bucket(_iota((tq, LANE), 0) - rel_blk * CMP_STRIDE - (CMP_BLOCK - 1))
        for r in range(NSA_R):
            pat_scr[r] = _bias_from_bucket(bucket, tbl_ref, g * NSA_R + r) - last_bias[r]

    t_col = q0 + _iota((tq, 1), 0)

    bias_tiles, mask_tiles = [], []
    for nt in range(ncp // LANE):
        c_end = (nt * LANE + _iota((tq, LANE), 1)) * CMP_STRIDE + (CMP_BLOCK - 1)
        dist = t_col - c_end
        max_dist = q0 + tq - 1 - (nt * LANE * CMP_STRIDE + CMP_BLOCK - 1)
        min_dist = q0 - ((nt * LANE + LANE - 1) * CMP_STRIDE + CMP_BLOCK - 1)
        special = jnp.logical_and(max_dist >= 0, min_dist < BUCKET_SAT_DIST)

        off = nt * LANE - q0 // CMP_STRIDE + CMP_PAT_CENTER

        def general(off=off):
            lanes = _iota((tq, LANE), 1) + off
            inside = jnp.logical_and(lanes >= 0, lanes < LANE)
            shift = (LANE - off % LANE) % LANE
            return jnp.stack([jnp.where(inside, pltpu.roll(pat_scr[r], shift, 1), 0.0) for r in range(NSA_R)])

        def saturated():
            return jnp.zeros((NSA_R, tq, LANE), F32)

        bias_tiles.append(lax.cond(special, general, saturated))
        mask_tiles.append(dist >= 0)
    mask_c = jnp.concatenate(mask_tiles, axis=1) if len(mask_tiles) > 1 else mask_tiles[0]
    kc = kc_ref[0, 0].astype(BF16)
    vc = vc_ref[0, 0].astype(BF16)
    pcsum = jnp.zeros((tq, ncp), F32)
    qk_c = [_dot_nt((q_ref[0, r] * scale).astype(BF16), kc) for r in range(NSA_R)]
    pcs = []
    for r in range(NSA_R):
        bias_r = jnp.concatenate([b[r] for b in bias_tiles], axis=1) if len(bias_tiles) > 1 else bias_tiles[0][r]
        s = jnp.where(mask_c, qk_c[r] + bias_r, NEG_BIG)
        m = jnp.max(s, axis=1, keepdims=True)
        p = jnp.where(mask_c, jnp.exp(s - m), 0.0)
        pc = p / jnp.maximum(jnp.sum(p, axis=1, keepdims=True), 1e-30)
        pcs.append(pc.astype(BF16))
        pcsum = pcsum + pc
    o_c = [_dot(pc, vc) for pc in pcs]

    ps_t = _dot_pick(smap_ref[...], pcsum, _dot_nt, exact_lhs=True)
    j_col = _iota((LANE, 1), 0)
    t_row = q0 + _iota((1, tq), 1)
    cur = lax.shift_right_logical(t_row, int(math.log2(SEL_BLOCK)))
    score = jnp.where(j_col * SEL_BLOCK <= t_row, ps_t, -1.0)
    score = jnp.where(j_col == cur - 1, FORCE_SCORE, score)
    score = jnp.where(j_col == cur, FORCE_SCORE, score)
    score = jnp.where(j_col == 0, FORCE_SCORE, score)
    score = jnp.where(j_col < n_sel, score, -3e38)
    score_scr[...] = score

    def rank_body(jp, rank):
        row = score_scr[pl.ds(jp, 1), :]
        tie = jnp.where(j_col > jp, 1.0, 0.0)
        return rank + jnp.where(row > score, 1.0, jnp.where(row == score, tie, 0.0))

    n_rank = jnp.minimum((q0 + tq - 1) // SEL_BLOCK + 1, LANE)
    rank = lax.fori_loop(0, n_rank, rank_body, jnp.zeros((LANE, tq), F32))
    sel_t = jnp.where(rank < SEL_TOPN, 1.0, 0.0).astype(BF16)
    eye = (_iota((tq, tq), 0) == _iota((tq, tq), 1)).astype(BF16)
    sel = _dot_nt(eye, sel_t).astype(BF16)

    qs = (q_ref[0].reshape(R, HD) * scale).astype(BF16)
    log2_blk = int(math.log2(SEL_BLOCK))

    def stack(a):
        return jnp.concatenate([a] * NSA_R, axis=0)

    def rel_dist(pos0, width):
        return (q0 + _iota((tq, width), 0)) - (pos0 + _iota((tq, width), 1))

    def near_bias(offset, width):
        tiles = [bias_scr[hi(lo(offset - j * NSA_TK, 0) // LANE, NSA_BIAS_TILES)] for j in range(width // NSA_TK)]
        return jnp.concatenate(tiles, axis=1) if len(tiles) > 1 else tiles[0]

    def sel_tile(k0, width, near):
        k = ks_ref[0, 0, pl.ds(k0, width), :]
        v = vs_ref[0, 0, pl.ds(k0, width), :]
        blk = lax.shift_right_logical(k0 + _iota((LANE, width), 1), log2_blk)
        expand = jnp.where(_iota((LANE, width), 0) == blk, 1.0, 0.0).astype(BF16)
        chosen = _dot(sel, expand)
        s = _dot_nt(qs, k)
        if near:
            chosen = jnp.where(rel_dist(k0, width) >= 0, chosen, 0.0)
            s = s + near_bias(q0 - k0, width)
        s = jnp.where(stack(chosen) > 0.5, s, NEG_BIG)
        _flash_tile(_lane_blocks(s), v, m_scr, l_scr, acc_scr)

    _flash_reset(m_scr, l_scr, acc_scr)
    kt_top = (q0 + tq - 1) // NSA_TK
    near_w = min(2 * NSA_TK, ks_ref.shape[2])
    sel_tile(aligned(lo(kt_top - 1, 0) * NSA_TK, NSA_TK), near_w, True)

    n_far = lo(kt_top - 1, 0)
    n_groups = n_far // NSA_FAR_GROUP

    def sel_far_group(gi, carry):
        sel_tile(pl.multiple_of(gi * (NSA_FAR_GROUP * NSA_TK), NSA_FAR_GROUP * NSA_TK), NSA_FAR_GROUP * NSA_TK, False)
        return carry

    def sel_far(kt, carry):
        sel_tile(pl.multiple_of(kt * NSA_TK, NSA_TK), NSA_TK, False)
        return carry

    if single:
        if n_far:
            sel_tile(0, n_far * NSA_TK, False)
    else:
        lax.fori_loop(0, n_groups, sel_far_group, 0)
        lax.fori_loop(n_groups * NSA_FAR_GROUP, n_far, sel_far, 0)
    o_s = acc_scr[...] / jnp.maximum(l_scr[...][:, :HD], 1e-30)

    n_win = -(-(WINDOW - 1 + tq) // NSA_TK) + (0 if tq % NSA_TK == 0 else 1)
    win_w = min(n_win * NSA_TK, tw)
    wt_top = (hi(q0 + tq - win_pos0, tw) - 1) // NSA_TK
    k0 = aligned(hi(lo(wt_top + 1 - win_w // NSA_TK, 0), (tw - win_w) // NSA_TK) * NSA_TK, NSA_TK)
    _flash_reset(m_scr, l_scr, acc_scr)
    dist = rel_dist(win_pos0 + k0, win_w)
    ok = jnp.where(dist >= 0, jnp.where(dist < WINDOW, 1.0, 0.0), 0.0)
    s = _dot_nt(qs, kw_ref[0, 0, pl.ds(k0, win_w), :]) + near_bias(q0 - win_pos0 - k0, win_w)
    s = jnp.where(stack(ok) > 0.5, s, NEG_BIG)
    _flash_tile(_lane_blocks(s), vw_ref[0, 0, pl.ds(k0, win_w), :], m_scr, l_scr, acc_scr)
    o_w = acc_scr[...] / jnp.maximum(l_scr[...][:, :HD], 1e-30)

    gates = gate_ref[0, 0]
    for r in range(NSA_R):
        o_ref[0, r] = (gates[:, 3 * r:3 * r + 1] * o_c[r]
                       + gates[:, 3 * r + 1:3 * r + 2] * o_s[r * tq:(r + 1) * tq]
                       + gates[:, 3 * r + 2:3 * r + 3] * o_w[r * tq:(r + 1) * tq])


def _selection_overlap_t(ncp):
    c_start = np.arange(ncp)[None, :] * CMP_STRIDE
    s_start = np.arange(LANE)[:, None] * SEL_BLOCK
    return ((c_start < s_start + SEL_BLOCK) & (c_start + CMP_BLOCK > s_start)).astype(np.float32)


def nsa_attend(tbl, q, gates, kcvc, sel_arr, sel_off, win_arr, win_off, *, tq, q_pos0, win_pos0, n_sel):
    B, _, Tq, _ = q.shape
    ncp = kcvc.shape[2]
    tks = sel_arr.shape[2]
    tw = win_arr.shape[2]
    R = NSA_R * tq
    smap = jnp.asarray(_selection_overlap_t(ncp))
    kv_spec = lambda rows, off: pl.BlockSpec((1, 1, rows, HD), lambda b, g, i: (b, off + g, 0, 0))
    return pl.pallas_call(
        functools.partial(_nsa_body, tq=tq, q_pos0=q_pos0, win_pos0=win_pos0, n_sel=n_sel, ncp=ncp, tw=tw,
                          single=(Tq == tq)),
        grid=(B, NSA_G, Tq // tq),
        in_specs=[pl.BlockSpec(memory_space=pltpu.SMEM),
                  pl.BlockSpec((1, NSA_R, tq, HD), lambda b, g, i: (b, g, i, 0)),
                  pl.BlockSpec((1, 1, tq, 3 * NSA_R), lambda b, g, i: (b, g, i, 0)),
                  kv_spec(ncp, 0), kv_spec(ncp, 2),
                  kv_spec(tks, sel_off), kv_spec(tks, sel_off + 2),
                  kv_spec(tw, win_off), kv_spec(tw, win_off + 2),
                  pl.BlockSpec((LANE, ncp), lambda b, g, i: (0, 0))],
        out_specs=pl.BlockSpec((1, NSA_R, tq, HD), lambda b, g, i: (b, g, i, 0)),
        out_shape=jax.ShapeDtypeStruct((B, NSA_G * NSA_R, Tq, HD), F32),
        scratch_shapes=[pltpu.VMEM((NSA_BIAS_TILES + 1, R, NSA_TK), F32), pltpu.VMEM((NSA_R, tq, LANE), F32),
                        pltpu.VMEM((LANE, tq), F32),
                        pltpu.VMEM((R, LANE), F32), pltpu.VMEM((R, LANE), F32), pltpu.VMEM((R, HD), F32)],
        compiler_params=_cparams(("parallel", "parallel", "arbitrary")),
        name="nsa_attend",
    )(tbl, q, gates, kcvc, kcvc, sel_arr, sel_arr, win_arr, win_arr, smap)


def _nsa_gather_body(pt_ref, new_ref, *refs, n_pages):
    pages = refs[:n_pages]
    cmp_ref, sel_ref, xt_scr = refs[n_pages:]
    ngrp = 2 * NSA_G
    half = ngrp * HD
    rows16 = PAGE // CMP_STRIDE
    for pg in range(n_pages):
        xt = pages[pg][0].T
        for j in range(ngrp):
            sel_ref[0, j, pg * PAGE:(pg + 1) * PAGE, :] = xt[:, half + j * HD:half + (j + 1) * HD].astype(BF16)
        for slab in range(half // LANE):
            xt_scr[slab] = xt[:, slab * LANE:(slab + 1) * LANE]
            for p in range(CMP_STRIDE):
                rows = xt_scr[slab, pl.ds(p, rows16, stride=CMP_STRIDE), :]
                for jj in range(LANE // HD):
                    cmp_ref[0, slab * (LANE // HD) + jj, pg * rows16:(pg + 1) * rows16, p * HD:(p + 1) * HD] = (
                        rows[:, jj * HD:(jj + 1) * HD])
    tn = new_ref.shape[1]
    tail = sel_ref.shape[2] - n_pages * PAGE
    for j in range(ngrp):
        new = new_ref[0, :, (ngrp + j) * HD:(ngrp + j + 1) * HD]
        tile = jnp.concatenate([new, jnp.zeros((tail - tn, HD), F32)], axis=0)
        sel_ref[0, j, n_pages * PAGE:, :] = tile.astype(BF16)


def nsa_gather(page_table, new_rows, pool_t):
    B, tn, width = new_rows.shape
    n_pages = page_table.shape[1]
    ngrp = 2 * NSA_G
    sel_rows = -(-(n_pages * PAGE + tn) // NSA_TK) * NSA_TK
    cmp_rows = n_pages * PAGE // CMP_STRIDE
    grid_spec = pltpu.PrefetchScalarGridSpec(
        num_scalar_prefetch=1,
        grid=(B,),
        in_specs=[pl.BlockSpec((1, tn, width), lambda b, pt: (b, 0, 0))]
                 + [pl.BlockSpec((1, 2 * ngrp * HD, PAGE),
                                 functools.partial(lambda b, pt, pg: (pt[b, pg], 0, 0), pg=pg))
                    for pg in range(n_pages)],
        out_specs=[pl.BlockSpec((1, ngrp, cmp_rows, CMP_STRIDE * HD), lambda b, pt: (b, 0, 0, 0)),
                   pl.BlockSpec((1, ngrp, sel_rows, HD), lambda b, pt: (b, 0, 0, 0))],
        scratch_shapes=[pltpu.VMEM((ngrp * HD // LANE, PAGE, LANE), F32)],
    )
    return pl.pallas_call(
        functools.partial(_nsa_gather_body, n_pages=n_pages),
        grid_spec=grid_spec,
        out_shape=[jax.ShapeDtypeStruct((B, ngrp, cmp_rows, CMP_STRIDE * HD), F32),
                   jax.ShapeDtypeStruct((B, ngrp, sel_rows, HD), BF16)],
        compiler_params=_cparams(("arbitrary",)),
        name="nsa_gather",
    )(page_table, new_rows, *([pool_t] * n_pages))


def _row_tile(m):
    return 512 if m % 512 == 0 else m


def _small_params(entries):
    sp = jnp.zeros((8, LANE), F32)
    for off, bias, act, log_scale in entries:
        n = bias.shape[0]
        sp = sp.at[0, off:off + n].set(bias.astype(F32))
        sp = sp.at[1, off:off + n].set(act)
        if log_scale is not None:
            sp = sp.at[2, off:off + n].set(log_scale.astype(F32))
    return sp


EVEN_WIDTHS = (256, 256, 512, 512, GD_CH, 512, LANE)


def _even_weights(w_in):
    s = np.cumsum((0, 256, 256, 512, 512, 4, 4, 512, 512, 512, 512, 4, 4))
    col = lambda i: w_in[:, s[i]:s[i + 1]]
    small = jnp.concatenate([col(4), col(5), col(10), col(11)], axis=1)
    small = jnp.pad(small, ((0, 0), (0, LANE - small.shape[1])))
    return jnp.concatenate([col(0), col(1), col(2), col(3), col(6), col(7), col(8), col(9), small], axis=1).astype(BF16)


def _chunk_rows(small, B, T, L, lanes):
    r = small.reshape(B, T // L, L, LANE)[..., lanes[0]:lanes[1]]
    r = jnp.swapaxes(r, 2, 3)
    return jnp.pad(r, ((0, 0), (0, 0), (0, 8 - r.shape[2]), (0, 0)))


def even_layer(x, p, past, L, Bb):
    B, T, D = x.shape
    M = B * T
    tm = _row_tile(M)
    sp = _small_params([(0, p['mi_b_i'], ACT_ID, None), (4, p['mi_b_f'], ACT_LOGSIG, None),
                        (8, jnp.zeros((4,), F32), ACT_SIG, None), (12, p['gd_dt_bias'], ACT_DECAY, p['gd_a_log'])])
    mq, mk, mv, mo, gx, gz, small = norm_proj(x.reshape(M, D), p['norm_mix'], _even_weights(p['w_in']), sp,
                                              EVEN_WIDTHS, tm, T, head_major=(0, 1))
    sc = small.reshape(B, T, LANE)
    if past is None:
        c0 = jnp.zeros((B, MI_H, MI_DQK, MI_DV), F32)
        n0 = jnp.zeros((B, MI_H, 1, MI_DQK), F32)
        m0 = jnp.zeros((B, 1, MI_H), F32)
        s0 = jnp.zeros((B, GD_H, GD_DK, GD_DV), F32)
        conv0 = jnp.zeros((B, GD_CONV - 1, GD_CH), F32)
    else:
        c0, n0, m0, s0, conv0 = past
        n0 = n0.reshape(B, MI_H, 1, MI_DQK)
        m0 = m0.reshape(B, 1, MI_H)
    hm, c1, n1, m1 = mlstm(mq, mk, mv.reshape(B, T, -1), mo.reshape(B, T, -1), sc,
                           _chunk_rows(small, B, T, L, (0, 8)), c0, n0, m0,
                           p['mi_norm'].reshape(MI_H, MI_DV), L, Bb)
    og, s1, conv1 = gdn(gx.reshape(B, T, GD_CH), gz.reshape(B, T, -1), sc,
                        _chunk_rows(small, B, T, L, (12, 16)), s0, conv0, p['gd_conv_w'], p['gd_norm'], L, Bb)
    y = out_proj_residual(x.reshape(M, D), hm.reshape(M, -1), og.reshape(M, -1), p['w_out'].astype(BF16), tm)
    return y.reshape(B, T, D), (c1, n1.reshape(B, MI_H, MI_DQK), m1.reshape(B, MI_H), s1, conv1)


NSA_QW = NSA_G * NSA_R * HD
NSA_KVW = 6 * NSA_G * HD
NSA_CACHE_W = 4 * NSA_G * HD
FOX_W = FOX_H * HD
N_GATE = 3 * NSA_G * NSA_R
ODD_WIDTHS = (NSA_QW, NSA_KVW, FOX_W, 2 * FOX_W, LANE)


def _odd_weights(w_in):
    s = np.cumsum((0, NSA_QW, NSA_KVW, N_GATE, FOX_W, FOX_W, FOX_W, FOX_H))
    col = lambda i: w_in[:, s[i]:s[i + 1]]
    small = jnp.concatenate([col(2), col(6)], axis=1)
    small = jnp.pad(small, ((0, 0), (0, LANE - small.shape[1])))
    return jnp.concatenate([col(0), col(1), col(3), col(4), col(5), small], axis=1).astype(BF16)


def _heads(a, B, T, n):
    return jnp.transpose(a.reshape(B, T, n, HD), (0, 2, 1, 3))


def _unheads(a):
    B, n, T, _ = a.shape
    return jnp.transpose(a, (0, 2, 1, 3)).reshape(B * T, n * HD)


def odd_layer(x, p, rel_bias, w_buf, past, page_table):
    B, T, D = x.shape
    M = B * T
    tm = _row_tile(M)
    sp = _small_params([(0, jnp.zeros((N_GATE,), F32), ACT_SIG, None), (N_GATE, p['fox_b_f'], ACT_LOGSIG, None)])
    q_heads, nkv, fq_heads, fkv, small = norm_proj(x.reshape(M, D), p['norm_mix'], _odd_weights(p['w_in']), sp,
                                                   ODD_WIDTHS, tm, T, head_major=(0, 2))
    new_nsa = nkv[:, :NSA_CACHE_W].reshape(B, T, 4, NSA_G, HD)
    new_win = nkv[:, NSA_CACHE_W:].reshape(B, T, 2, NSA_G, HD)
    new_fox = fkv.reshape(B, T, 2, FOX_H, HD)
    logf = small[:, N_GATE:N_GATE + FOX_H].reshape(B, T, FOX_H)
    gates = jnp.transpose(small[:, :N_GATE].reshape(B, T, NSA_G, 3 * NSA_R), (0, 2, 1, 3))
    cmp_args = (p['nsa_cmp_pos'], p['nsa_cmp_w1'], p['nsa_cmp_w2'])
    rows16 = CMP_STRIDE * HD
    if past is None:
        groups = _heads(nkv, B, T, 6 * NSA_G)
        kcvc = nsa_compress(groups.reshape(B, 6 * NSA_G, T // CMP_STRIDE, rows16), T // CMP_STRIDE, *cmp_args, 1)
        arr = groups.astype(BF16)
        o_n = nsa_attend(rel_bias, q_heads, gates, kcvc, arr, 2 * NSA_G, arr, 4 * NSA_G,
                         tq=min(T, NSA_TQ), q_pos0=0, win_pos0=0, n_sel=-(-T // SEL_BLOCK))
        f_t = cumsum_lanes(jnp.transpose(logf, (0, 2, 1)), B)
        kv_heads = _heads(fkv, B, T, 2 * FOX_H)
        q_aug, k_aug = fox_operands(fq_heads, kv_heads[:, :FOX_H], f_t)
        o_f = fox_prompt(q_aug, k_aug, kv_heads.astype(BF16), min(T, FOX_TQ), min(T, FOX_TK))
        o_f = _unheads(o_f)
        win_prev = jnp.zeros((B, WINDOW, 2, NSA_G, HD), F32)
    else:
        nsa_pool, win_prev, fox_pool, logf_pool = past
        n_pool = nsa_pool.shape[0]
        n_pages = page_table.shape[1]
        start = n_pages * PAGE
        feature_major = lambda pool: jnp.moveaxis(pool.reshape(n_pool, PAGE, -1), 1, 2)
        cmp_rows, sel_arr = nsa_gather(page_table, nkv[:, :NSA_CACHE_W].reshape(B, T, NSA_CACHE_W),
                                       feature_major(nsa_pool))
        kcvc = nsa_compress(cmp_rows, start // CMP_STRIDE, *cmp_args, math.gcd(B, 8))
        wp = win_prev.shape[1]
        win_all = jnp.concatenate([win_prev.reshape(B, wp, 2 * NSA_G * HD), nkv[:, NSA_CACHE_W:].reshape(B, T, -1)], 1)
        tw = -(-(wp + T) // NSA_TK) * NSA_TK
        win_arr = _heads(jnp.pad(win_all, ((0, 0), (0, tw - wp - T), (0, 0))), B, tw, 2 * NSA_G).astype(BF16)
        o_n = nsa_attend(rel_bias, q_heads, gates, kcvc, sel_arr, 0, win_arr, 0,
                         tq=T, q_pos0=start, win_pos0=start - wp, n_sel=-(-(start + T) // SEL_BLOCK))
        o_f = fox_decode(page_table, _unheads(fq_heads * HD ** -0.5).reshape(B, T, FOX_W),
                         fkv.reshape(B, T, 2 * FOX_W), logf,
                         feature_major(fox_pool), feature_major(logf_pool))
        o_f = o_f.reshape(M, FOX_W)
    win_state = jnp.concatenate([win_prev, new_win], axis=1)[:, -w_buf:]
    y = out_proj_residual(x.reshape(M, D), _unheads(o_n), o_f, p['w_out'].astype(BF16), tm)
    return y.reshape(B, T, D), (new_nsa, win_state, new_fox, logf)


def _trunk(x, past, page_table, P, w_buf, L, Bb):
    B, T, D = x.shape
    pe = dict(norm_mix=P['norm_mix'][0], w_in=P['w_in_even'][0], w_out=P['w_out_even'][0], mi_b_i=P['mi_b_i'][0],
              mi_b_f=P['mi_b_f'][0], mi_norm=P['mi_norm'][0], gd_conv_w=P['gd_conv_w'][0], gd_a_log=P['gd_a_log'][0],
              gd_dt_bias=P['gd_dt_bias'][0], gd_norm=P['gd_norm'][0])
    po = dict(norm_mix=P['norm_mix'][1], w_in=P['w_in_odd'][0], w_out=P['w_out_odd'][0],
              nsa_cmp_pos=P['nsa_cmp_pos'][0], nsa_cmp_w1=P['nsa_cmp_w1'][0], nsa_cmp_w2=P['nsa_cmp_w2'][0],
              fox_b_f=P['fox_b_f'][0])
    tm = _row_tile(B * T)
    mlp = lambda x, layer, final: mlp_residual(
        x.reshape(B * T, D), P['norm_mlp'][layer], P['w_up'][layer].astype(BF16), P['w_down'][layer].astype(BF16),
        P['norm_final'], final, tm, 1024).reshape(B, T, D)
    even_past = None if past is None else tuple(past[k][0] for k in ('mc', 'mn', 'mm', 'gs', 'gc'))
    odd_past = None if past is None else tuple(past[k][0] for k in ('nsa_kv', 'nsa_win', 'fox_kv', 'fox_logf'))
    x, st_e = even_layer(x, pe, even_past, L, Bb)
    x = mlp(x, 0, False)
    x, st_o = odd_layer(x, po, P['rel_bias'], w_buf, odd_past, page_table)
    y = mlp(x, 1, True)
    return y, tuple(a[None] for a in st_e + st_o)


def kernel(x_prompt, x_sample, state_mlstm_c, state_mlstm_n, state_mlstm_m, state_gdn_s, state_gdn_conv,
           cache_nsa_kv, state_nsa_win, cache_fox_kv, cache_fox_logf, page_table,
           norm_mix, norm_mlp, norm_final, w_up, w_down,
           w_in_even, w_out_even, mi_b_i, mi_b_f, mi_norm, gd_conv_w, gd_a_log, gd_dt_bias, gd_norm,
           w_in_odd, w_out_odd, nsa_cmp_pos, nsa_cmp_w1, nsa_cmp_w2, fox_b_f, rel_bias):
    P = dict(norm_mix=norm_mix, norm_mlp=norm_mlp, norm_final=norm_final, w_up=w_up, w_down=w_down,
             w_in_even=w_in_even, w_out_even=w_out_even, mi_b_i=mi_b_i, mi_b_f=mi_b_f, mi_norm=mi_norm,
             gd_conv_w=gd_conv_w, gd_a_log=gd_a_log, gd_dt_bias=gd_dt_bias, gd_norm=gd_norm,
             w_in_odd=w_in_odd, w_out_odd=w_out_odd, nsa_cmp_pos=nsa_cmp_pos, nsa_cmp_w1=nsa_cmp_w1,
             nsa_cmp_w2=nsa_cmp_w2, fox_b_f=fox_b_f, rel_bias=rel_bias)
    w_buf = state_nsa_win.shape[2]
    b_p, t_p = x_prompt.shape[:2]
    y_p, st_p = _trunk(x_prompt, None, None, P, w_buf, math.gcd(t_p, 64), math.gcd(b_p, RECURRENT_BATCH))
    past = dict(mc=state_mlstm_c, mn=state_mlstm_n, mm=state_mlstm_m, gs=state_gdn_s, gc=state_gdn_conv,
                nsa_kv=cache_nsa_kv, nsa_win=state_nsa_win, fox_kv=cache_fox_kv, fox_logf=cache_fox_logf)
    b_s, t_s = x_sample.shape[:2]
    y_s, st_s = _trunk(x_sample, past, page_table, P, w_buf, math.gcd(t_s, 64), math.gcd(b_s, RECURRENT_BATCH))
    return (y_p, y_s) + st_p + st_s
```

```python
import functools
import math

import jax
import jax.numpy as jnp
import numpy as np
from jax import lax
from jax.experimental import pallas as pl
from jax.experimental.pallas import tpu as pltpu

F32 = jnp.float32
BF16 = jnp.bfloat16
ROW_TILE = 512
MLP_FF_TILE = 1024

D_MODEL = 1024
D_FF = 4 * D_MODEL
EPS = 1e-6
NEG_BIG = -1e30
PAGE = 128

MI_H, MI_DQK, MI_DV = 4, 64, 128
GD_H, GD_DK, GD_DV, GD_CONV = 4, 128, 128, 4
GD_CH = 3 * GD_H * GD_DK
NSA_G, NSA_R, HD = 2, 4, 64
FOX_H = 8
CMP_BLOCK, CMP_STRIDE, CMP_HIDDEN = 32, 16, 256
SEL_BLOCK, SEL_TOPN, WINDOW = 64, 16, 512
FORCE_SCORE = 1e4
N_BUCKETS, MAX_DISTANCE = 32, 128
BUCKET_EXACT = N_BUCKETS // 2
BUCKET_SAT_DIST = 113
LANE = 128
VMEM_LIMIT = 56 * 1024 * 1024


def _cparams(sem):
    return pltpu.CompilerParams(dimension_semantics=sem, vmem_limit_bytes=VMEM_LIMIT)


def _dot(a, b, precision=None):
    return jnp.dot(a, b, preferred_element_type=F32, precision=precision)


def _dot_nt(a, b, precision=None):
    return lax.dot_general(a, b, (((1,), (1,)), ((), ())), preferred_element_type=F32, precision=precision)


def _dot_tn(a, b, precision=None):
    return lax.dot_general(a, b, (((0,), (0,)), ((), ())), preferred_element_type=F32, precision=precision)


def _softplus(x):
    return jnp.maximum(x, 0.0) + jnp.log1p(jnp.exp(-jnp.abs(x)))


def _sigmoid(x):
    return 1.0 / (1.0 + jnp.exp(-x))


def _silu(x):
    return x * _sigmoid(x)


def _iota(shape, dim):
    return lax.broadcasted_iota(jnp.int32, shape, dim)


ACT_ID, ACT_LOGSIG, ACT_SIG, ACT_DECAY = 0.0, 1.0, 2.0, 3.0


def _proj_body(x_ref, g_ref, w_ref, sp_ref, *out_refs, widths, head_major):
    x = x_ref[...]
    hn = (x * lax.rsqrt(jnp.mean(x * x, axis=-1, keepdims=True) + EPS) * g_ref[...]).astype(BF16)
    off = 0
    for i, (o_ref, n) in enumerate(zip(out_refs, widths)):
        r = _dot(hn, w_ref[:, off:off + n])
        if i == len(widths) - 1:
            z = r + sp_ref[0:1, :]
            mode = sp_ref[1:2, :]
            decay = -jnp.exp(sp_ref[2:3, :]) * _softplus(z)
            r = jnp.where(mode == ACT_LOGSIG, -_softplus(-z),
                          jnp.where(mode == ACT_SIG, _sigmoid(z),
                                    jnp.where(mode == ACT_DECAY, decay, z)))
        if i in head_major:
            bt, _, tt, _ = o_ref.shape
            r3 = r.reshape(bt, tt, n)
            for j in range(n // HD):
                o_ref[:, j] = r3[:, :, j * HD:(j + 1) * HD]
        else:
            o_ref[...] = r.astype(o_ref.dtype)
        off += n


def norm_proj(x, g, w_bf16, small_params, widths, tm, rows_per_batch, head_major=()):
    m, d = x.shape
    n_total = sum(widths)
    T = rows_per_batch
    assert w_bf16.shape == (d, n_total) and m % tm == 0 and (tm % T == 0 or T % tm == 0)
    bt, tt = max(tm // T, 1), min(tm, T)
    per_b = T // tt
    out_shape, out_specs = [], []
    for i, n in enumerate(widths):
        if i in head_major:
            out_shape.append(jax.ShapeDtypeStruct((m // T, n // HD, T, HD), F32))
            out_specs.append(pl.BlockSpec((bt, n // HD, tt, HD),
                                          (lambda i: (i, 0, 0, 0)) if bt > 1 else
                                          (lambda i: (i // per_b, 0, i % per_b, 0))))
        else:
            out_shape.append(jax.ShapeDtypeStruct((m, n), F32))
            out_specs.append(pl.BlockSpec((tm, n), lambda i: (i, 0)))
    return pl.pallas_call(
        functools.partial(_proj_body, widths=tuple(widths), head_major=tuple(head_major)),
        grid=(m // tm,),
        in_specs=[pl.BlockSpec((tm, d), lambda i: (i, 0)),
                  pl.BlockSpec((1, d), lambda i: (0, 0)),
                  pl.BlockSpec((d, n_total), lambda i: (0, 0)),
                  pl.BlockSpec((8, LANE), lambda i: (0, 0))],
        out_specs=out_specs,
        out_shape=out_shape,
        compiler_params=_cparams(("parallel",)),
        name="norm_proj",
    )(x, g.reshape(1, d), w_bf16, small_params)


def _outproj_body(x_ref, a1_ref, a2_ref, w_ref, o_ref):
    k1 = a1_ref.shape[1]
    y = _dot(a1_ref[...].astype(BF16), w_ref[0:k1, :]) + _dot(a2_ref[...].astype(BF16), w_ref[k1:, :])
    o_ref[...] = x_ref[...] + y


def out_proj_residual(x, a1, a2, w_bf16, tm):
    m, d = x.shape
    k1, k2 = a1.shape[1], a2.shape[1]
    return pl.pallas_call(
        _outproj_body,
        grid=(m // tm,),
        in_specs=[pl.BlockSpec((tm, d), lambda i: (i, 0)),
                  pl.BlockSpec((tm, k1), lambda i: (i, 0)),
                  pl.BlockSpec((tm, k2), lambda i: (i, 0)),
                  pl.BlockSpec((k1 + k2, d), lambda i: (0, 0))],
        out_specs=pl.BlockSpec((tm, d), lambda i: (i, 0)),
        out_shape=jax.ShapeDtypeStruct((m, d), F32),
        compiler_params=_cparams(("parallel",)),
        name="out_proj",
    )(x, a1, a2, w_bf16)


def _mlp_body(x_ref, g_ref, wu_ref, wd_ref, gf_ref, o_ref, hn_scr, acc_scr, *, final_norm):
    j = pl.program_id(1)

    @pl.when(j == 0)
    def _():
        x = x_ref[...]
        hn_scr[...] = (x * lax.rsqrt(jnp.mean(x * x, axis=-1, keepdims=True) + EPS) * g_ref[...]).astype(BF16)
        acc_scr[...] = jnp.zeros_like(acc_scr)

    u = jnp.maximum(_dot(hn_scr[...], wu_ref[...]), 0.0)
    acc_scr[...] += _dot((u * u).astype(BF16), wd_ref[...])

    @pl.when(j == pl.num_programs(1) - 1)
    def _():
        y = x_ref[...] + acc_scr[...]
        if final_norm:
            y = y * lax.rsqrt(jnp.mean(y * y, axis=-1, keepdims=True) + EPS) * gf_ref[...]
        o_ref[...] = y


def mlp_residual(x, g, w_up_bf16, w_down_bf16, g_final, final_norm, tm, tf):
    m, d = x.shape
    f = w_up_bf16.shape[1]
    return pl.pallas_call(
        functools.partial(_mlp_body, final_norm=final_norm),
        grid=(m // tm, f // tf),
        in_specs=[pl.BlockSpec((tm, d), lambda i, j: (i, 0)),
                  pl.BlockSpec((1, d), lambda i, j: (0, 0)),
                  pl.BlockSpec((d, tf), lambda i, j: (0, j)),
                  pl.BlockSpec((tf, d), lambda i, j: (j, 0)),
                  pl.BlockSpec((1, d), lambda i, j: (0, 0))],
        out_specs=pl.BlockSpec((tm, d), lambda i, j: (i, 0)),
        out_shape=jax.ShapeDtypeStruct((m, d), F32),
        scratch_shapes=[pltpu.VMEM((tm, d), BF16), pltpu.VMEM((tm, d), F32)],
        compiler_params=_cparams(("parallel", "arbitrary")),
        name="mlp",
    )(x, g.reshape(1, d), w_up_bf16, w_down_bf16, g_final.reshape(1, d))


def _tri(n):
    r = _iota((n, n), 0)
    c = _iota((n, n), 1)
    return r, c


def _bf16_terms(a, n):
    terms, rest = [], a
    for _ in range(n):
        t = rest.astype(BF16)
        terms.append(t)
        rest = rest - t.astype(F32)
    return terms


def _dot_split(a, b, f=None):
    f = f or _dot
    a_hi, a_lo = _bf16_terms(a, 2)
    b_hi, b_lo = _bf16_terms(b, 2)
    return f(a_hi, b_hi) + (f(a_hi, b_lo) + f(a_lo, b_hi))


def _dot_pick(a, b, f=None, exact_lhs=False):
    f = f or _dot
    if exact_lhs:
        return sum(f(a.astype(BF16), t) for t in _bf16_terms(b, 3))
    return sum(f(t, b.astype(BF16)) for t in _bf16_terms(a, 3))


def _chunk_dots(L):
    if L % 16 == 0:
        cast = lambda f: (lambda a, b: f(a.astype(BF16), b.astype(BF16)))
        return cast(_dot), cast(_dot_nt), cast(_dot_tn)
    full = lambda f: (lambda a, b: _dot_split(a, b, f))
    return full(_dot), full(_dot_nt), full(_dot_tn)


def _mlstm_group(bbs, q_ref, k_ref, v_ref, o_ref, sc_ref, sr_ref, nw_ref, h_ref, c_ref, n_ref, m_ref, L):
    nn, nt, tn = _chunk_dots(L)
    rows, cols = _tri(L)
    lower = rows >= cols
    tril = lower.astype(F32)
    triu = (rows <= cols).astype(F32)
    lane4 = _iota((1, MI_H), 1)
    scs = {bb: sc_ref[bb] for bb in bbs}
    srs = {bb: sr_ref[bb, 0] for bb in bbs}
    b_col = {bb: _dot_pick(tril, scs[bb][:, 4:8], exact_lhs=True) for bb in bbs}
    b_row = {bb: _dot_pick(srs[bb][4:8, :], triu) for bb in bbs}
    m_vec = {bb: m_ref[bb] for bb in bbs}
    chains = [(bb, h) for bb in bbs for h in range(MI_H)]
    q = {c: q_ref[c[0], c[1]] for c in chains}
    k = {c: k_ref[c[0], c[1]] * (MI_DQK ** -0.5) for c in chains}
    v = {c: v_ref[c[0], :, c[1] * MI_DV:(c[1] + 1) * MI_DV] for c in chains}
    qk = {c: nt(q[c], k[c]) for c in chains}
    c_prev = {c: c_ref[c[0], c[1]] for c in chains}
    qc = {c: nn(q[c], c_prev[c]) for c in chains}
    s, a_inter, m_t, m_new, a_prev, kw = {}, {}, {}, {}, {}, {}
    for c in chains:
        bb, h = c
        bc = b_col[bb][:, h:h + 1]
        m_prev = m_vec[bb][:, h:h + 1]
        d = jnp.where(lower, bc - b_row[bb][h:h + 1, :] + srs[bb][h:h + 1, :], NEG_BIG)
        inter = bc + m_prev
        m_t[c] = jnp.maximum(inter, jnp.max(d, axis=1, keepdims=True))
        s[c] = qk[c] * jnp.exp(d - m_t[c])
        a_inter[c] = jnp.exp(inter - m_t[c])
        b_last = bc[L - 1:L, :]
        g_col = b_last - bc + scs[bb][:, h:h + 1]
        m_new[c] = jnp.maximum(b_last + m_prev, jnp.max(g_col, axis=0, keepdims=True))
        a_prev[c] = jnp.exp(b_last + m_prev - m_new[c])
        kw[c] = k[c] * jnp.exp(g_col - m_new[c])
    sv = {c: nn(s[c], v[c]) for c in chains}
    kv = {c: tn(kw[c], v[c]) for c in chains}
    for c in chains:
        bb, h = c
        n_prev = n_ref[bb, h]
        num = sv[c] + a_inter[c] * qc[c]
        den = (jnp.sum(s[c], axis=1, keepdims=True)
               + a_inter[c] * jnp.sum(q[c] * n_prev, axis=1, keepdims=True))
        hh = num / jnp.maximum(jnp.abs(den), jnp.exp(-m_t[c]))
        c_ref[bb, h] = a_prev[c] * c_prev[c] + kv[c]
        n_ref[bb, h] = a_prev[c] * n_prev + jnp.sum(kw[c], axis=0, keepdims=True)
        m_vec[bb] = jnp.where(lane4 == h, m_new[c], m_vec[bb])
        hn = hh * lax.rsqrt(jnp.mean(hh * hh, axis=-1, keepdims=True) + EPS) * nw_ref[h:h + 1, :]
        gate = _sigmoid(o_ref[bb, :, h * MI_DV:(h + 1) * MI_DV])
        h_ref[bb, :, h * MI_DV:(h + 1) * MI_DV] = hn * gate
    for bb in bbs:
        m_ref[bb] = m_vec[bb]


def _mlstm_body(q_ref, k_ref, v_ref, o_ref, sc_ref, sr_ref, c0_ref, n0_ref, m0_ref, nw_ref,
                h_ref, c_ref, n_ref, m_ref, *, L, Bb):
    @pl.when(pl.program_id(1) == 0)
    def _():
        c_ref[...] = c0_ref[...]
        n_ref[...] = n0_ref[...]
        m_ref[...] = m0_ref[...]

    _mlstm_group(list(range(Bb)), q_ref, k_ref, v_ref, o_ref, sc_ref, sr_ref, nw_ref, h_ref, c_ref, n_ref, m_ref, L)


def mlstm(q, k, v, o, sc, sr, c0, n0, m0, norm_w, L, Bb):
    B, H, T, _ = q.shape
    nc = T // L
    hv = H * MI_DV
    bmap = lambda b, c: (b, 0, 0, 0)
    return pl.pallas_call(
        functools.partial(_mlstm_body, L=L, Bb=Bb),
        grid=(B // Bb, nc),
        in_specs=[pl.BlockSpec((Bb, H, L, MI_DQK), lambda b, c: (b, 0, c, 0)),
                  pl.BlockSpec((Bb, H, L, MI_DQK), lambda b, c: (b, 0, c, 0)),
                  pl.BlockSpec((Bb, L, hv), lambda b, c: (b, c, 0)),
                  pl.BlockSpec((Bb, L, hv), lambda b, c: (b, c, 0)),
                  pl.BlockSpec((Bb, L, LANE), lambda b, c: (b, c, 0)),
                  pl.BlockSpec((Bb, 1, 8, L), lambda b, c: (b, c, 0, 0)),
                  pl.BlockSpec((Bb, H, MI_DQK, MI_DV), bmap),
                  pl.BlockSpec((Bb, H, 1, MI_DQK), bmap),
                  pl.BlockSpec((Bb, 1, H), lambda b, c: (b, 0, 0)),
                  pl.BlockSpec((H, MI_DV), lambda b, c: (0, 0))],
        out_specs=[pl.BlockSpec((Bb, L, hv), lambda b, c: (b, c, 0)),
                   pl.BlockSpec((Bb, H, MI_DQK, MI_DV), bmap),
                   pl.BlockSpec((Bb, H, 1, MI_DQK), bmap),
                   pl.BlockSpec((Bb, 1, H), lambda b, c: (b, 0, 0))],
        out_shape=[jax.ShapeDtypeStruct((B, T, hv), F32),
                   jax.ShapeDtypeStruct((B, H, MI_DQK, MI_DV), F32),
                   jax.ShapeDtypeStruct((B, H, 1, MI_DQK), F32),
                   jax.ShapeDtypeStruct((B, 1, H), F32)],
        compiler_params=_cparams(("parallel", "arbitrary")),
        name="mlstm",
    )(q, k, v, o, sc, sr, c0, n0, m0, norm_w)


def _gdn_group(bbs, x_ref, z_ref, sc_ref, sr_ref, cw_ref, nw_ref, o_ref, s_ref, conv_ref, xp_scr, L):
    base = 8 - (GD_CONV - 1)
    nn, nt, tn = _chunk_dots(L)
    rows, cols = _tri(L)
    incl = rows >= cols
    strict = rows > cols
    eye = (rows == cols).astype(F32)
    tril = incl.astype(F32)
    triu = (rows <= cols).astype(F32)
    kw = GD_H * GD_DK
    ys, scs, gcols, grows = {}, {}, {}, {}
    for bb in bbs:
        xp_scr[bb, 8:8 + L, :] = x_ref[bb]
        y = xp_scr[bb, base:base + L, :] * cw_ref[0:1, :]
        for j in range(1, GD_CONV):
            y = y + xp_scr[bb, base + j:base + j + L, :] * cw_ref[j:j + 1, :]
        tail = xp_scr[bb, L + base:L + 8, :]
        xp_scr[bb, base:8, :] = tail
        conv_ref[bb] = tail
        ys[bb] = _silu(y)
        scs[bb] = sc_ref[bb]
        gcols[bb] = _dot_pick(tril, scs[bb][:, 12:16], exact_lhs=True)
        grows[bb] = _dot_pick(sr_ref[bb, 0][0:4, :], triu)
    chains = [(bb, h) for bb in bbs for h in range(GD_H)]
    q, k, vb, kb, dec, gcc, egc = {}, {}, {}, {}, {}, {}, {}
    for c in chains:
        bb, h = c
        y = ys[bb]
        qh = y[:, h * GD_DK:(h + 1) * GD_DK]
        kh = y[:, kw + h * GD_DK:kw + (h + 1) * GD_DK]
        vh = y[:, 2 * kw + h * GD_DV:2 * kw + (h + 1) * GD_DV]
        q[c] = qh * lax.rsqrt(jnp.sum(qh * qh, axis=-1, keepdims=True) + EPS) * (GD_DK ** -0.5)
        k[c] = kh * lax.rsqrt(jnp.sum(kh * kh, axis=-1, keepdims=True) + EPS)
        beta = scs[bb][:, 8 + h:9 + h]
        gcc[c] = gcols[bb][:, h:h + 1]
        dec[c] = jnp.exp(jnp.where(incl, gcc[c] - grows[bb][h:h + 1, :], NEG_BIG))
        egc[c] = jnp.exp(gcc[c])
        kb[c] = k[c] * beta
        vb[c] = vh * beta
    pw = {c: -(nt(kb[c], k[c]) * jnp.where(strict, dec[c], 0.0)) for c in chains}
    attn = {c: nt(q[c], k[c]) * dec[c] for c in chains}
    tinv = {c: eye + pw[c] for c in chains}
    for _ in range(int(math.log2(L)) - 1):
        pw = {c: _dot_split(pw[c], pw[c]) for c in chains}
        tinv = {c: tinv[c] + _dot_split(tinv[c], pw[c]) for c in chains}
    u = {c: _dot_split(tinv[c], vb[c]) for c in chains}
    w = {c: _dot_split(tinv[c], kb[c] * egc[c]) for c in chains}
    s_prev = {c: s_ref[c[0], c[1]] for c in chains}
    v_new = {c: u[c] - nn(w[c], s_prev[c]) for c in chains}
    o = {c: nn(q[c] * egc[c], s_prev[c]) + nn(attn[c], v_new[c]) for c in chains}
    for c in chains:
        bb, h = c
        g_last = gcc[c][L - 1:L, :]
        s_ref[bb, h] = jnp.exp(g_last) * s_prev[c] + tn(k[c] * jnp.exp(g_last - gcc[c]), v_new[c])
        on = o[c] * lax.rsqrt(jnp.mean(o[c] * o[c], axis=-1, keepdims=True) + EPS) * nw_ref[...]
        o_ref[bb, :, h * GD_DV:(h + 1) * GD_DV] = on * _silu(z_ref[bb, :, h * GD_DV:(h + 1) * GD_DV])


def _gdn_body(x_ref, z_ref, sc_ref, sr_ref, s0_ref, conv0_ref, cw_ref, nw_ref,
              o_ref, s_ref, conv_ref, xp_scr, *, L, Bb):
    @pl.when(pl.program_id(1) == 0)
    def _():
        s_ref[...] = s0_ref[...]
        xp_scr[:, 8 - (GD_CONV - 1):8, :] = conv0_ref[...]

    _gdn_group(list(range(Bb)), x_ref, z_ref, sc_ref, sr_ref, cw_ref, nw_ref, o_ref, s_ref, conv_ref, xp_scr, L)


def gdn(x, z, sc, sr, s0, conv0, conv_w, norm_w, L, Bb):
    B, T, ch = x.shape
    H = GD_H
    nc = T // L
    hv = H * GD_DV
    bmap = lambda b, c: (b, 0, 0, 0)
    return pl.pallas_call(
        functools.partial(_gdn_body, L=L, Bb=Bb),
        grid=(B // Bb, nc),
        in_specs=[pl.BlockSpec((Bb, L, ch), lambda b, c: (b, c, 0)),
                  pl.BlockSpec((Bb, L, hv), lambda b, c: (b, c, 0)),
                  pl.BlockSpec((Bb, L, LANE), lambda b, c: (b, c, 0)),
                  pl.BlockSpec((Bb, 1, 8, L), lambda b, c: (b, c, 0, 0)),
                  pl.BlockSpec((Bb, H, GD_DK, GD_DV), bmap),
                  pl.BlockSpec((Bb, GD_CONV - 1, ch), lambda b, c: (b, 0, 0)),
                  pl.BlockSpec((GD_CONV, ch), lambda b, c: (0, 0)),
                  pl.BlockSpec((1, GD_DV), lambda b, c: (0, 0))],
        out_specs=[pl.BlockSpec((Bb, L, hv), lambda b, c: (b, c, 0)),
                   pl.BlockSpec((Bb, H, GD_DK, GD_DV), bmap),
                   pl.BlockSpec((Bb, GD_CONV - 1, ch), lambda b, c: (b, 0, 0))],
        out_shape=[jax.ShapeDtypeStruct((B, T, hv), F32),
                   jax.ShapeDtypeStruct((B, H, GD_DK, GD_DV), F32),
                   jax.ShapeDtypeStruct((B, GD_CONV - 1, ch), F32)],
        scratch_shapes=[pltpu.VMEM((Bb, L + 8, ch), F32)],
        compiler_params=_cparams(("parallel", "arbitrary")),
        name="gdn",
    )(x, z, sc, sr, s0, conv0, conv_w, norm_w.reshape(1, GD_DV))


def _compress_body(x_ref, pos_ref, w1_ref, w2_ref, o_ref):
    Bb, _, R, half = x_ref.shape
    x = x_ref[:, 0].reshape(Bb * R, half).astype(F32)
    ua = _dot((x + pos_ref[0, 0:1, :]).astype(BF16), w1_ref[0, 0])
    ub = _dot((x + pos_ref[0, 1:2, :]).astype(BF16), w1_ref[0, 1])
    h = _silu(ua + pltpu.roll(ub, Bb * R - 1, 0))
    o_ref[:, 0, 0:R, :] = _dot(h.astype(BF16), w2_ref[0]).reshape(Bb, R, HD)
    rp = o_ref.shape[2]
    if rp > R:
        o_ref[:, 0, R:rp, :] = jnp.zeros((Bb, rp - R, HD), F32)


def nsa_compress(xr, R, pos, w1, w2, Bb):
    B = xr.shape[0]
    half = CMP_STRIDE * HD
    rp = -(-R // LANE) * LANE
    return pl.pallas_call(
        _compress_body,
        grid=(4, B // Bb),
        in_specs=[pl.BlockSpec((Bb, 1, R, half), lambda c, b: (b, c, 0, 0)),
                  pl.BlockSpec((1, 2, half), lambda c, b: (c // 2, 0, 0)),
                  pl.BlockSpec((1, 2, half, CMP_HIDDEN), lambda c, b: (c // 2, 0, 0, 0)),
                  pl.BlockSpec((1, CMP_HIDDEN, HD), lambda c, b: (c // 2, 0, 0))],
        out_specs=pl.BlockSpec((Bb, 1, rp, HD), lambda c, b: (b, c, 0, 0)),
        out_shape=jax.ShapeDtypeStruct((B, 4, rp, HD), F32),
        compiler_params=_cparams(("parallel", "parallel")),
        name="nsa_compress",
    )(xr, pos.reshape(2, 2, half), w1.reshape(2, 2, half, CMP_HIDDEN).astype(BF16), w2.astype(BF16))


def _cumsum_body(x_ref, o_ref):
    Bb, H, T = x_ref.shape
    rows, cols = _tri(LANE)
    triu = (rows <= cols).astype(F32)
    carry = jnp.zeros((Bb * H, 1), F32)
    for c in range(T // LANE):
        seg = x_ref[:, :, c * LANE:(c + 1) * LANE].reshape(Bb * H, LANE)
        loc = _dot_pick(seg, triu) + carry
        o_ref[:, :, c * LANE:(c + 1) * LANE] = loc.reshape(Bb, H, LANE)
        carry = loc[:, LANE - 1:LANE]


def cumsum_lanes(x, Bb):
    B, H, T = x.shape
    return pl.pallas_call(
        _cumsum_body,
        grid=(B // Bb,),
        in_specs=[pl.BlockSpec((Bb, H, T), lambda b: (b, 0, 0))],
        out_specs=pl.BlockSpec((Bb, H, T), lambda b: (b, 0, 0)),
        out_shape=jax.ShapeDtypeStruct((B, H, T), F32),
        compiler_params=_cparams(("parallel",)),
        name="cumsum",
    )(x)


def _flash_tile(s_blocks, v, m_scr, l_scr, acc_scr, exp=jnp.exp):
    dv = acc_scr.shape[-1]
    m_prev = m_scr[...]
    mx = s_blocks[0]
    for sb in s_blocks[1:]:
        mx = jnp.maximum(mx, sb)
    m_new = jnp.maximum(m_prev, jnp.max(mx, axis=1, keepdims=True))
    p_blocks = [exp(sb - m_new) for sb in s_blocks]
    sm = p_blocks[0]
    for pb in p_blocks[1:]:
        sm = sm + pb
    alpha = exp(m_prev - m_new)
    l_scr[...] = alpha * l_scr[...] + jnp.sum(sm, axis=1, keepdims=True)
    p = (jnp.concatenate(p_blocks, axis=1) if len(p_blocks) > 1 else p_blocks[0]).astype(BF16)
    acc_scr[...] = alpha[:, :dv] * acc_scr[...] + _dot(p, v)
    m_scr[...] = m_new


def _flash_tiles(items, exp=jnp.exp):
    m_prev = [m[...] for _, _, m, _, _ in items]
    m_new = []
    for (s_blocks, _, _, _, _), mp in zip(items, m_prev):
        mx = s_blocks[0]
        for sb in s_blocks[1:]:
            mx = jnp.maximum(mx, sb)
        m_new.append(jnp.maximum(mp, jnp.max(mx, axis=1, keepdims=True)))
    p_blocks = [[exp(sb - mn) for sb in it[0]] for it, mn in zip(items, m_new)]
    sums = []
    for pb in p_blocks:
        sm = pb[0]
        for x in pb[1:]:
            sm = sm + x
        sums.append(jnp.sum(sm, axis=1, keepdims=True))
    pv = [_dot((jnp.concatenate(pb, axis=1) if len(pb) > 1 else pb[0]).astype(BF16), it[1])
          for pb, it in zip(p_blocks, items)]
    for (_, _, m_ref, l_ref, acc_ref), mp, mn, sm, x in zip(items, m_prev, m_new, sums, pv):
        alpha = exp(mp - mn)
        l_ref[...] = alpha * l_ref[...] + sm
        acc_ref[...] = alpha[:, :acc_ref.shape[-1]] * acc_ref[...] + x
        m_ref[...] = mn


def _flash_reset(m_scr, l_scr, acc_scr):
    m_scr[...] = jnp.full_like(m_scr, NEG_BIG)
    l_scr[...] = jnp.zeros_like(l_scr)
    acc_scr[...] = jnp.zeros_like(acc_scr)


def _lane_blocks(s):
    return [s[:, i * LANE:(i + 1) * LANE] for i in range(s.shape[1] // LANE)]


def _fox_body(q_ref, k_ref, v_ref, o_ref, m_scr, l_scr, acc_scr, *, tq, tk):
    qi = pl.program_id(1)
    j = pl.program_id(2)
    top = (qi * tq + tq - 1) // tk

    @pl.when(j == 0)
    def _():
        _flash_reset(m_scr, l_scr, acc_scr)

    def tile(diag):
        if diag:
            mask = (top * tk + _iota((tq, tk), 1)) <= (qi * tq + _iota((tq, tk), 0))
        items = []
        for h in range(FOX_H):
            s = _dot_nt(q_ref[0, h], k_ref[0, h])
            if diag:
                s = jnp.where(mask, s, NEG_BIG)
            items.append((_lane_blocks(s), v_ref[0, h], m_scr.at[h], l_scr.at[h], acc_scr.at[h]))
        _flash_tiles(items, exp=jnp.exp2)

    @pl.when(j == 0)
    def _():
        tile(True)

    @pl.when(jnp.logical_and(j > 0, j <= top))
    def _():
        tile(False)

    @pl.when(j == pl.num_programs(2) - 1)
    def _():
        for h in range(FOX_H):
            o_ref[0, h] = acc_scr[h] / jnp.maximum(l_scr[h][:, :HD], 1e-30)


LOG2E = 1.0 / math.log(2.0)


def fox_operands(q_heads, k_heads, f_heads):
    def top_bits(x):
        return lax.bitcast_convert_type(lax.bitcast_convert_type(x, jnp.uint32) & jnp.uint32(0xFFFF0000), F32)

    terms, rest = [], f_heads * LOG2E
    for _ in range(3):
        t = top_bits(rest)
        terms.append(t.astype(BF16))
        rest = rest - t
    f_terms = jnp.stack(terms, axis=-1)
    ones = jnp.ones_like(f_terms)
    pad = jnp.zeros(q_heads.shape[:3] + (LANE - HD - 6,), BF16)
    q_aug = jnp.concatenate([(q_heads * (HD ** -0.5 * LOG2E)).astype(BF16), f_terms, ones, pad], axis=-1)
    k_aug = jnp.concatenate([k_heads.astype(BF16), ones, -f_terms, pad], axis=-1)
    return q_aug, k_aug


def fox_prompt(q_aug, k_aug, kv, tq, tk):
    B, H, T, _ = q_aug.shape
    kmap = lambda i, j: jnp.maximum((i * tq + tq - 1) // tk - j, 0)
    return pl.pallas_call(
        functools.partial(_fox_body, tq=tq, tk=tk),
        grid=(B, T // tq, T // tk),
        in_specs=[pl.BlockSpec((1, H, tq, LANE), lambda b, i, j: (b, 0, i, 0)),
                  pl.BlockSpec((1, H, tk, LANE), lambda b, i, j: (b, 0, kmap(i, j), 0)),
                  pl.BlockSpec((1, H, tk, HD), lambda b, i, j: (b, 1, kmap(i, j), 0))],
        out_specs=pl.BlockSpec((1, H, tq, HD), lambda b, i, j: (b, 0, i, 0)),
        out_shape=jax.ShapeDtypeStruct((B, H, T, HD), F32),
        scratch_shapes=[pltpu.VMEM((H, tq, LANE), F32), pltpu.VMEM((H, tq, LANE), F32), pltpu.VMEM((H, tq, HD), F32)],
        compiler_params=_cparams(("parallel", "parallel", "arbitrary")),
        name="fox_prompt",
    )(q_aug, k_aug, kv)


def _fox_decode_body(pt_ref, q_ref, newkv_ref, newlf_ref, *refs, n_pages, tn):
    kv_refs = refs[:n_pages]
    lf_refs = refs[n_pages:2 * n_pages]
    o_ref = refs[2 * n_pages]
    hw = FOX_H * HD
    R = FOX_H * tn
    q = q_ref[0]
    qrep = jnp.concatenate([q] * FOX_H, axis=0)
    blockmask = (_iota((R, hw), 0) // tn) == (_iota((R, hw), 1) // HD)
    qbd = jnp.where(blockmask, qrep, 0.0).astype(BF16)
    rows, cols = _tri(PAGE)
    triu = (rows <= cols).astype(F32)
    ones = jnp.ones((8, PAGE), F32)
    expand = lambda a: jnp.concatenate([jnp.broadcast_to(a[h:h + 1, :], (tn, a.shape[1])) for h in range(FOX_H)], 0)
    carry_c = jnp.zeros((FOX_H, 1), F32)
    carry_r = jnp.zeros((1, FOX_H), F32)
    s_tiles = []
    for pg in range(n_pages):
        lf = lf_refs[pg][0]
        f_t = _dot_pick(lf, triu) + carry_c
        carry_c = f_t[:, PAGE - 1:PAGE]
        carry_r = carry_r + _dot_pick(ones, lf, _dot_nt, exact_lhs=True)[0:1, :]
        s_tiles.append(_dot(qbd, kv_refs[pg][0, 0:hw, :].astype(BF16)) - expand(f_t))
    lfn = newlf_ref[0]
    r8, c8 = _tri(tn)
    fq_c = _dot_pick((r8 >= c8).astype(F32), lfn, exact_lhs=True) + carry_r
    fq_t = _dot_pick(lfn, (r8 <= c8).astype(F32), _dot_tn) + carry_c
    fq_rows = jnp.concatenate([fq_c[:, h:h + 1] for h in range(FOX_H)], axis=0)
    kn = newkv_ref[0, :, 0:hw]
    s_new = _dot_split(qbd.astype(F32), kn, _dot_nt) - expand(fq_t)
    causal = _iota((R, tn), 1) <= (_iota((R, tn), 0) % tn)
    s_new = jnp.where(causal, s_new + fq_rows, NEG_BIG)
    s_tiles = [s + fq_rows for s in s_tiles]
    m = jnp.max(s_new, axis=1, keepdims=True)
    for s in s_tiles:
        m = jnp.maximum(m, jnp.max(s, axis=1, keepdims=True))
    p_new = jnp.where(causal, jnp.exp(s_new - m), 0.0)
    l = jnp.sum(p_new, axis=1, keepdims=True)
    acc = _dot_split(p_new, newkv_ref[0, :, hw:2 * hw])
    for pg, s in enumerate(s_tiles):
        p = jnp.exp(s - m)
        l = l + jnp.sum(p, axis=1, keepdims=True)
        acc = acc + _dot_nt(p.astype(BF16), kv_refs[pg][0, hw:2 * hw, :].astype(BF16))
    acc = acc / jnp.maximum(l, 1e-30)
    o_ref[0] = jnp.concatenate([acc[h * tn:(h + 1) * tn, h * HD:(h + 1) * HD] for h in range(FOX_H)], axis=1)


def fox_decode(page_table, q, newkv, newlf, kv_pool_t, lf_pool_t):
    B, tn, hw = q.shape
    n_pages = page_table.shape[1]
    page_spec = lambda rows, pg: pl.BlockSpec((1, rows, PAGE), lambda b, pt: (pt[b, pg], 0, 0))
    grid_spec = pltpu.PrefetchScalarGridSpec(
        num_scalar_prefetch=1,
        grid=(B,),
        in_specs=[pl.BlockSpec((1, tn, hw), lambda b, pt: (b, 0, 0)),
                  pl.BlockSpec((1, tn, 2 * hw), lambda b, pt: (b, 0, 0)),
                  pl.BlockSpec((1, tn, FOX_H), lambda b, pt: (b, 0, 0))]
                 + [page_spec(2 * hw, pg) for pg in range(n_pages)]
                 + [page_spec(FOX_H, pg) for pg in range(n_pages)],
        out_specs=pl.BlockSpec((1, tn, hw), lambda b, pt: (b, 0, 0)),
    )
    return pl.pallas_call(
        functools.partial(_fox_decode_body, n_pages=n_pages, tn=tn),
        grid_spec=grid_spec,
        out_shape=jax.ShapeDtypeStruct((B, tn, hw), F32),
        compiler_params=_cparams(("arbitrary",)),
        name="fox_decode",
    )(page_table, q, newkv, newlf, *([kv_pool_t] * n_pages), *([lf_pool_t] * n_pages))


def _t5_bucket(dist):
    n = jnp.maximum(dist, 0)
    nf = jnp.maximum(n, 1).astype(F32)
    large = BUCKET_EXACT + (jnp.log(nf / BUCKET_EXACT) / math.log(MAX_DISTANCE / BUCKET_EXACT)
                            * (N_BUCKETS - BUCKET_EXACT)).astype(jnp.int32)
    return jnp.where(n < BUCKET_EXACT, n, jnp.minimum(large, N_BUCKETS - 1))


def _bias_from_bucket(bucket, tbl_ref, head):
    out = jnp.zeros(bucket.shape, F32)
    for kk in range(N_BUCKETS):
        out = jnp.where(bucket == kk, tbl_ref[kk, head], out)
    return out


FOX_TQ, FOX_TK = 512, 512
RECURRENT_BATCH = 4
NSA_TQ = 2 * LANE
NSA_TK = 2 * LANE
NSA_FAR_GROUP = 4
CMP_PAT_CENTER = LANE // 2
NSA_BIAS_TILES = (BUCKET_SAT_DIST + NSA_TK + LANE - 1) // LANE


def _nsa_body(tbl_ref, q_ref, gate_ref, kc_ref, vc_ref, ks_ref, vs_ref, kw_ref, vw_ref, smap_ref, o_ref,
              bias_scr, pat_scr, score_scr, m_scr, l_scr, acc_scr, *, tq, q_pos0, win_pos0, n_sel, ncp, tw, single):
    g = pl.program_id(1)
    qi = pl.program_id(2)
    q0 = q_pos0 if single else q_pos0 + qi * tq
    aligned = (lambda x, m: x) if single else pl.multiple_of
    lo, hi = (max, min) if single else (jnp.maximum, jnp.minimum)
    R = NSA_R * tq
    scale = HD ** -0.5
    last_bias = tuple(tbl_ref[N_BUCKETS - 1, g * NSA_R + r] for r in range(NSA_R))

    @pl.when(qi == 0)
    def _():
        ii = _iota((tq, NSA_TK), 0)
        jj = _iota((tq, NSA_TK), 1)
        for dd in range(NSA_BIAS_TILES):
            bucket = _t5_bucket(ii - jj + dd * LANE)
            for r in range(NSA_R):
                bias_scr[dd, r * tq:(r + 1) * tq, :] = (
                    _bias_from_bucket(bucket, tbl_ref, g * NSA_R + r) - last_bias[r])
        bias_scr[NSA_BIAS_TILES] = jnp.zeros((R, NSA_TK), F32)
        rel_blk = _iota((tq, LANE), 1) - CMP_PAT_CENTER
        bucket = _t5_bucket(_iota((tq, LANE), 0) - rel_blk * CMP_STRIDE - (CMP_BLOCK - 1))
        for r in range(NSA_R):
            pat_scr[r] = _bias_from_bucket(bucket, tbl_ref, g * NSA_R + r) - last_bias[r]

    t_col = q0 + _iota((tq, 1), 0)

    bias_tiles, mask_tiles = [], []
    for nt in range(ncp // LANE):
        c_end = (nt * LANE + _iota((tq, LANE), 1)) * CMP_STRIDE + (CMP_BLOCK - 1)
        dist = t_col - c_end
        max_dist = q0 + tq - 1 - (nt * LANE * CMP_STRIDE + CMP_BLOCK - 1)
        min_dist = q0 - ((nt * LANE + LANE - 1) * CMP_STRIDE + CMP_BLOCK - 1)
        special = jnp.logical_and(max_dist >= 0, min_dist < BUCKET_SAT_DIST)

        off = nt * LANE - q0 // CMP_STRIDE + CMP_PAT_CENTER

        def general(off=off):
            lanes = _iota((tq, LANE), 1) + off
            inside = jnp.logical_and(lanes >= 0, lanes < LANE)
            shift = (LANE - off % LANE) % LANE
            return jnp.stack([jnp.where(inside, pltpu.roll(pat_scr[r], shift, 1), 0.0) for r in range(NSA_R)])

        def saturated():
            return jnp.zeros((NSA_R, tq, LANE), F32)

        bias_tiles.append(lax.cond(special, general, saturated))
        mask_tiles.append(dist >= 0)
    mask_c = jnp.concatenate(mask_tiles, axis=1) if len(mask_tiles) > 1 else mask_tiles[0]
    kc = kc_ref[0, 0].astype(BF16)
    vc = vc_ref[0, 0].astype(BF16)
    pcsum = jnp.zeros((tq, ncp), F32)
    qk_c = [_dot_nt((q_ref[0, r] * scale).astype(BF16), kc) for r in range(NSA_R)]
    pcs = []
    for r in range(NSA_R):
        bias_r = jnp.concatenate([b[r] for b in bias_tiles], axis=1) if len(bias_tiles) > 1 else bias_tiles[0][r]
        s = jnp.where(mask_c, qk_c[r] + bias_r, NEG_BIG)
        m = jnp.max(s, axis=1, keepdims=True)
        p = jnp.where(mask_c, jnp.exp(s - m), 0.0)
        pc = p / jnp.maximum(jnp.sum(p, axis=1, keepdims=True), 1e-30)
        pcs.append(pc.astype(BF16))
        pcsum = pcsum + pc
    o_c = [_dot(pc, vc) for pc in pcs]

    ps_t = _dot_pick(smap_ref[...], pcsum, _dot_nt, exact_lhs=True)
    j_col = _iota((LANE, 1), 0)
    t_row = q0 + _iota((1, tq), 1)
    cur = lax.shift_right_logical(t_row, int(math.log2(SEL_BLOCK)))
    score = jnp.where(j_col * SEL_BLOCK <= t_row, ps_t, -1.0)
    score = jnp.where(j_col == cur - 1, FORCE_SCORE, score)
    score = jnp.where(j_col == cur, FORCE_SCORE, score)
    score = jnp.where(j_col == 0, FORCE_SCORE, score)
    score = jnp.where(j_col < n_sel, score, -3e38)
    score_scr[...] = score

    def rank_body(jp, rank):
        row = score_scr[pl.ds(jp, 1), :]
        tie = jnp.where(j_col > jp, 1.0, 0.0)
        return rank + jnp.where(row > score, 1.0, jnp.where(row == score, tie, 0.0))

    n_rank = jnp.minimum((q0 + tq - 1) // SEL_BLOCK + 1, LANE)
    rank = lax.fori_loop(0, n_rank, rank_body, jnp.zeros((LANE, tq), F32))
    sel_t = jnp.where(rank < SEL_TOPN, 1.0, 0.0).astype(BF16)
    eye = (_iota((tq, tq), 0) == _iota((tq, tq), 1)).astype(BF16)
    sel = _dot_nt(eye, sel_t).astype(BF16)

    qs = (q_ref[0].reshape(R, HD) * scale).astype(BF16)
    log2_blk = int(math.log2(SEL_BLOCK))

    def stack(a):
        return jnp.concatenate([a] * NSA_R, axis=0)

    def rel_dist(pos0, width):
        return (q0 + _iota((tq, width), 0)) - (pos0 + _iota((tq, width), 1))

    def near_bias(offset, width):
        tiles = [bias_scr[hi(lo(offset - j * NSA_TK, 0) // LANE, NSA_BIAS_TILES)] for j in range(width // NSA_TK)]
        return jnp.concatenate(tiles, axis=1) if len(tiles) > 1 else tiles[0]

    def sel_tile(k0, width, near):
        k = ks_ref[0, 0, pl.ds(k0, width), :]
        v = vs_ref[0, 0, pl.ds(k0, width), :]
        blk = lax.shift_right_logical(k0 + _iota((LANE, width), 1), log2_blk)
        expand = jnp.where(_iota((LANE, width), 0) == blk, 1.0, 0.0).astype(BF16)
        chosen = _dot(sel, expand)
        s = _dot_nt(qs, k)
        if near:
            chosen = jnp.where(rel_dist(k0, width) >= 0, chosen, 0.0)
            s = s + near_bias(q0 - k0, width)
        s = jnp.where(stack(chosen) > 0.5, s, NEG_BIG)
        _flash_tile(_lane_blocks(s), v, m_scr, l_scr, acc_scr)

    _flash_reset(m_scr, l_scr, acc_scr)
    kt_top = (q0 + tq - 1) // NSA_TK
    near_w = min(2 * NSA_TK, ks_ref.shape[2])
    sel_tile(aligned(lo(kt_top - 1, 0) * NSA_TK, NSA_TK), near_w, True)

    n_far = lo(kt_top - 1, 0)
    n_groups = n_far // NSA_FAR_GROUP

    def sel_far_group(gi, carry):
        sel_tile(pl.multiple_of(gi * (NSA_FAR_GROUP * NSA_TK), NSA_FAR_GROUP * NSA_TK), NSA_FAR_GROUP * NSA_TK, False)
        return carry

    def sel_far(kt, carry):
        sel_tile(pl.multiple_of(kt * NSA_TK, NSA_TK), NSA_TK, False)
        return carry

    if single:
        if n_far:
            sel_tile(0, n_far * NSA_TK, False)
    else:
        lax.fori_loop(0, n_groups, sel_far_group, 0)
        lax.fori_loop(n_groups * NSA_FAR_GROUP, n_far, sel_far, 0)
    o_s = acc_scr[...] / jnp.maximum(l_scr[...][:, :HD], 1e-30)

    n_win = -(-(WINDOW - 1 + tq) // NSA_TK) + (0 if tq % NSA_TK == 0 else 1)
    win_w = min(n_win * NSA_TK, tw)
    wt_top = (hi(q0 + tq - win_pos0, tw) - 1) // NSA_TK
    k0 = aligned(hi(lo(wt_top + 1 - win_w // NSA_TK, 0), (tw - win_w) // NSA_TK) * NSA_TK, NSA_TK)
    _flash_reset(m_scr, l_scr, acc_scr)
    dist = rel_dist(win_pos0 + k0, win_w)
    ok = jnp.where(dist >= 0, jnp.where(dist < WINDOW, 1.0, 0.0), 0.0)
    s = _dot_nt(qs, kw_ref[0, 0, pl.ds(k0, win_w), :]) + near_bias(q0 - win_pos0 - k0, win_w)
    s = jnp.where(stack(ok) > 0.5, s, NEG_BIG)
    _flash_tile(_lane_blocks(s), vw_ref[0, 0, pl.ds(k0, win_w), :], m_scr, l_scr, acc_scr)
    o_w = acc_scr[...] / jnp.maximum(l_scr[...][:, :HD], 1e-30)

    gates = gate_ref[0, 0]
    for r in range(NSA_R):
        o_ref[0, r] = (gates[:, 3 * r:3 * r + 1] * o_c[r]
                       + gates[:, 3 * r + 1:3 * r + 2] * o_s[r * tq:(r + 1) * tq]
                       + gates[:, 3 * r + 2:3 * r + 3] * o_w[r * tq:(r + 1) * tq])


def _selection_overlap_t(ncp):
    c_start = np.arange(ncp)[None, :] * CMP_STRIDE
    s_start = np.arange(LANE)[:, None] * SEL_BLOCK
    return ((c_start < s_start + SEL_BLOCK) & (c_start + CMP_BLOCK > s_start)).astype(np.float32)


def nsa_attend(tbl, q, gates, kcvc, sel_arr, sel_off, win_arr, win_off, *, tq, q_pos0, win_pos0, n_sel):
    B, _, Tq, _ = q.shape
    ncp = kcvc.shape[2]
    tks = sel_arr.shape[2]
    tw = win_arr.shape[2]
    R = NSA_R * tq
    smap = jnp.asarray(_selection_overlap_t(ncp))
    kv_spec = lambda rows, off: pl.BlockSpec((1, 1, rows, HD), lambda b, g, i: (b, off + g, 0, 0))
    return pl.pallas_call(
        functools.partial(_nsa_body, tq=tq, q_pos0=q_pos0, win_pos0=win_pos0, n_sel=n_sel, ncp=ncp, tw=tw,
                          single=(Tq == tq)),
        grid=(B, NSA_G, Tq // tq),
        in_specs=[pl.BlockSpec(memory_space=pltpu.SMEM),
                  pl.BlockSpec((1, NSA_R, tq, HD), lambda b, g, i: (b, g, i, 0)),
                  pl.BlockSpec((1, 1, tq, 3 * NSA_R), lambda b, g, i: (b, g, i, 0)),
                  kv_spec(ncp, 0), kv_spec(ncp, 2),
                  kv_spec(tks, sel_off), kv_spec(tks, sel_off + 2),
                  kv_spec(tw, win_off), kv_spec(tw, win_off + 2),
                  pl.BlockSpec((LANE, ncp), lambda b, g, i: (0, 0))],
        out_specs=pl.BlockSpec((1, NSA_R, tq, HD), lambda b, g, i: (b, g, i, 0)),
        out_shape=jax.ShapeDtypeStruct((B, NSA_G * NSA_R, Tq, HD), F32),
        scratch_shapes=[pltpu.VMEM((NSA_BIAS_TILES + 1, R, NSA_TK), F32), pltpu.VMEM((NSA_R, tq, LANE), F32),
                        pltpu.VMEM((LANE, tq), F32),
                        pltpu.VMEM((R, LANE), F32), pltpu.VMEM((R, LANE), F32), pltpu.VMEM((R, HD), F32)],
        compiler_params=_cparams(("parallel", "parallel", "arbitrary")),
        name="nsa_attend",
    )(tbl, q, gates, kcvc, kcvc, sel_arr, sel_arr, win_arr, win_arr, smap)


def _nsa_gather_body(pt_ref, new_ref, *refs, n_pages):
    pages = refs[:n_pages]
    cmp_ref, sel_ref, xt_scr = refs[n_pages:]
    ngrp = 2 * NSA_G
    half = ngrp * HD
    rows16 = PAGE // CMP_STRIDE
    for pg in range(n_pages):
        xt = pages[pg][0].T
        for j in range(ngrp):
            sel_ref[0, j, pg * PAGE:(pg + 1) * PAGE, :] = xt[:, half + j * HD:half + (j + 1) * HD].astype(BF16)
        for slab in range(half // LANE):
            xt_scr[slab] = xt[:, slab * LANE:(slab + 1) * LANE]
            for p in range(CMP_STRIDE):
                rows = xt_scr[slab, pl.ds(p, rows16, stride=CMP_STRIDE), :]
                for jj in range(LANE // HD):
                    cmp_ref[0, slab * (LANE // HD) + jj, pg * rows16:(pg + 1) * rows16, p * HD:(p + 1) * HD] = (
                        rows[:, jj * HD:(jj + 1) * HD])
    tn = new_ref.shape[1]
    tail = sel_ref.shape[2] - n_pages * PAGE
    for j in range(ngrp):
        new = new_ref[0, :, (ngrp + j) * HD:(ngrp + j + 1) * HD]
        tile = jnp.concatenate([new, jnp.zeros((tail - tn, HD), F32)], axis=0)
        sel_ref[0, j, n_pages * PAGE:, :] = tile.astype(BF16)


def nsa_gather(page_table, new_rows, pool_t):
    B, tn, width = new_rows.shape
    n_pages = page_table.shape[1]
    ngrp = 2 * NSA_G
    sel_rows = -(-(n_pages * PAGE + tn) // NSA_TK) * NSA_TK
    cmp_rows = n_pages * PAGE // CMP_STRIDE
    grid_spec = pltpu.PrefetchScalarGridSpec(
        num_scalar_prefetch=1,
        grid=(B,),
        in_specs=[pl.BlockSpec((1, tn, width), lambda b, pt: (b, 0, 0))]
                 + [pl.BlockSpec((1, 2 * ngrp * HD, PAGE),
                                 functools.partial(lambda b, pt, pg: (pt[b, pg], 0, 0), pg=pg))
                    for pg in range(n_pages)],
        out_specs=[pl.BlockSpec((1, ngrp, cmp_rows, CMP_STRIDE * HD), lambda b, pt: (b, 0, 0, 0)),
                   pl.BlockSpec((1, ngrp, sel_rows, HD), lambda b, pt: (b, 0, 0, 0))],
        scratch_shapes=[pltpu.VMEM((ngrp * HD // LANE, PAGE, LANE), F32)],
    )
    return pl.pallas_call(
        functools.partial(_nsa_gather_body, n_pages=n_pages),
        grid_spec=grid_spec,
        out_shape=[jax.ShapeDtypeStruct((B, ngrp, cmp_rows, CMP_STRIDE * HD), F32),
                   jax.ShapeDtypeStruct((B, ngrp, sel_rows, HD), BF16)],
        compiler_params=_cparams(("arbitrary",)),
        name="nsa_gather",
    )(page_table, new_rows, *([pool_t] * n_pages))


def _row_tile(m):
    return ROW_TILE if m % ROW_TILE == 0 else m


def _small_params(entries):
    sp = jnp.zeros((8, LANE), F32)
    for off, bias, act, log_scale in entries:
        n = bias.shape[0]
        sp = sp.at[0, off:off + n].set(bias.astype(F32))
        sp = sp.at[1, off:off + n].set(act)
        if log_scale is not None:
            sp = sp.at[2, off:off + n].set(log_scale.astype(F32))
    return sp


EVEN_WIDTHS = (256, 256, 512, 512, GD_CH, 512, LANE)


def _even_weights(w_in):
    s = np.cumsum((0, 256, 256, 512, 512, 4, 4, 512, 512, 512, 512, 4, 4))
    col = lambda i: w_in[:, s[i]:s[i + 1]]
    small = jnp.concatenate([col(4), col(5), col(10), col(11)], axis=1)
    small = jnp.pad(small, ((0, 0), (0, LANE - small.shape[1])))
    return jnp.concatenate([col(0), col(1), col(2), col(3), col(6), col(7), col(8), col(9), small], axis=1).astype(BF16)


def _chunk_rows(small, B, T, L, lanes):
    r = small.reshape(B, T // L, L, LANE)[..., lanes[0]:lanes[1]]
    r = jnp.swapaxes(r, 2, 3)
    return jnp.pad(r, ((0, 0), (0, 0), (0, 8 - r.shape[2]), (0, 0)))


def even_layer(x, p, past, L, Bb):
    B, T, D = x.shape
    M = B * T
    tm = _row_tile(M)
    sp = _small_params([(0, p['mi_b_i'], ACT_ID, None), (4, p['mi_b_f'], ACT_LOGSIG, None),
                        (8, jnp.zeros((4,), F32), ACT_SIG, None), (12, p['gd_dt_bias'], ACT_DECAY, p['gd_a_log'])])
    mq, mk, mv, mo, gx, gz, small = norm_proj(x.reshape(M, D), p['norm_mix'], _even_weights(p['w_in']), sp,
                                              EVEN_WIDTHS, tm, T, head_major=(0, 1))
    sc = small.reshape(B, T, LANE)
    if past is None:
        c0 = jnp.zeros((B, MI_H, MI_DQK, MI_DV), F32)
        n0 = jnp.zeros((B, MI_H, 1, MI_DQK), F32)
        m0 = jnp.zeros((B, 1, MI_H), F32)
        s0 = jnp.zeros((B, GD_H, GD_DK, GD_DV), F32)
        conv0 = jnp.zeros((B, GD_CONV - 1, GD_CH), F32)
    else:
        c0, n0, m0, s0, conv0 = past
        n0 = n0.reshape(B, MI_H, 1, MI_DQK)
        m0 = m0.reshape(B, 1, MI_H)
    hm, c1, n1, m1 = mlstm(mq, mk, mv.reshape(B, T, -1), mo.reshape(B, T, -1), sc,
                           _chunk_rows(small, B, T, L, (0, 8)), c0, n0, m0,
                           p['mi_norm'].reshape(MI_H, MI_DV), L, Bb)
    og, s1, conv1 = gdn(gx.reshape(B, T, GD_CH), gz.reshape(B, T, -1), sc,
                        _chunk_rows(small, B, T, L, (12, 16)), s0, conv0, p['gd_conv_w'], p['gd_norm'], L, Bb)
    y = out_proj_residual(x.reshape(M, D), hm.reshape(M, -1), og.reshape(M, -1), p['w_out'].astype(BF16), tm)
    return y.reshape(B, T, D), (c1, n1.reshape(B, MI_H, MI_DQK), m1.reshape(B, MI_H), s1, conv1)


NSA_QW = NSA_G * NSA_R * HD
NSA_KVW = 6 * NSA_G * HD
NSA_CACHE_W = 4 * NSA_G * HD
FOX_W = FOX_H * HD
N_GATE = 3 * NSA_G * NSA_R
ODD_WIDTHS = (NSA_QW, NSA_KVW, FOX_W, 2 * FOX_W, LANE)


def _odd_weights(w_in):
    s = np.cumsum((0, NSA_QW, NSA_KVW, N_GATE, FOX_W, FOX_W, FOX_W, FOX_H))
    col = lambda i: w_in[:, s[i]:s[i + 1]]
    small = jnp.concatenate([col(2), col(6)], axis=1)
    small = jnp.pad(small, ((0, 0), (0, LANE - small.shape[1])))
    return jnp.concatenate([col(0), col(1), col(3), col(4), col(5), small], axis=1).astype(BF16)


def _heads(a, B, T, n):
    return jnp.transpose(a.reshape(B, T, n, HD), (0, 2, 1, 3))


def _unheads(a):
    B, n, T, _ = a.shape
    return jnp.transpose(a, (0, 2, 1, 3)).reshape(B * T, n * HD)


def odd_layer(x, p, rel_bias, w_buf, past, page_table):
    B, T, D = x.shape
    M = B * T
    tm = _row_tile(M)
    sp = _small_params([(0, jnp.zeros((N_GATE,), F32), ACT_SIG, None), (N_GATE, p['fox_b_f'], ACT_LOGSIG, None)])
    q_heads, nkv, fq_heads, fkv, small = norm_proj(x.reshape(M, D), p['norm_mix'], _odd_weights(p['w_in']), sp,
                                                   ODD_WIDTHS, tm, T, head_major=(0, 2))
    new_nsa = nkv[:, :NSA_CACHE_W].reshape(B, T, 4, NSA_G, HD)
    new_win = nkv[:, NSA_CACHE_W:].reshape(B, T, 2, NSA_G, HD)
    new_fox = fkv.reshape(B, T, 2, FOX_H, HD)
    logf = small[:, N_GATE:N_GATE + FOX_H].reshape(B, T, FOX_H)
    gates = jnp.transpose(small[:, :N_GATE].reshape(B, T, NSA_G, 3 * NSA_R), (0, 2, 1, 3))
    cmp_args = (p['nsa_cmp_pos'], p['nsa_cmp_w1'], p['nsa_cmp_w2'])
    rows16 = CMP_STRIDE * HD
    if past is None:
        groups = _heads(nkv, B, T, 6 * NSA_G)
        kcvc = nsa_compress(groups.reshape(B, 6 * NSA_G, T // CMP_STRIDE, rows16), T // CMP_STRIDE, *cmp_args, 1)
        arr = groups.astype(BF16)
        o_n = nsa_attend(rel_bias, q_heads, gates, kcvc, arr, 2 * NSA_G, arr, 4 * NSA_G,
                         tq=min(T, NSA_TQ), q_pos0=0, win_pos0=0, n_sel=-(-T // SEL_BLOCK))
        f_t = cumsum_lanes(jnp.transpose(logf, (0, 2, 1)), B)
        kv_heads = _heads(fkv, B, T, 2 * FOX_H)
        q_aug, k_aug = fox_operands(fq_heads, kv_heads[:, :FOX_H], f_t)
        o_f = fox_prompt(q_aug, k_aug, kv_heads.astype(BF16), min(T, FOX_TQ), min(T, FOX_TK))
        o_f = _unheads(o_f)
        win_prev = jnp.zeros((B, WINDOW, 2, NSA_G, HD), F32)
    else:
        nsa_pool, win_prev, fox_pool, logf_pool = past
        n_pool = nsa_pool.shape[0]
        n_pages = page_table.shape[1]
        start = n_pages * PAGE
        feature_major = lambda pool: jnp.moveaxis(pool.reshape(n_pool, PAGE, -1), 1, 2)
        cmp_rows, sel_arr = nsa_gather(page_table, nkv[:, :NSA_CACHE_W].reshape(B, T, NSA_CACHE_W),
                                       feature_major(nsa_pool))
        kcvc = nsa_compress(cmp_rows, start // CMP_STRIDE, *cmp_args, math.gcd(B, 8))
        wp = win_prev.shape[1]
        win_all = jnp.concatenate([win_prev.reshape(B, wp, 2 * NSA_G * HD), nkv[:, NSA_CACHE_W:].reshape(B, T, -1)], 1)
        tw = -(-(wp + T) // NSA_TK) * NSA_TK
        win_arr = _heads(jnp.pad(win_all, ((0, 0), (0, tw - wp - T), (0, 0))), B, tw, 2 * NSA_G).astype(BF16)
        o_n = nsa_attend(rel_bias, q_heads, gates, kcvc, sel_arr, 0, win_arr, 0,
                         tq=T, q_pos0=start, win_pos0=start - wp, n_sel=-(-(start + T) // SEL_BLOCK))
        o_f = fox_decode(page_table, _unheads(fq_heads * HD ** -0.5).reshape(B, T, FOX_W),
                         fkv.reshape(B, T, 2 * FOX_W), logf,
                         feature_major(fox_pool), feature_major(logf_pool))
        o_f = o_f.reshape(M, FOX_W)
    win_state = jnp.concatenate([win_prev, new_win], axis=1)[:, -w_buf:]
    y = out_proj_residual(x.reshape(M, D), _unheads(o_n), o_f, p['w_out'].astype(BF16), tm)
    return y.reshape(B, T, D), (new_nsa, win_state, new_fox, logf)


def _trunk(x, past, page_table, P, w_buf, L, Bb):
    B, T, D = x.shape
    pe = dict(norm_mix=P['norm_mix'][0], w_in=P['w_in_even'][0], w_out=P['w_out_even'][0], mi_b_i=P['mi_b_i'][0],
              mi_b_f=P['mi_b_f'][0], mi_norm=P['mi_norm'][0], gd_conv_w=P['gd_conv_w'][0], gd_a_log=P['gd_a_log'][0],
              gd_dt_bias=P['gd_dt_bias'][0], gd_norm=P['gd_norm'][0])
    po = dict(norm_mix=P['norm_mix'][1], w_in=P['w_in_odd'][0], w_out=P['w_out_odd'][0],
              nsa_cmp_pos=P['nsa_cmp_pos'][0], nsa_cmp_w1=P['nsa_cmp_w1'][0], nsa_cmp_w2=P['nsa_cmp_w2'][0],
              fox_b_f=P['fox_b_f'][0])
    tm = _row_tile(B * T)
    mlp = lambda x, layer, final: mlp_residual(
        x.reshape(B * T, D), P['norm_mlp'][layer], P['w_up'][layer].astype(BF16), P['w_down'][layer].astype(BF16),
        P['norm_final'], final, tm, MLP_FF_TILE).reshape(B, T, D)
    even_past = None if past is None else tuple(past[k][0] for k in ('mc', 'mn', 'mm', 'gs', 'gc'))
    odd_past = None if past is None else tuple(past[k][0] for k in ('nsa_kv', 'nsa_win', 'fox_kv', 'fox_logf'))
    x, st_e = even_layer(x, pe, even_past, L, Bb)
    x = mlp(x, 0, False)
    x, st_o = odd_layer(x, po, P['rel_bias'], w_buf, odd_past, page_table)
    y = mlp(x, 1, True)
    return y, tuple(a[None] for a in st_e + st_o)


def kernel(x_prompt, x_sample, state_mlstm_c, state_mlstm_n, state_mlstm_m, state_gdn_s, state_gdn_conv,
           cache_nsa_kv, state_nsa_win, cache_fox_kv, cache_fox_logf, page_table,
           norm_mix, norm_mlp, norm_final, w_up, w_down,
           w_in_even, w_out_even, mi_b_i, mi_b_f, mi_norm, gd_conv_w, gd_a_log, gd_dt_bias, gd_norm,
           w_in_odd, w_out_odd, nsa_cmp_pos, nsa_cmp_w1, nsa_cmp_w2, fox_b_f, rel_bias):
    P = dict(norm_mix=norm_mix, norm_mlp=norm_mlp, norm_final=norm_final, w_up=w_up, w_down=w_down,
             w_in_even=w_in_even, w_out_even=w_out_even, mi_b_i=mi_b_i, mi_b_f=mi_b_f, mi_norm=mi_norm,
             gd_conv_w=gd_conv_w, gd_a_log=gd_a_log, gd_dt_bias=gd_dt_bias, gd_norm=gd_norm,
             w_in_odd=w_in_odd, w_out_odd=w_out_odd, nsa_cmp_pos=nsa_cmp_pos, nsa_cmp_w1=nsa_cmp_w1,
             nsa_cmp_w2=nsa_cmp_w2, fox_b_f=fox_b_f, rel_bias=rel_bias)
    w_buf = state_nsa_win.shape[2]
    b_p, t_p = x_prompt.shape[:2]
    y_p, st_p = _trunk(x_prompt, None, None, P, w_buf, math.gcd(t_p, 64), math.gcd(b_p, RECURRENT_BATCH))
    past = dict(mc=state_mlstm_c, mn=state_mlstm_n, mm=state_mlstm_m, gs=state_gdn_s, gc=state_gdn_conv,
                nsa_kv=cache_nsa_kv, nsa_win=state_nsa_win, fox_kv=cache_fox_kv, fox_logf=cache_fox_logf)
    b_s, t_s = x_sample.shape[:2]
    y_s, st_s = _trunk(x_sample, past, page_table, P, w_buf, math.gcd(t_s, 64), math.gcd(b_s, RECURRENT_BATCH))
    return (y_p, y_s) + st_p + st_s
```
